```python
import jax, jax.numpy as jnp
from jax import lax
import numpy as np

D_MODEL = 1024
BATCH = 4
SEQ = 8192
DEPTH = 1

HEAD_DIM = 64
MIX_WIDTH = D_MODEL
RWKV_WIDTH = MIX_WIDTH // 2
FOX_WIDTH = MIX_WIDTH - RWKV_WIDTH
RWKV_HEADS = RWKV_WIDTH // HEAD_DIM
FOX_HEADS = FOX_WIDTH // HEAD_DIM
DECAY_LORA = 64
AAA_LORA = 64
GATE_LORA = 128
RWKV_IN = 3 * RWKV_WIDTH + DECAY_LORA + AAA_LORA + GATE_LORA
FOX_IN = 3 * FOX_WIDTH + FOX_HEADS
IN_WIDTH = RWKV_IN + FOX_IN
RWKV_SPLITS = [RWKV_WIDTH, 2 * RWKV_WIDTH, 3 * RWKV_WIDTH,
               3 * RWKV_WIDTH + DECAY_LORA, 3 * RWKV_WIDTH + DECAY_LORA + AAA_LORA]
FOX_SPLITS = [FOX_WIDTH, 2 * FOX_WIDTH, 3 * FOX_WIDTH]
Q_BLOCK = 128
N_EXPERTS = 32
TOP_K = 4
D_FF = D_MODEL
MOE_BLOCK = 128
SWIGLU_ALPHA = 1.702
SWIGLU_LIMIT = 7.0
RMS_EPS = 1e-5
RWKV_GN_EPS = 64e-5

kernel_name = "hymba_rwkv7_fox_moe_layer"


def rms_norm(x, g, eps=RMS_EPS):
    xf = x.astype(jnp.float32)
    y = xf * lax.rsqrt(jnp.mean(xf * xf, axis=-1, keepdims=True) + eps)
    return (y * g.astype(jnp.float32)).astype(x.dtype)


def rwkv7_group(u, mu, w0, w_up, a0, a_up, g_up, k_k, k_a, r_k, ln_w, ln_b):
    B, T, _ = u.shape
    H, N = RWKV_HEADS, HEAD_DIM
    u = u.astype(jnp.float32)
    u_prev = jnp.pad(u[:, :-1], ((0, 0), (1, 0), (0, 0)))
    u = u + (u_prev - u) * mu
    r, k, v, wl, al, gl = jnp.split(u, RWKV_SPLITS, axis=-1)
    w = -jax.nn.softplus(-(w0 + jnp.tanh(wl) @ w_up)) - 0.5
    decay = jnp.exp(-jnp.exp(w))
    a = jax.nn.sigmoid(a0 + al @ a_up)
    g = jax.nn.sigmoid(gl) @ g_up
    kk = (k * k_k).reshape(B, T, H, N)
    kk = kk / jnp.maximum(jnp.sqrt(jnp.sum(kk * kk, axis=-1, keepdims=True)), 1e-12)
    k = k * (1.0 + (a - 1.0) * k_a)
    r, k, v, decay, a = (t.reshape(B, T, H, N) for t in (r, k, v, decay, a))

    def step(S, inp):
        r_t, w_t, k_t, v_t, kk_t, a_t = inp
        s_kk = jnp.einsum('bhvk,bhk->bhv', S, kk_t)
        S = (S * w_t[:, :, None, :]
             - s_kk[..., None] * (kk_t * a_t)[:, :, None, :]
             + v_t[..., None] * k_t[:, :, None, :])
        return S, jnp.einsum('bhvk,bhk->bhv', S, r_t)

    S0 = jnp.zeros((B, H, N, N), jnp.float32)
    seq_inputs = tuple(jnp.moveaxis(t, 1, 0) for t in (r, decay, k, v, kk, a))
    _, y = lax.scan(step, S0, seq_inputs)
    y = jnp.moveaxis(y, 0, 1)
    mean = jnp.mean(y, axis=-1, keepdims=True)
    var = jnp.mean(jnp.square(y - mean), axis=-1, keepdims=True)
    y = ((y - mean) * lax.rsqrt(var + RWKV_GN_EPS)).reshape(B, T, RWKV_WIDTH) * ln_w + ln_b
    bonus = jnp.sum(r * k * r_k, axis=-1, keepdims=True) * v
    return (y + bonus.reshape(B, T, RWKV_WIDTH)) * g


def fox_group(u, f_bias, out_g):
    B, T, _ = u.shape
    H, Dh = FOX_HEADS, HEAD_DIM
    q, k, v, fl = jnp.split(u.astype(jnp.float32), FOX_SPLITS, axis=-1)
    q = q.reshape(B, T, H, Dh) * (Dh ** -0.5)
    k = k.reshape(B, T, H, Dh)
    v = v.reshape(B, T, H, Dh)
    log_f = jax.nn.log_sigmoid(fl + f_bias)
    c = jnp.transpose(jnp.cumsum(log_f, axis=1), (0, 2, 1))
    key_pos = jnp.arange(T)

    def query_block(i):
        start = i * Q_BLOCK
        qb = lax.dynamic_slice_in_dim(q, start, Q_BLOCK, axis=1)
        cb = lax.dynamic_slice_in_dim(c, start, Q_BLOCK, axis=2)
        s = jnp.einsum('bqhd,bkhd->bhqk', qb, k) + cb[..., :, None] - c[..., None, :]
        q_pos = start + jnp.arange(Q_BLOCK)
        s = jnp.where(key_pos[None, :] <= q_pos[:, None], s, -jnp.inf)
        p = jax.nn.softmax(s, axis=-1)
        return jnp.einsum('bhqk,bkhd->bqhd', p, v)

    o = lax.map(query_block, jnp.arange(T // Q_BLOCK))
    o = jnp.transpose(o, (1, 0, 2, 3, 4)).reshape(B, T, H, Dh)
    o = o * lax.rsqrt(jnp.mean(o * o, axis=-1, keepdims=True) + RMS_EPS)
    return o.reshape(B, T, FOX_WIDTH) * out_g


def clamped_swiglu(h):
    glu, lin = h[..., ::2], h[..., 1::2]
    glu = jnp.minimum(glu, SWIGLU_LIMIT)
    lin = jnp.clip(lin, -SWIGLU_LIMIT, SWIGLU_LIMIT)
    return glu * jax.nn.sigmoid(SWIGLU_ALPHA * glu) * (lin + 1.0)


def moe_ffn(h, router_w, router_b, w1, b1, w2, b2):
    B, T, D = h.shape
    n_tok = B * T
    n_slots = n_tok * TOP_K
    xt = h.reshape(n_tok, D)
    logits = (xt @ router_w + router_b).astype(jnp.float32)
    top_val, top_idx = lax.top_k(logits, TOP_K)
    gates = jax.nn.softmax(top_val, axis=-1)
    flat_e = top_idx.reshape(-1)
    order = jnp.argsort(flat_e)
    sorted_e = flat_e[order]
    sorted_tok = (order // TOP_K).astype(jnp.int32)
    sorted_gate = gates.reshape(-1)[order]
    sizes = jnp.bincount(flat_e, length=N_EXPERTS).astype(jnp.int32)
    padded = (sizes + MOE_BLOCK - 1) // MOE_BLOCK * MOE_BLOCK
    starts = jnp.cumsum(sizes) - sizes
    pad_ends = jnp.cumsum(padded)
    pad_starts = pad_ends - padded
    dest = pad_starts[sorted_e] + jnp.arange(n_slots, dtype=jnp.int32) - starts[sorted_e]
    n_pad = n_slots + N_EXPERTS * MOE_BLOCK
    n_blocks = n_pad // MOE_BLOCK
    slot_tok = jnp.zeros((n_pad,), jnp.int32).at[dest].set(sorted_tok)
    slot_gate = jnp.zeros((n_pad,), jnp.float32).at[dest].set(sorted_gate)
    block_start = jnp.arange(n_blocks, dtype=jnp.int32) * MOE_BLOCK
    block_e = jnp.minimum(jnp.searchsorted(pad_ends, block_start, side='right'), N_EXPERTS - 1)

    def expert_block(args):
        tok, gate, e = args
        hid = xt[tok] @ w1[e] + b1[e]
        out = clamped_swiglu(hid) @ w2[e] + b2[e]
        return out * gate[:, None].astype(out.dtype)

    outs = lax.map(expert_block, (slot_tok.reshape(n_blocks, MOE_BLOCK),
                                  slot_gate.reshape(n_blocks, MOE_BLOCK), block_e))
    y = jnp.zeros((n_tok, D), outs.dtype).at[slot_tok].add(outs.reshape(n_pad, D))
    return y.reshape(B, T, D).astype(h.dtype)


def setup_inputs(seed: int = 0) -> dict:
    key = jax.random.key(seed)
    ks = jax.random.split(key, 25)
    L = DEPTH
    f32 = jnp.float32

    def nrm(k, shape, scale):
        return scale * jax.random.normal(k, shape, f32)

    return {
        "x": jax.random.normal(ks[0], (BATCH, SEQ, D_MODEL), f32),
        "attn_norm_g": 1.0 + nrm(ks[1], (L, D_MODEL), 0.02),
        "w_in": nrm(ks[2], (L, D_MODEL, IN_WIDTH), D_MODEL ** -0.5),
        "rwkv_mu": jax.random.uniform(ks[3], (L, RWKV_IN), f32),
        "rwkv_w0": jax.random.uniform(ks[4], (L, RWKV_WIDTH), f32, -6.0, -0.5),
        "rwkv_w_up": nrm(ks[5], (L, DECAY_LORA, RWKV_WIDTH), 0.5 * DECAY_LORA ** -0.5),
        "rwkv_a0": nrm(ks[6], (L, RWKV_WIDTH), 0.5),
        "rwkv_a_up": nrm(ks[7], (L, AAA_LORA, RWKV_WIDTH), AAA_LORA ** -0.5),
        "rwkv_g_up": nrm(ks[8], (L, GATE_LORA, RWKV_WIDTH), GATE_LORA ** -0.5),
        "rwkv_k_k": 0.85 + nrm(ks[9], (L, RWKV_WIDTH), 0.02),
        "rwkv_k_a": 1.0 + nrm(ks[10], (L, RWKV_WIDTH), 0.02),
        "rwkv_r_k": nrm(ks[11], (L, RWKV_HEADS, HEAD_DIM), 0.1),
        "rwkv_ln_w": 1.0 + nrm(ks[12], (L, RWKV_WIDTH), 0.02),
        "rwkv_ln_b": nrm(ks[13], (L, RWKV_WIDTH), 0.01),
        "fox_f_bias": jax.random.uniform(ks[14], (L, FOX_HEADS), f32, 1.0, 4.0),
        "fox_out_g": 1.0 + nrm(ks[15], (L, FOX_WIDTH), 0.02),
        "w_out": nrm(ks[16], (L, MIX_WIDTH, D_MODEL), MIX_WIDTH ** -0.5),
        "ffn_norm_g": 1.0 + nrm(ks[17], (L, D_MODEL), 0.02),
        "router_w": nrm(ks[18], (L, D_MODEL, N_EXPERTS), D_MODEL ** -0.5),
        "router_b": nrm(ks[19], (L, N_EXPERTS), 0.01),
        "expert_w1": nrm(ks[20], (L, N_EXPERTS, D_MODEL, 2 * D_FF), D_MODEL ** -0.5),
        "expert_b1": nrm(ks[21], (L, N_EXPERTS, 2 * D_FF), 0.01),
        "expert_w2": nrm(ks[22], (L, N_EXPERTS, D_FF, D_MODEL), D_FF ** -0.5),
        "expert_b2": nrm(ks[23], (L, N_EXPERTS, D_MODEL), 0.01),
        "final_norm_g": 1.0 + nrm(ks[24], (D_MODEL,), 0.02),
    }


def reference(x, attn_norm_g, w_in, rwkv_mu, rwkv_w0, rwkv_w_up, rwkv_a0, rwkv_a_up,
              rwkv_g_up, rwkv_k_k, rwkv_k_a, rwkv_r_k, rwkv_ln_w, rwkv_ln_b,
              fox_f_bias, fox_out_g, w_out, ffn_norm_g, router_w, router_b,
              expert_w1, expert_b1, expert_w2, expert_b2, final_norm_g):
    for l in range(DEPTH):
        h = rms_norm(x, attn_norm_g[l])
        u = h @ w_in[l]
        y_rwkv = rwkv7_group(u[..., :RWKV_IN], rwkv_mu[l], rwkv_w0[l], rwkv_w_up[l],
                             rwkv_a0[l], rwkv_a_up[l], rwkv_g_up[l], rwkv_k_k[l],
                             rwkv_k_a[l], rwkv_r_k[l], rwkv_ln_w[l], rwkv_ln_b[l])
        y_fox = fox_group(u[..., RWKV_IN:], fox_f_bias[l], fox_out_g[l])
        mix = jnp.concatenate([y_rwkv.astype(x.dtype), y_fox.astype(x.dtype)], axis=-1)
        x = x + mix @ w_out[l]
        x = x + moe_ffn(rms_norm(x, ffn_norm_g[l]), router_w[l], router_b[l],
                        expert_w1[l], expert_b1[l], expert_w2[l], expert_b2[l])
    return rms_norm(x, final_norm_g)
```

```python
import functools

import jax
import jax.numpy as jnp
from jax import lax
from jax.experimental import pallas as pl
from jax.experimental.pallas import tpu as pltpu

F32 = jnp.float32
BF16 = jnp.bfloat16

HEAD_DIM = 64
N_HEADS = 8
GROUP_WIDTH = N_HEADS * HEAD_DIM
DECAY_LORA = 64
AAA_LORA = 64
GATE_LORA = 128
RWKV_IN = 3 * GROUP_WIDTH + DECAY_LORA + AAA_LORA + GATE_LORA
LORA_OFF = 3 * GROUP_WIDTH
N_EXPERTS = 32
TOP_K = 4
SWIGLU_ALPHA = 1.702
SWIGLU_LIMIT = 7.0
RMS_EPS = 1e-5
RWKV_GN_EPS = 64e-5
LANES = 128
RWKV_CHUNK = 64
VMEM_LIMIT = 56 * 1024 * 1024


def _cparams(semantics):
    return pltpu.CompilerParams(dimension_semantics=semantics, vmem_limit_bytes=VMEM_LIMIT)


def _dot(a, b):
    return jnp.dot(a, b, preferred_element_type=F32)


def _dot_nt(a, b):
    return lax.dot_general(a, b, (((1,), (1,)), ((), ())), preferred_element_type=F32)


def _dot_tn(a, b):
    return lax.dot_general(a, b, (((0,), (0,)), ((), ())), preferred_element_type=F32)


def _split3(x):
    hi = x.astype(BF16)
    r1 = x - hi.astype(F32)
    mid = r1.astype(BF16)
    lo = (r1 - mid.astype(F32)).astype(BF16)
    return hi, mid, lo


def _dot_exact_lhs(a_bf16, x):
    hi, mid, lo = _split3(x)
    return _dot(a_bf16, hi) + _dot(a_bf16, mid) + _dot(a_bf16, lo)


def _dot_exact_rhs(x, b_bf16):
    hi, mid, lo = _split3(x)
    return _dot(hi, b_bf16) + _dot(mid, b_bf16) + _dot(lo, b_bf16)


def _softplus(z):
    return jnp.maximum(z, 0.0) + jnp.log1p(jnp.exp(-jnp.abs(z)))


def _sigmoid(z):
    return 1.0 / (1.0 + jnp.exp(-z))


def _inproj_kernel(x_ref, g_ref, wr_ref, wq_ref, wf_ref, ur_ref, q_ref, k_ref, v_ref, fl_ref):
    x = x_ref[...]
    h = x * lax.rsqrt(jnp.mean(x * x, axis=-1, keepdims=True) + RMS_EPS) * g_ref[...]
    hb = h.astype(BF16)
    ur_ref[...] = _dot(hb, wr_ref[...])
    qkv = _dot(hb, wq_ref[...])
    q_ref[...] = (qkv[:, :GROUP_WIDTH] * (HEAD_DIM ** -0.5)).astype(BF16)
    k_ref[...] = qkv[:, GROUP_WIDTH:2 * GROUP_WIDTH].astype(BF16)
    v_ref[...] = qkv[:, 2 * GROUP_WIDTH:].astype(BF16)
    fl_ref[...] = _dot(hb, wf_ref[...])


def _inproj(x2, g, w_r, w_qkv, w_f, tm):
    n, d = x2.shape
    const = lambda i: (0, 0)
    row = lambda i: (i, 0)
    return pl.pallas_call(
        _inproj_kernel,
        grid=(n // tm,),
        in_specs=[
            pl.BlockSpec((tm, d), row),
            pl.BlockSpec((1, d), const),
            pl.BlockSpec(w_r.shape, const),
            pl.BlockSpec(w_qkv.shape, const),
            pl.BlockSpec(w_f.shape, const),
        ],
        out_specs=[
            pl.BlockSpec((tm, RWKV_IN), row),
            pl.BlockSpec((tm, GROUP_WIDTH), row),
            pl.BlockSpec((tm, GROUP_WIDTH), row),
            pl.BlockSpec((tm, GROUP_WIDTH), row),
            pl.BlockSpec((tm, LANES), row),
        ],
        out_shape=[
            jax.ShapeDtypeStruct((n, RWKV_IN), F32),
            jax.ShapeDtypeStruct((n, GROUP_WIDTH), BF16),
            jax.ShapeDtypeStruct((n, GROUP_WIDTH), BF16),
            jax.ShapeDtypeStruct((n, GROUP_WIDTH), BF16),
            jax.ShapeDtypeStruct((n, LANES), F32),
        ],
        compiler_params=_cparams(("parallel",)),
        name="inproj",
    )(x2, g, w_r, w_qkv, w_f)


def _fox_gate_kernel(fl_ref, fb_ref, c_ref, carry):
    tt = fl_ref.shape[1]

    @pl.when(pl.program_id(1) == 0)
    def _():
        carry[...] = jnp.zeros_like(carry)

    z = fl_ref[0] + fb_ref[...]
    log_f = jnp.minimum(z, 0.0) - jnp.log1p(jnp.exp(-jnp.abs(z)))
    ri = lax.broadcasted_iota(jnp.int32, (tt, tt), 0)
    ci = lax.broadcasted_iota(jnp.int32, (tt, tt), 1)
    tri = jnp.where(ri >= ci, 1.0, 0.0).astype(BF16)
    c = _dot_exact_lhs(tri, log_f) + carry[...]
    carry[...] = c[tt - 1:tt, :]
    c_ref[0] = jnp.transpose(c)[:N_HEADS, :]


def _fox_gate(fl3, fb_pad, tt):
    b, t, _ = fl3.shape
    return pl.pallas_call(
        _fox_gate_kernel,
        grid=(b, t // tt),
        in_specs=[
            pl.BlockSpec((1, tt, LANES), lambda i, j: (i, j, 0)),
            pl.BlockSpec((1, LANES), lambda i, j: (0, 0)),
        ],
        out_specs=pl.BlockSpec((1, N_HEADS, tt), lambda i, j: (i, 0, j)),
        out_shape=jax.ShapeDtypeStruct((b, N_HEADS, t), F32),
        scratch_shapes=[pltpu.VMEM((1, LANES), F32)],
        compiler_params=_cparams(("parallel", "arbitrary")),
        name="fox_gate",
    )(fl3, fb_pad)


def _rwkv_prep_kernel(u_ref, mu_ref, w0_ref, wup_ref, a0_ref, aup_ref, gup_ref, kk_ref, ka_ref,
                      bd_ref, r_out, k_out, v_out, lw_out, kk_out, b_out, g_out, carry):
    tt = u_ref.shape[1]

    @pl.when(pl.program_id(1) == 0)
    def _():
        carry[...] = jnp.zeros_like(carry)

    u = u_ref[0]
    prev = pltpu.roll(u, 1, axis=0)
    row = lax.broadcasted_iota(jnp.int32, u.shape, 0)
    prev = jnp.where(row == 0, carry[...], prev)
    carry[...] = u[tt - 1:tt, :]
    us = u + (prev - u) * mu_ref[...]

    r = us[:, :GROUP_WIDTH]
    k = us[:, GROUP_WIDTH:2 * GROUP_WIDTH]
    v = us[:, 2 * GROUP_WIDTH:LORA_OFF]
    wa = us[:, LORA_OFF:LORA_OFF + LANES]
    gl = us[:, LORA_OFF + LANES:]

    w_lin = _dot(jnp.tanh(wa).astype(BF16), wup_ref[...])
    a_lin = _dot(wa.astype(BF16), aup_ref[...])
    w = -_softplus(-(w0_ref[...] + w_lin)) - 0.5
    lw_out[0] = -jnp.exp(w)
    a = _sigmoid(a0_ref[...] + a_lin)
    g_out[0] = _dot(_sigmoid(gl).astype(BF16), gup_ref[...])

    kkr = k * kk_ref[...]
    ss = _dot_exact_rhs(kkr * kkr, bd_ref[...])
    kk = kkr / jnp.maximum(jnp.sqrt(ss), 1e-12)
    r_out[0] = r
    k_out[0] = k * (1.0 + (a - 1.0) * ka_ref[...])
    v_out[0] = v
    kk_out[0] = kk
    b_out[0] = kk * a


def _rwkv_prep(u3, mu, w0, wup_pad, a0, aup_pad, gup, k_k, k_a, bd, tt):
    b, t, _ = u3.shape
    const = lambda i, j: (0, 0)
    tile = lambda i, j: (i, j, 0)
    out_sds = jax.ShapeDtypeStruct((b, t, GROUP_WIDTH), F32)
    vec = pl.BlockSpec((1, GROUP_WIDTH), const)
    return pl.pallas_call(
        _rwkv_prep_kernel,
        grid=(b, t // tt),
        in_specs=[
            pl.BlockSpec((1, tt, RWKV_IN), tile),
            pl.BlockSpec((1, RWKV_IN), const),
            vec,
            pl.BlockSpec((LANES, GROUP_WIDTH), const),
            vec,
            pl.BlockSpec((LANES, GROUP_WIDTH), const),
            pl.BlockSpec((GATE_LORA, GROUP_WIDTH), const),
            vec,
            vec,
            pl.BlockSpec((GROUP_WIDTH, GROUP_WIDTH), const),
        ],
        out_specs=[pl.BlockSpec((1, tt, GROUP_WIDTH), tile)] * 7,
        out_shape=[out_sds] * 7,
        scratch_shapes=[pltpu.VMEM((1, RWKV_IN), F32)],
        compiler_params=_cparams(("parallel", "arbitrary")),
        name="rwkv_prep",
    )(u3, mu, w0, wup_pad, a0, aup_pad, gup, k_k, k_a, bd)


def _unit_lower_inverse(l_strict, level_masks, eye):
    t = eye - jnp.where(level_masks[0], l_strict, 0.0)
    for m in level_masks[1:]:
        c = jnp.where(m, l_strict, 0.0).astype(BF16)
        tb = t.astype(BF16)
        t = t - _dot(tb, _dot(c, tb).astype(BF16))
    return t


def _rwkv_scan_kernel(r_ref, k_ref, v_ref, lw_ref, kk_ref, b_ref, g_ref, rk_ref, lnw_ref, lnb_ref,
                      bd_ref, o_ref, s_scr, y_scr):
    c = RWKV_CHUNK

    @pl.when(pl.program_id(1) == 0)
    def _():
        s_scr[...] = jnp.zeros_like(s_scr)

    r = r_ref[0]
    k = k_ref[0]
    v = v_ref[0]
    lw = lw_ref[0]
    kk = kk_ref[0]
    b = b_ref[0]

    ri = lax.broadcasted_iota(jnp.int32, (c, c), 0)
    ci = lax.broadcasted_iota(jnp.int32, (c, c), 1)
    incl = ri >= ci
    strict = ri > ci
    eye = jnp.where(ri == ci, 1.0, 0.0)
    level_masks = []
    s = 1
    while s < c:
        same = (ri // (2 * s)) == (ci // (2 * s))
        level_masks.append(same & ((ri % (2 * s)) >= s) & ((ci % (2 * s)) < s))
        s *= 2
    ri2 = lax.broadcasted_iota(jnp.int32, (c, 2 * c), 0)
    ci2 = lax.broadcasted_iota(jnp.int32, (c, 2 * c), 1) % c
    incl2 = ri2 >= ci2

    g_cum = _dot_exact_lhs(jnp.where(incl, 1.0, 0.0).astype(BF16), lw)
    g_last = g_cum[c - 1:c, :]
    r_t = (r * jnp.exp(g_cum)).astype(BF16)
    kk_t = (kk * jnp.exp(g_cum - lw)).astype(BF16)
    e_neg = jnp.exp(-g_cum)
    b_n = (b * e_neg).astype(BF16)
    k_n = (k * e_neg).astype(BF16)
    e_end = jnp.exp(g_last - g_cum)
    b_e = (b * e_end).astype(BF16)
    k_e = (k * e_end).astype(BF16)
    gamma = jnp.exp(g_last)
    vb = v.astype(BF16)

    for h in range(N_HEADS):
        sl = slice(h * HEAD_DIM, (h + 1) * HEAD_DIM)
        lhs = jnp.concatenate([r_t[:, sl], kk_t[:, sl]], axis=0)
        rhs = jnp.concatenate([b_n[:, sl], k_n[:, sl]], axis=0)
        p = _dot_nt(lhs, rhs)
        s0 = s_scr[h]
        q0 = _dot_nt(lhs, s0.astype(BF16))
        p_r = jnp.where(incl2, p[:c], 0.0).astype(BF16)
        l_b = jnp.where(strict, p[c:, :c], 0.0)
        l_k = jnp.where(strict, p[c:, c:], 0.0).astype(BF16)
        z = q0[c:] + _dot(l_k, vb[:, sl])
        t_inv = _unit_lower_inverse(l_b, level_masks, eye)
        u = -_dot(t_inv.astype(BF16), z.astype(BF16))
        w_cat = jnp.concatenate([u.astype(BF16), vb[:, sl]], axis=0)
        y_scr[:, sl] = q0[:c] + _dot(p_r, w_cat)
        x_cat = jnp.concatenate([b_e[:, sl], k_e[:, sl]], axis=0)
        s_scr[h] = s0 * gamma[:, sl] + _dot_tn(w_cat, x_cat)

    y = y_scr[...]
    bd = bd_ref[...]
    inv_n = 1.0 / HEAD_DIM
    mean = _dot_exact_rhs(y, bd) * inv_n
    d = y - mean
    var = _dot_exact_rhs(d * d, bd) * inv_n
    yn = d * lax.rsqrt(var + RWKV_GN_EPS) * lnw_ref[...] + lnb_ref[...]
    bonus = _dot_exact_rhs(r * k * rk_ref[...], bd) * v
    o_ref[0] = ((yn + bonus) * g_ref[0]).astype(o_ref.dtype)


def _rwkv_scan(r, k, v, lw, kk, b, g, r_k, ln_w, ln_b, bd):
    bsz, t, _ = r.shape
    c = RWKV_CHUNK
    const = lambda i, j: (0, 0)
    tile = pl.BlockSpec((1, c, GROUP_WIDTH), lambda i, j: (i, j, 0))
    vec = pl.BlockSpec((1, GROUP_WIDTH), const)
    return pl.pallas_call(
        _rwkv_scan_kernel,
        grid=(bsz, t // c),
        in_specs=[tile] * 7 + [vec, vec, vec, pl.BlockSpec((GROUP_WIDTH, GROUP_WIDTH), const)],
        out_specs=tile,
        out_shape=jax.ShapeDtypeStruct((bsz, t, GROUP_WIDTH), BF16),
        scratch_shapes=[
            pltpu.VMEM((N_HEADS, HEAD_DIM, HEAD_DIM), F32),
            pltpu.VMEM((c, GROUP_WIDTH), F32),
        ],
        compiler_params=_cparams(("parallel", "arbitrary")),
        name="rwkv_scan",
    )(r, k, v, lw, kk, b, g, r_k, ln_w, ln_b, bd)


def _fox_attn_kernel(q_ref, k_ref, v_ref, c_ref, og_ref, o_ref, m_scr, l_scr, acc_scr, *, tq, tk):
    qi = pl.program_id(2)
    q = q_ref[0]
    lane = lax.broadcasted_iota(jnp.int32, (tq, LANES), 1)
    lo = lane < HEAD_DIM
    zero = jnp.zeros_like(q)
    qcat = jnp.concatenate([jnp.where(lo, q, zero), jnp.where(lo, zero, q)], axis=0)

    m_scr[...] = jnp.full_like(m_scr, -jnp.inf)
    l_scr[...] = jnp.zeros_like(l_scr)
    acc_scr[...] = jnp.zeros_like(acc_scr)

    def step(j, masked):
        start = pl.multiple_of(j * tk, tk)
        kj = k_ref[0, pl.ds(start, tk), :]
        vj = v_ref[0, pl.ds(start, tk), :]
        cj = c_ref[0, 0, j]
        s = _dot_nt(qcat, kj)
        z = jnp.concatenate([s[:tq] - cj[0:1], s[tq:] - cj[1:2]], axis=0)
        if masked:
            rr = lax.broadcasted_iota(jnp.int32, (2 * tq, tk), 0) % tq
            cc = lax.broadcasted_iota(jnp.int32, (2 * tq, tk), 1)
            z = jnp.where(cc <= rr, z, -jnp.inf)
        m_prev = m_scr[...]
        m_new = jnp.maximum(m_prev, jnp.max(z, axis=-1, keepdims=True))
        alpha = jnp.exp(m_prev - m_new)
        p = jnp.exp(z - m_new)
        l_scr[...] = alpha * l_scr[...] + jnp.sum(p, axis=-1, keepdims=True)
        acc_scr[...] = alpha * acc_scr[...] + _dot(p.astype(BF16), vj)
        m_scr[...] = m_new

    def body(j, carry):
        step(j, False)
        return carry

    lax.fori_loop(0, qi, body, 0)
    step(qi, True)

    o = acc_scr[...] / l_scr[...]
    o0 = jnp.where(lo, o[:tq], 0.0)
    o1 = jnp.where(lo, 0.0, o[tq:])
    inv_n = 1.0 / HEAD_DIM
    o0 = o0 * lax.rsqrt(jnp.sum(o0 * o0, axis=-1, keepdims=True) * inv_n + RMS_EPS)
    o1 = o1 * lax.rsqrt(jnp.sum(o1 * o1, axis=-1, keepdims=True) * inv_n + RMS_EPS)
    o_ref[0] = ((o0 + o1) * og_ref[...]).astype(o_ref.dtype)


def _fox_attn(q, k, v, c5, out_g, tq, tk):
    bsz, t, _ = q.shape
    pairs = N_HEADS // 2
    kern = functools.partial(_fox_attn_kernel, tq=tq, tk=tk)
    return pl.pallas_call(
        kern,
        grid=(bsz, pairs, t // tq),
        in_specs=[
            pl.BlockSpec((1, tq, LANES), lambda b, p, i: (b, i, p)),
            pl.BlockSpec((1, t, LANES), lambda b, p, i: (b, 0, p)),
            pl.BlockSpec((1, t, LANES), lambda b, p, i: (b, 0, p)),
            pl.BlockSpec((1, 1, t // tk, 2, tk), lambda b, p, i: (b, p, 0, 0, 0)),
            pl.BlockSpec((1, LANES), lambda b, p, i: (0, p)),
        ],
        out_specs=pl.BlockSpec((1, tq, LANES), lambda b, p, i: (b, i, p)),
        out_shape=jax.ShapeDtypeStruct((bsz, t, GROUP_WIDTH), BF16),
        scratch_shapes=[
            pltpu.VMEM((2 * tq, 1), F32),
            pltpu.VMEM((2 * tq, 1), F32),
            pltpu.VMEM((2 * tq, LANES), F32),
        ],
        compiler_params=_cparams(("parallel", "parallel", "arbitrary")),
        name="fox_attn",
    )(q, k, v, c5, out_g)


def _outproj_router_kernel(x_ref, yr_ref, yf_ref, wo_r_ref, wo_f_ref, g_ref, rwt_ref, rb_ref,
                           x1_ref, h_ref, idx_ref, gate_ref):
    x1 = x_ref[...] + _dot(yr_ref[...], wo_r_ref[...]) + _dot(yf_ref[...], wo_f_ref[...])
    x1_ref[...] = x1
    h = x1 * lax.rsqrt(jnp.mean(x1 * x1, axis=-1, keepdims=True) + RMS_EPS) * g_ref[...]
    h_ref[...] = h
    logits = lax.dot_general(rwt_ref[...], h, (((1,), (1,)), ((), ())),
                             precision=lax.Precision.HIGHEST,
                             preferred_element_type=F32) + rb_ref[...]
    eidx = lax.broadcasted_iota(jnp.int32, logits.shape, 0)
    vals, idxs = [], []
    for _ in range(TOP_K):
        m = jnp.max(logits, axis=0, keepdims=True)
        i = jnp.min(jnp.where(logits == m, eidx, N_EXPERTS), axis=0, keepdims=True)
        vals.append(m)
        idxs.append(i)
        logits = jnp.where(eidx == i, -jnp.inf, logits)
    es = [jnp.exp(val - vals[0]) for val in vals]
    denom = es[0] + es[1] + es[2] + es[3]
    idx_ref[...] = jnp.concatenate(idxs, axis=0)
    gate_ref[...] = jnp.concatenate([e / denom for e in es], axis=0)


def _outproj_router(x2, yr, yf, wo_r, wo_f, g, rwt, rb, tm):
    n, d = x2.shape
    const = lambda i: (0, 0)
    row = lambda i: (i, 0)
    col = lambda i: (0, i)
    return pl.pallas_call(
        _outproj_router_kernel,
        grid=(n // tm,),
        in_specs=[
            pl.BlockSpec((tm, d), row),
            pl.BlockSpec((tm, GROUP_WIDTH), row),
            pl.BlockSpec((tm, GROUP_WIDTH), row),
            pl.BlockSpec((GROUP_WIDTH, d), const),
            pl.BlockSpec((GROUP_WIDTH, d), const),
            pl.BlockSpec((1, d), const),
            pl.BlockSpec((N_EXPERTS, d), const),
            pl.BlockSpec((N_EXPERTS, 1), const),
        ],
        out_specs=[
            pl.BlockSpec((tm, d), row),
            pl.BlockSpec((tm, d), row),
            pl.BlockSpec((TOP_K, tm), col),
            pl.BlockSpec((TOP_K, tm), col),
        ],
        out_shape=[
            jax.ShapeDtypeStruct((n, d), F32),
            jax.ShapeDtypeStruct((n, d), F32),
            jax.ShapeDtypeStruct((TOP_K, n), jnp.int32),
            jax.ShapeDtypeStruct((TOP_K, n), F32),
        ],
        compiler_params=_cparams(("parallel",)),
        name="outproj_router",
    )(x2, yr, yf, wo_r, wo_f, g, rwt, rb)


def _expert_kernel(be_ref, tok_ref, h_hbm, w1g_ref, w1l_ref, b1g_ref, b1l_ref, w2_ref, b2_ref,
                   o_ref, xbuf, sem, *, bm):
    del be_ref

    def issue(r, carry):
        tok = tok_ref[0, 0, r]
        pltpu.make_async_copy(h_hbm.at[pl.ds(tok, 1), :], xbuf.at[pl.ds(r, 1), :], sem).start()
        return carry

    lax.fori_loop(0, bm, issue, 0)
    pltpu.make_async_copy(h_hbm.at[pl.ds(0, bm), :], xbuf, sem).wait()

    xb = xbuf[...].astype(BF16)
    glu = _dot(xb, w1g_ref[0]) + b1g_ref[0]
    lin = _dot(xb, w1l_ref[0]) + b1l_ref[0]
    glu = jnp.minimum(glu, SWIGLU_LIMIT)
    lin = jnp.clip(lin, -SWIGLU_LIMIT, SWIGLU_LIMIT)
    act = glu * _sigmoid(SWIGLU_ALPHA * glu) * (lin + 1.0)
    o_ref[...] = _dot(act.astype(BF16), w2_ref[0]) + b2_ref[0]


def _expert_mlp(block_e, slot_tok3, h2, w1g, w1l, b1g, b1l, w2, b2, bm):
    n_blocks = slot_tok3.shape[0]
    d = h2.shape[1]
    dff = w1g.shape[2]
    wmap = lambda i, be: (be[i], 0, 0)
    grid_spec = pltpu.PrefetchScalarGridSpec(
        num_scalar_prefetch=1,
        grid=(n_blocks,),
        in_specs=[
            pl.BlockSpec((1, 1, bm), lambda i, be: (i, 0, 0), memory_space=pltpu.SMEM),
            pl.BlockSpec(memory_space=pl.ANY),
            pl.BlockSpec((1, d, dff), wmap),
            pl.BlockSpec((1, d, dff), wmap),
            pl.BlockSpec((1, 1, dff), wmap),
            pl.BlockSpec((1, 1, dff), wmap),
            pl.BlockSpec((1, dff, d), wmap),
            pl.BlockSpec((1, 1, d), wmap),
        ],
        out_specs=pl.BlockSpec((bm, d), lambda i, be: (i, 0)),
        scratch_shapes=[pltpu.VMEM((bm, d), F32), pltpu.SemaphoreType.DMA(())],
    )
    return pl.pallas_call(
        functools.partial(_expert_kernel, bm=bm),
        grid_spec=grid_spec,
        out_shape=jax.ShapeDtypeStruct((n_blocks * bm, d), F32),
        compiler_params=_cparams(("arbitrary",)),
        name="expert_mlp",
    )(block_e, slot_tok3, h2, w1g, w1l, b1g, b1l, w2, b2)


def _combine_kernel(pos_ref, gate_ref, x1_ref, g_ref, outs_hbm, o_ref, buf, sem, *, tc):
    for kk in range(TOP_K):
        def issue(r, carry, kk=kk):
            p = pos_ref[kk, r]
            pltpu.make_async_copy(outs_hbm.at[pl.ds(p, 1), :], buf.at[kk, pl.ds(r, 1), :],
                                  sem).start()
            return carry

        lax.fori_loop(0, tc, issue, 0)
    for kk in range(TOP_K):
        pltpu.make_async_copy(outs_hbm.at[pl.ds(0, tc), :], buf.at[kk], sem).wait()

    gates = gate_ref[...]
    y = x1_ref[...]
    for kk in range(TOP_K):
        y = y + buf[kk] * gates[:, kk:kk + 1]
    o_ref[...] = y * lax.rsqrt(jnp.mean(y * y, axis=-1, keepdims=True) + RMS_EPS) * g_ref[...]


def _combine(pos, gates_t, x1, g, outs, tc):
    n, d = x1.shape
    return pl.pallas_call(
        functools.partial(_combine_kernel, tc=tc),
        grid=(n // tc,),
        in_specs=[
            pl.BlockSpec((TOP_K, tc), lambda i: (0, i), memory_space=pltpu.SMEM),
            pl.BlockSpec((tc, TOP_K), lambda i: (i, 0)),
            pl.BlockSpec((tc, d), lambda i: (i, 0)),
            pl.BlockSpec((1, d), lambda i: (0, 0)),
            pl.BlockSpec(memory_space=pl.ANY),
        ],
        out_specs=pl.BlockSpec((tc, d), lambda i: (i, 0)),
        out_shape=jax.ShapeDtypeStruct((n, d), F32),
        scratch_shapes=[pltpu.VMEM((TOP_K, tc, d), F32), pltpu.SemaphoreType.DMA(())],
        compiler_params=_cparams(("arbitrary",)),
        name="combine",
    )(pos, gates_t, x1, g, outs)


def _dispatch_plan(idx, bm):
    n = idx.shape[1]
    n_slots = TOP_K * n
    e_flat = idx.reshape(-1)
    onehot = (e_flat[:, None] == jnp.arange(N_EXPERTS, dtype=jnp.int32)[None, :]).astype(jnp.int32)
    csum = jnp.cumsum(onehot, axis=0)
    rank = jnp.sum(csum * onehot, axis=1) - 1
    sizes = csum[-1]
    padded = (sizes + bm - 1) // bm * bm
    pad_ends = jnp.cumsum(padded)
    pad_starts = pad_ends - padded
    pos = (pad_starts[e_flat] + rank).astype(jnp.int32)
    n_pad = n_slots + N_EXPERTS * bm
    n_blocks = n_pad // bm
    tok = jnp.tile(jnp.arange(n, dtype=jnp.int32), TOP_K)
    slot_tok = jnp.zeros((n_pad,), jnp.int32).at[pos].set(tok, unique_indices=True)
    block_start = jnp.arange(n_blocks, dtype=jnp.int32) * bm
    block_e = jnp.minimum(jnp.searchsorted(pad_ends, block_start, side='right'),
                          N_EXPERTS - 1).astype(jnp.int32)
    return pos.reshape(TOP_K, n), slot_tok.reshape(n_blocks, 1, bm), block_e


def _block_diag_ones():
    hid = jnp.arange(GROUP_WIDTH, dtype=jnp.int32) // HEAD_DIM
    return (hid[:, None] == hid[None, :]).astype(BF16)


def _pick(n, pref):
    return pref if n % pref == 0 else n


def kernel(x, attn_norm_g, w_in, rwkv_mu, rwkv_w0, rwkv_w_up, rwkv_a0, rwkv_a_up, rwkv_g_up,
           rwkv_k_k, rwkv_k_a, rwkv_r_k, rwkv_ln_w, rwkv_ln_b, fox_f_bias, fox_out_g, w_out,
           ffn_norm_g, router_w, router_b, expert_w1, expert_b1, expert_w2, expert_b2,
           final_norm_g):
    bsz, t, d = x.shape
    n = bsz * t
    depth = w_in.shape[0]
    bd = _block_diag_ones()
    x2 = x.reshape(n, d)
    for l in range(depth):
        w_l = w_in[l]
        w_r = w_l[:, :RWKV_IN].astype(BF16)
        w_qkv = w_l[:, RWKV_IN:RWKV_IN + 3 * GROUP_WIDTH].astype(BF16)
        w_f = jnp.pad(w_l[:, RWKV_IN + 3 * GROUP_WIDTH:], ((0, 0), (0, LANES - N_HEADS))).astype(BF16)
        fb_pad = jnp.pad(fox_f_bias[l], (0, LANES - N_HEADS)).reshape(1, LANES)
        wup_pad = jnp.pad(rwkv_w_up[l], ((0, LANES - DECAY_LORA), (0, 0))).astype(BF16)
        aup_pad = jnp.pad(rwkv_a_up[l], ((DECAY_LORA, 0), (0, 0))).astype(BF16)
        gup = rwkv_g_up[l].astype(BF16)
        vec = lambda a: a.reshape(1, -1)

        u_r, q, k, v, fl = _inproj(x2, vec(attn_norm_g[l]), w_r, w_qkv, w_f, _pick(n, 512))
        c = _fox_gate(fl.reshape(bsz, t, LANES), fb_pad, _pick(t, 512))
        tq = _pick(t, 512)
        tk = tq
        c5 = c.reshape(bsz, N_HEADS // 2, 2, t // tk, tk).transpose(0, 1, 3, 2, 4)
        r_, k_, v_, lw, kk, b_, g_ = _rwkv_prep(
            u_r.reshape(bsz, t, RWKV_IN), vec(rwkv_mu[l]), vec(rwkv_w0[l]), wup_pad,
            vec(rwkv_a0[l]), aup_pad, gup, vec(rwkv_k_k[l]), vec(rwkv_k_a[l]), bd, _pick(t, 256))
        y_rwkv = _rwkv_scan(r_, k_, v_, lw, kk, b_, g_, vec(rwkv_r_k[l]), vec(rwkv_ln_w[l]),
                            vec(rwkv_ln_b[l]), bd)
        y_fox = _fox_attn(q.reshape(bsz, t, GROUP_WIDTH), k.reshape(bsz, t, GROUP_WIDTH),
                          v.reshape(bsz, t, GROUP_WIDTH), c5, vec(fox_out_g[l]), tq, tk)

        wo = w_out[l].astype(BF16)
        x1, h2, idx, gates = _outproj_router(
            x2, y_rwkv.reshape(n, GROUP_WIDTH), y_fox.reshape(n, GROUP_WIDTH),
            wo[:GROUP_WIDTH], wo[GROUP_WIDTH:], vec(ffn_norm_g[l]),
            router_w[l].T, router_b[l].reshape(N_EXPERTS, 1), _pick(n, 256))

        bm = 256
        pos, slot_tok3, block_e = _dispatch_plan(idx, bm)
        w1 = expert_w1[l]
        w1g = w1[:, :, 0::2].astype(BF16)
        w1l = w1[:, :, 1::2].astype(BF16)
        b1 = expert_b1[l]
        b1g = b1[:, None, 0::2]
        b1l = b1[:, None, 1::2]
        outs = _expert_mlp(block_e, slot_tok3, h2, w1g, w1l, b1g, b1l,
                           expert_w2[l].astype(BF16), expert_b2[l][:, None, :], bm)
        last = l == depth - 1
        gn = final_norm_g if last else jnp.ones((d,), F32)
        x2 = _combine(pos, gates.T, x1, vec(gn), outs, _pick(n, 128))
        if not last:
            raise NotImplementedError("multi-layer stacks need an un-normalised combine output")
    return x2.reshape(bsz, t, d)
```

```python
import functools

import jax
import jax.numpy as jnp
from jax import lax
from jax.experimental import pallas as pl
from jax.experimental.pallas import tpu as pltpu

F32 = jnp.float32
BF16 = jnp.bfloat16

HEAD_DIM = 64
N_HEADS = 8
GROUP_WIDTH = N_HEADS * HEAD_DIM
DECAY_LORA = 64
AAA_LORA = 64
GATE_LORA = 128
RWKV_IN = 3 * GROUP_WIDTH + DECAY_LORA + AAA_LORA + GATE_LORA
LORA_OFF = 3 * GROUP_WIDTH
N_EXPERTS = 32
TOP_K = 4
SWIGLU_ALPHA = 1.702
SWIGLU_LIMIT = 7.0
RMS_EPS = 1e-5
RWKV_GN_EPS = 64e-5
LANES = 128
RWKV_CHUNK = 64
ATTN_TILE = 512
MXU_WIDTH = 256
LOG2E = 1.4426950408889634
Q_SCALE = HEAD_DIM ** -0.5 * LOG2E
VMEM_LIMIT = 56 * 1024 * 1024


def _cparams(semantics):
    return pltpu.CompilerParams(dimension_semantics=semantics, vmem_limit_bytes=VMEM_LIMIT)


def _dot(a, b):
    return jnp.dot(a, b, preferred_element_type=F32)


def _dot_nt(a, b):
    return lax.dot_general(a, b, (((1,), (1,)), ((), ())), preferred_element_type=F32)


def _dot_tn(a, b):
    return lax.dot_general(a, b, (((0,), (0,)), ((), ())), preferred_element_type=F32)


def _split3(x):
    hi = x.astype(BF16)
    r1 = x - hi.astype(F32)
    mid = r1.astype(BF16)
    lo = (r1 - mid.astype(F32)).astype(BF16)
    return hi, mid, lo


def _dot_exact_lhs(a_bf16, x):
    hi, mid, lo = _split3(x)
    return _dot(a_bf16, hi) + _dot(a_bf16, mid) + _dot(a_bf16, lo)


def _dot_exact_rhs(x, b_bf16):
    hi, mid, lo = _split3(x)
    return _dot(hi, b_bf16) + _dot(mid, b_bf16) + _dot(lo, b_bf16)


def _softplus(z):
    return jnp.maximum(z, 0.0) + jnp.log1p(jnp.exp(-jnp.abs(z)))


def _sigmoid(z):
    return 1.0 / (1.0 + jnp.exp(-z))


def _inproj_kernel(x_ref, g_ref, wr_ref, wqt_ref, wk_ref, wvt_ref, wf_ref,
                   ur_ref, qt_ref, k_ref, vt_ref, fl_ref):
    x = x_ref[...]
    h = x * lax.rsqrt(jnp.mean(x * x, axis=-1, keepdims=True) + RMS_EPS) * g_ref[...]
    hb = h.astype(BF16)
    ur_ref[...] = _dot(hb, wr_ref[...])
    k_ref[...] = _dot(hb, wk_ref[...]).astype(BF16)
    fl_ref[...] = _dot(hb, wf_ref[...])
    qt = (_dot_nt(wqt_ref[...], hb) * Q_SCALE).astype(BF16)
    vt = _dot_nt(wvt_ref[...], hb).astype(BF16)
    for p in range(N_HEADS // 2):
        qt_ref[0, p, 0] = qt[p * LANES:(p + 1) * LANES]
        vt_ref[0, p, 0] = vt[p * LANES:(p + 1) * LANES]


def _inproj(x2, g, w_r, w_qt, w_k, w_vt, w_f, bsz, tm):
    n, d = x2.shape
    nt = n // bsz // tm
    pairs = N_HEADS // 2
    const = lambda i: (0, 0)
    row = lambda i: (i, 0)
    fm = lambda i: (i // nt, 0, i % nt, 0, 0)
    fm_sds = jax.ShapeDtypeStruct((bsz, pairs, nt, LANES, tm), BF16)
    return pl.pallas_call(
        _inproj_kernel,
        grid=(n // tm,),
        in_specs=[
            pl.BlockSpec((tm, d), row),
            pl.BlockSpec((1, d), const),
            pl.BlockSpec(w_r.shape, const),
            pl.BlockSpec(w_qt.shape, const),
            pl.BlockSpec(w_k.shape, const),
            pl.BlockSpec(w_vt.shape, const),
            pl.BlockSpec(w_f.shape, const),
        ],
        out_specs=[
            pl.BlockSpec((tm, RWKV_IN), row),
            pl.BlockSpec((1, pairs, 1, LANES, tm), fm),
            pl.BlockSpec((tm, GROUP_WIDTH), row),
            pl.BlockSpec((1, pairs, 1, LANES, tm), fm),
            pl.BlockSpec((tm, LANES), row),
        ],
        out_shape=[
            jax.ShapeDtypeStruct((n, RWKV_IN), F32),
            fm_sds,
            jax.ShapeDtypeStruct((n, GROUP_WIDTH), BF16),
            fm_sds,
            jax.ShapeDtypeStruct((n, LANES), F32),
        ],
        compiler_params=_cparams(("parallel",)),
        name="inproj",
    )(x2, g, w_r, w_qt, w_k, w_vt, w_f)


def _fox_gate_kernel(fl_ref, fb_ref, sel_ref, c_ref, carry):
    tt = fl_ref.shape[1]

    @pl.when(pl.program_id(1) == 0)
    def _():
        carry[...] = jnp.zeros_like(carry)

    z = fl_ref[0] + fb_ref[...]
    log_f = jnp.minimum(z, 0.0) - jnp.log1p(jnp.exp(-jnp.abs(z)))
    ri = lax.broadcasted_iota(jnp.int32, (tt, tt), 0)
    ci = lax.broadcasted_iota(jnp.int32, (tt, tt), 1)
    tri = jnp.where(ri >= ci, 1.0, 0.0).astype(BF16)
    c = _dot_exact_lhs(tri, log_f) + carry[...]
    carry[...] = c[tt - 1:tt, :]
    hi, mid, lo = _split3(c * LOG2E)
    c_ref[0] = (_dot(hi, sel_ref[0]) + _dot(mid, sel_ref[1]) + _dot(lo, sel_ref[2])).astype(BF16)


def _gate_piece_selectors():
    h = jnp.arange(LANES, dtype=jnp.int32)[:, None]
    col = jnp.arange(GROUP_WIDTH, dtype=jnp.int32)[None, :]
    sels = []
    for m in range(3):
        target = LANES * (h // 2) + 3 * (h % 2) + m
        sels.append(((col == target) & (h < N_HEADS)).astype(BF16))
    return jnp.stack(sels)


def _fox_gate(fl3, fb_pad, sel, tt):
    b, t, _ = fl3.shape
    return pl.pallas_call(
        _fox_gate_kernel,
        grid=(b, t // tt),
        in_specs=[
            pl.BlockSpec((1, tt, LANES), lambda i, j: (i, j, 0)),
            pl.BlockSpec((1, LANES), lambda i, j: (0, 0)),
            pl.BlockSpec((3, LANES, GROUP_WIDTH), lambda i, j: (0, 0, 0)),
        ],
        out_specs=pl.BlockSpec((1, tt, GROUP_WIDTH), lambda i, j: (i, j, 0)),
        out_shape=jax.ShapeDtypeStruct((b, t, GROUP_WIDTH), BF16),
        scratch_shapes=[pltpu.VMEM((1, LANES), F32)],
        compiler_params=_cparams(("parallel", "arbitrary")),
        name="fox_gate",
    )(fl3, fb_pad, sel)


def _rwkv_prep_kernel(u_ref, mu_ref, w0_ref, wup_ref, a0_ref, aup_ref, gup_ref, kk_ref, ka_ref,
                      bd_ref, r_out, k_out, v_out, lw_out, kk_out, b_out, g_out, carry):
    tt = u_ref.shape[1]

    @pl.when(pl.program_id(1) == 0)
    def _():
        carry[...] = jnp.zeros_like(carry)

    u = u_ref[0]
    prev = pltpu.roll(u, 1, axis=0)
    row = lax.broadcasted_iota(jnp.int32, u.shape, 0)
    prev = jnp.where(row == 0, carry[...], prev)
    carry[...] = u[tt - 1:tt, :]
    us = u + (prev - u) * mu_ref[...]

    r = us[:, :GROUP_WIDTH]
    k = us[:, GROUP_WIDTH:2 * GROUP_WIDTH]
    v = us[:, 2 * GROUP_WIDTH:LORA_OFF]
    wa = us[:, LORA_OFF:LORA_OFF + LANES]
    gl = us[:, LORA_OFF + LANES:]

    w_lin = _dot(jnp.tanh(wa).astype(BF16), wup_ref[...])
    a_lin = _dot(wa.astype(BF16), aup_ref[...])
    w = -_softplus(-(w0_ref[...] + w_lin)) - 0.5
    lw_out[0] = -jnp.exp(w)
    a = _sigmoid(a0_ref[...] + a_lin)
    g_out[0] = _dot(_sigmoid(gl).astype(BF16), gup_ref[...])

    kkr = k * kk_ref[...]
    ss = _dot_exact_rhs(kkr * kkr, bd_ref[...])
    kk = kkr / jnp.maximum(jnp.sqrt(ss), 1e-12)
    r_out[0] = r
    k_out[0] = k * (1.0 + (a - 1.0) * ka_ref[...])
    v_out[0] = v
    kk_out[0] = kk
    b_out[0] = kk * a


def _rwkv_prep(u3, mu, w0, wup_pad, a0, aup_pad, gup, k_k, k_a, bd, tt):
    b, t, _ = u3.shape
    const = lambda i, j: (0, 0)
    tile = lambda i, j: (i, j, 0)
    out_sds = jax.ShapeDtypeStruct((b, t, GROUP_WIDTH), F32)
    vec = pl.BlockSpec((1, GROUP_WIDTH), const)
    return pl.pallas_call(
        _rwkv_prep_kernel,
        grid=(b, t // tt),
        in_specs=[
            pl.BlockSpec((1, tt, RWKV_IN), tile),
            pl.BlockSpec((1, RWKV_IN), const),
            vec,
            pl.BlockSpec((LANES, GROUP_WIDTH), const),
            vec,
            pl.BlockSpec((LANES, GROUP_WIDTH), const),
            pl.BlockSpec((GATE_LORA, GROUP_WIDTH), const),
            vec,
            vec,
            pl.BlockSpec((GROUP_WIDTH, GROUP_WIDTH), const),
        ],
        out_specs=[pl.BlockSpec((1, tt, GROUP_WIDTH), tile)] * 7,
        out_shape=[out_sds] * 7,
        scratch_shapes=[pltpu.VMEM((1, RWKV_IN), F32)],
        compiler_params=_cparams(("parallel", "arbitrary")),
        name="rwkv_prep",
    )(u3, mu, w0, wup_pad, a0, aup_pad, gup, k_k, k_a, bd)


def _unit_lower_inverses(l_stricts, level_masks, eye):
    ts = [eye - jnp.where(level_masks[0], l, 0.0) for l in l_stricts]
    for m in level_masks[1:]:
        tbs = [t.astype(BF16) for t in ts]
        cts = [_dot(jnp.where(m, l, 0.0).astype(BF16), tb) for l, tb in zip(l_stricts, tbs)]
        ts = [t - _dot(tb, ct.astype(BF16)) for t, tb, ct in zip(ts, tbs, cts)]
    return ts


def _rwkv_scan_kernel(r_ref, k_ref, v_ref, lw_ref, kk_ref, b_ref, g_ref, rk_ref, lnw_ref, lnb_ref,
                      bd_ref, o_ref, s_scr):
    c = RWKV_CHUNK

    @pl.when(pl.program_id(1) == 0)
    def _():
        s_scr[...] = jnp.zeros_like(s_scr)

    r = r_ref[0]
    k = k_ref[0]
    v = v_ref[0]
    lw = lw_ref[0]
    kk = kk_ref[0]
    b = b_ref[0]

    ri = lax.broadcasted_iota(jnp.int32, (c, c), 0)
    ci = lax.broadcasted_iota(jnp.int32, (c, c), 1)
    incl = ri >= ci
    strict = ri > ci
    eye = jnp.where(ri == ci, 1.0, 0.0)
    level_masks = []
    s = 1
    while s < c:
        same = (ri // (2 * s)) == (ci // (2 * s))
        level_masks.append(same & ((ri % (2 * s)) >= s) & ((ci % (2 * s)) < s))
        s *= 2
    ri2 = lax.broadcasted_iota(jnp.int32, (c, 2 * c), 0)
    ci2 = lax.broadcasted_iota(jnp.int32, (c, 2 * c), 1) % c
    incl2 = ri2 >= ci2

    g_cum = _dot_exact_lhs(jnp.where(incl, 1.0, 0.0).astype(BF16), lw)
    g_last = g_cum[c - 1:c, :]
    r_t = (r * jnp.exp(g_cum)).astype(BF16)
    kk_t = (kk * jnp.exp(g_cum - lw)).astype(BF16)
    e_neg = jnp.exp(-g_cum)
    b_n = (b * e_neg).astype(BF16)
    k_n = (k * e_neg).astype(BF16)
    e_end = jnp.exp(g_last - g_cum)
    b_e = (b * e_end).astype(BF16)
    k_e = (k * e_end).astype(BF16)
    gamma = jnp.exp(g_last)
    vb = v.astype(BF16)

    heads = range(N_HEADS)
    sls = [slice(h * HEAD_DIM, (h + 1) * HEAD_DIM) for h in heads]
    s_old = [s_scr[h] for h in heads]
    lhs = [jnp.concatenate([r_t[:, sl], kk_t[:, sl]], axis=0) for sl in sls]
    rhs = [jnp.concatenate([b_n[:, sl], k_n[:, sl]], axis=0) for sl in sls]
    vh = [vb[:, sl] for sl in sls]
    p = [_dot_nt(lhs[h], rhs[h]) for h in heads]
    q0 = [_dot_nt(lhs[h], s_old[h].astype(BF16)) for h in heads]
    l_b = [jnp.where(strict, p[h][c:, :c], 0.0) for h in heads]
    l_k = [jnp.where(strict, p[h][c:, c:], 0.0).astype(BF16) for h in heads]
    z = [q0[h][c:] + _dot(l_k[h], vh[h]) for h in heads]
    t_inv = _unit_lower_inverses(l_b, level_masks, eye)
    u = [-_dot(t_inv[h].astype(BF16), z[h].astype(BF16)) for h in heads]
    w_cat = [jnp.concatenate([u[h].astype(BF16), vh[h]], axis=0) for h in heads]
    p_r = [jnp.where(incl2, p[h][:c], 0.0).astype(BF16) for h in heads]
    ys = [q0[h][:c] + _dot(p_r[h], w_cat[h]) for h in heads]
    x_cat = [jnp.concatenate([b_e[:, sl], k_e[:, sl]], axis=0) for sl in sls]
    s_new = [s_old[h] * gamma[:, sls[h]] + _dot_tn(w_cat[h], x_cat[h]) for h in heads]
    for h in heads:
        s_scr[h] = s_new[h]

    y = jnp.concatenate(ys, axis=1)
    bd = bd_ref[...]
    inv_n = 1.0 / HEAD_DIM
    mean = _dot_exact_rhs(y, bd) * inv_n
    d = y - mean
    var = _dot_exact_rhs(d * d, bd) * inv_n
    yn = d * lax.rsqrt(var + RWKV_GN_EPS) * lnw_ref[...] + lnb_ref[...]
    bonus = _dot_exact_rhs(r * k * rk_ref[...], bd) * v
    o_ref[0] = ((yn + bonus) * g_ref[0]).astype(o_ref.dtype)


def _rwkv_scan(r, k, v, lw, kk, b, g, r_k, ln_w, ln_b, bd):
    bsz, t, _ = r.shape
    c = RWKV_CHUNK
    const = lambda i, j: (0, 0)
    tile = pl.BlockSpec((1, c, GROUP_WIDTH), lambda i, j: (i, j, 0))
    vec = pl.BlockSpec((1, GROUP_WIDTH), const)
    return pl.pallas_call(
        _rwkv_scan_kernel,
        grid=(bsz, t // c),
        in_specs=[tile] * 7 + [vec, vec, vec, pl.BlockSpec((GROUP_WIDTH, GROUP_WIDTH), const)],
        out_specs=tile,
        out_shape=jax.ShapeDtypeStruct((bsz, t, GROUP_WIDTH), BF16),
        scratch_shapes=[pltpu.VMEM((N_HEADS, HEAD_DIM, HEAD_DIM), F32)],
        compiler_params=_cparams(("parallel", "arbitrary")),
        name="rwkv_scan",
    )(r, k, v, lw, kk, b, g, r_k, ln_w, ln_b, bd)


def _fox_attn_kernel(qt_ref, k_ref, ce_ref, vt_ref, og_ref, o_ref, m_scr, l_scr, acc_scr, *, t):
    qi = pl.program_id(2)
    n_strips = 2 * t // LANES
    qt = qt_ref[0, 0, 0]
    frow = lax.broadcasted_iota(jnp.int32, (LANES, t), 0)
    zero = jnp.zeros_like(qt)
    main = jnp.concatenate([jnp.where(frow < HEAD_DIM, qt, zero),
                            jnp.where(frow < HEAD_DIM, zero, qt)], axis=1)
    erow = lax.broadcasted_iota(jnp.int32, (LANES, 2 * t), 0)
    ecol = lax.broadcasted_iota(jnp.int32, (LANES, 2 * t), 1)
    off = jnp.where(ecol < t, 0, 3)
    extra = jnp.where((erow >= off) & (erow < off + 3), -1.0, 0.0).astype(BF16)
    q_aug = jnp.concatenate([main, extra], axis=0)

    m_scr[...] = jnp.full_like(m_scr, -jnp.inf)
    l_scr[...] = jnp.zeros_like(l_scr)
    acc_scr[...] = jnp.zeros_like(acc_scr)

    def step(j, masked):
        start = pl.multiple_of(j * t, t)
        k_aug = jnp.concatenate([k_ref[0, pl.ds(start, t), :], ce_ref[0, pl.ds(start, t), :]],
                                axis=1)
        vt = vt_ref[0, 0, j]
        zt = _dot(k_aug, q_aug)
        m_prev = m_scr[...]
        l_prev = l_scr[...]
        acc_prev = acc_scr[...]
        m_out, l_out, acc_out = [], [], [[], []]
        for s in range(n_strips):
            head = s // (n_strips // 2)
            cs = slice(s * LANES, (s + 1) * LANES)
            z = zt[:, cs]
            if masked:
                key = lax.broadcasted_iota(jnp.int32, (t, LANES), 0)
                qry = lax.broadcasted_iota(jnp.int32, (t, LANES), 1) + (s * LANES) % t
                z = jnp.where(key <= qry, z, -jnp.inf)
            m_new = jnp.maximum(m_prev[:, cs], jnp.max(z, axis=0, keepdims=True))
            alpha = jnp.exp2(m_prev[:, cs] - m_new)
            p = jnp.exp2(z - m_new)
            l_out.append(alpha * l_prev[:, cs] + jnp.sum(p, axis=0, keepdims=True))
            m_out.append(m_new)
            hs = slice(head * HEAD_DIM, (head + 1) * HEAD_DIM)
            qs = slice((s * LANES) % t, (s * LANES) % t + LANES)
            pv = _dot(vt[hs], p.astype(BF16))
            acc_out[head].append(alpha * acc_prev[hs, qs] + pv)
        m_scr[...] = jnp.concatenate(m_out, axis=1)
        l_scr[...] = jnp.concatenate(l_out, axis=1)
        acc_scr[...] = jnp.concatenate([jnp.concatenate(acc_out[0], axis=1),
                                        jnp.concatenate(acc_out[1], axis=1)], axis=0)

    def body(j, carry):
        step(j, False)
        return carry

    lax.fori_loop(0, qi, body, 0)
    step(qi, True)

    l = l_scr[...]
    acc = acc_scr[...]
    inv_n = 1.0 / HEAD_DIM
    halves = []
    for head in range(2):
        o = acc[head * HEAD_DIM:(head + 1) * HEAD_DIM] / l[:, head * t:(head + 1) * t]
        halves.append(o * lax.rsqrt(jnp.sum(o * o, axis=0, keepdims=True) * inv_n + RMS_EPS))
    o_t = jnp.concatenate(halves, axis=0)
    o_ref[0] = (jnp.transpose(o_t) * og_ref[...]).astype(o_ref.dtype)


def _fox_attn(qt5, k, cext, vt5, out_g, tile):
    bsz, t_all, _ = k.shape
    pairs = N_HEADS // 2
    kern = functools.partial(_fox_attn_kernel, t=tile)
    return pl.pallas_call(
        kern,
        grid=(bsz, pairs, t_all // tile),
        in_specs=[
            pl.BlockSpec((1, 1, 1, LANES, tile), lambda b, p, i: (b, p, i, 0, 0)),
            pl.BlockSpec((1, t_all, LANES), lambda b, p, i: (b, 0, p)),
            pl.BlockSpec((1, t_all, LANES), lambda b, p, i: (b, 0, p)),
            pl.BlockSpec((1, 1, t_all // tile, LANES, tile), lambda b, p, i: (b, p, 0, 0, 0)),
            pl.BlockSpec((1, LANES), lambda b, p, i: (0, p)),
        ],
        out_specs=pl.BlockSpec((1, tile, LANES), lambda b, p, i: (b, i, p)),
        out_shape=jax.ShapeDtypeStruct((bsz, t_all, GROUP_WIDTH), BF16),
        scratch_shapes=[
            pltpu.VMEM((1, 2 * tile), F32),
            pltpu.VMEM((1, 2 * tile), F32),
            pltpu.VMEM((LANES, tile), F32),
        ],
        compiler_params=_cparams(("parallel", "parallel", "arbitrary")),
        name="fox_attn",
    )(qt5, k, cext, vt5, out_g)


def _outproj_router_kernel(x_ref, yr_ref, yf_ref, wo_r_ref, wo_f_ref, g_ref, rwt_ref, rb_ref,
                           x1_ref, h_ref, idx_ref, gate_ref):
    x1 = x_ref[...] + _dot(yr_ref[...], wo_r_ref[...]) + _dot(yf_ref[...], wo_f_ref[...])
    x1_ref[...] = x1
    h = x1 * lax.rsqrt(jnp.mean(x1 * x1, axis=-1, keepdims=True) + RMS_EPS) * g_ref[...]
    h_ref[...] = h
    logits = lax.dot_general(rwt_ref[...], h, (((1,), (1,)), ((), ())),
                             precision=lax.Precision.HIGHEST,
                             preferred_element_type=F32) + rb_ref[...]
    eidx = lax.broadcasted_iota(jnp.int32, logits.shape, 0)
    vals, idxs = [], []
    for _ in range(TOP_K):
        m = jnp.max(logits, axis=0, keepdims=True)
        i = jnp.min(jnp.where(logits == m, eidx, N_EXPERTS), axis=0, keepdims=True)
        vals.append(m)
        idxs.append(i)
        logits = jnp.where(eidx == i, -jnp.inf, logits)
    es = [jnp.exp(val - vals[0]) for val in vals]
    denom = es[0] + es[1] + es[2] + es[3]
    idx_ref[...] = jnp.concatenate(idxs, axis=0)
    gate_ref[...] = jnp.concatenate([e / denom for e in es], axis=0)


def _outproj_router(x2, yr, yf, wo_r, wo_f, g, rwt, rb, tm):
    n, d = x2.shape
    const = lambda i: (0, 0)
    row = lambda i: (i, 0)
    col = lambda i: (0, i)
    return pl.pallas_call(
        _outproj_router_kernel,
        grid=(n // tm,),
        in_specs=[
            pl.BlockSpec((tm, d), row),
            pl.BlockSpec((tm, GROUP_WIDTH), row),
            pl.BlockSpec((tm, GROUP_WIDTH), row),
            pl.BlockSpec((GROUP_WIDTH, d), const),
            pl.BlockSpec((GROUP_WIDTH, d), const),
            pl.BlockSpec((1, d), const),
            pl.BlockSpec((N_EXPERTS, d), const),
            pl.BlockSpec((N_EXPERTS, 1), const),
        ],
        out_specs=[
            pl.BlockSpec((tm, d), row),
            pl.BlockSpec((tm, d), row),
            pl.BlockSpec((TOP_K, tm), col),
            pl.BlockSpec((TOP_K, tm), col),
        ],
        out_shape=[
            jax.ShapeDtypeStruct((n, d), F32),
            jax.ShapeDtypeStruct((n, d), F32),
            jax.ShapeDtypeStruct((TOP_K, n), jnp.int32),
            jax.ShapeDtypeStruct((TOP_K, n), F32),
        ],
        compiler_params=_cparams(("parallel",)),
        name="outproj_router",
    )(x2, yr, yf, wo_r, wo_f, g, rwt, rb)


def _w1_split_kernel(w_ref, perm_ref, g_ref, l_ref):
    half = MXU_WIDTH // 2
    perm = perm_ref[...]
    for grp in range(w_ref.shape[2] // MXU_WIDTH):
        blk = w_ref[0, :, grp * MXU_WIDTH:(grp + 1) * MXU_WIDTH].astype(BF16)
        r = _dot(blk, perm)
        g_ref[0, :, grp * half:(grp + 1) * half] = r[:, :half].astype(BF16)
        l_ref[0, :, grp * half:(grp + 1) * half] = r[:, half:].astype(BF16)


def _w1_split(w1, tr):
    e, d, two_f = w1.shape
    half = MXU_WIDTH // 2
    src = jnp.arange(MXU_WIDTH, dtype=jnp.int32)[:, None]
    dst = jnp.arange(MXU_WIDTH, dtype=jnp.int32)[None, :]
    perm = (src == jnp.where(dst < half, 2 * dst, 2 * (dst - half) + 1)).astype(BF16)
    out_sds = jax.ShapeDtypeStruct((e, d, two_f // 2), BF16)
    return pl.pallas_call(
        _w1_split_kernel,
        grid=(e, d // tr),
        in_specs=[
            pl.BlockSpec((1, tr, two_f), lambda i, j: (i, j, 0)),
            pl.BlockSpec((MXU_WIDTH, MXU_WIDTH), lambda i, j: (0, 0)),
        ],
        out_specs=[pl.BlockSpec((1, tr, two_f // 2), lambda i, j: (i, j, 0))] * 2,
        out_shape=[out_sds, out_sds],
        compiler_params=_cparams(("parallel", "parallel")),
        name="w1_split",
    )(w1, perm)


def _expert_kernel(be_ref, tok_ref, h_hbm, w1g_ref, w1l_ref, b1g_ref, b1l_ref, w2_ref, b2_ref,
                   o_ref, xbuf, sem, *, bm):
    del be_ref

    def issue(r, carry):
        tok = tok_ref[0, 0, r]
        pltpu.make_async_copy(h_hbm.at[pl.ds(tok, 1), :], xbuf.at[pl.ds(r, 1), :], sem).start()
        return carry

    lax.fori_loop(0, bm, issue, 0)
    pltpu.make_async_copy(h_hbm.at[pl.ds(0, bm), :], xbuf, sem).wait()

    xb = xbuf[...].astype(BF16)
    glu = _dot(xb, w1g_ref[0]) + b1g_ref[0]
    lin = _dot(xb, w1l_ref[0]) + b1l_ref[0]
    glu = jnp.minimum(glu, SWIGLU_LIMIT)
    lin = jnp.clip(lin, -SWIGLU_LIMIT, SWIGLU_LIMIT)
    act = glu * _sigmoid(SWIGLU_ALPHA * glu) * (lin + 1.0)
    o_ref[...] = _dot(act.astype(BF16), w2_ref[0]) + b2_ref[0]


def _expert_mlp(block_e, slot_tok3, h2, w1g, w1l, b1g, b1l, w2, b2, bm):
    n_blocks = slot_tok3.shape[0]
    d = h2.shape[1]
    dff = w1g.shape[2]
    wmap = lambda i, be: (be[i], 0, 0)
    grid_spec = pltpu.PrefetchScalarGridSpec(
        num_scalar_prefetch=1,
        grid=(n_blocks,),
        in_specs=[
            pl.BlockSpec((1, 1, bm), lambda i, be: (i, 0, 0), memory_space=pltpu.SMEM),
            pl.BlockSpec(memory_space=pl.ANY),
            pl.BlockSpec((1, d, dff), wmap),
            pl.BlockSpec((1, d, dff), wmap),
            pl.BlockSpec((1, 1, dff), wmap),
            pl.BlockSpec((1, 1, dff), wmap),
            pl.BlockSpec((1, dff, d), wmap),
            pl.BlockSpec((1, 1, d), wmap),
        ],
        out_specs=pl.BlockSpec((bm, d), lambda i, be: (i, 0)),
        scratch_shapes=[pltpu.VMEM((bm, d), F32), pltpu.SemaphoreType.DMA(())],
    )
    return pl.pallas_call(
        functools.partial(_expert_kernel, bm=bm),
        grid_spec=grid_spec,
        out_shape=jax.ShapeDtypeStruct((n_blocks * bm, d), F32),
        compiler_params=_cparams(("arbitrary",)),
        name="expert_mlp",
    )(block_e, slot_tok3, h2, w1g, w1l, b1g, b1l, w2, b2)


def _combine_kernel(pos_ref, gate_ref, x1_ref, g_ref, outs_hbm, o_ref, buf, sem, *, tc):
    for kk in range(TOP_K):
        def issue(r, carry, kk=kk):
            p = pos_ref[kk, r]
            pltpu.make_async_copy(outs_hbm.at[pl.ds(p, 1), :], buf.at[kk, pl.ds(r, 1), :],
                                  sem).start()
            return carry

        lax.fori_loop(0, tc, issue, 0)
    for kk in range(TOP_K):
        pltpu.make_async_copy(outs_hbm.at[pl.ds(0, tc), :], buf.at[kk], sem).wait()

    gates = gate_ref[...]
    y = x1_ref[...]
    for kk in range(TOP_K):
        y = y + buf[kk] * gates[:, kk:kk + 1]
    o_ref[...] = y * lax.rsqrt(jnp.mean(y * y, axis=-1, keepdims=True) + RMS_EPS) * g_ref[...]


def _combine(pos, gates_t, x1, g, outs, tc):
    n, d = x1.shape
    return pl.pallas_call(
        functools.partial(_combine_kernel, tc=tc),
        grid=(n // tc,),
        in_specs=[
            pl.BlockSpec((TOP_K, tc), lambda i: (0, i), memory_space=pltpu.SMEM),
            pl.BlockSpec((tc, TOP_K), lambda i: (i, 0)),
            pl.BlockSpec((tc, d), lambda i: (i, 0)),
            pl.BlockSpec((1, d), lambda i: (0, 0)),
            pl.BlockSpec(memory_space=pl.ANY),
        ],
        out_specs=pl.BlockSpec((tc, d), lambda i: (i, 0)),
        out_shape=jax.ShapeDtypeStruct((n, d), F32),
        scratch_shapes=[pltpu.VMEM((TOP_K, tc, d), F32), pltpu.SemaphoreType.DMA(())],
        compiler_params=_cparams(("arbitrary",)),
        name="combine",
    )(pos, gates_t, x1, g, outs)


def _dispatch_plan(idx, bm):
    n = idx.shape[1]
    n_slots = TOP_K * n
    e_flat = idx.reshape(-1)
    onehot = (e_flat[:, None] == jnp.arange(N_EXPERTS, dtype=jnp.int32)[None, :]).astype(jnp.int32)
    csum = jnp.cumsum(onehot, axis=0)
    rank = jnp.sum(csum * onehot, axis=1) - 1
    sizes = csum[-1]
    padded = (sizes + bm - 1) // bm * bm
    pad_ends = jnp.cumsum(padded)
    pad_starts = pad_ends - padded
    pos = (pad_starts[e_flat] + rank).astype(jnp.int32)
    n_pad = n_slots + N_EXPERTS * bm
    n_blocks = n_pad // bm
    tok = jnp.tile(jnp.arange(n, dtype=jnp.int32), TOP_K)
    slot_tok = jnp.zeros((n_pad,), jnp.int32).at[pos].set(tok, unique_indices=True)
    block_start = jnp.arange(n_blocks, dtype=jnp.int32) * bm
    block_e = jnp.minimum(jnp.searchsorted(pad_ends, block_start, side='right'),
                          N_EXPERTS - 1).astype(jnp.int32)
    return pos.reshape(TOP_K, n), slot_tok.reshape(n_blocks, 1, bm), block_e


def _block_diag_ones():
    hid = jnp.arange(GROUP_WIDTH, dtype=jnp.int32) // HEAD_DIM
    return (hid[:, None] == hid[None, :]).astype(BF16)


def _pick(n, pref):
    return pref if n % pref == 0 else n


def kernel(x, attn_norm_g, w_in, rwkv_mu, rwkv_w0, rwkv_w_up, rwkv_a0, rwkv_a_up, rwkv_g_up,
           rwkv_k_k, rwkv_k_a, rwkv_r_k, rwkv_ln_w, rwkv_ln_b, fox_f_bias, fox_out_g, w_out,
           ffn_norm_g, router_w, router_b, expert_w1, expert_b1, expert_w2, expert_b2,
           final_norm_g):
    bsz, t, d = x.shape
    n = bsz * t
    depth = w_in.shape[0]
    bd = _block_diag_ones()
    x2 = x.reshape(n, d)
    for l in range(depth):
        w_l = w_in[l]
        w_r = w_l[:, :RWKV_IN].astype(BF16)
        w_qkv = w_l[:, RWKV_IN:RWKV_IN + 3 * GROUP_WIDTH].astype(BF16)
        w_qt = w_qkv[:, :GROUP_WIDTH].T
        w_k = w_qkv[:, GROUP_WIDTH:2 * GROUP_WIDTH]
        w_vt = w_qkv[:, 2 * GROUP_WIDTH:].T
        w_f = jnp.pad(w_l[:, RWKV_IN + 3 * GROUP_WIDTH:], ((0, 0), (0, LANES - N_HEADS))).astype(BF16)
        fb_pad = jnp.pad(fox_f_bias[l], (0, LANES - N_HEADS)).reshape(1, LANES)
        wup_pad = jnp.pad(rwkv_w_up[l], ((0, LANES - DECAY_LORA), (0, 0))).astype(BF16)
        aup_pad = jnp.pad(rwkv_a_up[l], ((DECAY_LORA, 0), (0, 0))).astype(BF16)
        gup = rwkv_g_up[l].astype(BF16)
        vec = lambda a: a.reshape(1, -1)

        u_r, qt5, k, vt5, fl = _inproj(x2, vec(attn_norm_g[l]), w_r, w_qt, w_k, w_vt, w_f,
                                       bsz, ATTN_TILE)
        cext = _fox_gate(fl.reshape(bsz, t, LANES), fb_pad, _gate_piece_selectors(), ATTN_TILE)
        r_, k_, v_, lw, kk, b_, g_ = _rwkv_prep(
            u_r.reshape(bsz, t, RWKV_IN), vec(rwkv_mu[l]), vec(rwkv_w0[l]), wup_pad,
            vec(rwkv_a0[l]), aup_pad, gup, vec(rwkv_k_k[l]), vec(rwkv_k_a[l]), bd, _pick(t, 256))
        y_rwkv = _rwkv_scan(r_, k_, v_, lw, kk, b_, g_, vec(rwkv_r_k[l]), vec(rwkv_ln_w[l]),
                            vec(rwkv_ln_b[l]), bd)
        y_fox = _fox_attn(qt5, k.reshape(bsz, t, GROUP_WIDTH), cext, vt5, vec(fox_out_g[l]),
                          ATTN_TILE)

        wo = w_out[l].astype(BF16)
        x1, h2, idx, gates = _outproj_router(
            x2, y_rwkv.reshape(n, GROUP_WIDTH), y_fox.reshape(n, GROUP_WIDTH),
            wo[:GROUP_WIDTH], wo[GROUP_WIDTH:], vec(ffn_norm_g[l]),
            router_w[l].T, router_b[l].reshape(N_EXPERTS, 1), _pick(n, 256))

        bm = 256
        pos, slot_tok3, block_e = _dispatch_plan(idx, bm)
        w1g, w1l = _w1_split(expert_w1[l], 512)
        b1 = expert_b1[l]
        b1g = b1[:, None, 0::2]
        b1l = b1[:, None, 1::2]
        outs = _expert_mlp(block_e, slot_tok3, h2, w1g, w1l, b1g, b1l,
                           expert_w2[l].astype(BF16), expert_b2[l][:, None, :], bm)
        last = l == depth - 1
        gn = final_norm_g if last else jnp.ones((d,), F32)
        x2 = _combine(pos, gates.T, x1, vec(gn), outs, _pick(n, 128))
        if not last:
            raise NotImplementedError("multi-layer stacks need an un-normalised combine output")
    return x2.reshape(bsz, t, d)
```

```python
import functools

import jax
import jax.numpy as jnp
from jax import lax
from jax.experimental import pallas as pl
from jax.experimental.pallas import tpu as pltpu

F32 = jnp.float32
BF16 = jnp.bfloat16

HEAD_DIM = 64
N_HEADS = 8
GROUP_WIDTH = N_HEADS * HEAD_DIM
DECAY_LORA = 64
AAA_LORA = 64
GATE_LORA = 128
RWKV_IN = 3 * GROUP_WIDTH + DECAY_LORA + AAA_LORA + GATE_LORA
LORA_OFF = 3 * GROUP_WIDTH
N_EXPERTS = 32
TOP_K = 4
SWIGLU_ALPHA = 1.702
SWIGLU_LIMIT = 7.0
RMS_EPS = 1e-5
RWKV_GN_EPS = 64e-5
LANES = 128
RWKV_CHUNK = 64
ATTN_TILE = 512
MXU_WIDTH = 256
LOG2E = 1.4426950408889634
Q_SCALE = HEAD_DIM ** -0.5 * LOG2E
VMEM_LIMIT = 56 * 1024 * 1024


def _cparams(semantics):
    return pltpu.CompilerParams(dimension_semantics=semantics, vmem_limit_bytes=VMEM_LIMIT)


def _dot(a, b):
    return jnp.dot(a, b, preferred_element_type=F32)


def _dot_nt(a, b):
    return lax.dot_general(a, b, (((1,), (1,)), ((), ())), preferred_element_type=F32)


def _dot_tn(a, b):
    return lax.dot_general(a, b, (((0,), (0,)), ((), ())), preferred_element_type=F32)


def _split3(x):
    hi = x.astype(BF16)
    r1 = x - hi.astype(F32)
    mid = r1.astype(BF16)
    lo = (r1 - mid.astype(F32)).astype(BF16)
    return hi, mid, lo


def _dot_exact_lhs(a_bf16, x):
    hi, mid, lo = _split3(x)
    return _dot(a_bf16, hi) + _dot(a_bf16, mid) + _dot(a_bf16, lo)


def _dot_exact_rhs(x, b_bf16):
    hi, mid, lo = _split3(x)
    return _dot(hi, b_bf16) + _dot(mid, b_bf16) + _dot(lo, b_bf16)


def _softplus(z):
    return jnp.maximum(z, 0.0) + jnp.log1p(jnp.exp(-jnp.abs(z)))


def _sigmoid(z):
    return 1.0 / (1.0 + jnp.exp(-z))


def _inproj_kernel(x_ref, g_ref, wr_ref, wqt_ref, wk_ref, wvt_ref, wf_ref,
                   ur_ref, qt_ref, k_ref, vt_ref, fl_ref):
    x = x_ref[...]
    h = x * lax.rsqrt(jnp.mean(x * x, axis=-1, keepdims=True) + RMS_EPS) * g_ref[...]
    hb = h.astype(BF16)
    ur_ref[...] = _dot(hb, wr_ref[...])
    k_ref[...] = _dot(hb, wk_ref[...]).astype(BF16)
    fl_ref[...] = _dot(hb, wf_ref[...])
    qt = (_dot_nt(wqt_ref[...], hb) * Q_SCALE).astype(BF16)
    vt = _dot_nt(wvt_ref[...], hb).astype(BF16)
    for p in range(N_HEADS // 2):
        qt_ref[0, p, 0] = qt[p * LANES:(p + 1) * LANES]
        vt_ref[0, p, 0] = vt[p * LANES:(p + 1) * LANES]


def _inproj(x2, g, w_r, w_qt, w_k, w_vt, w_f, bsz, tm):
    n, d = x2.shape
    nt = n // bsz // tm
    pairs = N_HEADS // 2
    const = lambda i: (0, 0)
    row = lambda i: (i, 0)
    fm = lambda i: (i // nt, 0, i % nt, 0, 0)
    fm_sds = jax.ShapeDtypeStruct((bsz, pairs, nt, LANES, tm), BF16)
    return pl.pallas_call(
        _inproj_kernel,
        grid=(n // tm,),
        in_specs=[
            pl.BlockSpec((tm, d), row),
            pl.BlockSpec((1, d), const),
            pl.BlockSpec(w_r.shape, const),
            pl.BlockSpec(w_qt.shape, const),
            pl.BlockSpec(w_k.shape, const),
            pl.BlockSpec(w_vt.shape, const),
            pl.BlockSpec(w_f.shape, const),
        ],
        out_specs=[
            pl.BlockSpec((tm, RWKV_IN), row),
            pl.BlockSpec((1, pairs, 1, LANES, tm), fm),
            pl.BlockSpec((tm, GROUP_WIDTH), row),
            pl.BlockSpec((1, pairs, 1, LANES, tm), fm),
            pl.BlockSpec((tm, LANES), row),
        ],
        out_shape=[
            jax.ShapeDtypeStruct((n, RWKV_IN), F32),
            fm_sds,
            jax.ShapeDtypeStruct((n, GROUP_WIDTH), BF16),
            fm_sds,
            jax.ShapeDtypeStruct((n, LANES), F32),
        ],
        compiler_params=_cparams(("parallel",)),
        name="inproj",
    )(x2, g, w_r, w_qt, w_k, w_vt, w_f)


def _fox_gate_kernel(fl_ref, fb_ref, sel_ref, c_ref, carry):
    tt = fl_ref.shape[1]

    @pl.when(pl.program_id(1) == 0)
    def _():
        carry[...] = jnp.zeros_like(carry)

    z = fl_ref[0] + fb_ref[...]
    log_f = jnp.minimum(z, 0.0) - jnp.log1p(jnp.exp(-jnp.abs(z)))
    ri = lax.broadcasted_iota(jnp.int32, (tt, tt), 0)
    ci = lax.broadcasted_iota(jnp.int32, (tt, tt), 1)
    tri = jnp.where(ri >= ci, 1.0, 0.0).astype(BF16)
    c = _dot_exact_lhs(tri, log_f) + carry[...]
    carry[...] = c[tt - 1:tt, :]
    hi, mid, lo = _split3(c * LOG2E)
    c_ref[0] = (_dot(hi, sel_ref[0]) + _dot(mid, sel_ref[1]) + _dot(lo, sel_ref[2])).astype(BF16)


def _gate_piece_selectors():
    h = jnp.arange(LANES, dtype=jnp.int32)[:, None]
    col = jnp.arange(GROUP_WIDTH, dtype=jnp.int32)[None, :]
    sels = []
    for m in range(3):
        target = LANES * (h // 2) + 3 * (h % 2) + m
        sels.append(((col == target) & (h < N_HEADS)).astype(BF16))
    return jnp.stack(sels)


def _fox_gate(fl3, fb_pad, sel, tt):
    b, t, _ = fl3.shape
    return pl.pallas_call(
        _fox_gate_kernel,
        grid=(b, t // tt),
        in_specs=[
            pl.BlockSpec((1, tt, LANES), lambda i, j: (i, j, 0)),
            pl.BlockSpec((1, LANES), lambda i, j: (0, 0)),
            pl.BlockSpec((3, LANES, GROUP_WIDTH), lambda i, j: (0, 0, 0)),
        ],
        out_specs=pl.BlockSpec((1, tt, GROUP_WIDTH), lambda i, j: (i, j, 0)),
        out_shape=jax.ShapeDtypeStruct((b, t, GROUP_WIDTH), BF16),
        scratch_shapes=[pltpu.VMEM((1, LANES), F32)],
        compiler_params=_cparams(("parallel", "arbitrary")),
        name="fox_gate",
    )(fl3, fb_pad, sel)


def _rwkv_prep_kernel(u_ref, mu_ref, w0_ref, wup_ref, a0_ref, aup_ref, gup_ref, kk_ref, ka_ref,
                      bd_ref, r_out, k_out, v_out, lw_out, kk_out, b_out, g_out, carry):
    tt = u_ref.shape[1]

    @pl.when(pl.program_id(1) == 0)
    def _():
        carry[...] = jnp.zeros_like(carry)

    u = u_ref[0]
    prev = pltpu.roll(u, 1, axis=0)
    row = lax.broadcasted_iota(jnp.int32, u.shape, 0)
    prev = jnp.where(row == 0, carry[...], prev)
    carry[...] = u[tt - 1:tt, :]
    us = u + (prev - u) * mu_ref[...]

    r = us[:, :GROUP_WIDTH]
    k = us[:, GROUP_WIDTH:2 * GROUP_WIDTH]
    v = us[:, 2 * GROUP_WIDTH:LORA_OFF]
    wa = us[:, LORA_OFF:LORA_OFF + LANES]
    gl = us[:, LORA_OFF + LANES:]

    w_lin = _dot(jnp.tanh(wa).astype(BF16), wup_ref[...])
    a_lin = _dot(wa.astype(BF16), aup_ref[...])
    w = -_softplus(-(w0_ref[...] + w_lin)) - 0.5
    lw_out[0] = -jnp.exp(w)
    a = _sigmoid(a0_ref[...] + a_lin)
    g_out[0] = _dot(_sigmoid(gl).astype(BF16), gup_ref[...])

    kkr = k * kk_ref[...]
    ss = _dot_exact_rhs(kkr * kkr, bd_ref[...])
    kk = kkr / jnp.maximum(jnp.sqrt(ss), 1e-12)
    r_out[0] = r
    k_out[0] = k * (1.0 + (a - 1.0) * ka_ref[...])
    v_out[0] = v
    kk_out[0] = kk
    b_out[0] = kk * a


def _rwkv_prep(u3, mu, w0, wup_pad, a0, aup_pad, gup, k_k, k_a, bd, tt):
    b, t, _ = u3.shape
    const = lambda i, j: (0, 0)
    tile = lambda i, j: (i, j, 0)
    out_sds = jax.ShapeDtypeStruct((b, t, GROUP_WIDTH), F32)
    vec = pl.BlockSpec((1, GROUP_WIDTH), const)
    return pl.pallas_call(
        _rwkv_prep_kernel,
        grid=(b, t // tt),
        in_specs=[
            pl.BlockSpec((1, tt, RWKV_IN), tile),
            pl.BlockSpec((1, RWKV_IN), const),
            vec,
            pl.BlockSpec((LANES, GROUP_WIDTH), const),
            vec,
            pl.BlockSpec((LANES, GROUP_WIDTH), const),
            pl.BlockSpec((GATE_LORA, GROUP_WIDTH), const),
            vec,
            vec,
            pl.BlockSpec((GROUP_WIDTH, GROUP_WIDTH), const),
        ],
        out_specs=[pl.BlockSpec((1, tt, GROUP_WIDTH), tile)] * 7,
        out_shape=[out_sds] * 7,
        scratch_shapes=[pltpu.VMEM((1, RWKV_IN), F32)],
        compiler_params=_cparams(("parallel", "arbitrary")),
        name="rwkv_prep",
    )(u3, mu, w0, wup_pad, a0, aup_pad, gup, k_k, k_a, bd)


def _unit_lower_inverses(l_stricts, level_masks, eye):
    ts = [eye - jnp.where(level_masks[0], l, 0.0) for l in l_stricts]
    for m in level_masks[1:]:
        tbs = [t.astype(BF16) for t in ts]
        cts = [_dot(jnp.where(m, l, 0.0).astype(BF16), tb) for l, tb in zip(l_stricts, tbs)]
        ts = [t - _dot(tb, ct.astype(BF16)) for t, tb, ct in zip(ts, tbs, cts)]
    return ts


def _rwkv_scan_kernel(r_ref, k_ref, v_ref, lw_ref, kk_ref, b_ref, g_ref, rk_ref, lnw_ref, lnb_ref,
                      bd_ref, o_ref, s_scr):
    c = RWKV_CHUNK

    @pl.when(pl.program_id(1) == 0)
    def _():
        s_scr[...] = jnp.zeros_like(s_scr)

    r = r_ref[0]
    k = k_ref[0]
    v = v_ref[0]
    lw = lw_ref[0]
    kk = kk_ref[0]
    b = b_ref[0]

    ri = lax.broadcasted_iota(jnp.int32, (c, c), 0)
    ci = lax.broadcasted_iota(jnp.int32, (c, c), 1)
    incl = ri >= ci
    strict = ri > ci
    eye = jnp.where(ri == ci, 1.0, 0.0)
    level_masks = []
    s = 1
    while s < c:
        same = (ri // (2 * s)) == (ci // (2 * s))
        level_masks.append(same & ((ri % (2 * s)) >= s) & ((ci % (2 * s)) < s))
        s *= 2
    ri2 = lax.broadcasted_iota(jnp.int32, (c, 2 * c), 0)
    ci2 = lax.broadcasted_iota(jnp.int32, (c, 2 * c), 1) % c
    incl2 = ri2 >= ci2

    g_cum = _dot_exact_lhs(jnp.where(incl, 1.0, 0.0).astype(BF16), lw)
    g_last = g_cum[c - 1:c, :]
    r_t = (r * jnp.exp(g_cum)).astype(BF16)
    kk_t = (kk * jnp.exp(g_cum - lw)).astype(BF16)
    e_neg = jnp.exp(-g_cum)
    b_n = (b * e_neg).astype(BF16)
    k_n = (k * e_neg).astype(BF16)
    e_end = jnp.exp(g_last - g_cum)
    b_e = (b * e_end).astype(BF16)
    k_e = (k * e_end).astype(BF16)
    gamma = jnp.exp(g_last)
    vb = v.astype(BF16)

    heads = range(N_HEADS)
    sls = [slice(h * HEAD_DIM, (h + 1) * HEAD_DIM) for h in heads]
    s_old = [s_scr[h] for h in heads]
    lhs = [jnp.concatenate([r_t[:, sl], kk_t[:, sl]], axis=0) for sl in sls]
    rhs = [jnp.concatenate([b_n[:, sl], k_n[:, sl]], axis=0) for sl in sls]
    vh = [vb[:, sl] for sl in sls]
    p = [_dot_nt(lhs[h], rhs[h]) for h in heads]
    q0 = [_dot_nt(lhs[h], s_old[h].astype(BF16)) for h in heads]
    l_b = [jnp.where(strict, p[h][c:, :c], 0.0) for h in heads]
    l_k = [jnp.where(strict, p[h][c:, c:], 0.0).astype(BF16) for h in heads]
    z = [q0[h][c:] + _dot(l_k[h], vh[h]) for h in heads]
    t_inv = _unit_lower_inverses(l_b, level_masks, eye)
    u = [-_dot(t_inv[h].astype(BF16), z[h].astype(BF16)) for h in heads]
    w_cat = [jnp.concatenate([u[h].astype(BF16), vh[h]], axis=0) for h in heads]
    p_r = [jnp.where(incl2, p[h][:c], 0.0).astype(BF16) for h in heads]
    ys = [q0[h][:c] + _dot(p_r[h], w_cat[h]) for h in heads]
    x_cat = [jnp.concatenate([b_e[:, sl], k_e[:, sl]], axis=0) for sl in sls]
    s_new = [s_old[h] * gamma[:, sls[h]] + _dot_tn(w_cat[h], x_cat[h]) for h in heads]
    for h in heads:
        s_scr[h] = s_new[h]

    y = jnp.concatenate(ys, axis=1)
    bd = bd_ref[...]
    inv_n = 1.0 / HEAD_DIM
    mean = _dot_exact_rhs(y, bd) * inv_n
    d = y - mean
    var = _dot_exact_rhs(d * d, bd) * inv_n
    yn = d * lax.rsqrt(var + RWKV_GN_EPS) * lnw_ref[...] + lnb_ref[...]
    bonus = _dot_exact_rhs(r * k * rk_ref[...], bd) * v
    o_ref[0] = ((yn + bonus) * g_ref[0]).astype(o_ref.dtype)


def _rwkv_scan(r, k, v, lw, kk, b, g, r_k, ln_w, ln_b, bd):
    bsz, t, _ = r.shape
    c = RWKV_CHUNK
    const = lambda i, j: (0, 0)
    tile = pl.BlockSpec((1, c, GROUP_WIDTH), lambda i, j: (i, j, 0))
    vec = pl.BlockSpec((1, GROUP_WIDTH), const)
    return pl.pallas_call(
        _rwkv_scan_kernel,
        grid=(bsz, t // c),
        in_specs=[tile] * 7 + [vec, vec, vec, pl.BlockSpec((GROUP_WIDTH, GROUP_WIDTH), const)],
        out_specs=tile,
        out_shape=jax.ShapeDtypeStruct((bsz, t, GROUP_WIDTH), BF16),
        scratch_shapes=[pltpu.VMEM((N_HEADS, HEAD_DIM, HEAD_DIM), F32)],
        compiler_params=_cparams(("parallel", "arbitrary")),
        name="rwkv_scan",
    )(r, k, v, lw, kk, b, g, r_k, ln_w, ln_b, bd)


def _fox_attn_kernel(qt_ref, k_ref, ce_ref, vt_ref, og_ref, o_ref, m_scr, l_scr, acc_scr, *, t):
    qi = pl.program_id(2)
    n_strips = 2 * t // LANES
    qt = qt_ref[0, 0, 0]
    frow = lax.broadcasted_iota(jnp.int32, (LANES, t), 0)
    zero = jnp.zeros_like(qt)
    main = jnp.concatenate([jnp.where(frow < HEAD_DIM, qt, zero),
                            jnp.where(frow < HEAD_DIM, zero, qt)], axis=1)
    erow = lax.broadcasted_iota(jnp.int32, (LANES, 2 * t), 0)
    ecol = lax.broadcasted_iota(jnp.int32, (LANES, 2 * t), 1)
    off = jnp.where(ecol < t, 0, 3)
    extra = jnp.where((erow >= off) & (erow < off + 3), -1.0, 0.0).astype(BF16)
    q_aug = jnp.concatenate([main, extra], axis=0)

    m_scr[...] = jnp.full_like(m_scr, -jnp.inf)
    l_scr[...] = jnp.zeros_like(l_scr)
    acc_scr[...] = jnp.zeros_like(acc_scr)

    def step(j, masked):
        start = pl.multiple_of(j * t, t)
        k_aug = jnp.concatenate([k_ref[0, pl.ds(start, t), :], ce_ref[0, pl.ds(start, t), :]],
                                axis=1)
        vt = vt_ref[0, 0, j]
        zt = _dot(k_aug, q_aug)
        m_prev = m_scr[...]
        l_prev = l_scr[...]
        acc_prev = acc_scr[...]
        m_out, l_out, acc_out = [], [], [[], []]
        for s in range(n_strips):
            head = s // (n_strips // 2)
            cs = slice(s * LANES, (s + 1) * LANES)
            z = zt[:, cs]
            if masked:
                key = lax.broadcasted_iota(jnp.int32, (t, LANES), 0)
                qry = lax.broadcasted_iota(jnp.int32, (t, LANES), 1) + (s * LANES) % t
                z = jnp.where(key <= qry, z, -jnp.inf)
            m_new = jnp.maximum(m_prev[:, cs], jnp.max(z, axis=0, keepdims=True))
            alpha = jnp.exp2(m_prev[:, cs] - m_new)
            p = jnp.exp2(z - m_new)
            l_out.append(alpha * l_prev[:, cs] + jnp.sum(p, axis=0, keepdims=True))
            m_out.append(m_new)
            hs = slice(head * HEAD_DIM, (head + 1) * HEAD_DIM)
            qs = slice((s * LANES) % t, (s * LANES) % t + LANES)
            pv = _dot(vt[hs], p.astype(BF16))
            acc_out[head].append(alpha * acc_prev[hs, qs] + pv)
        m_scr[...] = jnp.concatenate(m_out, axis=1)
        l_scr[...] = jnp.concatenate(l_out, axis=1)
        acc_scr[...] = jnp.concatenate([jnp.concatenate(acc_out[0], axis=1),
                                        jnp.concatenate(acc_out[1], axis=1)], axis=0)

    def body(j, carry):
        step(j, False)
        return carry

    lax.fori_loop(0, qi, body, 0)
    step(qi, True)

    l = l_scr[...]
    acc = acc_scr[...]
    inv_n = 1.0 / HEAD_DIM
    halves = []
    for head in range(2):
        o = acc[head * HEAD_DIM:(head + 1) * HEAD_DIM] / l[:, head * t:(head + 1) * t]
        halves.append(o * lax.rsqrt(jnp.sum(o * o, axis=0, keepdims=True) * inv_n + RMS_EPS))
    o_t = jnp.concatenate(halves, axis=0)
    o_ref[0] = (jnp.transpose(o_t) * og_ref[...]).astype(o_ref.dtype)


def _fox_attn(qt5, k, cext, vt5, out_g, tile):
    bsz, t_all, _ = k.shape
    pairs = N_HEADS // 2
    kern = functools.partial(_fox_attn_kernel, t=tile)
    return pl.pallas_call(
        kern,
        grid=(bsz, pairs, t_all // tile),
        in_specs=[
            pl.BlockSpec((1, 1, 1, LANES, tile), lambda b, p, i: (b, p, i, 0, 0)),
            pl.BlockSpec((1, t_all, LANES), lambda b, p, i: (b, 0, p)),
            pl.BlockSpec((1, t_all, LANES), lambda b, p, i: (b, 0, p)),
            pl.BlockSpec((1, 1, t_all // tile, LANES, tile), lambda b, p, i: (b, p, 0, 0, 0)),
            pl.BlockSpec((1, LANES), lambda b, p, i: (0, p)),
        ],
        out_specs=pl.BlockSpec((1, tile, LANES), lambda b, p, i: (b, i, p)),
        out_shape=jax.ShapeDtypeStruct((bsz, t_all, GROUP_WIDTH), BF16),
        scratch_shapes=[
            pltpu.VMEM((1, 2 * tile), F32),
            pltpu.VMEM((1, 2 * tile), F32),
            pltpu.VMEM((LANES, tile), F32),
        ],
        compiler_params=_cparams(("parallel", "parallel", "arbitrary")),
        name="fox_attn",
    )(qt5, k, cext, vt5, out_g)


def _outproj_router_kernel(x_ref, yr_ref, yf_ref, wo_r_ref, wo_f_ref, g_ref, rwt_ref, rb_ref,
                           x1_ref, h_ref, idx_ref, gate_ref):
    x1 = x_ref[...] + _dot(yr_ref[...], wo_r_ref[...]) + _dot(yf_ref[...], wo_f_ref[...])
    x1_ref[...] = x1
    h = x1 * lax.rsqrt(jnp.mean(x1 * x1, axis=-1, keepdims=True) + RMS_EPS) * g_ref[...]
    h_ref[:, 0, :] = h
    logits = lax.dot_general(rwt_ref[...], h, (((1,), (1,)), ((), ())),
                             precision=lax.Precision.HIGHEST,
                             preferred_element_type=F32) + rb_ref[...]
    eidx = lax.broadcasted_iota(jnp.int32, logits.shape, 0)
    vals, idxs = [], []
    for _ in range(TOP_K):
        m = jnp.max(logits, axis=0, keepdims=True)
        i = jnp.min(jnp.where(logits == m, eidx, N_EXPERTS), axis=0, keepdims=True)
        vals.append(m)
        idxs.append(i)
        logits = jnp.where(eidx == i, -jnp.inf, logits)
    es = [jnp.exp(val - vals[0]) for val in vals]
    denom = es[0] + es[1] + es[2] + es[3]
    idx_ref[...] = jnp.concatenate(idxs, axis=0)
    gate_ref[...] = jnp.concatenate([e / denom for e in es], axis=0)


def _outproj_router(x2, yr, yf, wo_r, wo_f, g, rwt, rb, tm):
    n, d = x2.shape
    const = lambda i: (0, 0)
    row = lambda i: (i, 0)
    col = lambda i: (0, i)
    return pl.pallas_call(
        _outproj_router_kernel,
        grid=(n // tm,),
        in_specs=[
            pl.BlockSpec((tm, d), row),
            pl.BlockSpec((tm, GROUP_WIDTH), row),
            pl.BlockSpec((tm, GROUP_WIDTH), row),
            pl.BlockSpec((GROUP_WIDTH, d), const),
            pl.BlockSpec((GROUP_WIDTH, d), const),
            pl.BlockSpec((1, d), const),
            pl.BlockSpec((N_EXPERTS, d), const),
            pl.BlockSpec((N_EXPERTS, 1), const),
        ],
        out_specs=[
            pl.BlockSpec((tm, d), row),
            pl.BlockSpec((tm, 1, d), lambda i: (i, 0, 0)),
            pl.BlockSpec((TOP_K, tm), col),
            pl.BlockSpec((TOP_K, tm), col),
        ],
        out_shape=[
            jax.ShapeDtypeStruct((n, d), F32),
            jax.ShapeDtypeStruct((n, 1, d), F32),
            jax.ShapeDtypeStruct((TOP_K, n), jnp.int32),
            jax.ShapeDtypeStruct((TOP_K, n), F32),
        ],
        compiler_params=_cparams(("parallel",)),
        name="outproj_router",
    )(x2, yr, yf, wo_r, wo_f, g, rwt, rb)


def _w1_split_kernel(w_ref, perm_ref, g_ref, l_ref):
    half = MXU_WIDTH // 2
    perm = perm_ref[...]
    for grp in range(w_ref.shape[2] // MXU_WIDTH):
        blk = w_ref[0, :, grp * MXU_WIDTH:(grp + 1) * MXU_WIDTH].astype(BF16)
        r = _dot(blk, perm)
        g_ref[0, :, grp * half:(grp + 1) * half] = r[:, :half].astype(BF16)
        l_ref[0, :, grp * half:(grp + 1) * half] = r[:, half:].astype(BF16)


def _w1_split(w1, tr):
    e, d, two_f = w1.shape
    half = MXU_WIDTH // 2
    src = jnp.arange(MXU_WIDTH, dtype=jnp.int32)[:, None]
    dst = jnp.arange(MXU_WIDTH, dtype=jnp.int32)[None, :]
    perm = (src == jnp.where(dst < half, 2 * dst, 2 * (dst - half) + 1)).astype(BF16)
    out_sds = jax.ShapeDtypeStruct((e, d, two_f // 2), BF16)
    return pl.pallas_call(
        _w1_split_kernel,
        grid=(e, d // tr),
        in_specs=[
            pl.BlockSpec((1, tr, two_f), lambda i, j: (i, j, 0)),
            pl.BlockSpec((MXU_WIDTH, MXU_WIDTH), lambda i, j: (0, 0)),
        ],
        out_specs=[pl.BlockSpec((1, tr, two_f // 2), lambda i, j: (i, j, 0))] * 2,
        out_shape=[out_sds, out_sds],
        compiler_params=_cparams(("parallel", "parallel")),
        name="w1_split",
    )(w1, perm)


def _expert_kernel(be_ref, tok_a_ref, tok_b_ref, tok_a_next_ref, dst_b_prev_ref, dst_a_ref,
                   dst_b_ref, h_hbm, w1g_a, w1l_a, b1g_a, b1l_a, w2_a, b2_a,
                   w1g_b, w1l_b, b1g_b, b1l_b, w2_b, b2_b,
                   y_hbm, xbuf_a, xbuf_b, obuf_a, obuf_b, gsem, osem, *, bm, n_real_rows):
    del be_ref
    i = pl.program_id(0)

    def gather_start(idx_ref, xbuf, sem, r):
        pltpu.make_async_copy(h_hbm.at[idx_ref[0, 0, r]], xbuf.at[pl.ds(r, 1), :], sem).start()

    def scatter_start(idx_ref, obuf, sem, r):
        pltpu.make_async_copy(obuf.at[pl.ds(r, 1), :], y_hbm.at[idx_ref[0, 0, r]], sem).start()

    def rows_wait(buf, sem):
        pltpu.make_async_copy(buf, buf, sem).wait()

    def mlp(xbuf, w1g_ref, w1l_ref, b1g_ref, b1l_ref, w2_ref, b2_ref, start_copies):
        xb = xbuf[...].astype(BF16)
        dff = w1g_ref.shape[2]
        n_pieces = dff // MXU_WIDTH
        rows_per_piece = bm // n_pieces
        acts = []
        for piece in range(n_pieces):
            for r in range(piece * rows_per_piece, (piece + 1) * rows_per_piece):
                start_copies(r)
            cs = slice(piece * MXU_WIDTH, (piece + 1) * MXU_WIDTH)
            glu = _dot(xb, w1g_ref[0, :, cs]) + b1g_ref[0, :, cs]
            lin = _dot(xb, w1l_ref[0, :, cs]) + b1l_ref[0, :, cs]
            glu = jnp.minimum(glu, SWIGLU_LIMIT)
            lin = jnp.clip(lin, -SWIGLU_LIMIT, SWIGLU_LIMIT)
            acts.append((glu * _sigmoid(SWIGLU_ALPHA * glu) * (lin + 1.0)).astype(BF16))
        return _dot(jnp.concatenate(acts, axis=1), w2_ref[0]) + b2_ref[0]

    @pl.when(i == 0)
    def _():
        obuf_a[...] = jnp.zeros_like(obuf_a)
        obuf_b[...] = jnp.zeros_like(obuf_b)

        def first(r, carry):
            pltpu.make_async_copy(obuf_a.at[pl.ds(r, 1), :], y_hbm.at[n_real_rows + 2 * bm + r],
                                  osem.at[0]).start()
            gather_start(tok_a_ref, xbuf_a, gsem.at[0], r)
            return carry

        lax.fori_loop(0, bm, first, 0)

    def copies_during_a(r):
        gather_start(tok_b_ref, xbuf_b, gsem.at[1], r)
        scatter_start(dst_b_prev_ref, obuf_b, osem.at[1], r)

    def copies_during_b(r):
        gather_start(tok_a_next_ref, xbuf_a, gsem.at[0], r)
        scatter_start(dst_a_ref, obuf_a, osem.at[0], r)

    rows_wait(xbuf_a, gsem.at[0])
    out_a = mlp(xbuf_a, w1g_a, w1l_a, b1g_a, b1l_a, w2_a, b2_a, copies_during_a)
    rows_wait(obuf_a, osem.at[0])
    obuf_a[...] = out_a

    rows_wait(xbuf_b, gsem.at[1])
    out_b = mlp(xbuf_b, w1g_b, w1l_b, b1g_b, b1l_b, w2_b, b2_b, copies_during_b)
    rows_wait(obuf_b, osem.at[1])
    obuf_b[...] = out_b

    @pl.when(i == pl.num_programs(0) - 1)
    def _():
        def last(r, carry):
            scatter_start(dst_b_ref, obuf_b, osem.at[1], r)
            return carry

        lax.fori_loop(0, bm, last, 0)
        rows_wait(obuf_b, osem.at[1])
        rows_wait(obuf_a, osem.at[0])
        rows_wait(xbuf_a, gsem.at[0])


def _expert_mlp(block_e, tok_blocks, dst_blocks, h2, w1g, w1l, b1g, b1l, w2, b2, bm):
    n_blocks = tok_blocks.shape[0]
    assert n_blocks % 2 == 0
    n, _, d = h2.shape
    dff = w1g.shape[2]
    n_real_rows = TOP_K * n
    idx_spec = lambda fn: pl.BlockSpec((1, 1, bm), fn, memory_space=pltpu.SMEM)

    def weight_specs(which):
        wmap = lambda i, be: (be[2 * i + which], 0, 0)
        return [
            pl.BlockSpec((1, d, dff), wmap),
            pl.BlockSpec((1, d, dff), wmap),
            pl.BlockSpec((1, 1, dff), wmap),
            pl.BlockSpec((1, 1, dff), wmap),
            pl.BlockSpec((1, dff, d), wmap),
            pl.BlockSpec((1, 1, d), wmap),
        ]

    grid_spec = pltpu.PrefetchScalarGridSpec(
        num_scalar_prefetch=1,
        grid=(n_blocks // 2,),
        in_specs=[
            idx_spec(lambda i, be: (2 * i, 0, 0)),
            idx_spec(lambda i, be: (2 * i + 1, 0, 0)),
            idx_spec(lambda i, be: (jnp.minimum(2 * i + 2, n_blocks - 1), 0, 0)),
            idx_spec(lambda i, be: (2 * i, 0, 0)),
            idx_spec(lambda i, be: (2 * i + 1, 0, 0)),
            idx_spec(lambda i, be: (2 * i + 2, 0, 0)),
            pl.BlockSpec(memory_space=pl.ANY),
        ] + weight_specs(0) + weight_specs(1),
        out_specs=pl.BlockSpec(memory_space=pl.ANY),
        scratch_shapes=[
            pltpu.VMEM((bm, d), F32),
            pltpu.VMEM((bm, d), F32),
            pltpu.VMEM((bm, d), F32),
            pltpu.VMEM((bm, d), F32),
            pltpu.SemaphoreType.DMA((2,)),
            pltpu.SemaphoreType.DMA((2,)),
        ],
    )
    weights = (w1g, w1l, b1g, b1l, w2, b2)
    return pl.pallas_call(
        functools.partial(_expert_kernel, bm=bm, n_real_rows=n_real_rows),
        grid_spec=grid_spec,
        out_shape=jax.ShapeDtypeStruct((n_real_rows + 3 * bm, 1, d), F32),
        compiler_params=_cparams(("arbitrary",)),
        name="expert_mlp",
    )(block_e, tok_blocks, tok_blocks, tok_blocks, dst_blocks, dst_blocks, dst_blocks, h2,
      *weights, *weights)


def _combine_kernel(gate_ref, x1_ref, g_ref, y0_ref, y1_ref, y2_ref, y3_ref, o_ref):
    gates = gate_ref[...]
    y = x1_ref[...]
    for kk, y_ref in enumerate((y0_ref, y1_ref, y2_ref, y3_ref)):
        y = y + y_ref[:, 0, :] * gates[:, kk:kk + 1]
    o_ref[...] = y * lax.rsqrt(jnp.mean(y * y, axis=-1, keepdims=True) + RMS_EPS) * g_ref[...]


def _combine(gates_t, x1, g, y_all, tc):
    n, d = x1.shape
    tiles = n // tc
    y_spec = lambda kk: pl.BlockSpec((tc, 1, d), lambda i: (kk * tiles + i, 0, 0))
    return pl.pallas_call(
        _combine_kernel,
        grid=(tiles,),
        in_specs=[
            pl.BlockSpec((tc, TOP_K), lambda i: (i, 0)),
            pl.BlockSpec((tc, d), lambda i: (i, 0)),
            pl.BlockSpec((1, d), lambda i: (0, 0)),
        ] + [y_spec(kk) for kk in range(TOP_K)],
        out_specs=pl.BlockSpec((tc, d), lambda i: (i, 0)),
        out_shape=jax.ShapeDtypeStruct((n, d), F32),
        compiler_params=_cparams(("parallel",)),
        name="combine",
    )(gates_t, x1, g, y_all, y_all, y_all, y_all)


def _dispatch_plan(idx, bm):
    n = idx.shape[1]
    n_slots = TOP_K * n
    e_flat = idx.reshape(-1)
    onehot = (e_flat[:, None] == jnp.arange(N_EXPERTS, dtype=jnp.int32)[None, :]).astype(jnp.int32)
    csum = jnp.cumsum(onehot, axis=0)
    rank = jnp.sum(csum * onehot, axis=1) - 1
    sizes = csum[-1]
    padded = (sizes + bm - 1) // bm * bm
    pad_ends = jnp.cumsum(padded)
    pad_starts = pad_ends - padded
    pos = (pad_starts[e_flat] + rank).astype(jnp.int32)
    n_pad = n_slots + N_EXPERTS * bm
    n_blocks = n_pad // bm
    slot_src = jnp.full((n_pad,), -1, jnp.int32).at[pos].set(
        jnp.arange(n_slots, dtype=jnp.int32), unique_indices=True)
    p = jnp.arange(n_pad, dtype=jnp.int32)
    spare = n_slots + (p // bm) % 2 * bm + p % bm
    real = slot_src >= 0
    tok_blocks = jnp.where(real, slot_src % n, 0).reshape(n_blocks, 1, bm)
    dst = jnp.where(real, slot_src, spare)
    dst_blocks = jnp.concatenate([spare[bm:2 * bm], dst]).reshape(n_blocks + 1, 1, bm)
    block_start = jnp.arange(n_blocks, dtype=jnp.int32) * bm
    block_e = jnp.minimum(jnp.searchsorted(pad_ends, block_start, side='right'),
                          N_EXPERTS - 1).astype(jnp.int32)
    return tok_blocks, dst_blocks, block_e


def _block_diag_ones():
    hid = jnp.arange(GROUP_WIDTH, dtype=jnp.int32) // HEAD_DIM
    return (hid[:, None] == hid[None, :]).astype(BF16)


def _pick(n, pref):
    return pref if n % pref == 0 else n


def kernel(x, attn_norm_g, w_in, rwkv_mu, rwkv_w0, rwkv_w_up, rwkv_a0, rwkv_a_up, rwkv_g_up,
           rwkv_k_k, rwkv_k_a, rwkv_r_k, rwkv_ln_w, rwkv_ln_b, fox_f_bias, fox_out_g, w_out,
           ffn_norm_g, router_w, router_b, expert_w1, expert_b1, expert_w2, expert_b2,
           final_norm_g):
    bsz, t, d = x.shape
    n = bsz * t
    depth = w_in.shape[0]
    assert depth == 1, "the final norm is fused into the last stage of a single layer"
    bd = _block_diag_ones()
    x2 = x.reshape(n, d)
    for l in range(depth):
        w_l = w_in[l]
        w_r = w_l[:, :RWKV_IN].astype(BF16)
        w_qkv = w_l[:, RWKV_IN:RWKV_IN + 3 * GROUP_WIDTH].astype(BF16)
        w_qt = w_qkv[:, :GROUP_WIDTH].T
        w_k = w_qkv[:, GROUP_WIDTH:2 * GROUP_WIDTH]
        w_vt = w_qkv[:, 2 * GROUP_WIDTH:].T
        w_f = jnp.pad(w_l[:, RWKV_IN + 3 * GROUP_WIDTH:], ((0, 0), (0, LANES - N_HEADS))).astype(BF16)
        fb_pad = jnp.pad(fox_f_bias[l], (0, LANES - N_HEADS)).reshape(1, LANES)
        wup_pad = jnp.pad(rwkv_w_up[l], ((0, LANES - DECAY_LORA), (0, 0))).astype(BF16)
        aup_pad = jnp.pad(rwkv_a_up[l], ((DECAY_LORA, 0), (0, 0))).astype(BF16)
        gup = rwkv_g_up[l].astype(BF16)
        vec = lambda a: a.reshape(1, -1)

        u_r, qt5, k, vt5, fl = _inproj(x2, vec(attn_norm_g[l]), w_r, w_qt, w_k, w_vt, w_f,
                                       bsz, ATTN_TILE)
        cext = _fox_gate(fl.reshape(bsz, t, LANES), fb_pad, _gate_piece_selectors(), ATTN_TILE)
        r_, k_, v_, lw, kk, b_, g_ = _rwkv_prep(
            u_r.reshape(bsz, t, RWKV_IN), vec(rwkv_mu[l]), vec(rwkv_w0[l]), wup_pad,
            vec(rwkv_a0[l]), aup_pad, gup, vec(rwkv_k_k[l]), vec(rwkv_k_a[l]), bd, _pick(t, 256))
        y_rwkv = _rwkv_scan(r_, k_, v_, lw, kk, b_, g_, vec(rwkv_r_k[l]), vec(rwkv_ln_w[l]),
                            vec(rwkv_ln_b[l]), bd)
        y_fox = _fox_attn(qt5, k.reshape(bsz, t, GROUP_WIDTH), cext, vt5, vec(fox_out_g[l]),
                          ATTN_TILE)

        wo = w_out[l].astype(BF16)
        x1, h2, idx, gates = _outproj_router(
            x2, y_rwkv.reshape(n, GROUP_WIDTH), y_fox.reshape(n, GROUP_WIDTH),
            wo[:GROUP_WIDTH], wo[GROUP_WIDTH:], vec(ffn_norm_g[l]),
            router_w[l].T, router_b[l].reshape(N_EXPERTS, 1), _pick(n, 256))

        bm = 256
        tok_blocks, dst_blocks, block_e = _dispatch_plan(idx, bm)
        w1g, w1l = _w1_split(expert_w1[l], 512)
        b1 = expert_b1[l]
        b1g = b1[:, None, 0::2]
        b1l = b1[:, None, 1::2]
        y_all = _expert_mlp(block_e, tok_blocks, dst_blocks, h2, w1g, w1l, b1g, b1l,
                            expert_w2[l].astype(BF16), expert_b2[l][:, None, :], bm)
        x2 = _combine(gates.T, x1, vec(final_norm_g), y_all, _pick(n, 256))
    return x2.reshape(bsz, t, d)
```

```python
import functools

import jax
import jax.numpy as jnp
from jax import lax
from jax.experimental import pallas as pl
from jax.experimental.pallas import tpu as pltpu

F32 = jnp.float32
BF16 = jnp.bfloat16

HEAD_DIM = 64
N_HEADS = 8
GROUP_WIDTH = N_HEADS * HEAD_DIM
DECAY_LORA = 64
AAA_LORA = 64
GATE_LORA = 128
RWKV_IN = 3 * GROUP_WIDTH + DECAY_LORA + AAA_LORA + GATE_LORA
LORA_OFF = 3 * GROUP_WIDTH
N_EXPERTS = 32
TOP_K = 4
SWIGLU_ALPHA = 1.702
SWIGLU_LIMIT = 7.0
RMS_EPS = 1e-5
RWKV_GN_EPS = 64e-5
LANES = 128
RWKV_CHUNK = 64
ATTN_TILE = 512
MXU_WIDTH = 256
LOG2E = 1.4426950408889634
Q_SCALE = HEAD_DIM ** -0.5 * LOG2E
ZERO_PROB_EXP = -152.0
NORM_SLACK = 1.0 + 2.0 ** -6
VMEM_LIMIT = 56 * 1024 * 1024


def _cparams(semantics):
    return pltpu.CompilerParams(dimension_semantics=semantics, vmem_limit_bytes=VMEM_LIMIT)


def _dot(a, b):
    return jnp.dot(a, b, preferred_element_type=F32)


def _dot_nt(a, b):
    return lax.dot_general(a, b, (((1,), (1,)), ((), ())), preferred_element_type=F32)


def _dot_tn(a, b):
    return lax.dot_general(a, b, (((0,), (0,)), ((), ())), preferred_element_type=F32)


def _split3(x):
    hi = x.astype(BF16)
    r1 = x - hi.astype(F32)
    mid = r1.astype(BF16)
    lo = (r1 - mid.astype(F32)).astype(BF16)
    return hi, mid, lo


def _dot_exact_lhs(a_bf16, x):
    hi, mid, lo = _split3(x)
    return _dot(a_bf16, hi) + _dot(a_bf16, mid) + _dot(a_bf16, lo)


def _dot_exact_rhs(x, b_bf16):
    hi, mid, lo = _split3(x)
    return _dot(hi, b_bf16) + _dot(mid, b_bf16) + _dot(lo, b_bf16)


def _softplus(z):
    return jnp.maximum(z, 0.0) + jnp.log1p(jnp.exp(-jnp.abs(z)))


def _sigmoid(z):
    return 1.0 / (1.0 + jnp.exp(-z))


def _inproj_kernel(x_ref, g_ref, wr_ref, wqt_ref, wk_ref, wvt_ref, wf_ref,
                   ur_ref, qt_ref, k_ref, vt_ref, fl_ref):
    x = x_ref[...]
    h = x * lax.rsqrt(jnp.mean(x * x, axis=-1, keepdims=True) + RMS_EPS) * g_ref[...]
    hb = h.astype(BF16)
    ur_ref[...] = _dot(hb, wr_ref[...])
    k_ref[...] = _dot(hb, wk_ref[...]).astype(BF16)
    fl_ref[...] = _dot(hb, wf_ref[...])
    qt = (_dot_nt(wqt_ref[...], hb) * Q_SCALE).astype(BF16)
    vt = _dot_nt(wvt_ref[...], hb).astype(BF16)
    for p in range(N_HEADS // 2):
        qt_ref[0, p, 0] = qt[p * LANES:(p + 1) * LANES]
        vt_ref[0, p, 0] = vt[p * LANES:(p + 1) * LANES]


def _inproj(x2, g, w_r, w_qt, w_k, w_vt, w_f, bsz, tm):
    n, d = x2.shape
    nt = n // bsz // tm
    pairs = N_HEADS // 2
    const = lambda i: (0, 0)
    row = lambda i: (i, 0)
    fm = lambda i: (i // nt, 0, i % nt, 0, 0)
    fm_sds = jax.ShapeDtypeStruct((bsz, pairs, nt, LANES, tm), BF16)
    return pl.pallas_call(
        _inproj_kernel,
        grid=(n // tm,),
        in_specs=[
            pl.BlockSpec((tm, d), row),
            pl.BlockSpec((1, d), const),
            pl.BlockSpec(w_r.shape, const),
            pl.BlockSpec(w_qt.shape, const),
            pl.BlockSpec(w_k.shape, const),
            pl.BlockSpec(w_vt.shape, const),
            pl.BlockSpec(w_f.shape, const),
        ],
        out_specs=[
            pl.BlockSpec((tm, RWKV_IN), row),
            pl.BlockSpec((1, pairs, 1, LANES, tm), fm),
            pl.BlockSpec((tm, GROUP_WIDTH), row),
            pl.BlockSpec((1, pairs, 1, LANES, tm), fm),
            pl.BlockSpec((tm, LANES), row),
        ],
        out_shape=[
            jax.ShapeDtypeStruct((n, RWKV_IN), F32),
            fm_sds,
            jax.ShapeDtypeStruct((n, GROUP_WIDTH), BF16),
            fm_sds,
            jax.ShapeDtypeStruct((n, LANES), F32),
        ],
        compiler_params=_cparams(("parallel",)),
        name="inproj",
    )(x2, g, w_r, w_qt, w_k, w_vt, w_f)


def _fox_gate_kernel(fl_ref, fb_ref, sel_ref, c_ref, cend_ref, carry):
    tt = fl_ref.shape[1]

    @pl.when(pl.program_id(1) == 0)
    def _():
        carry[...] = jnp.zeros_like(carry)

    z = fl_ref[0] + fb_ref[...]
    log_f = jnp.minimum(z, 0.0) - jnp.log1p(jnp.exp(-jnp.abs(z)))
    ri = lax.broadcasted_iota(jnp.int32, (tt, tt), 0)
    ci = lax.broadcasted_iota(jnp.int32, (tt, tt), 1)
    tri = jnp.where(ri >= ci, 1.0, 0.0).astype(BF16)
    c = _dot_exact_lhs(tri, log_f) + carry[...]
    carry[...] = c[tt - 1:tt, :]
    cend_ref[0, 0] = c[tt - 1:tt, :]
    hi, mid, lo = _split3(c * LOG2E)
    c_ref[0] = (_dot(hi, sel_ref[0]) + _dot(mid, sel_ref[1]) + _dot(lo, sel_ref[2])).astype(BF16)


def _gate_piece_selectors():
    h = jnp.arange(LANES, dtype=jnp.int32)[:, None]
    col = jnp.arange(GROUP_WIDTH, dtype=jnp.int32)[None, :]
    sels = []
    for m in range(3):
        target = LANES * (h // 2) + 3 * (h % 2) + m
        sels.append(((col == target) & (h < N_HEADS)).astype(BF16))
    return jnp.stack(sels)


def _fox_gate(fl3, fb_pad, sel, tt):
    b, t, _ = fl3.shape
    return pl.pallas_call(
        _fox_gate_kernel,
        grid=(b, t // tt),
        in_specs=[
            pl.BlockSpec((1, tt, LANES), lambda i, j: (i, j, 0)),
            pl.BlockSpec((1, LANES), lambda i, j: (0, 0)),
            pl.BlockSpec((3, LANES, GROUP_WIDTH), lambda i, j: (0, 0, 0)),
        ],
        out_specs=[
            pl.BlockSpec((1, tt, GROUP_WIDTH), lambda i, j: (i, j, 0)),
            pl.BlockSpec((1, 1, 1, LANES), lambda i, j: (i, j, 0, 0)),
        ],
        out_shape=[
            jax.ShapeDtypeStruct((b, t, GROUP_WIDTH), BF16),
            jax.ShapeDtypeStruct((b, t // tt, 1, LANES), F32),
        ],
        scratch_shapes=[pltpu.VMEM((1, LANES), F32)],
        compiler_params=_cparams(("parallel", "arbitrary")),
        name="fox_gate",
    )(fl3, fb_pad, sel)


def _rwkv_prep_kernel(u_ref, mu_ref, w0_ref, wup_ref, a0_ref, aup_ref, gup_ref, kk_ref, ka_ref,
                      bd_ref, r_out, k_out, v_out, lw_out, kk_out, b_out, g_out, carry):
    tt = u_ref.shape[1]

    @pl.when(pl.program_id(1) == 0)
    def _():
        carry[...] = jnp.zeros_like(carry)

    u = u_ref[0]
    prev = pltpu.roll(u, 1, axis=0)
    row = lax.broadcasted_iota(jnp.int32, u.shape, 0)
    prev = jnp.where(row == 0, carry[...], prev)
    carry[...] = u[tt - 1:tt, :]
    us = u + (prev - u) * mu_ref[...]

    r = us[:, :GROUP_WIDTH]
    k = us[:, GROUP_WIDTH:2 * GROUP_WIDTH]
    v = us[:, 2 * GROUP_WIDTH:LORA_OFF]
    wa = us[:, LORA_OFF:LORA_OFF + LANES]
    gl = us[:, LORA_OFF + LANES:]

    w_lin = _dot(jnp.tanh(wa).astype(BF16), wup_ref[...])
    a_lin = _dot(wa.astype(BF16), aup_ref[...])
    w = -_softplus(-(w0_ref[...] + w_lin)) - 0.5
    lw_out[0] = -jnp.exp(w)
    a = _sigmoid(a0_ref[...] + a_lin)
    g_out[0] = _dot(_sigmoid(gl).astype(BF16), gup_ref[...])

    kkr = k * kk_ref[...]
    ss = _dot_exact_rhs(kkr * kkr, bd_ref[...])
    kk = kkr / jnp.maximum(jnp.sqrt(ss), 1e-12)
    r_out[0] = r
    k_out[0] = k * (1.0 + (a - 1.0) * ka_ref[...])
    v_out[0] = v
    kk_out[0] = kk
    b_out[0] = kk * a


def _rwkv_prep(u3, mu, w0, wup_pad, a0, aup_pad, gup, k_k, k_a, bd, tt):
    b, t, _ = u3.shape
    const = lambda i, j: (0, 0)
    tile = lambda i, j: (i, j, 0)
    out_sds = jax.ShapeDtypeStruct((b, t, GROUP_WIDTH), F32)
    vec = pl.BlockSpec((1, GROUP_WIDTH), const)
    return pl.pallas_call(
        _rwkv_prep_kernel,
        grid=(b, t // tt),
        in_specs=[
            pl.BlockSpec((1, tt, RWKV_IN), tile),
            pl.BlockSpec((1, RWKV_IN), const),
            vec,
            pl.BlockSpec((LANES, GROUP_WIDTH), const),
            vec,
            pl.BlockSpec((LANES, GROUP_WIDTH), const),
            pl.BlockSpec((GATE_LORA, GROUP_WIDTH), const),
            vec,
            vec,
            pl.BlockSpec((GROUP_WIDTH, GROUP_WIDTH), const),
        ],
        out_specs=[pl.BlockSpec((1, tt, GROUP_WIDTH), tile)] * 7,
        out_shape=[out_sds] * 7,
        scratch_shapes=[pltpu.VMEM((1, RWKV_IN), F32)],
        compiler_params=_cparams(("parallel", "arbitrary")),
        name="rwkv_prep",
    )(u3, mu, w0, wup_pad, a0, aup_pad, gup, k_k, k_a, bd)


def _unit_lower_inverses(l_stricts, level_masks, eye):
    ts = [eye - jnp.where(level_masks[0], l, 0.0) for l in l_stricts]
    for m in level_masks[1:]:
        tbs = [t.astype(BF16) for t in ts]
        cts = [_dot(jnp.where(m, l, 0.0).astype(BF16), tb) for l, tb in zip(l_stricts, tbs)]
        ts = [t - _dot(tb, ct.astype(BF16)) for t, tb, ct in zip(ts, tbs, cts)]
    return ts


def _rwkv_scan_kernel(r_ref, k_ref, v_ref, lw_ref, kk_ref, b_ref, g_ref, rk_ref, lnw_ref, lnb_ref,
                      bd_ref, o_ref, s_scr):
    c = RWKV_CHUNK

    @pl.when(pl.program_id(1) == 0)
    def _():
        s_scr[...] = jnp.zeros_like(s_scr)

    r = r_ref[0]
    k = k_ref[0]
    v = v_ref[0]
    lw = lw_ref[0]
    kk = kk_ref[0]
    b = b_ref[0]

    ri = lax.broadcasted_iota(jnp.int32, (c, c), 0)
    ci = lax.broadcasted_iota(jnp.int32, (c, c), 1)
    incl = ri >= ci
    strict = ri > ci
    eye = jnp.where(ri == ci, 1.0, 0.0)
    level_masks = []
    s = 1
    while s < c:
        same = (ri // (2 * s)) == (ci // (2 * s))
        level_masks.append(same & ((ri % (2 * s)) >= s) & ((ci % (2 * s)) < s))
        s *= 2
    ri2 = lax.broadcasted_iota(jnp.int32, (c, 2 * c), 0)
    ci2 = lax.broadcasted_iota(jnp.int32, (c, 2 * c), 1) % c
    incl2 = ri2 >= ci2

    g_cum = _dot_exact_lhs(jnp.where(incl, 1.0, 0.0).astype(BF16), lw)
    g_last = g_cum[c - 1:c, :]
    r_t = (r * jnp.exp(g_cum)).astype(BF16)
    kk_t = (kk * jnp.exp(g_cum - lw)).astype(BF16)
    e_neg = jnp.exp(-g_cum)
    b_n = (b * e_neg).astype(BF16)
    k_n = (k * e_neg).astype(BF16)
    e_end = jnp.exp(g_last - g_cum)
    b_e = (b * e_end).astype(BF16)
    k_e = (k * e_end).astype(BF16)
    gamma = jnp.exp(g_last)
    vb = v.astype(BF16)

    heads = range(N_HEADS)
    sls = [slice(h * HEAD_DIM, (h + 1) * HEAD_DIM) for h in heads]
    s_old = [s_scr[h] for h in heads]
    lhs = [jnp.concatenate([r_t[:, sl], kk_t[:, sl]], axis=0) for sl in sls]
    rhs = [jnp.concatenate([b_n[:, sl], k_n[:, sl]], axis=0) for sl in sls]
    vh = [vb[:, sl] for sl in sls]
    p = [_dot_nt(lhs[h], rhs[h]) for h in heads]
    q0 = [_dot_nt(lhs[h], s_old[h].astype(BF16)) for h in heads]
    l_b = [jnp.where(strict, p[h][c:, :c], 0.0) for h in heads]
    l_k = [jnp.where(strict, p[h][c:, c:], 0.0).astype(BF16) for h in heads]
    z = [q0[h][c:] + _dot(l_k[h], vh[h]) for h in heads]
    t_inv = _unit_lower_inverses(l_b, level_masks, eye)
    u = [-_dot(t_inv[h].astype(BF16), z[h].astype(BF16)) for h in heads]
    w_cat = [jnp.concatenate([u[h].astype(BF16), vh[h]], axis=0) for h in heads]
    p_r = [jnp.where(incl2, p[h][:c], 0.0).astype(BF16) for h in heads]
    ys = [q0[h][:c] + _dot(p_r[h], w_cat[h]) for h in heads]
    x_cat = [jnp.concatenate([b_e[:, sl], k_e[:, sl]], axis=0) for sl in sls]
    s_new = [s_old[h] * gamma[:, sls[h]] + _dot_tn(w_cat[h], x_cat[h]) for h in heads]
    for h in heads:
        s_scr[h] = s_new[h]

    y = jnp.concatenate(ys, axis=1)
    bd = bd_ref[...]
    inv_n = 1.0 / HEAD_DIM
    mean = _dot_exact_rhs(y, bd) * inv_n
    d = y - mean
    var = _dot_exact_rhs(d * d, bd) * inv_n
    yn = d * lax.rsqrt(var + RWKV_GN_EPS) * lnw_ref[...] + lnb_ref[...]
    bonus = _dot_exact_rhs(r * k * rk_ref[...], bd) * v
    o_ref[0] = ((yn + bonus) * g_ref[0]).astype(o_ref.dtype)


def _rwkv_scan(r, k, v, lw, kk, b, g, r_k, ln_w, ln_b, bd):
    bsz, t, _ = r.shape
    c = RWKV_CHUNK
    const = lambda i, j: (0, 0)
    tile = pl.BlockSpec((1, c, GROUP_WIDTH), lambda i, j: (i, j, 0))
    vec = pl.BlockSpec((1, GROUP_WIDTH), const)
    return pl.pallas_call(
        _rwkv_scan_kernel,
        grid=(bsz, t // c),
        in_specs=[tile] * 7 + [vec, vec, vec, pl.BlockSpec((GROUP_WIDTH, GROUP_WIDTH), const)],
        out_specs=tile,
        out_shape=jax.ShapeDtypeStruct((bsz, t, GROUP_WIDTH), BF16),
        scratch_shapes=[pltpu.VMEM((N_HEADS, HEAD_DIM, HEAD_DIM), F32)],
        compiler_params=_cparams(("parallel", "arbitrary")),
        name="rwkv_scan",
    )(r, k, v, lw, kk, b, g, r_k, ln_w, ln_b, bd)


def _fox_attn_kernel(qt_ref, k_ref, ce_ref, vt_ref, cend_ref, og_ref, o_ref,
                     m_scr, l_scr, acc_scr, kmax_scr, *, t):
    qi = pl.program_id(2)
    n_strips = 2 * t // LANES
    qt = qt_ref[0, 0, 0]
    frow = lax.broadcasted_iota(jnp.int32, (LANES, t), 0)
    zero = jnp.zeros_like(qt)
    main = jnp.concatenate([jnp.where(frow < HEAD_DIM, qt, zero),
                            jnp.where(frow < HEAD_DIM, zero, qt)], axis=1)
    erow = lax.broadcasted_iota(jnp.int32, (LANES, 2 * t), 0)
    ecol = lax.broadcasted_iota(jnp.int32, (LANES, 2 * t), 1)
    off = jnp.where(ecol < t, 0, 3)
    extra = jnp.where((erow >= off) & (erow < off + 3), -1.0, 0.0).astype(BF16)
    q_aug = jnp.concatenate([main, extra], axis=0)

    m_scr[...] = jnp.full_like(m_scr, -jnp.inf)
    l_scr[...] = jnp.zeros_like(l_scr)
    acc_scr[...] = jnp.zeros_like(acc_scr)

    @pl.when(qi == 0)
    def _():
        hid_r = lax.broadcasted_iota(jnp.int32, (LANES, LANES), 0) // HEAD_DIM
        hid_c = lax.broadcasted_iota(jnp.int32, (LANES, LANES), 1) // HEAD_DIM
        same_head = jnp.where(hid_r == hid_c, 1.0, 0.0).astype(BF16)

        def tile_max(j, best):
            kf = k_ref[0, pl.ds(pl.multiple_of(j * t, t), t), :].astype(F32)
            sq = _dot((kf * kf).astype(BF16), same_head)
            return jnp.maximum(best, jnp.max(sq, axis=0, keepdims=True))

        best = lax.fori_loop(0, k_ref.shape[1] // t, tile_max, jnp.zeros((1, LANES), F32))
        kmax_scr[...] = jnp.sqrt(best) * NORM_SLACK

    def step(j, masked):
        start = pl.multiple_of(j * t, t)
        k_aug = jnp.concatenate([k_ref[0, pl.ds(start, t), :], ce_ref[0, pl.ds(start, t), :]],
                                axis=1)
        vt = vt_ref[0, 0, j]
        zt = _dot(k_aug, q_aug)
        m_prev = m_scr[...]
        l_prev = l_scr[...]
        acc_prev = acc_scr[...]
        m_out, l_out, acc_out = [], [], [[], []]
        for s in range(n_strips):
            head = s // (n_strips // 2)
            cs = slice(s * LANES, (s + 1) * LANES)
            z = zt[:, cs]
            if masked:
                key = lax.broadcasted_iota(jnp.int32, (t, LANES), 0)
                qry = lax.broadcasted_iota(jnp.int32, (t, LANES), 1) + (s * LANES) % t
                z = jnp.where(key <= qry, z, -jnp.inf)
            m_new = jnp.maximum(m_prev[:, cs], jnp.max(z, axis=0, keepdims=True))
            alpha = jnp.exp2(m_prev[:, cs] - m_new)
            p = jnp.exp2(z - m_new)
            l_out.append(alpha * l_prev[:, cs] + jnp.sum(p, axis=0, keepdims=True))
            m_out.append(m_new)
            hs = slice(head * HEAD_DIM, (head + 1) * HEAD_DIM)
            qs = slice((s * LANES) % t, (s * LANES) % t + LANES)
            pv = _dot(vt[hs], p.astype(BF16))
            acc_out[head].append(alpha * acc_prev[hs, qs] + pv)
        m_scr[...] = jnp.concatenate(m_out, axis=1)
        l_scr[...] = jnp.concatenate(l_out, axis=1)
        acc_scr[...] = jnp.concatenate([jnp.concatenate(acc_out[0], axis=1),
                                        jnp.concatenate(acc_out[1], axis=1)], axis=0)

    def body(j, carry):
        step(j, False)
        return carry

    step(qi, True)

    qf = qt.astype(F32)
    qsq = qf * qf
    qnorm = jnp.sqrt(jnp.concatenate(
        [jnp.sum(qsq[:HEAD_DIM], axis=0, keepdims=True),
         jnp.sum(qsq[HEAD_DIM:], axis=0, keepdims=True)], axis=1)) * NORM_SLACK
    kmax = kmax_scr[...]
    kmax2 = jnp.concatenate([jnp.broadcast_to(kmax[:, 0:1], (1, t)),
                             jnp.broadcast_to(kmax[:, HEAD_DIM:HEAD_DIM + 1], (1, t))], axis=1)
    slack = qnorm * kmax2 - m_scr[...]
    cend = cend_ref[0][:, 0, :]
    lane = lax.broadcasted_iota(jnp.int32, cend.shape, 1)
    jrow = lax.broadcasted_iota(jnp.int32, (cend.shape[0], 1), 0)
    needed = jrow < 0
    for head in range(2):
        worst = jnp.max(slack[:, head * t:(head + 1) * t], axis=1, keepdims=True)
        c_head = jnp.sum(jnp.where(lane == 2 * pl.program_id(1) + head, cend, 0.0),
                         axis=1, keepdims=True)
        needed = needed | (worst - c_head * LOG2E > ZERO_PROB_EXP)
    first = jnp.min(jnp.where(needed & (jrow < qi), jrow, qi))
    lax.fori_loop(first, qi, body, 0)

    l = l_scr[...]
    acc = acc_scr[...]
    inv_n = 1.0 / HEAD_DIM
    halves = []
    for head in range(2):
        o = acc[head * HEAD_DIM:(head + 1) * HEAD_DIM] / l[:, head * t:(head + 1) * t]
        halves.append(o * lax.rsqrt(jnp.sum(o * o, axis=0, keepdims=True) * inv_n + RMS_EPS))
    o_t = jnp.concatenate(halves, axis=0)
    o_ref[0] = (jnp.transpose(o_t) * og_ref[...]).astype(o_ref.dtype)


def _fox_attn(qt5, k, cext, vt5, cend, out_g, tile):
    bsz, t_all, _ = k.shape
    pairs = N_HEADS // 2
    tiles = t_all // tile
    kern = functools.partial(_fox_attn_kernel, t=tile)
    return pl.pallas_call(
        kern,
        grid=(bsz, pairs, tiles),
        in_specs=[
            pl.BlockSpec((1, 1, 1, LANES, tile), lambda b, p, i: (b, p, i, 0, 0)),
            pl.BlockSpec((1, t_all, LANES), lambda b, p, i: (b, 0, p)),
            pl.BlockSpec((1, t_all, LANES), lambda b, p, i: (b, 0, p)),
            pl.BlockSpec((1, 1, tiles, LANES, tile), lambda b, p, i: (b, p, 0, 0, 0)),
            pl.BlockSpec((1, tiles, 1, LANES), lambda b, p, i: (b, 0, 0, 0)),
            pl.BlockSpec((1, LANES), lambda b, p, i: (0, p)),
        ],
        out_specs=pl.BlockSpec((1, tile, LANES), lambda b, p, i: (b, i, p)),
        out_shape=jax.ShapeDtypeStruct((bsz, t_all, GROUP_WIDTH), BF16),
        scratch_shapes=[
            pltpu.VMEM((1, 2 * tile), F32),
            pltpu.VMEM((1, 2 * tile), F32),
            pltpu.VMEM((LANES, tile), F32),
            pltpu.VMEM((1, LANES), F32),
        ],
        compiler_params=_cparams(("parallel", "parallel", "arbitrary")),
        name="fox_attn",
    )(qt5, k, cext, vt5, cend, out_g)


def _outproj_router_kernel(x_ref, yr_ref, yf_ref, wo_r_ref, wo_f_ref, g_ref, rwt_ref, rb_ref,
                           x1_ref, h_ref, idx_ref, gate_ref):
    x1 = x_ref[...] + _dot(yr_ref[...], wo_r_ref[...]) + _dot(yf_ref[...], wo_f_ref[...])
    x1_ref[...] = x1
    h = x1 * lax.rsqrt(jnp.mean(x1 * x1, axis=-1, keepdims=True) + RMS_EPS) * g_ref[...]
    h_ref[:, 0, :] = h
    logits = lax.dot_general(rwt_ref[...], h, (((1,), (1,)), ((), ())),
                             precision=lax.Precision.HIGHEST,
                             preferred_element_type=F32) + rb_ref[...]
    eidx = lax.broadcasted_iota(jnp.int32, logits.shape, 0)
    vals, idxs = [], []
    for _ in range(TOP_K):
        m = jnp.max(logits, axis=0, keepdims=True)
        i = jnp.min(jnp.where(logits == m, eidx, N_EXPERTS), axis=0, keepdims=True)
        vals.append(m)
        idxs.append(i)
        logits = jnp.where(eidx == i, -jnp.inf, logits)
    es = [jnp.exp(val - vals[0]) for val in vals]
    denom = es[0] + es[1] + es[2] + es[3]
    idx_ref[...] = jnp.concatenate(idxs, axis=0)
    gate_ref[...] = jnp.concatenate([e / denom for e in es], axis=0)


def _outproj_router(x2, yr, yf, wo_r, wo_f, g, rwt, rb, tm):
    n, d = x2.shape
    const = lambda i: (0, 0)
    row = lambda i: (i, 0)
    col = lambda i: (0, i)
    return pl.pallas_call(
        _outproj_router_kernel,
        grid=(n // tm,),
        in_specs=[
            pl.BlockSpec((tm, d), row),
            pl.BlockSpec((tm, GROUP_WIDTH), row),
            pl.BlockSpec((tm, GROUP_WIDTH), row),
            pl.BlockSpec((GROUP_WIDTH, d), const),
            pl.BlockSpec((GROUP_WIDTH, d), const),
            pl.BlockSpec((1, d), const),
            pl.BlockSpec((N_EXPERTS, d), const),
            pl.BlockSpec((N_EXPERTS, 1), const),
        ],
        out_specs=[
            pl.BlockSpec((tm, d), row),
            pl.BlockSpec((tm, 1, d), lambda i: (i, 0, 0)),
            pl.BlockSpec((TOP_K, tm), col),
            pl.BlockSpec((TOP_K, tm), col),
        ],
        out_shape=[
            jax.ShapeDtypeStruct((n, d), F32),
            jax.ShapeDtypeStruct((n, 1, d), F32),
            jax.ShapeDtypeStruct((TOP_K, n), jnp.int32),
            jax.ShapeDtypeStruct((TOP_K, n), F32),
        ],
        compiler_params=_cparams(("parallel",)),
        name="outproj_router",
    )(x2, yr, yf, wo_r, wo_f, g, rwt, rb)


def _w1_split_kernel(w_ref, perm_ref, g_ref, l_ref):
    half = MXU_WIDTH // 2
    perm = perm_ref[...]
    for grp in range(w_ref.shape[2] // MXU_WIDTH):
        blk = w_ref[0, :, grp * MXU_WIDTH:(grp + 1) * MXU_WIDTH].astype(BF16)
        r = _dot(blk, perm)
        g_ref[0, :, grp * half:(grp + 1) * half] = r[:, :half].astype(BF16)
        l_ref[0, :, grp * half:(grp + 1) * half] = r[:, half:].astype(BF16)


def _w1_split(w1, tr):
    e, d, two_f = w1.shape
    half = MXU_WIDTH // 2
    src = jnp.arange(MXU_WIDTH, dtype=jnp.int32)[:, None]
    dst = jnp.arange(MXU_WIDTH, dtype=jnp.int32)[None, :]
    perm = (src == jnp.where(dst < half, 2 * dst, 2 * (dst - half) + 1)).astype(BF16)
    out_sds = jax.ShapeDtypeStruct((e, d, two_f // 2), BF16)
    return pl.pallas_call(
        _w1_split_kernel,
        grid=(e, d // tr),
        in_specs=[
            pl.BlockSpec((1, tr, two_f), lambda i, j: (i, j, 0)),
            pl.BlockSpec((MXU_WIDTH, MXU_WIDTH), lambda i, j: (0, 0)),
        ],
        out_specs=[pl.BlockSpec((1, tr, two_f // 2), lambda i, j: (i, j, 0))] * 2,
        out_shape=[out_sds, out_sds],
        compiler_params=_cparams(("parallel", "parallel")),
        name="w1_split",
    )(w1, perm)


def _expert_kernel(be_ref, tok_a_ref, tok_b_ref, tok_a_next_ref, dst_b_prev_ref, dst_a_ref,
                   dst_b_ref, h_hbm, w1g_a, w1l_a, b1g_a, b1l_a, w2_a, b2_a,
                   w1g_b, w1l_b, b1g_b, b1l_b, w2_b, b2_b,
                   y_hbm, xbuf_a, xbuf_b, obuf_a, obuf_b, xrows, gsem, osem, *, bm, n_real_rows):
    del be_ref
    i = pl.program_id(0)

    def gather_start(idx_ref, xbuf, sem, r):
        pltpu.make_async_copy(h_hbm.at[idx_ref[0, 0, r]], xbuf.at[r], sem).start()

    def scatter_start(idx_ref, obuf, sem, r):
        pltpu.make_async_copy(obuf.at[r], y_hbm.at[idx_ref[0, 0, r]], sem).start()

    def rows_wait(buf, sem):
        pltpu.make_async_copy(buf, buf, sem).wait()

    def mlp(xbuf, w1g_ref, w1l_ref, b1g_ref, b1l_ref, w2_ref, b2_ref, start_copies):
        xrows[...] = xbuf[:, 0, :]
        xb = xrows[...].astype(BF16)
        dff = w1g_ref.shape[2]
        n_pieces = dff // MXU_WIDTH
        rows_per_piece = bm // n_pieces
        acts = []
        for piece in range(n_pieces):
            for r in range(piece * rows_per_piece, (piece + 1) * rows_per_piece):
                start_copies(r)
            cs = slice(piece * MXU_WIDTH, (piece + 1) * MXU_WIDTH)
            glu = _dot(xb, w1g_ref[0, :, cs]) + b1g_ref[0, :, cs]
            lin = _dot(xb, w1l_ref[0, :, cs]) + b1l_ref[0, :, cs]
            glu = jnp.minimum(glu, SWIGLU_LIMIT)
            lin = jnp.clip(lin, -SWIGLU_LIMIT, SWIGLU_LIMIT)
            acts.append((glu * _sigmoid(SWIGLU_ALPHA * glu) * (lin + 1.0)).astype(BF16))
        return _dot(jnp.concatenate(acts, axis=1), w2_ref[0]) + b2_ref[0]

    @pl.when(i == 0)
    def _():
        obuf_a[...] = jnp.zeros_like(obuf_a)
        obuf_b[...] = jnp.zeros_like(obuf_b)

        def first(r, carry):
            pltpu.make_async_copy(obuf_a.at[r], y_hbm.at[n_real_rows + 2 * bm + r],
                                  osem.at[0]).start()
            gather_start(tok_a_ref, xbuf_a, gsem.at[0], r)
            return carry

        lax.fori_loop(0, bm, first, 0)

    def copies_during_a(r):
        gather_start(tok_b_ref, xbuf_b, gsem.at[1], r)
        scatter_start(dst_b_prev_ref, obuf_b, osem.at[1], r)

    def copies_during_b(r):
        gather_start(tok_a_next_ref, xbuf_a, gsem.at[0], r)
        scatter_start(dst_a_ref, obuf_a, osem.at[0], r)

    rows_wait(xbuf_a, gsem.at[0])
    out_a = mlp(xbuf_a, w1g_a, w1l_a, b1g_a, b1l_a, w2_a, b2_a, copies_during_a)
    rows_wait(obuf_a, osem.at[0])
    obuf_a[:, 0, :] = out_a

    rows_wait(xbuf_b, gsem.at[1])
    out_b = mlp(xbuf_b, w1g_b, w1l_b, b1g_b, b1l_b, w2_b, b2_b, copies_during_b)
    rows_wait(obuf_b, osem.at[1])
    obuf_b[:, 0, :] = out_b

    @pl.when(i == pl.num_programs(0) - 1)
    def _():
        def last(r, carry):
            scatter_start(dst_b_ref, obuf_b, osem.at[1], r)
            return carry

        lax.fori_loop(0, bm, last, 0)
        rows_wait(obuf_b, osem.at[1])
        rows_wait(obuf_a, osem.at[0])
        rows_wait(xbuf_a, gsem.at[0])


def _expert_mlp(block_e, tok_blocks, dst_blocks, h2, w1g, w1l, b1g, b1l, w2, b2, bm):
    n_blocks = tok_blocks.shape[0]
    assert n_blocks % 2 == 0
    n, _, d = h2.shape
    dff = w1g.shape[2]
    n_real_rows = TOP_K * n
    idx_spec = lambda fn: pl.BlockSpec((1, 1, bm), fn, memory_space=pltpu.SMEM)

    def weight_specs(which):
        wmap = lambda i, be: (be[2 * i + which], 0, 0)
        return [
            pl.BlockSpec((1, d, dff), wmap),
            pl.BlockSpec((1, d, dff), wmap),
            pl.BlockSpec((1, 1, dff), wmap),
            pl.BlockSpec((1, 1, dff), wmap),
            pl.BlockSpec((1, dff, d), wmap),
            pl.BlockSpec((1, 1, d), wmap),
        ]

    grid_spec = pltpu.PrefetchScalarGridSpec(
        num_scalar_prefetch=1,
        grid=(n_blocks // 2,),
        in_specs=[
            idx_spec(lambda i, be: (2 * i, 0, 0)),
            idx_spec(lambda i, be: (2 * i + 1, 0, 0)),
            idx_spec(lambda i, be: (jnp.minimum(2 * i + 2, n_blocks - 1), 0, 0)),
            idx_spec(lambda i, be: (2 * i, 0, 0)),
            idx_spec(lambda i, be: (2 * i + 1, 0, 0)),
            idx_spec(lambda i, be: (2 * i + 2, 0, 0)),
            pl.BlockSpec(memory_space=pl.ANY),
        ] + weight_specs(0) + weight_specs(1),
        out_specs=pl.BlockSpec(memory_space=pl.ANY),
        scratch_shapes=[
            pltpu.VMEM((bm, 1, d), F32),
            pltpu.VMEM((bm, 1, d), F32),
            pltpu.VMEM((bm, 1, d), F32),
            pltpu.VMEM((bm, 1, d), F32),
            pltpu.VMEM((bm, d), F32),
            pltpu.SemaphoreType.DMA((2,)),
            pltpu.SemaphoreType.DMA((2,)),
        ],
    )
    weights = (w1g, w1l, b1g, b1l, w2, b2)
    return pl.pallas_call(
        functools.partial(_expert_kernel, bm=bm, n_real_rows=n_real_rows),
        grid_spec=grid_spec,
        out_shape=jax.ShapeDtypeStruct((n_real_rows + 3 * bm, 1, d), F32),
        compiler_params=_cparams(("arbitrary",)),
        name="expert_mlp",
    )(block_e, tok_blocks, tok_blocks, tok_blocks, dst_blocks, dst_blocks, dst_blocks, h2,
      *weights, *weights)


def _combine_kernel(gate_ref, x1_ref, g_ref, y0_ref, y1_ref, y2_ref, y3_ref, o_ref):
    gates = gate_ref[...]
    y = x1_ref[...]
    for kk, y_ref in enumerate((y0_ref, y1_ref, y2_ref, y3_ref)):
        y = y + y_ref[:, 0, :] * gates[:, kk:kk + 1]
    o_ref[...] = y * lax.rsqrt(jnp.mean(y * y, axis=-1, keepdims=True) + RMS_EPS) * g_ref[...]


def _combine(gates_t, x1, g, y_all, tc):
    n, d = x1.shape
    tiles = n // tc
    y_spec = lambda kk: pl.BlockSpec((tc, 1, d), lambda i: (kk * tiles + i, 0, 0))
    return pl.pallas_call(
        _combine_kernel,
        grid=(tiles,),
        in_specs=[
            pl.BlockSpec((tc, TOP_K), lambda i: (i, 0)),
            pl.BlockSpec((tc, d), lambda i: (i, 0)),
            pl.BlockSpec((1, d), lambda i: (0, 0)),
        ] + [y_spec(kk) for kk in range(TOP_K)],
        out_specs=pl.BlockSpec((tc, d), lambda i: (i, 0)),
        out_shape=jax.ShapeDtypeStruct((n, d), F32),
        compiler_params=_cparams(("parallel",)),
        name="combine",
    )(gates_t, x1, g, y_all, y_all, y_all, y_all)


def _dispatch_plan(idx, bm):
    n = idx.shape[1]
    n_slots = TOP_K * n
    e_flat = idx.reshape(-1)
    onehot = (e_flat[:, None] == jnp.arange(N_EXPERTS, dtype=jnp.int32)[None, :]).astype(jnp.int32)
    csum = jnp.cumsum(onehot, axis=0)
    rank = jnp.sum(csum * onehot, axis=1) - 1
    sizes = csum[-1]
    padded = (sizes + bm - 1) // bm * bm
    pad_ends = jnp.cumsum(padded)
    pad_starts = pad_ends - padded
    pos = (pad_starts[e_flat] + rank).astype(jnp.int32)
    n_pad = n_slots + N_EXPERTS * bm
    n_blocks = n_pad // bm
    slot_src = jnp.full((n_pad,), -1, jnp.int32).at[pos].set(
        jnp.arange(n_slots, dtype=jnp.int32), unique_indices=True)
    p = jnp.arange(n_pad, dtype=jnp.int32)
    spare = n_slots + (p // bm) % 2 * bm + p % bm
    real = slot_src >= 0
    tok_blocks = jnp.where(real, slot_src % n, 0).reshape(n_blocks, 1, bm)
    dst = jnp.where(real, slot_src, spare)
    dst_blocks = jnp.concatenate([spare[bm:2 * bm], dst]).reshape(n_blocks + 1, 1, bm)
    block_start = jnp.arange(n_blocks, dtype=jnp.int32) * bm
    block_e = jnp.minimum(jnp.searchsorted(pad_ends, block_start, side='right'),
                          N_EXPERTS - 1).astype(jnp.int32)
    return tok_blocks, dst_blocks, block_e


def _block_diag_ones():
    hid = jnp.arange(GROUP_WIDTH, dtype=jnp.int32) // HEAD_DIM
    return (hid[:, None] == hid[None, :]).astype(BF16)


def _pick(n, pref):
    return pref if n % pref == 0 else n


def kernel(x, attn_norm_g, w_in, rwkv_mu, rwkv_w0, rwkv_w_up, rwkv_a0, rwkv_a_up, rwkv_g_up,
           rwkv_k_k, rwkv_k_a, rwkv_r_k, rwkv_ln_w, rwkv_ln_b, fox_f_bias, fox_out_g, w_out,
           ffn_norm_g, router_w, router_b, expert_w1, expert_b1, expert_w2, expert_b2,
           final_norm_g):
    bsz, t, d = x.shape
    n = bsz * t
    depth = w_in.shape[0]
    assert depth == 1, "the final norm is fused into the last stage of a single layer"
    bd = _block_diag_ones()
    x2 = x.reshape(n, d)
    for l in range(depth):
        w_l = w_in[l]
        w_r = w_l[:, :RWKV_IN].astype(BF16)
        w_qkv = w_l[:, RWKV_IN:RWKV_IN + 3 * GROUP_WIDTH].astype(BF16)
        w_qt = w_qkv[:, :GROUP_WIDTH].T
        w_k = w_qkv[:, GROUP_WIDTH:2 * GROUP_WIDTH]
        w_vt = w_qkv[:, 2 * GROUP_WIDTH:].T
        w_f = jnp.pad(w_l[:, RWKV_IN + 3 * GROUP_WIDTH:], ((0, 0), (0, LANES - N_HEADS))).astype(BF16)
        fb_pad = jnp.pad(fox_f_bias[l], (0, LANES - N_HEADS)).reshape(1, LANES)
        wup_pad = jnp.pad(rwkv_w_up[l], ((0, LANES - DECAY_LORA), (0, 0))).astype(BF16)
        aup_pad = jnp.pad(rwkv_a_up[l], ((DECAY_LORA, 0), (0, 0))).astype(BF16)
        gup = rwkv_g_up[l].astype(BF16)
        vec = lambda a: a.reshape(1, -1)

        u_r, qt5, k, vt5, fl = _inproj(x2, vec(attn_norm_g[l]), w_r, w_qt, w_k, w_vt, w_f,
                                       bsz, ATTN_TILE)
        cext, cend = _fox_gate(fl.reshape(bsz, t, LANES), fb_pad, _gate_piece_selectors(),
                               ATTN_TILE)
        r_, k_, v_, lw, kk, b_, g_ = _rwkv_prep(
            u_r.reshape(bsz, t, RWKV_IN), vec(rwkv_mu[l]), vec(rwkv_w0[l]), wup_pad,
            vec(rwkv_a0[l]), aup_pad, gup, vec(rwkv_k_k[l]), vec(rwkv_k_a[l]), bd, _pick(t, 256))
        y_rwkv = _rwkv_scan(r_, k_, v_, lw, kk, b_, g_, vec(rwkv_r_k[l]), vec(rwkv_ln_w[l]),
                            vec(rwkv_ln_b[l]), bd)
        y_fox = _fox_attn(qt5, k.reshape(bsz, t, GROUP_WIDTH), cext, vt5, cend,
                          vec(fox_out_g[l]), ATTN_TILE)

        wo = w_out[l].astype(BF16)
        x1, h2, idx, gates = _outproj_router(
            x2, y_rwkv.reshape(n, GROUP_WIDTH), y_fox.reshape(n, GROUP_WIDTH),
            wo[:GROUP_WIDTH], wo[GROUP_WIDTH:], vec(ffn_norm_g[l]),
            router_w[l].T, router_b[l].reshape(N_EXPERTS, 1), _pick(n, 256))

        bm = 256
        tok_blocks, dst_blocks, block_e = _dispatch_plan(idx, bm)
        w1g, w1l = _w1_split(expert_w1[l], 512)
        b1 = expert_b1[l]
        b1g = b1[:, None, 0::2]
        b1l = b1[:, None, 1::2]
        y_all = _expert_mlp(block_e, tok_blocks, dst_blocks, h2, w1g, w1l, b1g, b1l,
                            expert_w2[l].astype(BF16), expert_b2[l][:, None, :], bm)
        x2 = _combine(gates.T, x1, vec(final_norm_g), y_all, _pick(n, 256))
    return x2.reshape(bsz, t, d)
```

```python
import functools

import jax
import jax.numpy as jnp
from jax import lax
from jax.experimental import pallas as pl
from jax.experimental.pallas import tpu as pltpu

F32 = jnp.float32
BF16 = jnp.bfloat16

HEAD_DIM = 64
N_HEADS = 8
GROUP_WIDTH = N_HEADS * HEAD_DIM
DECAY_LORA = 64
AAA_LORA = 64
GATE_LORA = 128
RWKV_IN = 3 * GROUP_WIDTH + DECAY_LORA + AAA_LORA + GATE_LORA
LORA_OFF = 3 * GROUP_WIDTH
N_EXPERTS = 32
TOP_K = 4
SWIGLU_ALPHA = 1.702
SWIGLU_LIMIT = 7.0
RMS_EPS = 1e-5
RWKV_GN_EPS = 64e-5
LANES = 128
RWKV_CHUNK = 64
ATTN_TILE = 512
MXU_WIDTH = 256
LOG2E = 1.4426950408889634
Q_SCALE = HEAD_DIM ** -0.5 * LOG2E
ZERO_PROB_EXP = -152.0
NORM_SLACK = 1.0 + 2.0 ** -6
VMEM_LIMIT = 56 * 1024 * 1024


def _cparams(semantics):
    return pltpu.CompilerParams(dimension_semantics=semantics, vmem_limit_bytes=VMEM_LIMIT)


def _dot(a, b):
    return jnp.dot(a, b, preferred_element_type=F32)


def _dot_nt(a, b):
    return lax.dot_general(a, b, (((1,), (1,)), ((), ())), preferred_element_type=F32)


def _dot_tn(a, b):
    return lax.dot_general(a, b, (((0,), (0,)), ((), ())), preferred_element_type=F32)


def _split3(x):
    hi = x.astype(BF16)
    r1 = x - hi.astype(F32)
    mid = r1.astype(BF16)
    lo = (r1 - mid.astype(F32)).astype(BF16)
    return hi, mid, lo


def _dot_exact_lhs(a_bf16, x):
    hi, mid, lo = _split3(x)
    return _dot(a_bf16, hi) + _dot(a_bf16, mid) + _dot(a_bf16, lo)


def _dot_exact_rhs(x, b_bf16):
    hi, mid, lo = _split3(x)
    return _dot(hi, b_bf16) + _dot(mid, b_bf16) + _dot(lo, b_bf16)


def _softplus(z):
    return jnp.maximum(z, 0.0) + jnp.log1p(jnp.exp(-jnp.abs(z)))


def _sigmoid(z):
    return 1.0 / (1.0 + jnp.exp(-z))


def _inproj_kernel(x_ref, g_ref, wr_ref, wqt_ref, wk_ref, wvt_ref, wf_ref,
                   ur_ref, qt_ref, k_ref, vt_ref, fl_ref):
    x = x_ref[...]
    h = x * lax.rsqrt(jnp.mean(x * x, axis=-1, keepdims=True) + RMS_EPS) * g_ref[...]
    hb = h.astype(BF16)
    ur_ref[...] = _dot(hb, wr_ref[...])
    k_ref[...] = _dot(hb, wk_ref[...]).astype(BF16)
    fl_ref[...] = _dot(hb, wf_ref[...])
    qt = (_dot_nt(wqt_ref[...], hb) * Q_SCALE).astype(BF16)
    vt = _dot_nt(wvt_ref[...], hb).astype(BF16)
    for p in range(N_HEADS // 2):
        qt_ref[0, p, 0] = qt[p * LANES:(p + 1) * LANES]
        vt_ref[0, p, 0] = vt[p * LANES:(p + 1) * LANES]


def _inproj(x2, g, w_r, w_qt, w_k, w_vt, w_f, bsz, tm):
    n, d = x2.shape
    nt = n // bsz // tm
    pairs = N_HEADS // 2
    const = lambda i: (0, 0)
    row = lambda i: (i, 0)
    fm = lambda i: (i // nt, 0, i % nt, 0, 0)
    fm_sds = jax.ShapeDtypeStruct((bsz, pairs, nt, LANES, tm), BF16)
    return pl.pallas_call(
        _inproj_kernel,
        grid=(n // tm,),
        in_specs=[
            pl.BlockSpec((tm, d), row),
            pl.BlockSpec((1, d), const),
            pl.BlockSpec(w_r.shape, const),
            pl.BlockSpec(w_qt.shape, const),
            pl.BlockSpec(w_k.shape, const),
            pl.BlockSpec(w_vt.shape, const),
            pl.BlockSpec(w_f.shape, const),
        ],
        out_specs=[
            pl.BlockSpec((tm, RWKV_IN), row),
            pl.BlockSpec((1, pairs, 1, LANES, tm), fm),
            pl.BlockSpec((tm, GROUP_WIDTH), row),
            pl.BlockSpec((1, pairs, 1, LANES, tm), fm),
            pl.BlockSpec((tm, LANES), row),
        ],
        out_shape=[
            jax.ShapeDtypeStruct((n, RWKV_IN), F32),
            fm_sds,
            jax.ShapeDtypeStruct((n, GROUP_WIDTH), BF16),
            fm_sds,
            jax.ShapeDtypeStruct((n, LANES), F32),
        ],
        compiler_params=_cparams(("parallel",)),
        name="inproj",
    )(x2, g, w_r, w_qt, w_k, w_vt, w_f)


def _fox_gate_kernel(fl_ref, fb_ref, sel_ref, c_ref, cend_ref, carry):
    tt = fl_ref.shape[1]

    @pl.when(pl.program_id(1) == 0)
    def _():
        carry[...] = jnp.zeros_like(carry)

    z = fl_ref[0] + fb_ref[...]
    log_f = jnp.minimum(z, 0.0) - jnp.log1p(jnp.exp(-jnp.abs(z)))
    ri = lax.broadcasted_iota(jnp.int32, (tt, tt), 0)
    ci = lax.broadcasted_iota(jnp.int32, (tt, tt), 1)
    tri = jnp.where(ri >= ci, 1.0, 0.0).astype(BF16)
    c = _dot_exact_lhs(tri, log_f) + carry[...]
    carry[...] = c[tt - 1:tt, :]
    cend_ref[0, 0] = c[tt - 1:tt, :]
    hi, mid, lo = _split3(c * LOG2E)
    c_ref[0] = (_dot(hi, sel_ref[0]) + _dot(mid, sel_ref[1]) + _dot(lo, sel_ref[2])).astype(BF16)


def _gate_piece_selectors():
    h = jnp.arange(LANES, dtype=jnp.int32)[:, None]
    col = jnp.arange(GROUP_WIDTH, dtype=jnp.int32)[None, :]
    sels = []
    for m in range(3):
        target = LANES * (h // 2) + 3 * (h % 2) + m
        sels.append(((col == target) & (h < N_HEADS)).astype(BF16))
    return jnp.stack(sels)


def _fox_gate(fl3, fb_pad, sel, tt):
    b, t, _ = fl3.shape
    return pl.pallas_call(
        _fox_gate_kernel,
        grid=(b, t // tt),
        in_specs=[
            pl.BlockSpec((1, tt, LANES), lambda i, j: (i, j, 0)),
            pl.BlockSpec((1, LANES), lambda i, j: (0, 0)),
            pl.BlockSpec((3, LANES, GROUP_WIDTH), lambda i, j: (0, 0, 0)),
        ],
        out_specs=[
            pl.BlockSpec((1, tt, GROUP_WIDTH), lambda i, j: (i, j, 0)),
            pl.BlockSpec((1, 1, 1, LANES), lambda i, j: (i, j, 0, 0)),
        ],
        out_shape=[
            jax.ShapeDtypeStruct((b, t, GROUP_WIDTH), BF16),
            jax.ShapeDtypeStruct((b, t // tt, 1, LANES), F32),
        ],
        scratch_shapes=[pltpu.VMEM((1, LANES), F32)],
        compiler_params=_cparams(("parallel", "arbitrary")),
        name="fox_gate",
    )(fl3, fb_pad, sel)


def _rwkv_prep_kernel(u_ref, mu_ref, w0_ref, wup_ref, a0_ref, aup_ref, gup_ref, kk_ref, ka_ref,
                      bd_ref, r_out, k_out, v_out, lw_out, kk_out, b_out, g_out, carry):
    tt = u_ref.shape[1]

    @pl.when(pl.program_id(1) == 0)
    def _():
        carry[...] = jnp.zeros_like(carry)

    u = u_ref[0]
    prev = pltpu.roll(u, 1, axis=0)
    row = lax.broadcasted_iota(jnp.int32, u.shape, 0)
    prev = jnp.where(row == 0, carry[...], prev)
    carry[...] = u[tt - 1:tt, :]
    us = u + (prev - u) * mu_ref[...]

    r = us[:, :GROUP_WIDTH]
    k = us[:, GROUP_WIDTH:2 * GROUP_WIDTH]
    v = us[:, 2 * GROUP_WIDTH:LORA_OFF]
    wa = us[:, LORA_OFF:LORA_OFF + LANES]
    gl = us[:, LORA_OFF + LANES:]

    w_lin = _dot(jnp.tanh(wa).astype(BF16), wup_ref[...])
    a_lin = _dot(wa.astype(BF16), aup_ref[...])
    w = -_softplus(-(w0_ref[...] + w_lin)) - 0.5
    lw_out[0] = -jnp.exp(w)
    a = _sigmoid(a0_ref[...] + a_lin)
    g_out[0] = _dot(_sigmoid(gl).astype(BF16), gup_ref[...])

    kkr = k * kk_ref[...]
    ss = _dot_exact_rhs(kkr * kkr, bd_ref[...])
    kk = kkr / jnp.maximum(jnp.sqrt(ss), 1e-12)
    r_out[0] = r
    k_out[0] = k * (1.0 + (a - 1.0) * ka_ref[...])
    v_out[0] = v
    kk_out[0] = kk
    b_out[0] = kk * a


def _rwkv_prep(u3, mu, w0, wup_pad, a0, aup_pad, gup, k_k, k_a, bd, tt):
    b, t, _ = u3.shape
    const = lambda i, j: (0, 0)
    tile = lambda i, j: (i, j, 0)
    out_sds = jax.ShapeDtypeStruct((b, t, GROUP_WIDTH), F32)
    vec = pl.BlockSpec((1, GROUP_WIDTH), const)
    return pl.pallas_call(
        _rwkv_prep_kernel,
        grid=(b, t // tt),
        in_specs=[
            pl.BlockSpec((1, tt, RWKV_IN), tile),
            pl.BlockSpec((1, RWKV_IN), const),
            vec,
            pl.BlockSpec((LANES, GROUP_WIDTH), const),
            vec,
            pl.BlockSpec((LANES, GROUP_WIDTH), const),
            pl.BlockSpec((GATE_LORA, GROUP_WIDTH), const),
            vec,
            vec,
            pl.BlockSpec((GROUP_WIDTH, GROUP_WIDTH), const),
        ],
        out_specs=[pl.BlockSpec((1, tt, GROUP_WIDTH), tile)] * 7,
        out_shape=[out_sds] * 7,
        scratch_shapes=[pltpu.VMEM((1, RWKV_IN), F32)],
        compiler_params=_cparams(("parallel", "arbitrary")),
        name="rwkv_prep",
    )(u3, mu, w0, wup_pad, a0, aup_pad, gup, k_k, k_a, bd)


def _unit_lower_inverses(l_stricts, level_masks, eye):
    ts = [eye - jnp.where(level_masks[0], l, 0.0) for l in l_stricts]
    for m in level_masks[1:]:
        tbs = [t.astype(BF16) for t in ts]
        cts = [_dot(jnp.where(m, l, 0.0).astype(BF16), tb) for l, tb in zip(l_stricts, tbs)]
        ts = [t - _dot(tb, ct.astype(BF16)) for t, tb, ct in zip(ts, tbs, cts)]
    return ts


def _rwkv_scan_kernel(r_ref, k_ref, v_ref, lw_ref, kk_ref, b_ref, g_ref, rk_ref, lnw_ref, lnb_ref,
                      bd_ref, o_ref, s_scr):
    c = RWKV_CHUNK

    @pl.when(pl.program_id(1) == 0)
    def _():
        s_scr[...] = jnp.zeros_like(s_scr)

    r = r_ref[0]
    k = k_ref[0]
    v = v_ref[0]
    lw = lw_ref[0]
    kk = kk_ref[0]
    b = b_ref[0]

    ri = lax.broadcasted_iota(jnp.int32, (c, c), 0)
    ci = lax.broadcasted_iota(jnp.int32, (c, c), 1)
    incl = ri >= ci
    strict = ri > ci
    eye = jnp.where(ri == ci, 1.0, 0.0)
    level_masks = []
    s = 1
    while s < c:
        same = (ri // (2 * s)) == (ci // (2 * s))
        level_masks.append(same & ((ri % (2 * s)) >= s) & ((ci % (2 * s)) < s))
        s *= 2
    ri2 = lax.broadcasted_iota(jnp.int32, (c, 2 * c), 0)
    ci2 = lax.broadcasted_iota(jnp.int32, (c, 2 * c), 1) % c
    incl2 = ri2 >= ci2

    g_cum = _dot_exact_lhs(jnp.where(incl, 1.0, 0.0).astype(BF16), lw)
    g_last = g_cum[c - 1:c, :]
    r_t = (r * jnp.exp(g_cum)).astype(BF16)
    kk_t = (kk * jnp.exp(g_cum - lw)).astype(BF16)
    e_neg = jnp.exp(-g_cum)
    b_n = (b * e_neg).astype(BF16)
    k_n = (k * e_neg).astype(BF16)
    e_end = jnp.exp(g_last - g_cum)
    b_e = (b * e_end).astype(BF16)
    k_e = (k * e_end).astype(BF16)
    gamma = jnp.exp(g_last)
    vb = v.astype(BF16)

    heads = range(N_HEADS)
    sls = [slice(h * HEAD_DIM, (h + 1) * HEAD_DIM) for h in heads]
    s_old = [s_scr[h] for h in heads]
    lhs = [jnp.concatenate([r_t[:, sl], kk_t[:, sl]], axis=0) for sl in sls]
    rhs = [jnp.concatenate([b_n[:, sl], k_n[:, sl]], axis=0) for sl in sls]
    vh = [vb[:, sl] for sl in sls]
    p = [_dot_nt(lhs[h], rhs[h]) for h in heads]
    q0 = [_dot_nt(lhs[h], s_old[h].astype(BF16)) for h in heads]
    l_b = [jnp.where(strict, p[h][c:, :c], 0.0) for h in heads]
    l_k = [jnp.where(strict, p[h][c:, c:], 0.0).astype(BF16) for h in heads]
    z = [q0[h][c:] + _dot(l_k[h], vh[h]) for h in heads]
    t_inv = _unit_lower_inverses(l_b, level_masks, eye)
    u = [-_dot(t_inv[h].astype(BF16), z[h].astype(BF16)) for h in heads]
    w_cat = [jnp.concatenate([u[h].astype(BF16), vh[h]], axis=0) for h in heads]
    p_r = [jnp.where(incl2, p[h][:c], 0.0).astype(BF16) for h in heads]
    ys = [q0[h][:c] + _dot(p_r[h], w_cat[h]) for h in heads]
    x_cat = [jnp.concatenate([b_e[:, sl], k_e[:, sl]], axis=0) for sl in sls]
    s_new = [s_old[h] * gamma[:, sls[h]] + _dot_tn(w_cat[h], x_cat[h]) for h in heads]
    for h in heads:
        s_scr[h] = s_new[h]

    y = jnp.concatenate(ys, axis=1)
    bd = bd_ref[...]
    inv_n = 1.0 / HEAD_DIM
    mean = _dot_exact_rhs(y, bd) * inv_n
    d = y - mean
    var = _dot_exact_rhs(d * d, bd) * inv_n
    yn = d * lax.rsqrt(var + RWKV_GN_EPS) * lnw_ref[...] + lnb_ref[...]
    bonus = _dot_exact_rhs(r * k * rk_ref[...], bd) * v
    o_ref[0] = ((yn + bonus) * g_ref[0]).astype(o_ref.dtype)


def _rwkv_scan(r, k, v, lw, kk, b, g, r_k, ln_w, ln_b, bd):
    bsz, t, _ = r.shape
    c = RWKV_CHUNK
    const = lambda i, j: (0, 0)
    tile = pl.BlockSpec((1, c, GROUP_WIDTH), lambda i, j: (i, j, 0))
    vec = pl.BlockSpec((1, GROUP_WIDTH), const)
    return pl.pallas_call(
        _rwkv_scan_kernel,
        grid=(bsz, t // c),
        in_specs=[tile] * 7 + [vec, vec, vec, pl.BlockSpec((GROUP_WIDTH, GROUP_WIDTH), const)],
        out_specs=tile,
        out_shape=jax.ShapeDtypeStruct((bsz, t, GROUP_WIDTH), BF16),
        scratch_shapes=[pltpu.VMEM((N_HEADS, HEAD_DIM, HEAD_DIM), F32)],
        compiler_params=_cparams(("parallel", "arbitrary")),
        name="rwkv_scan",
    )(r, k, v, lw, kk, b, g, r_k, ln_w, ln_b, bd)


def _fox_attn_kernel(qt_ref, k_ref, ce_ref, vt_ref, cend_ref, og_ref, o_ref,
                     m_scr, l_scr, acc_scr, kmax_scr, *, t):
    qi = pl.program_id(2)
    n_strips = 2 * t // LANES
    qt = qt_ref[0, 0, 0]
    frow = lax.broadcasted_iota(jnp.int32, (LANES, t), 0)
    zero = jnp.zeros_like(qt)
    main = jnp.concatenate([jnp.where(frow < HEAD_DIM, qt, zero),
                            jnp.where(frow < HEAD_DIM, zero, qt)], axis=1)
    erow = lax.broadcasted_iota(jnp.int32, (LANES, 2 * t), 0)
    ecol = lax.broadcasted_iota(jnp.int32, (LANES, 2 * t), 1)
    off = jnp.where(ecol < t, 0, 3)
    extra = jnp.where((erow >= off) & (erow < off + 3), -1.0, 0.0).astype(BF16)
    q_aug = jnp.concatenate([main, extra], axis=0)

    m_scr[...] = jnp.full_like(m_scr, -jnp.inf)
    l_scr[...] = jnp.zeros_like(l_scr)
    acc_scr[...] = jnp.zeros_like(acc_scr)

    @pl.when(qi == 0)
    def _():
        hid_r = lax.broadcasted_iota(jnp.int32, (LANES, LANES), 0) // HEAD_DIM
        hid_c = lax.broadcasted_iota(jnp.int32, (LANES, LANES), 1) // HEAD_DIM
        same_head = jnp.where(hid_r == hid_c, 1.0, 0.0).astype(BF16)

        def tile_max(j, best):
            kf = k_ref[0, pl.ds(pl.multiple_of(j * t, t), t), :].astype(F32)
            sq = _dot((kf * kf).astype(BF16), same_head)
            return jnp.maximum(best, jnp.max(sq, axis=0, keepdims=True))

        best = lax.fori_loop(0, k_ref.shape[1] // t, tile_max, jnp.zeros((1, LANES), F32))
        kmax_scr[...] = jnp.sqrt(best) * NORM_SLACK

    def step(j, masked):
        start = pl.multiple_of(j * t, t)
        k_aug = jnp.concatenate([k_ref[0, pl.ds(start, t), :], ce_ref[0, pl.ds(start, t), :]],
                                axis=1)
        vt = vt_ref[0, 0, j]
        zt = _dot(k_aug, q_aug)
        m_prev = m_scr[...]
        l_prev = l_scr[...]
        acc_prev = acc_scr[...]
        m_out, l_out, acc_out = [], [], [[], []]
        for s in range(n_strips):
            head = s // (n_strips // 2)
            cs = slice(s * LANES, (s + 1) * LANES)
            z = zt[:, cs]
            if masked:
                key = lax.broadcasted_iota(jnp.int32, (t, LANES), 0)
                qry = lax.broadcasted_iota(jnp.int32, (t, LANES), 1) + (s * LANES) % t
                z = jnp.where(key <= qry, z, -jnp.inf)
            m_new = jnp.maximum(m_prev[:, cs], jnp.max(z, axis=0, keepdims=True))
            alpha = jnp.exp2(m_prev[:, cs] - m_new)
            p = jnp.exp2(z - m_new)
            l_out.append(alpha * l_prev[:, cs] + jnp.sum(p, axis=0, keepdims=True))
            m_out.append(m_new)
            hs = slice(head * HEAD_DIM, (head + 1) * HEAD_DIM)
            qs = slice((s * LANES) % t, (s * LANES) % t + LANES)
            pv = _dot(vt[hs], p.astype(BF16))
            acc_out[head].append(alpha * acc_prev[hs, qs] + pv)
        m_scr[...] = jnp.concatenate(m_out, axis=1)
        l_scr[...] = jnp.concatenate(l_out, axis=1)
        acc_scr[...] = jnp.concatenate([jnp.concatenate(acc_out[0], axis=1),
                                        jnp.concatenate(acc_out[1], axis=1)], axis=0)

    def body(j, carry):
        step(j, False)
        return carry

    step(qi, True)

    qf = qt.astype(F32)
    qsq = qf * qf
    qnorm = jnp.sqrt(jnp.concatenate(
        [jnp.sum(qsq[:HEAD_DIM], axis=0, keepdims=True),
         jnp.sum(qsq[HEAD_DIM:], axis=0, keepdims=True)], axis=1)) * NORM_SLACK
    kmax = kmax_scr[...]
    kmax2 = jnp.concatenate([jnp.broadcast_to(kmax[:, 0:1], (1, t)),
                             jnp.broadcast_to(kmax[:, HEAD_DIM:HEAD_DIM + 1], (1, t))], axis=1)
    slack = qnorm * kmax2 - m_scr[...]
    cend = cend_ref[0][:, 0, :]
    lane = lax.broadcasted_iota(jnp.int32, cend.shape, 1)
    jrow = lax.broadcasted_iota(jnp.int32, (cend.shape[0], 1), 0)
    needed = jrow < 0
    for head in range(2):
        worst = jnp.max(slack[:, head * t:(head + 1) * t], axis=1, keepdims=True)
        c_head = jnp.sum(jnp.where(lane == 2 * pl.program_id(1) + head, cend, 0.0),
                         axis=1, keepdims=True)
        needed = needed | (worst - c_head * LOG2E > ZERO_PROB_EXP)
    first = jnp.min(jnp.where(needed & (jrow < qi), jrow, qi))
    lax.fori_loop(first, qi, body, 0)

    l = l_scr[...]
    acc = acc_scr[...]
    inv_n = 1.0 / HEAD_DIM
    halves = []
    for head in range(2):
        o = acc[head * HEAD_DIM:(head + 1) * HEAD_DIM] / l[:, head * t:(head + 1) * t]
        halves.append(o * lax.rsqrt(jnp.sum(o * o, axis=0, keepdims=True) * inv_n + RMS_EPS))
    o_t = jnp.concatenate(halves, axis=0)
    o_ref[0] = (jnp.transpose(o_t) * og_ref[...]).astype(o_ref.dtype)


def _fox_attn(qt5, k, cext, vt5, cend, out_g, tile):
    bsz, t_all, _ = k.shape
    pairs = N_HEADS // 2
    tiles = t_all // tile
    kern = functools.partial(_fox_attn_kernel, t=tile)
    return pl.pallas_call(
        kern,
        grid=(bsz, pairs, tiles),
        in_specs=[
            pl.BlockSpec((1, 1, 1, LANES, tile), lambda b, p, i: (b, p, i, 0, 0)),
            pl.BlockSpec((1, t_all, LANES), lambda b, p, i: (b, 0, p)),
            pl.BlockSpec((1, t_all, LANES), lambda b, p, i: (b, 0, p)),
            pl.BlockSpec((1, 1, tiles, LANES, tile), lambda b, p, i: (b, p, 0, 0, 0)),
            pl.BlockSpec((1, tiles, 1, LANES), lambda b, p, i: (b, 0, 0, 0)),
            pl.BlockSpec((1, LANES), lambda b, p, i: (0, p)),
        ],
        out_specs=pl.BlockSpec((1, tile, LANES), lambda b, p, i: (b, i, p)),
        out_shape=jax.ShapeDtypeStruct((bsz, t_all, GROUP_WIDTH), BF16),
        scratch_shapes=[
            pltpu.VMEM((1, 2 * tile), F32),
            pltpu.VMEM((1, 2 * tile), F32),
            pltpu.VMEM((LANES, tile), F32),
            pltpu.VMEM((1, LANES), F32),
        ],
        compiler_params=_cparams(("parallel", "parallel", "arbitrary")),
        name="fox_attn",
    )(qt5, k, cext, vt5, cend, out_g)


def _outproj_router_kernel(x_ref, yr_ref, yf_ref, wo_r_ref, wo_f_ref, g_ref, rwt_ref, rb_ref,
                           x1_ref, h_ref, idx_ref, gate_ref):
    x1 = x_ref[...] + _dot(yr_ref[...], wo_r_ref[...]) + _dot(yf_ref[...], wo_f_ref[...])
    x1_ref[...] = x1
    h = x1 * lax.rsqrt(jnp.mean(x1 * x1, axis=-1, keepdims=True) + RMS_EPS) * g_ref[...]
    h_ref[:, 0, :] = h
    logits = lax.dot_general(rwt_ref[...], h, (((1,), (1,)), ((), ())),
                             precision=lax.Precision.HIGHEST,
                             preferred_element_type=F32) + rb_ref[...]
    eidx = lax.broadcasted_iota(jnp.int32, logits.shape, 0)
    vals, idxs = [], []
    for _ in range(TOP_K):
        m = jnp.max(logits, axis=0, keepdims=True)
        i = jnp.min(jnp.where(logits == m, eidx, N_EXPERTS), axis=0, keepdims=True)
        vals.append(m)
        idxs.append(i)
        logits = jnp.where(eidx == i, -jnp.inf, logits)
    es = [jnp.exp(val - vals[0]) for val in vals]
    denom = es[0] + es[1] + es[2] + es[3]
    idx_ref[...] = jnp.concatenate(idxs, axis=0)
    gate_ref[...] = jnp.concatenate([e / denom for e in es], axis=0)


def _outproj_router(x2, yr, yf, wo_r, wo_f, g, rwt, rb, tm):
    n, d = x2.shape
    const = lambda i: (0, 0)
    row = lambda i: (i, 0)
    col = lambda i: (0, i)
    return pl.pallas_call(
        _outproj_router_kernel,
        grid=(n // tm,),
        in_specs=[
            pl.BlockSpec((tm, d), row),
            pl.BlockSpec((tm, GROUP_WIDTH), row),
            pl.BlockSpec((tm, GROUP_WIDTH), row),
            pl.BlockSpec((GROUP_WIDTH, d), const),
            pl.BlockSpec((GROUP_WIDTH, d), const),
            pl.BlockSpec((1, d), const),
            pl.BlockSpec((N_EXPERTS, d), const),
            pl.BlockSpec((N_EXPERTS, 1), const),
        ],
        out_specs=[
            pl.BlockSpec((tm, d), row),
            pl.BlockSpec((tm, 1, d), lambda i: (i, 0, 0)),
            pl.BlockSpec((TOP_K, tm), col),
            pl.BlockSpec((TOP_K, tm), col),
        ],
        out_shape=[
            jax.ShapeDtypeStruct((n, d), F32),
            jax.ShapeDtypeStruct((n, 1, d), F32),
            jax.ShapeDtypeStruct((TOP_K, n), jnp.int32),
            jax.ShapeDtypeStruct((TOP_K, n), F32),
        ],
        compiler_params=_cparams(("parallel",)),
        name="outproj_router",
    )(x2, yr, yf, wo_r, wo_f, g, rwt, rb)


def _w1_split_kernel(w_ref, perm_ref, g_ref, l_ref):
    half = MXU_WIDTH // 2
    perm = perm_ref[...]
    for grp in range(w_ref.shape[2] // MXU_WIDTH):
        blk = w_ref[0, :, grp * MXU_WIDTH:(grp + 1) * MXU_WIDTH].astype(BF16)
        r = _dot(blk, perm)
        g_ref[0, :, grp * half:(grp + 1) * half] = r[:, :half].astype(BF16)
        l_ref[0, :, grp * half:(grp + 1) * half] = r[:, half:].astype(BF16)


def _w1_split(w1, tr):
    e, d, two_f = w1.shape
    half = MXU_WIDTH // 2
    src = jnp.arange(MXU_WIDTH, dtype=jnp.int32)[:, None]
    dst = jnp.arange(MXU_WIDTH, dtype=jnp.int32)[None, :]
    perm = (src == jnp.where(dst < half, 2 * dst, 2 * (dst - half) + 1)).astype(BF16)
    out_sds = jax.ShapeDtypeStruct((e, d, two_f // 2), BF16)
    return pl.pallas_call(
        _w1_split_kernel,
        grid=(e, d // tr),
        in_specs=[
            pl.BlockSpec((1, tr, two_f), lambda i, j: (i, j, 0)),
            pl.BlockSpec((MXU_WIDTH, MXU_WIDTH), lambda i, j: (0, 0)),
        ],
        out_specs=[pl.BlockSpec((1, tr, two_f // 2), lambda i, j: (i, j, 0))] * 2,
        out_shape=[out_sds, out_sds],
        compiler_params=_cparams(("parallel", "parallel")),
        name="w1_split",
    )(w1, perm)


def _expert_kernel(be_ref, tok_a_ref, tok_b_ref, tok_a_next_ref, dst_b_prev_ref, dst_a_ref,
                   dst_b_ref, h_hbm, w1g_a, w1l_a, b1g_a, b1l_a, w2_a, b2_a,
                   w1g_b, w1l_b, b1g_b, b1l_b, w2_b, b2_b,
                   y_hbm, xbuf_a, xbuf_b, obuf_a, obuf_b, xrows, gsem, osem, *, bm, n_real_rows):
    del be_ref
    i = pl.program_id(0)

    def gather_start(idx_ref, xbuf, sem, r):
        pltpu.make_async_copy(h_hbm.at[idx_ref[0, 0, r]], xbuf.at[r], sem).start()

    def scatter_start(idx_ref, obuf, sem, r):
        pltpu.make_async_copy(obuf.at[r], y_hbm.at[idx_ref[0, 0, r]], sem).start()

    def rows_wait(buf, sem):
        pltpu.make_async_copy(buf, buf, sem).wait()

    def mlp(xbuf, w1g_ref, w1l_ref, b1g_ref, b1l_ref, w2_ref, b2_ref):
        xrows[...] = xbuf[:, 0, :]
        xb = xrows[...].astype(BF16)
        dff = w1g_ref.shape[2]
        acts = []
        for piece in range(dff // MXU_WIDTH):
            cs = slice(piece * MXU_WIDTH, (piece + 1) * MXU_WIDTH)
            glu = _dot(xb, w1g_ref[0, :, cs]) + b1g_ref[0, :, cs]
            lin = _dot(xb, w1l_ref[0, :, cs]) + b1l_ref[0, :, cs]
            glu = jnp.minimum(glu, SWIGLU_LIMIT)
            lin = jnp.clip(lin, -SWIGLU_LIMIT, SWIGLU_LIMIT)
            acts.append((glu * _sigmoid(SWIGLU_ALPHA * glu) * (lin + 1.0)).astype(BF16))
        return _dot(jnp.concatenate(acts, axis=1), w2_ref[0]) + b2_ref[0]

    @pl.when(i == 0)
    def _():
        obuf_a[...] = jnp.zeros_like(obuf_a)
        obuf_b[...] = jnp.zeros_like(obuf_b)

        def first(r, carry):
            pltpu.make_async_copy(obuf_a.at[r], y_hbm.at[n_real_rows + 2 * bm + r],
                                  osem.at[0]).start()
            gather_start(tok_a_ref, xbuf_a, gsem.at[0], r)
            return carry

        lax.fori_loop(0, bm, first, 0)

    for r in range(bm):
        gather_start(tok_b_ref, xbuf_b, gsem.at[1], r)
        scatter_start(dst_b_prev_ref, obuf_b, osem.at[1], r)
    rows_wait(xbuf_a, gsem.at[0])
    rows_wait(obuf_a, osem.at[0])
    obuf_a[:, 0, :] = mlp(xbuf_a, w1g_a, w1l_a, b1g_a, b1l_a, w2_a, b2_a)

    for r in range(bm):
        gather_start(tok_a_next_ref, xbuf_a, gsem.at[0], r)
        scatter_start(dst_a_ref, obuf_a, osem.at[0], r)
    rows_wait(xbuf_b, gsem.at[1])
    rows_wait(obuf_b, osem.at[1])
    obuf_b[:, 0, :] = mlp(xbuf_b, w1g_b, w1l_b, b1g_b, b1l_b, w2_b, b2_b)

    @pl.when(i == pl.num_programs(0) - 1)
    def _():
        def last(r, carry):
            scatter_start(dst_b_ref, obuf_b, osem.at[1], r)
            return carry

        lax.fori_loop(0, bm, last, 0)
        rows_wait(obuf_b, osem.at[1])
        rows_wait(obuf_a, osem.at[0])
        rows_wait(xbuf_a, gsem.at[0])


def _expert_mlp(block_e, tok_blocks, dst_blocks, h2, w1g, w1l, b1g, b1l, w2, b2, bm):
    n_blocks = tok_blocks.shape[0]
    assert n_blocks % 2 == 0
    n, _, d = h2.shape
    dff = w1g.shape[2]
    n_real_rows = TOP_K * n
    idx_spec = lambda fn: pl.BlockSpec((1, 1, bm), fn, memory_space=pltpu.SMEM)

    def weight_specs(which):
        wmap = lambda i, be: (be[2 * i + which], 0, 0)
        return [
            pl.BlockSpec((1, d, dff), wmap),
            pl.BlockSpec((1, d, dff), wmap),
            pl.BlockSpec((1, 1, dff), wmap),
            pl.BlockSpec((1, 1, dff), wmap),
            pl.BlockSpec((1, dff, d), wmap),
            pl.BlockSpec((1, 1, d), wmap),
        ]

    grid_spec = pltpu.PrefetchScalarGridSpec(
        num_scalar_prefetch=1,
        grid=(n_blocks // 2,),
        in_specs=[
            idx_spec(lambda i, be: (2 * i, 0, 0)),
            idx_spec(lambda i, be: (2 * i + 1, 0, 0)),
            idx_spec(lambda i, be: (jnp.minimum(2 * i + 2, n_blocks - 1), 0, 0)),
            idx_spec(lambda i, be: (2 * i, 0, 0)),
            idx_spec(lambda i, be: (2 * i + 1, 0, 0)),
            idx_spec(lambda i, be: (2 * i + 2, 0, 0)),
            pl.BlockSpec(memory_space=pl.ANY),
        ] + weight_specs(0) + weight_specs(1),
        out_specs=pl.BlockSpec(memory_space=pl.ANY),
        scratch_shapes=[
            pltpu.VMEM((bm, 1, d), F32),
            pltpu.VMEM((bm, 1, d), F32),
            pltpu.VMEM((bm, 1, d), F32),
            pltpu.VMEM((bm, 1, d), F32),
            pltpu.VMEM((bm, d), F32),
            pltpu.SemaphoreType.DMA((2,)),
            pltpu.SemaphoreType.DMA((2,)),
        ],
    )
    weights = (w1g, w1l, b1g, b1l, w2, b2)
    return pl.pallas_call(
        functools.partial(_expert_kernel, bm=bm, n_real_rows=n_real_rows),
        grid_spec=grid_spec,
        out_shape=jax.ShapeDtypeStruct((n_real_rows + 3 * bm, 1, d), F32),
        compiler_params=_cparams(("arbitrary",)),
        name="expert_mlp",
    )(block_e, tok_blocks, tok_blocks, tok_blocks, dst_blocks, dst_blocks, dst_blocks, h2,
      *weights, *weights)


def _combine_kernel(gate_ref, x1_ref, g_ref, y0_ref, y1_ref, y2_ref, y3_ref, o_ref):
    gates = gate_ref[...]
    y = x1_ref[...]
    for kk, y_ref in enumerate((y0_ref, y1_ref, y2_ref, y3_ref)):
        y = y + y_ref[:, 0, :] * gates[:, kk:kk + 1]
    o_ref[...] = y * lax.rsqrt(jnp.mean(y * y, axis=-1, keepdims=True) + RMS_EPS) * g_ref[...]


def _combine(gates_t, x1, g, y_all, tc):
    n, d = x1.shape
    tiles = n // tc
    y_spec = lambda kk: pl.BlockSpec((tc, 1, d), lambda i: (kk * tiles + i, 0, 0))
    return pl.pallas_call(
        _combine_kernel,
        grid=(tiles,),
        in_specs=[
            pl.BlockSpec((tc, TOP_K), lambda i: (i, 0)),
            pl.BlockSpec((tc, d), lambda i: (i, 0)),
            pl.BlockSpec((1, d), lambda i: (0, 0)),
        ] + [y_spec(kk) for kk in range(TOP_K)],
        out_specs=pl.BlockSpec((tc, d), lambda i: (i, 0)),
        out_shape=jax.ShapeDtypeStruct((n, d), F32),
        compiler_params=_cparams(("parallel",)),
        name="combine",
    )(gates_t, x1, g, y_all, y_all, y_all, y_all)


def _dispatch_plan(idx, bm):
    n = idx.shape[1]
    n_slots = TOP_K * n
    e_flat = idx.reshape(-1)
    onehot = (e_flat[:, None] == jnp.arange(N_EXPERTS, dtype=jnp.int32)[None, :]).astype(jnp.int32)
    csum = jnp.cumsum(onehot, axis=0)
    rank = jnp.sum(csum * onehot, axis=1) - 1
    sizes = csum[-1]
    padded = (sizes + bm - 1) // bm * bm
    pad_ends = jnp.cumsum(padded)
    pad_starts = pad_ends - padded
    pos = (pad_starts[e_flat] + rank).astype(jnp.int32)
    n_pad = n_slots + N_EXPERTS * bm
    n_blocks = n_pad // bm
    slot_src = jnp.full((n_pad,), -1, jnp.int32).at[pos].set(
        jnp.arange(n_slots, dtype=jnp.int32), unique_indices=True)
    p = jnp.arange(n_pad, dtype=jnp.int32)
    spare = n_slots + (p // bm) % 2 * bm + p % bm
    real = slot_src >= 0
    tok_blocks = jnp.where(real, slot_src % n, 0).reshape(n_blocks, 1, bm)
    dst = jnp.where(real, slot_src, spare)
    dst_blocks = jnp.concatenate([spare[bm:2 * bm], dst]).reshape(n_blocks + 1, 1, bm)
    block_start = jnp.arange(n_blocks, dtype=jnp.int32) * bm
    block_e = jnp.minimum(jnp.searchsorted(pad_ends, block_start, side='right'),
                          N_EXPERTS - 1).astype(jnp.int32)
    return tok_blocks, dst_blocks, block_e


def _block_diag_ones():
    hid = jnp.arange(GROUP_WIDTH, dtype=jnp.int32) // HEAD_DIM
    return (hid[:, None] == hid[None, :]).astype(BF16)


def _pick(n, pref):
    return pref if n % pref == 0 else n


def kernel(x, attn_norm_g, w_in, rwkv_mu, rwkv_w0, rwkv_w_up, rwkv_a0, rwkv_a_up, rwkv_g_up,
           rwkv_k_k, rwkv_k_a, rwkv_r_k, rwkv_ln_w, rwkv_ln_b, fox_f_bias, fox_out_g, w_out,
           ffn_norm_g, router_w, router_b, expert_w1, expert_b1, expert_w2, expert_b2,
           final_norm_g):
    bsz, t, d = x.shape
    n = bsz * t
    depth = w_in.shape[0]
    assert depth == 1, "the final norm is fused into the last stage of a single layer"
    bd = _block_diag_ones()
    x2 = x.reshape(n, d)
    for l in range(depth):
        w_l = w_in[l]
        w_r = w_l[:, :RWKV_IN].astype(BF16)
        w_qkv = w_l[:, RWKV_IN:RWKV_IN + 3 * GROUP_WIDTH].astype(BF16)
        w_qt = w_qkv[:, :GROUP_WIDTH].T
        w_k = w_qkv[:, GROUP_WIDTH:2 * GROUP_WIDTH]
        w_vt = w_qkv[:, 2 * GROUP_WIDTH:].T
        w_f = jnp.pad(w_l[:, RWKV_IN + 3 * GROUP_WIDTH:], ((0, 0), (0, LANES - N_HEADS))).astype(BF16)
        fb_pad = jnp.pad(fox_f_bias[l], (0, LANES - N_HEADS)).reshape(1, LANES)
        wup_pad = jnp.pad(rwkv_w_up[l], ((0, LANES - DECAY_LORA), (0, 0))).astype(BF16)
        aup_pad = jnp.pad(rwkv_a_up[l], ((DECAY_LORA, 0), (0, 0))).astype(BF16)
        gup = rwkv_g_up[l].astype(BF16)
        vec = lambda a: a.reshape(1, -1)

        u_r, qt5, k, vt5, fl = _inproj(x2, vec(attn_norm_g[l]), w_r, w_qt, w_k, w_vt, w_f,
                                       bsz, ATTN_TILE)
        cext, cend = _fox_gate(fl.reshape(bsz, t, LANES), fb_pad, _gate_piece_selectors(),
                               ATTN_TILE)
        r_, k_, v_, lw, kk, b_, g_ = _rwkv_prep(
            u_r.reshape(bsz, t, RWKV_IN), vec(rwkv_mu[l]), vec(rwkv_w0[l]), wup_pad,
            vec(rwkv_a0[l]), aup_pad, gup, vec(rwkv_k_k[l]), vec(rwkv_k_a[l]), bd, _pick(t, 256))
        y_rwkv = _rwkv_scan(r_, k_, v_, lw, kk, b_, g_, vec(rwkv_r_k[l]), vec(rwkv_ln_w[l]),
                            vec(rwkv_ln_b[l]), bd)
        y_fox = _fox_attn(qt5, k.reshape(bsz, t, GROUP_WIDTH), cext, vt5, cend,
                          vec(fox_out_g[l]), ATTN_TILE)

        wo = w_out[l].astype(BF16)
        x1, h2, idx, gates = _outproj_router(
            x2, y_rwkv.reshape(n, GROUP_WIDTH), y_fox.reshape(n, GROUP_WIDTH),
            wo[:GROUP_WIDTH], wo[GROUP_WIDTH:], vec(ffn_norm_g[l]),
            router_w[l].T, router_b[l].reshape(N_EXPERTS, 1), _pick(n, 256))

        bm = 256
        tok_blocks, dst_blocks, block_e = _dispatch_plan(idx, bm)
        w1g, w1l = _w1_split(expert_w1[l], 512)
        b1 = expert_b1[l]
        b1g = b1[:, None, 0::2]
        b1l = b1[:, None, 1::2]
        y_all = _expert_mlp(block_e, tok_blocks, dst_blocks, h2, w1g, w1l, b1g, b1l,
                            expert_w2[l].astype(BF16), expert_b2[l][:, None, :], bm)
        x2 = _combine(gates.T, x1, vec(final_norm_g), y_all, _pick(n, 256))
    return x2.reshape(bsz, t, d)
```

```python
import functools

import jax
import jax.numpy as jnp
from jax import lax
from jax.experimental import pallas as pl
from jax.experimental.pallas import tpu as pltpu

F32 = jnp.float32
BF16 = jnp.bfloat16

HEAD_DIM = 64
N_HEADS = 8
GROUP_WIDTH = N_HEADS * HEAD_DIM
DECAY_LORA = 64
AAA_LORA = 64
GATE_LORA = 128
RWKV_IN = 3 * GROUP_WIDTH + DECAY_LORA + AAA_LORA + GATE_LORA
LORA_OFF = 3 * GROUP_WIDTH
N_EXPERTS = 32
TOP_K = 4
SWIGLU_ALPHA = 1.702
SWIGLU_LIMIT = 7.0
RMS_EPS = 1e-5
RWKV_GN_EPS = 64e-5
LANES = 128
RWKV_CHUNK = 64
RWKV_CHUNKS_PER_STEP = 4
ATTN_TILE = 512
MXU_WIDTH = 256
LOG2E = 1.4426950408889634
Q_SCALE = HEAD_DIM ** -0.5 * LOG2E
ZERO_PROB_EXP = -152.0
NORM_SLACK = 1.0 + 2.0 ** -6
VMEM_LIMIT = 56 * 1024 * 1024


def _cparams(semantics):
    return pltpu.CompilerParams(dimension_semantics=semantics, vmem_limit_bytes=VMEM_LIMIT)


def _dot(a, b):
    return jnp.dot(a, b, preferred_element_type=F32)


def _dot_nt(a, b):
    return lax.dot_general(a, b, (((1,), (1,)), ((), ())), preferred_element_type=F32)


def _dot_tn(a, b):
    return lax.dot_general(a, b, (((0,), (0,)), ((), ())), preferred_element_type=F32)


def _split3(x):
    hi = x.astype(BF16)
    r1 = x - hi.astype(F32)
    mid = r1.astype(BF16)
    lo = (r1 - mid.astype(F32)).astype(BF16)
    return hi, mid, lo


def _dot_exact_lhs(a_bf16, x):
    hi, mid, lo = _split3(x)
    return _dot(a_bf16, hi) + _dot(a_bf16, mid) + _dot(a_bf16, lo)


def _dot_exact_rhs(x, b_bf16):
    hi, mid, lo = _split3(x)
    return _dot(hi, b_bf16) + _dot(mid, b_bf16) + _dot(lo, b_bf16)


def _softplus(z):
    return jnp.maximum(z, 0.0) + jnp.log1p(jnp.exp(-jnp.abs(z)))


def _sigmoid(z):
    return 1.0 / (1.0 + jnp.exp(-z))


def _inproj_kernel(x_ref, g_ref, wr_ref, wqt_ref, wk_ref, wvt_ref, wf_ref,
                   ur_ref, qt_ref, k_ref, vt_ref, fl_ref):
    x = x_ref[...]
    h = x * lax.rsqrt(jnp.mean(x * x, axis=-1, keepdims=True) + RMS_EPS) * g_ref[...]
    hb = h.astype(BF16)
    ur_ref[...] = _dot(hb, wr_ref[...])
    k_ref[...] = _dot(hb, wk_ref[...]).astype(BF16)
    fl_ref[...] = _dot(hb, wf_ref[...])
    qt = (_dot_nt(wqt_ref[...], hb) * Q_SCALE).astype(BF16)
    vt = _dot_nt(wvt_ref[...], hb).astype(BF16)
    for p in range(N_HEADS // 2):
        qt_ref[0, p, 0] = qt[p * LANES:(p + 1) * LANES]
        vt_ref[0, p, 0] = vt[p * LANES:(p + 1) * LANES]


def _inproj(x2, g, w_r, w_qt, w_k, w_vt, w_f, bsz, tm):
    n, d = x2.shape
    nt = n // bsz // tm
    pairs = N_HEADS // 2
    const = lambda i: (0, 0)
    row = lambda i: (i, 0)
    fm = lambda i: (i // nt, 0, i % nt, 0, 0)
    fm_sds = jax.ShapeDtypeStruct((bsz, pairs, nt, LANES, tm), BF16)
    return pl.pallas_call(
        _inproj_kernel,
        grid=(n // tm,),
        in_specs=[
            pl.BlockSpec((tm, d), row),
            pl.BlockSpec((1, d), const),
            pl.BlockSpec(w_r.shape, const),
            pl.BlockSpec(w_qt.shape, const),
            pl.BlockSpec(w_k.shape, const),
            pl.BlockSpec(w_vt.shape, const),
            pl.BlockSpec(w_f.shape, const),
        ],
        out_specs=[
            pl.BlockSpec((tm, RWKV_IN), row),
            pl.BlockSpec((1, pairs, 1, LANES, tm), fm),
            pl.BlockSpec((tm, GROUP_WIDTH), row),
            pl.BlockSpec((1, pairs, 1, LANES, tm), fm),
            pl.BlockSpec((tm, LANES), row),
        ],
        out_shape=[
            jax.ShapeDtypeStruct((n, RWKV_IN), F32),
            fm_sds,
            jax.ShapeDtypeStruct((n, GROUP_WIDTH), BF16),
            fm_sds,
            jax.ShapeDtypeStruct((n, LANES), F32),
        ],
        compiler_params=_cparams(("parallel",)),
        name="inproj",
    )(x2, g, w_r, w_qt, w_k, w_vt, w_f)


def _fox_gate_kernel(fl_ref, fb_ref, sel_ref, c_ref, cend_ref, carry):
    tt = fl_ref.shape[1]

    @pl.when(pl.program_id(1) == 0)
    def _():
        carry[...] = jnp.zeros_like(carry)

    z = fl_ref[0] + fb_ref[...]
    log_f = jnp.minimum(z, 0.0) - jnp.log1p(jnp.exp(-jnp.abs(z)))
    ri = lax.broadcasted_iota(jnp.int32, (tt, tt), 0)
    ci = lax.broadcasted_iota(jnp.int32, (tt, tt), 1)
    tri = jnp.where(ri >= ci, 1.0, 0.0).astype(BF16)
    c = _dot_exact_lhs(tri, log_f) + carry[...]
    carry[...] = c[tt - 1:tt, :]
    cend_ref[0, 0] = c[tt - 1:tt, :]
    hi, mid, lo = _split3(c * LOG2E)
    c_ref[0] = (_dot(hi, sel_ref[0]) + _dot(mid, sel_ref[1]) + _dot(lo, sel_ref[2])).astype(BF16)


def _gate_piece_selectors():
    h = jnp.arange(LANES, dtype=jnp.int32)[:, None]
    col = jnp.arange(GROUP_WIDTH, dtype=jnp.int32)[None, :]
    sels = []
    for m in range(3):
        target = LANES * (h // 2) + 3 * (h % 2) + m
        sels.append(((col == target) & (h < N_HEADS)).astype(BF16))
    return jnp.stack(sels)


def _fox_gate(fl3, fb_pad, sel, tt):
    b, t, _ = fl3.shape
    return pl.pallas_call(
        _fox_gate_kernel,
        grid=(b, t // tt),
        in_specs=[
            pl.BlockSpec((1, tt, LANES), lambda i, j: (i, j, 0)),
            pl.BlockSpec((1, LANES), lambda i, j: (0, 0)),
            pl.BlockSpec((3, LANES, GROUP_WIDTH), lambda i, j: (0, 0, 0)),
        ],
        out_specs=[
            pl.BlockSpec((1, tt, GROUP_WIDTH), lambda i, j: (i, j, 0)),
            pl.BlockSpec((1, 1, 1, LANES), lambda i, j: (i, j, 0, 0)),
        ],
        out_shape=[
            jax.ShapeDtypeStruct((b, t, GROUP_WIDTH), BF16),
            jax.ShapeDtypeStruct((b, t // tt, 1, LANES), F32),
        ],
        scratch_shapes=[pltpu.VMEM((1, LANES), F32)],
        compiler_params=_cparams(("parallel", "arbitrary")),
        name="fox_gate",
    )(fl3, fb_pad, sel)


def _rwkv_prep_kernel(u_ref, mu_ref, w0_ref, wup_ref, a0_ref, aup_ref, gup_ref, kk_ref, ka_ref,
                      bd_ref, r_out, k_out, v_out, lw_out, kk_out, b_out, g_out, carry):
    tt = u_ref.shape[1]

    @pl.when(pl.program_id(1) == 0)
    def _():
        carry[...] = jnp.zeros_like(carry)

    u = u_ref[0]
    prev = pltpu.roll(u, 1, axis=0)
    row = lax.broadcasted_iota(jnp.int32, u.shape, 0)
    prev = jnp.where(row == 0, carry[...], prev)
    carry[...] = u[tt - 1:tt, :]
    us = u + (prev - u) * mu_ref[...]

    r = us[:, :GROUP_WIDTH]
    k = us[:, GROUP_WIDTH:2 * GROUP_WIDTH]
    v = us[:, 2 * GROUP_WIDTH:LORA_OFF]
    wa = us[:, LORA_OFF:LORA_OFF + LANES]
    gl = us[:, LORA_OFF + LANES:]

    w_lin = _dot(jnp.tanh(wa).astype(BF16), wup_ref[...])
    a_lin = _dot(wa.astype(BF16), aup_ref[...])
    w = -_softplus(-(w0_ref[...] + w_lin)) - 0.5
    lw_out[0] = -jnp.exp(w)
    a = _sigmoid(a0_ref[...] + a_lin)
    g_out[0] = _dot(_sigmoid(gl).astype(BF16), gup_ref[...])

    kkr = k * kk_ref[...]
    ss = _dot_exact_rhs(kkr * kkr, bd_ref[...])
    kk = kkr / jnp.maximum(jnp.sqrt(ss), 1e-12)
    r_out[0] = r
    k_out[0] = k * (1.0 + (a - 1.0) * ka_ref[...])
    v_out[0] = v
    kk_out[0] = kk
    b_out[0] = kk * a


def _rwkv_prep(u3, mu, w0, wup_pad, a0, aup_pad, gup, k_k, k_a, bd, tt):
    b, t, _ = u3.shape
    const = lambda i, j: (0, 0)
    tile = lambda i, j: (i, j, 0)
    out_sds = jax.ShapeDtypeStruct((b, t, GROUP_WIDTH), F32)
    vec = pl.BlockSpec((1, GROUP_WIDTH), const)
    return pl.pallas_call(
        _rwkv_prep_kernel,
        grid=(b, t // tt),
        in_specs=[
            pl.BlockSpec((1, tt, RWKV_IN), tile),
            pl.BlockSpec((1, RWKV_IN), const),
            vec,
            pl.BlockSpec((LANES, GROUP_WIDTH), const),
            vec,
            pl.BlockSpec((LANES, GROUP_WIDTH), const),
            pl.BlockSpec((GATE_LORA, GROUP_WIDTH), const),
            vec,
            vec,
            pl.BlockSpec((GROUP_WIDTH, GROUP_WIDTH), const),
        ],
        out_specs=[pl.BlockSpec((1, tt, GROUP_WIDTH), tile)] * 7,
        out_shape=[out_sds] * 7,
        scratch_shapes=[pltpu.VMEM((1, RWKV_IN), F32)],
        compiler_params=_cparams(("parallel", "arbitrary")),
        name="rwkv_prep",
    )(u3, mu, w0, wup_pad, a0, aup_pad, gup, k_k, k_a, bd)


def _rwkv_scan_kernel(r_ref, k_ref, v_ref, lw_ref, kk_ref, b_ref, g_ref, rk_ref, lnw_ref, lnb_ref,
                      bd_ref, o_ref, s_scr, *, n_chunks):
    c = RWKV_CHUNK
    w = MXU_WIDTH
    hpg = w // HEAD_DIM
    n_groups = GROUP_WIDTH // w

    @pl.when(pl.program_id(1) == 0)
    def _():
        s_scr[...] = jnp.zeros_like(s_scr)

    row = lax.broadcasted_iota(jnp.int32, (c, w), 0)
    u = lax.broadcasted_iota(jnp.int32, (c, w), 1) % HEAD_DIM
    strict = row > u
    incl = row >= u
    eye = jnp.where(row == u, 1.0, 0.0)
    level_masks = []
    s = 1
    while s < c:
        same = (row // (2 * s)) == (u // (2 * s))
        level_masks.append(same & ((row % (2 * s)) >= s) & ((u % (2 * s)) < s))
        s *= 2
    same_head = (lax.broadcasted_iota(jnp.int32, (w, w), 0) // HEAD_DIM
                 == lax.broadcasted_iota(jnp.int32, (w, w), 1) // HEAD_DIM)
    tri = jnp.where(lax.broadcasted_iota(jnp.int32, (c, c), 0)
                    >= lax.broadcasted_iota(jnp.int32, (c, c), 1), 1.0, 0.0).astype(BF16)

    def bdiag(x):
        xb = x.astype(BF16)
        tiled = jnp.concatenate([xb] * hpg, axis=0)
        return jnp.where(same_head, tiled, jnp.zeros_like(tiled))

    chains = [(ci, gi) for ci in range(n_chunks) for gi in range(n_groups)]
    lhs, rk_t, vbs, xcat, gam, rt32 = {}, {}, {}, {}, {}, {}
    for ci in range(n_chunks):
        rs = slice(ci * c, (ci + 1) * c)
        r = r_ref[0, rs, :]
        k = k_ref[0, rs, :]
        lw = lw_ref[0, rs, :]
        kk = kk_ref[0, rs, :]
        b = b_ref[0, rs, :]
        g_cum = _dot_exact_lhs(tri, lw)
        g_last = g_cum[c - 1:c, :]
        r_t = r * jnp.exp(g_cum)
        kk_t = kk * jnp.exp(g_cum - lw)
        e_neg = jnp.exp(-g_cum)
        b_n = b * e_neg
        k_n = k * e_neg
        e_end = jnp.exp(g_last - g_cum)
        b_e = (b * e_end).astype(BF16)
        k_e = (k * e_end).astype(BF16)
        gamma = jnp.exp(g_last)
        vb = v_ref[0, rs, :].astype(BF16)
        for gi in range(n_groups):
            gs = slice(gi * w, (gi + 1) * w)
            ch = (ci, gi)
            lhs[ch] = jnp.concatenate([r_t[:, gs].astype(BF16), kk_t[:, gs].astype(BF16)], axis=0)
            rk_t[ch] = (jnp.concatenate([bdiag(b_n[:, gs]), bdiag(k_n[:, gs])], axis=0),
                        bdiag(kk_t[:, gs]))
            vbs[ch] = vb[:, gs]
            xcat[ch] = jnp.concatenate([b_e[:, gs], k_e[:, gs]], axis=0)
            gam[ch] = gamma[:, gs]
            rt32[ch] = r_t[:, gs]

    p = {ch: _dot_nt(lhs[ch], rk_t[ch][0]) for ch in chains}
    l_b = {ch: jnp.where(strict, p[ch][c:, :w], 0.0) for ch in chains}
    l_k = {ch: jnp.where(strict, p[ch][c:, w:], 0.0).astype(BF16) for ch in chains}
    p_br = {ch: jnp.where(incl, p[ch][:c, :w], 0.0).astype(BF16) for ch in chains}
    p_kr = {ch: jnp.where(incl, p[ch][:c, w:], 0.0).astype(BF16) for ch in chains}
    v_bd = {ch: bdiag(vbs[ch]) for ch in chains}
    lkv = {ch: _dot(l_k[ch], v_bd[ch]) for ch in chains}

    t_inv = {ch: eye - jnp.where(level_masks[0], l_b[ch], 0.0) for ch in chains}
    for m in level_masks[1:]:
        tb = {ch: t_inv[ch].astype(BF16) for ch in chains}
        ct = {ch: _dot(jnp.where(m, l_b[ch], 0.0).astype(BF16), bdiag(tb[ch])) for ch in chains}
        t_inv = {ch: t_inv[ch] - _dot(tb[ch], bdiag(ct[ch])) for ch in chains}

    mm = {ch: _dot(t_inv[ch].astype(BF16),
                   jnp.concatenate([rk_t[ch][1], bdiag(lkv[ch])], axis=1)) for ch in chains}
    pm = {ch: _dot(p_br[ch], jnp.concatenate([bdiag(mm[ch][:, :w]), bdiag(mm[ch][:, w:])], axis=1))
          for ch in chains}
    pkv = {ch: _dot(p_kr[ch], v_bd[ch]) for ch in chains}
    nm = {ch: jnp.concatenate([(rt32[ch] - pm[ch][:, :w]).astype(BF16),
                               mm[ch][:, :w].astype(BF16)], axis=0) for ch in chains}
    n2 = {ch: pkv[ch] - pm[ch][:, w:] for ch in chains}

    state = [s_scr[gi] for gi in range(n_groups)]
    ys = []
    for ci in range(n_chunks):
        q = {gi: _dot_nt(nm[(ci, gi)], state[gi].astype(BF16)) for gi in range(n_groups)}
        y_parts = []
        for gi in range(n_groups):
            ch = (ci, gi)
            y_parts.append(q[gi][:c] + n2[ch])
            u_new = -(q[gi][c:] + mm[ch][:, w:])
            w_cat = jnp.concatenate([u_new.astype(BF16), vbs[ch]], axis=0)
            upd = _dot_tn(w_cat, xcat[ch])
            state[gi] = state[gi] * gam[ch] + jnp.where(same_head, upd, 0.0)
        ys.append(jnp.concatenate(y_parts, axis=1))
    for gi in range(n_groups):
        s_scr[gi] = state[gi]

    y = jnp.concatenate(ys, axis=0)
    r = r_ref[0]
    k = k_ref[0]
    v = v_ref[0]
    bd = bd_ref[...]
    inv_n = 1.0 / HEAD_DIM
    mean = _dot_exact_rhs(y, bd) * inv_n
    d = y - mean
    var = _dot_exact_rhs(d * d, bd) * inv_n
    yn = d * lax.rsqrt(var + RWKV_GN_EPS) * lnw_ref[...] + lnb_ref[...]
    bonus = _dot_exact_rhs(r * k * rk_ref[...], bd) * v
    o_ref[0] = ((yn + bonus) * g_ref[0]).astype(o_ref.dtype)


def _rwkv_scan(r, k, v, lw, kk, b, g, r_k, ln_w, ln_b, bd, n_chunks):
    bsz, t, _ = r.shape
    rows = RWKV_CHUNK * n_chunks
    const = lambda i, j: (0, 0)
    tile = pl.BlockSpec((1, rows, GROUP_WIDTH), lambda i, j: (i, j, 0))
    vec = pl.BlockSpec((1, GROUP_WIDTH), const)
    return pl.pallas_call(
        functools.partial(_rwkv_scan_kernel, n_chunks=n_chunks),
        grid=(bsz, t // rows),
        in_specs=[tile] * 7 + [vec, vec, vec, pl.BlockSpec((GROUP_WIDTH, GROUP_WIDTH), const)],
        out_specs=tile,
        out_shape=jax.ShapeDtypeStruct((bsz, t, GROUP_WIDTH), BF16),
        scratch_shapes=[pltpu.VMEM((GROUP_WIDTH // MXU_WIDTH, MXU_WIDTH, MXU_WIDTH), F32)],
        compiler_params=_cparams(("parallel", "arbitrary")),
        name="rwkv_scan",
    )(r, k, v, lw, kk, b, g, r_k, ln_w, ln_b, bd)


def _fox_attn_kernel(qt_ref, k_ref, ce_ref, vt_ref, cend_ref, og_ref, o_ref,
                     m_scr, l_scr, acc_scr, kmax_scr, *, t):
    qi = pl.program_id(2)
    n_strips = 2 * t // LANES
    qt = qt_ref[0, 0, 0]
    frow = lax.broadcasted_iota(jnp.int32, (LANES, t), 0)
    zero = jnp.zeros_like(qt)
    main = jnp.concatenate([jnp.where(frow < HEAD_DIM, qt, zero),
                            jnp.where(frow < HEAD_DIM, zero, qt)], axis=1)
    erow = lax.broadcasted_iota(jnp.int32, (LANES, 2 * t), 0)
    ecol = lax.broadcasted_iota(jnp.int32, (LANES, 2 * t), 1)
    off = jnp.where(ecol < t, 0, 3)
    extra = jnp.where((erow >= off) & (erow < off + 3), -1.0, 0.0).astype(BF16)
    q_aug = jnp.concatenate([main, extra], axis=0)

    m_scr[...] = jnp.full_like(m_scr, -jnp.inf)
    l_scr[...] = jnp.zeros_like(l_scr)
    acc_scr[...] = jnp.zeros_like(acc_scr)

    @pl.when(qi == 0)
    def _():
        hid_r = lax.broadcasted_iota(jnp.int32, (LANES, LANES), 0) // HEAD_DIM
        hid_c = lax.broadcasted_iota(jnp.int32, (LANES, LANES), 1) // HEAD_DIM
        same_head = jnp.where(hid_r == hid_c, 1.0, 0.0).astype(BF16)

        def tile_max(j, best):
            kf = k_ref[0, pl.ds(pl.multiple_of(j * t, t), t), :].astype(F32)
            sq = _dot((kf * kf).astype(BF16), same_head)
            return jnp.maximum(best, jnp.max(sq, axis=0, keepdims=True))

        best = lax.fori_loop(0, k_ref.shape[1] // t, tile_max, jnp.zeros((1, LANES), F32))
        kmax_scr[...] = jnp.sqrt(best) * NORM_SLACK

    def step(j, masked):
        start = pl.multiple_of(j * t, t)
        k_aug = jnp.concatenate([k_ref[0, pl.ds(start, t), :], ce_ref[0, pl.ds(start, t), :]],
                                axis=1)
        vt = vt_ref[0, 0, j]
        zt = _dot(k_aug, q_aug)
        m_prev = m_scr[...]
        l_prev = l_scr[...]
        acc_prev = acc_scr[...]
        m_out, l_out, acc_out = [], [], [[], []]
        for s in range(n_strips):
            head = s // (n_strips // 2)
            cs = slice(s * LANES, (s + 1) * LANES)
            z = zt[:, cs]
            if masked:
                key = lax.broadcasted_iota(jnp.int32, (t, LANES), 0)
                qry = lax.broadcasted_iota(jnp.int32, (t, LANES), 1) + (s * LANES) % t
                z = jnp.where(key <= qry, z, -jnp.inf)
            m_new = jnp.maximum(m_prev[:, cs], jnp.max(z, axis=0, keepdims=True))
            alpha = jnp.exp2(m_prev[:, cs] - m_new)
            p = jnp.exp2(z - m_new)
            l_out.append(alpha * l_prev[:, cs] + jnp.sum(p, axis=0, keepdims=True))
            m_out.append(m_new)
            hs = slice(head * HEAD_DIM, (head + 1) * HEAD_DIM)
            qs = slice((s * LANES) % t, (s * LANES) % t + LANES)
            pv = _dot(vt[hs], p.astype(BF16))
            acc_out[head].append(alpha * acc_prev[hs, qs] + pv)
        m_scr[...] = jnp.concatenate(m_out, axis=1)
        l_scr[...] = jnp.concatenate(l_out, axis=1)
        acc_scr[...] = jnp.concatenate([jnp.concatenate(acc_out[0], axis=1),
                                        jnp.concatenate(acc_out[1], axis=1)], axis=0)

    def body(j, carry):
        step(j, False)
        return carry

    step(qi, True)

    qf = qt.astype(F32)
    qsq = qf * qf
    qnorm = jnp.sqrt(jnp.concatenate(
        [jnp.sum(qsq[:HEAD_DIM], axis=0, keepdims=True),
         jnp.sum(qsq[HEAD_DIM:], axis=0, keepdims=True)], axis=1)) * NORM_SLACK
    kmax = kmax_scr[...]
    kmax2 = jnp.concatenate([jnp.broadcast_to(kmax[:, 0:1], (1, t)),
                             jnp.broadcast_to(kmax[:, HEAD_DIM:HEAD_DIM + 1], (1, t))], axis=1)
    slack = qnorm * kmax2 - m_scr[...]
    cend = cend_ref[0][:, 0, :]
    lane = lax.broadcasted_iota(jnp.int32, cend.shape, 1)
    jrow = lax.broadcasted_iota(jnp.int32, (cend.shape[0], 1), 0)
    needed = jrow < 0
    for head in range(2):
        worst = jnp.max(slack[:, head * t:(head + 1) * t], axis=1, keepdims=True)
        c_head = jnp.sum(jnp.where(lane == 2 * pl.program_id(1) + head, cend, 0.0),
                         axis=1, keepdims=True)
        needed = needed | (worst - c_head * LOG2E > ZERO_PROB_EXP)
    first = jnp.min(jnp.where(needed & (jrow < qi), jrow, qi))
    lax.fori_loop(first, qi, body, 0)

    l = l_scr[...]
    acc = acc_scr[...]
    inv_n = 1.0 / HEAD_DIM
    halves = []
    for head in range(2):
        o = acc[head * HEAD_DIM:(head + 1) * HEAD_DIM] / l[:, head * t:(head + 1) * t]
        halves.append(o * lax.rsqrt(jnp.sum(o * o, axis=0, keepdims=True) * inv_n + RMS_EPS))
    o_t = jnp.concatenate(halves, axis=0)
    o_ref[0] = (jnp.transpose(o_t) * og_ref[...]).astype(o_ref.dtype)


def _fox_attn(qt5, k, cext, vt5, cend, out_g, tile):
    bsz, t_all, _ = k.shape
    pairs = N_HEADS // 2
    tiles = t_all // tile
    kern = functools.partial(_fox_attn_kernel, t=tile)
    return pl.pallas_call(
        kern,
        grid=(bsz, pairs, tiles),
        in_specs=[
            pl.BlockSpec((1, 1, 1, LANES, tile), lambda b, p, i: (b, p, i, 0, 0)),
            pl.BlockSpec((1, t_all, LANES), lambda b, p, i: (b, 0, p)),
            pl.BlockSpec((1, t_all, LANES), lambda b, p, i: (b, 0, p)),
            pl.BlockSpec((1, 1, tiles, LANES, tile), lambda b, p, i: (b, p, 0, 0, 0)),
            pl.BlockSpec((1, tiles, 1, LANES), lambda b, p, i: (b, 0, 0, 0)),
            pl.BlockSpec((1, LANES), lambda b, p, i: (0, p)),
        ],
        out_specs=pl.BlockSpec((1, tile, LANES), lambda b, p, i: (b, i, p)),
        out_shape=jax.ShapeDtypeStruct((bsz, t_all, GROUP_WIDTH), BF16),
        scratch_shapes=[
            pltpu.VMEM((1, 2 * tile), F32),
            pltpu.VMEM((1, 2 * tile), F32),
            pltpu.VMEM((LANES, tile), F32),
            pltpu.VMEM((1, LANES), F32),
        ],
        compiler_params=_cparams(("parallel", "parallel", "arbitrary")),
        name="fox_attn",
    )(qt5, k, cext, vt5, cend, out_g)


def _outproj_router_kernel(x_ref, yr_ref, yf_ref, wo_r_ref, wo_f_ref, g_ref, rwt_ref, rb_ref,
                           x1_ref, h_ref, idx_ref, gate_ref, rank_ref, count_ref):
    @pl.when(pl.program_id(0) == 0)
    def _():
        count_ref[...] = jnp.zeros_like(count_ref)

    x1 = x_ref[...] + _dot(yr_ref[...], wo_r_ref[...]) + _dot(yf_ref[...], wo_f_ref[...])
    x1_ref[...] = x1
    h = x1 * lax.rsqrt(jnp.mean(x1 * x1, axis=-1, keepdims=True) + RMS_EPS) * g_ref[...]
    h_ref[:, 0, :] = h
    logits = lax.dot_general(rwt_ref[...], h, (((1,), (1,)), ((), ())),
                             precision=lax.Precision.HIGHEST,
                             preferred_element_type=F32) + rb_ref[...]
    eidx = lax.broadcasted_iota(jnp.int32, logits.shape, 0)
    vals, idxs, picks = [], [], []
    for _ in range(TOP_K):
        m = jnp.max(logits, axis=0, keepdims=True)
        i = jnp.min(jnp.where(logits == m, eidx, N_EXPERTS), axis=0, keepdims=True)
        vals.append(m)
        idxs.append(i)
        picks.append(eidx == i)
        logits = jnp.where(picks[-1], -jnp.inf, logits)
    es = [jnp.exp(val - vals[0]) for val in vals]
    denom = es[0] + es[1] + es[2] + es[3]
    idx_ref[...] = jnp.concatenate(idxs, axis=0)
    gate_ref[...] = jnp.concatenate([e / denom for e in es], axis=0)

    tm = logits.shape[1]
    chosen = [jnp.where(pk, 1.0, 0.0) for pk in picks]
    any_k = chosen[0] + chosen[1] + chosen[2] + chosen[3]
    before = (lax.broadcasted_iota(jnp.int32, (tm, tm), 0)
              < lax.broadcasted_iota(jnp.int32, (tm, tm), 1))
    prefix = _dot(any_k.astype(BF16), jnp.where(before, 1.0, 0.0).astype(BF16))
    seen = count_ref[:, 0:1] + prefix
    rank_ref[...] = jnp.concatenate(
        [jnp.sum(ch * seen, axis=0, keepdims=True) for ch in chosen], axis=0).astype(jnp.int32)
    count_ref[...] = count_ref[...] + jnp.sum(any_k, axis=1, keepdims=True)


def _outproj_router(x2, yr, yf, wo_r, wo_f, g, rwt, rb, tm):
    n, d = x2.shape
    const = lambda i: (0, 0)
    row = lambda i: (i, 0)
    col = lambda i: (0, i)
    return pl.pallas_call(
        _outproj_router_kernel,
        grid=(n // tm,),
        in_specs=[
            pl.BlockSpec((tm, d), row),
            pl.BlockSpec((tm, GROUP_WIDTH), row),
            pl.BlockSpec((tm, GROUP_WIDTH), row),
            pl.BlockSpec((GROUP_WIDTH, d), const),
            pl.BlockSpec((GROUP_WIDTH, d), const),
            pl.BlockSpec((1, d), const),
            pl.BlockSpec((N_EXPERTS, d), const),
            pl.BlockSpec((N_EXPERTS, 1), const),
        ],
        out_specs=[
            pl.BlockSpec((tm, d), row),
            pl.BlockSpec((tm, 1, d), lambda i: (i, 0, 0)),
            pl.BlockSpec((TOP_K, tm), col),
            pl.BlockSpec((TOP_K, tm), col),
            pl.BlockSpec((TOP_K, tm), col),
            pl.BlockSpec((N_EXPERTS, LANES), const),
        ],
        out_shape=[
            jax.ShapeDtypeStruct((n, d), F32),
            jax.ShapeDtypeStruct((n, 1, d), F32),
            jax.ShapeDtypeStruct((TOP_K, n), jnp.int32),
            jax.ShapeDtypeStruct((TOP_K, n), F32),
            jax.ShapeDtypeStruct((TOP_K, n), jnp.int32),
            jax.ShapeDtypeStruct((N_EXPERTS, LANES), F32),
        ],
        compiler_params=_cparams(("arbitrary",)),
        name="outproj_router",
    )(x2, yr, yf, wo_r, wo_f, g, rwt, rb)


def _w1_split_kernel(w_ref, perm_ref, g_ref, l_ref):
    half = MXU_WIDTH // 2
    perm = perm_ref[...]
    for grp in range(w_ref.shape[2] // MXU_WIDTH):
        blk = w_ref[0, :, grp * MXU_WIDTH:(grp + 1) * MXU_WIDTH].astype(BF16)
        r = _dot(blk, perm)
        g_ref[0, :, grp * half:(grp + 1) * half] = r[:, :half].astype(BF16)
        l_ref[0, :, grp * half:(grp + 1) * half] = r[:, half:].astype(BF16)


def _w1_split(w1, tr):
    e, d, two_f = w1.shape
    half = MXU_WIDTH // 2
    src = jnp.arange(MXU_WIDTH, dtype=jnp.int32)[:, None]
    dst = jnp.arange(MXU_WIDTH, dtype=jnp.int32)[None, :]
    perm = (src == jnp.where(dst < half, 2 * dst, 2 * (dst - half) + 1)).astype(BF16)
    out_sds = jax.ShapeDtypeStruct((e, d, two_f // 2), BF16)
    return pl.pallas_call(
        _w1_split_kernel,
        grid=(e, d // tr),
        in_specs=[
            pl.BlockSpec((1, tr, two_f), lambda i, j: (i, j, 0)),
            pl.BlockSpec((MXU_WIDTH, MXU_WIDTH), lambda i, j: (0, 0)),
        ],
        out_specs=[pl.BlockSpec((1, tr, two_f // 2), lambda i, j: (i, j, 0))] * 2,
        out_shape=[out_sds, out_sds],
        compiler_params=_cparams(("parallel", "parallel")),
        name="w1_split",
    )(w1, perm)


def _expert_kernel(be_ref, tok_a_ref, tok_b_ref, tok_a_next_ref, dst_b_prev_ref, dst_a_ref,
                   dst_b_ref, h_hbm, w1g_a, w1l_a, b1g_a, b1l_a, w2_a, b2_a,
                   w1g_b, w1l_b, b1g_b, b1l_b, w2_b, b2_b,
                   y_hbm, xbuf_a, xbuf_b, obuf_a, obuf_b, xrows, gsem, osem, *, bm, n_real_rows):
    del be_ref
    i = pl.program_id(0)

    def gather_start(idx_ref, xbuf, sem, r):
        pltpu.make_async_copy(h_hbm.at[idx_ref[0, 0, r]], xbuf.at[r], sem).start()

    def scatter_start(idx_ref, obuf, sem, r):
        pltpu.make_async_copy(obuf.at[r], y_hbm.at[idx_ref[0, 0, r]], sem).start()

    def rows_wait(buf, sem):
        pltpu.make_async_copy(buf, buf, sem).wait()

    def mlp(xbuf, w1g_ref, w1l_ref, b1g_ref, b1l_ref, w2_ref, b2_ref):
        xrows[...] = xbuf[:, 0, :]
        xb = xrows[...].astype(BF16)
        dff = w1g_ref.shape[2]
        acts = []
        for piece in range(dff // MXU_WIDTH):
            cs = slice(piece * MXU_WIDTH, (piece + 1) * MXU_WIDTH)
            glu = _dot(xb, w1g_ref[0, :, cs]) + b1g_ref[0, :, cs]
            lin = _dot(xb, w1l_ref[0, :, cs]) + b1l_ref[0, :, cs]
            glu = jnp.minimum(glu, SWIGLU_LIMIT)
            lin = jnp.clip(lin, -SWIGLU_LIMIT, SWIGLU_LIMIT)
            acts.append((glu * _sigmoid(SWIGLU_ALPHA * glu) * (lin + 1.0)).astype(BF16))
        return _dot(jnp.concatenate(acts, axis=1), w2_ref[0]) + b2_ref[0]

    @pl.when(i == 0)
    def _():
        obuf_a[...] = jnp.zeros_like(obuf_a)
        obuf_b[...] = jnp.zeros_like(obuf_b)

        def first(r, carry):
            pltpu.make_async_copy(obuf_a.at[r], y_hbm.at[n_real_rows + 2 * bm + r],
                                  osem.at[0]).start()
            gather_start(tok_a_ref, xbuf_a, gsem.at[0], r)
            return carry

        lax.fori_loop(0, bm, first, 0)

    for r in range(bm):
        gather_start(tok_b_ref, xbuf_b, gsem.at[1], r)
        scatter_start(dst_b_prev_ref, obuf_b, osem.at[1], r)
    rows_wait(xbuf_a, gsem.at[0])
    rows_wait(obuf_a, osem.at[0])
    obuf_a[:, 0, :] = mlp(xbuf_a, w1g_a, w1l_a, b1g_a, b1l_a, w2_a, b2_a)

    for r in range(bm):
        gather_start(tok_a_next_ref, xbuf_a, gsem.at[0], r)
        scatter_start(dst_a_ref, obuf_a, osem.at[0], r)
    rows_wait(xbuf_b, gsem.at[1])
    rows_wait(obuf_b, osem.at[1])
    obuf_b[:, 0, :] = mlp(xbuf_b, w1g_b, w1l_b, b1g_b, b1l_b, w2_b, b2_b)

    @pl.when(i == pl.num_programs(0) - 1)
    def _():
        def last(r, carry):
            scatter_start(dst_b_ref, obuf_b, osem.at[1], r)
            return carry

        lax.fori_loop(0, bm, last, 0)
        rows_wait(obuf_b, osem.at[1])
        rows_wait(obuf_a, osem.at[0])
        rows_wait(xbuf_a, gsem.at[0])


def _expert_mlp(block_e, tok_blocks, dst_blocks, h2, w1g, w1l, b1g, b1l, w2, b2, bm):
    n_blocks = tok_blocks.shape[0]
    assert n_blocks % 2 == 0
    n, _, d = h2.shape
    dff = w1g.shape[2]
    n_real_rows = TOP_K * n
    idx_spec = lambda fn: pl.BlockSpec((1, 1, bm), fn, memory_space=pltpu.SMEM)

    def weight_specs(which):
        wmap = lambda i, be: (be[2 * i + which], 0, 0)
        return [
            pl.BlockSpec((1, d, dff), wmap),
            pl.BlockSpec((1, d, dff), wmap),
            pl.BlockSpec((1, 1, dff), wmap),
            pl.BlockSpec((1, 1, dff), wmap),
            pl.BlockSpec((1, dff, d), wmap),
            pl.BlockSpec((1, 1, d), wmap),
        ]

    grid_spec = pltpu.PrefetchScalarGridSpec(
        num_scalar_prefetch=1,
        grid=(n_blocks // 2,),
        in_specs=[
            idx_spec(lambda i, be: (2 * i, 0, 0)),
            idx_spec(lambda i, be: (2 * i + 1, 0, 0)),
            idx_spec(lambda i, be: (jnp.minimum(2 * i + 2, n_blocks - 1), 0, 0)),
            idx_spec(lambda i, be: (2 * i, 0, 0)),
            idx_spec(lambda i, be: (2 * i + 1, 0, 0)),
            idx_spec(lambda i, be: (2 * i + 2, 0, 0)),
            pl.BlockSpec(memory_space=pl.ANY),
        ] + weight_specs(0) + weight_specs(1),
        out_specs=pl.BlockSpec(memory_space=pl.ANY),
        scratch_shapes=[
            pltpu.VMEM((bm, 1, d), F32),
            pltpu.VMEM((bm, 1, d), F32),
            pltpu.VMEM((bm, 1, d), F32),
            pltpu.VMEM((bm, 1, d), F32),
            pltpu.VMEM((bm, d), F32),
            pltpu.SemaphoreType.DMA((2,)),
            pltpu.SemaphoreType.DMA((2,)),
        ],
    )
    weights = (w1g, w1l, b1g, b1l, w2, b2)
    return pl.pallas_call(
        functools.partial(_expert_kernel, bm=bm, n_real_rows=n_real_rows),
        grid_spec=grid_spec,
        out_shape=jax.ShapeDtypeStruct((n_real_rows + 3 * bm, 1, d), F32),
        compiler_params=_cparams(("arbitrary",)),
        name="expert_mlp",
    )(block_e, tok_blocks, tok_blocks, tok_blocks, dst_blocks, dst_blocks, dst_blocks, h2,
      *weights, *weights)


def _combine_kernel(gate_ref, x1_ref, g_ref, y0_ref, y1_ref, y2_ref, y3_ref, o_ref):
    gates = gate_ref[...]
    y = x1_ref[...]
    for kk, y_ref in enumerate((y0_ref, y1_ref, y2_ref, y3_ref)):
        y = y + y_ref[:, 0, :] * gates[:, kk:kk + 1]
    o_ref[...] = y * lax.rsqrt(jnp.mean(y * y, axis=-1, keepdims=True) + RMS_EPS) * g_ref[...]


def _combine(gates_t, x1, g, y_all, tc):
    n, d = x1.shape
    tiles = n // tc
    y_spec = lambda kk: pl.BlockSpec((tc, 1, d), lambda i: (kk * tiles + i, 0, 0))
    return pl.pallas_call(
        _combine_kernel,
        grid=(tiles,),
        in_specs=[
            pl.BlockSpec((tc, TOP_K), lambda i: (i, 0)),
            pl.BlockSpec((tc, d), lambda i: (i, 0)),
            pl.BlockSpec((1, d), lambda i: (0, 0)),
        ] + [y_spec(kk) for kk in range(TOP_K)],
        out_specs=pl.BlockSpec((tc, d), lambda i: (i, 0)),
        out_shape=jax.ShapeDtypeStruct((n, d), F32),
        compiler_params=_cparams(("parallel",)),
        name="combine",
    )(gates_t, x1, g, y_all, y_all, y_all, y_all)


def _dispatch_plan(idx, rank, counts, bm):
    n = idx.shape[1]
    n_slots = TOP_K * n
    sizes = counts[:, 0].astype(jnp.int32)
    padded = (sizes + bm - 1) // bm * bm
    pad_ends = jnp.cumsum(padded)
    pad_starts = pad_ends - padded
    pos = (pad_starts[idx] + rank).reshape(-1)
    n_pad = n_slots + N_EXPERTS * bm
    n_blocks = n_pad // bm
    slot_src = jnp.full((n_pad,), -1, jnp.int32).at[pos].set(
        jnp.arange(n_slots, dtype=jnp.int32), unique_indices=True)
    p = jnp.arange(n_pad, dtype=jnp.int32)
    spare = n_slots + (p // bm) % 2 * bm + p % bm
    real = slot_src >= 0
    tok_blocks = jnp.where(real, slot_src % n, 0).reshape(n_blocks, 1, bm)
    dst = jnp.where(real, slot_src, spare)
    dst_blocks = jnp.concatenate([spare[bm:2 * bm], dst]).reshape(n_blocks + 1, 1, bm)
    block_start = jnp.arange(n_blocks, dtype=jnp.int32) * bm
    block_e = jnp.minimum(jnp.searchsorted(pad_ends, block_start, side='right'),
                          N_EXPERTS - 1).astype(jnp.int32)
    return tok_blocks, dst_blocks, block_e


def _block_diag_ones():
    hid = jnp.arange(GROUP_WIDTH, dtype=jnp.int32) // HEAD_DIM
    return (hid[:, None] == hid[None, :]).astype(BF16)


def _pick(n, pref):
    return pref if n % pref == 0 else n


def kernel(x, attn_norm_g, w_in, rwkv_mu, rwkv_w0, rwkv_w_up, rwkv_a0, rwkv_a_up, rwkv_g_up,
           rwkv_k_k, rwkv_k_a, rwkv_r_k, rwkv_ln_w, rwkv_ln_b, fox_f_bias, fox_out_g, w_out,
           ffn_norm_g, router_w, router_b, expert_w1, expert_b1, expert_w2, expert_b2,
           final_norm_g):
    bsz, t, d = x.shape
    n = bsz * t
    depth = w_in.shape[0]
    assert depth == 1, "the final norm is fused into the last stage of a single layer"
    bd = _block_diag_ones()
    x2 = x.reshape(n, d)
    for l in range(depth):
        w_l = w_in[l]
        w_r = w_l[:, :RWKV_IN].astype(BF16)
        w_qkv = w_l[:, RWKV_IN:RWKV_IN + 3 * GROUP_WIDTH].astype(BF16)
        w_qt = w_qkv[:, :GROUP_WIDTH].T
        w_k = w_qkv[:, GROUP_WIDTH:2 * GROUP_WIDTH]
        w_vt = w_qkv[:, 2 * GROUP_WIDTH:].T
        w_f = jnp.pad(w_l[:, RWKV_IN + 3 * GROUP_WIDTH:], ((0, 0), (0, LANES - N_HEADS))).astype(BF16)
        fb_pad = jnp.pad(fox_f_bias[l], (0, LANES - N_HEADS)).reshape(1, LANES)
        wup_pad = jnp.pad(rwkv_w_up[l], ((0, LANES - DECAY_LORA), (0, 0))).astype(BF16)
        aup_pad = jnp.pad(rwkv_a_up[l], ((DECAY_LORA, 0), (0, 0))).astype(BF16)
        gup = rwkv_g_up[l].astype(BF16)
        vec = lambda a: a.reshape(1, -1)

        u_r, qt5, k, vt5, fl = _inproj(x2, vec(attn_norm_g[l]), w_r, w_qt, w_k, w_vt, w_f,
                                       bsz, ATTN_TILE)
        cext, cend = _fox_gate(fl.reshape(bsz, t, LANES), fb_pad, _gate_piece_selectors(),
                               ATTN_TILE)
        r_, k_, v_, lw, kk, b_, g_ = _rwkv_prep(
            u_r.reshape(bsz, t, RWKV_IN), vec(rwkv_mu[l]), vec(rwkv_w0[l]), wup_pad,
            vec(rwkv_a0[l]), aup_pad, gup, vec(rwkv_k_k[l]), vec(rwkv_k_a[l]), bd, _pick(t, 256))
        y_rwkv = _rwkv_scan(r_, k_, v_, lw, kk, b_, g_, vec(rwkv_r_k[l]), vec(rwkv_ln_w[l]),
                            vec(rwkv_ln_b[l]), bd, RWKV_CHUNKS_PER_STEP)
        y_fox = _fox_attn(qt5, k.reshape(bsz, t, GROUP_WIDTH), cext, vt5, cend,
                          vec(fox_out_g[l]), ATTN_TILE)

        wo = w_out[l].astype(BF16)
        x1, h2, idx, gates, rank, counts = _outproj_router(
            x2, y_rwkv.reshape(n, GROUP_WIDTH), y_fox.reshape(n, GROUP_WIDTH),
            wo[:GROUP_WIDTH], wo[GROUP_WIDTH:], vec(ffn_norm_g[l]),
            router_w[l].T, router_b[l].reshape(N_EXPERTS, 1), _pick(n, 256))

        bm = 256
        tok_blocks, dst_blocks, block_e = _dispatch_plan(idx, rank, counts, bm)
        w1g, w1l = _w1_split(expert_w1[l], 512)
        b1 = expert_b1[l]
        b1g = b1[:, None, 0::2]
        b1l = b1[:, None, 1::2]
        y_all = _expert_mlp(block_e, tok_blocks, dst_blocks, h2, w1g, w1l, b1g, b1l,
                            expert_w2[l].astype(BF16), expert_b2[l][:, None, :], bm)
        x2 = _combine(gates.T, x1, vec(final_norm_g), y_all, _pick(n, 256))
    return x2.reshape(bsz, t, d)
```

```python
import functools

import jax
import jax.numpy as jnp
from jax import lax
from jax.experimental import pallas as pl
from jax.experimental.pallas import tpu as pltpu

F32 = jnp.float32
BF16 = jnp.bfloat16

HEAD_DIM = 64
N_HEADS = 8
GROUP_WIDTH = N_HEADS * HEAD_DIM
DECAY_LORA = 64
AAA_LORA = 64
GATE_LORA = 128
RWKV_IN = 3 * GROUP_WIDTH + DECAY_LORA + AAA_LORA + GATE_LORA
LORA_OFF = 3 * GROUP_WIDTH
N_EXPERTS = 32
TOP_K = 4
SWIGLU_ALPHA = 1.702
SWIGLU_LIMIT = 7.0
RMS_EPS = 1e-5
RWKV_GN_EPS = 64e-5
LANES = 128
RWKV_CHUNK = 64
RWKV_CHUNKS_PER_STEP = 4
ATTN_TILE = 512
MXU_WIDTH = 256
LOG2E = 1.4426950408889634
Q_SCALE = HEAD_DIM ** -0.5 * LOG2E
ZERO_PROB_EXP = -152.0
NORM_SLACK = 1.0 + 2.0 ** -6
VMEM_LIMIT = 56 * 1024 * 1024


def _cparams(semantics):
    return pltpu.CompilerParams(dimension_semantics=semantics, vmem_limit_bytes=VMEM_LIMIT)


def _dot(a, b):
    return jnp.dot(a, b, preferred_element_type=F32)


def _dot_nt(a, b):
    return lax.dot_general(a, b, (((1,), (1,)), ((), ())), preferred_element_type=F32)


def _dot_tn(a, b):
    return lax.dot_general(a, b, (((0,), (0,)), ((), ())), preferred_element_type=F32)


def _split3(x):
    hi = x.astype(BF16)
    r1 = x - hi.astype(F32)
    mid = r1.astype(BF16)
    lo = (r1 - mid.astype(F32)).astype(BF16)
    return hi, mid, lo


def _dot_exact_lhs(a_bf16, x):
    hi, mid, lo = _split3(x)
    return _dot(a_bf16, hi) + _dot(a_bf16, mid) + _dot(a_bf16, lo)


def _dot_exact_rhs(x, b_bf16):
    hi, mid, lo = _split3(x)
    return _dot(hi, b_bf16) + _dot(mid, b_bf16) + _dot(lo, b_bf16)


def _softplus(z):
    return jnp.maximum(z, 0.0) + jnp.log1p(jnp.exp(-jnp.abs(z)))


def _sigmoid(z):
    return 1.0 / (1.0 + jnp.exp(-z))


def _inproj_kernel(x_ref, g_ref, wr_ref, wqt_ref, wk_ref, wvt_ref, wf_ref,
                   ur_ref, qt_ref, k_ref, vt_ref, fl_ref):
    x = x_ref[...]
    h = x * lax.rsqrt(jnp.mean(x * x, axis=-1, keepdims=True) + RMS_EPS) * g_ref[...]
    hb = h.astype(BF16)
    ur_ref[...] = _dot(hb, wr_ref[...])
    k_ref[...] = _dot(hb, wk_ref[...]).astype(BF16)
    fl_ref[...] = _dot(hb, wf_ref[...])
    qt = (_dot_nt(wqt_ref[...], hb) * Q_SCALE).astype(BF16)
    vt = _dot_nt(wvt_ref[...], hb).astype(BF16)
    for p in range(N_HEADS // 2):
        qt_ref[0, p, 0] = qt[p * LANES:(p + 1) * LANES]
        vt_ref[0, p, 0] = vt[p * LANES:(p + 1) * LANES]


def _inproj(x2, g, w_r, w_qt, w_k, w_vt, w_f, bsz, tm):
    n, d = x2.shape
    nt = n // bsz // tm
    pairs = N_HEADS // 2
    const = lambda i: (0, 0)
    row = lambda i: (i, 0)
    fm = lambda i: (i // nt, 0, i % nt, 0, 0)
    fm_sds = jax.ShapeDtypeStruct((bsz, pairs, nt, LANES, tm), BF16)
    return pl.pallas_call(
        _inproj_kernel,
        grid=(n // tm,),
        in_specs=[
            pl.BlockSpec((tm, d), row),
            pl.BlockSpec((1, d), const),
            pl.BlockSpec(w_r.shape, const),
            pl.BlockSpec(w_qt.shape, const),
            pl.BlockSpec(w_k.shape, const),
            pl.BlockSpec(w_vt.shape, const),
            pl.BlockSpec(w_f.shape, const),
        ],
        out_specs=[
            pl.BlockSpec((tm, RWKV_IN), row),
            pl.BlockSpec((1, pairs, 1, LANES, tm), fm),
            pl.BlockSpec((tm, GROUP_WIDTH), row),
            pl.BlockSpec((1, pairs, 1, LANES, tm), fm),
            pl.BlockSpec((tm, LANES), row),
        ],
        out_shape=[
            jax.ShapeDtypeStruct((n, RWKV_IN), F32),
            fm_sds,
            jax.ShapeDtypeStruct((n, GROUP_WIDTH), BF16),
            fm_sds,
            jax.ShapeDtypeStruct((n, LANES), F32),
        ],
        compiler_params=_cparams(("parallel",)),
        name="inproj",
    )(x2, g, w_r, w_qt, w_k, w_vt, w_f)


def _fox_gate_kernel(fl_ref, fb_ref, sel_ref, c_ref, cend_ref, carry):
    tt = fl_ref.shape[1]

    @pl.when(pl.program_id(1) == 0)
    def _():
        carry[...] = jnp.zeros_like(carry)

    z = fl_ref[0] + fb_ref[...]
    log_f = jnp.minimum(z, 0.0) - jnp.log1p(jnp.exp(-jnp.abs(z)))
    ri = lax.broadcasted_iota(jnp.int32, (tt, tt), 0)
    ci = lax.broadcasted_iota(jnp.int32, (tt, tt), 1)
    tri = jnp.where(ri >= ci, 1.0, 0.0).astype(BF16)
    c = _dot_exact_lhs(tri, log_f) + carry[...]
    carry[...] = c[tt - 1:tt, :]
    cend_ref[0, 0] = c[tt - 1:tt, :]
    hi, mid, lo = _split3(c * LOG2E)
    c_ref[0] = (_dot(hi, sel_ref[0]) + _dot(mid, sel_ref[1]) + _dot(lo, sel_ref[2])).astype(BF16)


def _gate_piece_selectors():
    h = jnp.arange(LANES, dtype=jnp.int32)[:, None]
    col = jnp.arange(GROUP_WIDTH, dtype=jnp.int32)[None, :]
    sels = []
    for m in range(3):
        target = LANES * (h // 2) + 3 * (h % 2) + m
        sels.append(((col == target) & (h < N_HEADS)).astype(BF16))
    return jnp.stack(sels)


def _fox_gate(fl3, fb_pad, sel, tt):
    b, t, _ = fl3.shape
    return pl.pallas_call(
        _fox_gate_kernel,
        grid=(b, t // tt),
        in_specs=[
            pl.BlockSpec((1, tt, LANES), lambda i, j: (i, j, 0)),
            pl.BlockSpec((1, LANES), lambda i, j: (0, 0)),
            pl.BlockSpec((3, LANES, GROUP_WIDTH), lambda i, j: (0, 0, 0)),
        ],
        out_specs=[
            pl.BlockSpec((1, tt, GROUP_WIDTH), lambda i, j: (i, j, 0)),
            pl.BlockSpec((1, 1, 1, LANES), lambda i, j: (i, j, 0, 0)),
        ],
        out_shape=[
            jax.ShapeDtypeStruct((b, t, GROUP_WIDTH), BF16),
            jax.ShapeDtypeStruct((b, t // tt, 1, LANES), F32),
        ],
        scratch_shapes=[pltpu.VMEM((1, LANES), F32)],
        compiler_params=_cparams(("parallel", "arbitrary")),
        name="fox_gate",
    )(fl3, fb_pad, sel)


def _rwkv_prep_kernel(u_ref, mu_ref, w0_ref, wup_ref, a0_ref, aup_ref, gup_ref, kk_ref, ka_ref,
                      bd_ref, r_out, k_out, v_out, lw_out, kk_out, b_out, g_out, carry):
    tt = u_ref.shape[1]

    @pl.when(pl.program_id(1) == 0)
    def _():
        carry[...] = jnp.zeros_like(carry)

    u = u_ref[0]
    prev = pltpu.roll(u, 1, axis=0)
    row = lax.broadcasted_iota(jnp.int32, u.shape, 0)
    prev = jnp.where(row == 0, carry[...], prev)
    carry[...] = u[tt - 1:tt, :]
    us = u + (prev - u) * mu_ref[...]

    r = us[:, :GROUP_WIDTH]
    k = us[:, GROUP_WIDTH:2 * GROUP_WIDTH]
    v = us[:, 2 * GROUP_WIDTH:LORA_OFF]
    wa = us[:, LORA_OFF:LORA_OFF + LANES]
    gl = us[:, LORA_OFF + LANES:]

    w_lin = _dot(jnp.tanh(wa).astype(BF16), wup_ref[...])
    a_lin = _dot(wa.astype(BF16), aup_ref[...])
    w = -_softplus(-(w0_ref[...] + w_lin)) - 0.5
    lw_out[0] = -jnp.exp(w)
    a = _sigmoid(a0_ref[...] + a_lin)
    g_out[0] = _dot(_sigmoid(gl).astype(BF16), gup_ref[...])

    kkr = k * kk_ref[...]
    ss = _dot_exact_rhs(kkr * kkr, bd_ref[...])
    kk = kkr / jnp.maximum(jnp.sqrt(ss), 1e-12)
    r_out[0] = r
    k_out[0] = k * (1.0 + (a - 1.0) * ka_ref[...])
    v_out[0] = v
    kk_out[0] = kk
    b_out[0] = kk * a


def _rwkv_prep(u3, mu, w0, wup_pad, a0, aup_pad, gup, k_k, k_a, bd, tt):
    b, t, _ = u3.shape
    const = lambda i, j: (0, 0)
    tile = lambda i, j: (i, j, 0)
    out_sds = jax.ShapeDtypeStruct((b, t, GROUP_WIDTH), F32)
    vec = pl.BlockSpec((1, GROUP_WIDTH), const)
    return pl.pallas_call(
        _rwkv_prep_kernel,
        grid=(b, t // tt),
        in_specs=[
            pl.BlockSpec((1, tt, RWKV_IN), tile),
            pl.BlockSpec((1, RWKV_IN), const),
            vec,
            pl.BlockSpec((LANES, GROUP_WIDTH), const),
            vec,
            pl.BlockSpec((LANES, GROUP_WIDTH), const),
            pl.BlockSpec((GATE_LORA, GROUP_WIDTH), const),
            vec,
            vec,
            pl.BlockSpec((GROUP_WIDTH, GROUP_WIDTH), const),
        ],
        out_specs=[pl.BlockSpec((1, tt, GROUP_WIDTH), tile)] * 7,
        out_shape=[out_sds] * 7,
        scratch_shapes=[pltpu.VMEM((1, RWKV_IN), F32)],
        compiler_params=_cparams(("parallel", "arbitrary")),
        name="rwkv_prep",
    )(u3, mu, w0, wup_pad, a0, aup_pad, gup, k_k, k_a, bd)


def _rwkv_scan_kernel(r_ref, k_ref, v_ref, lw_ref, kk_ref, b_ref, g_ref, rk_ref, lnw_ref, lnb_ref,
                      bd_ref, o_ref, s_scr, *, n_chunks):
    c = RWKV_CHUNK
    w = MXU_WIDTH
    hpg = w // HEAD_DIM
    n_groups = GROUP_WIDTH // w

    @pl.when(pl.program_id(1) == 0)
    def _():
        s_scr[...] = jnp.zeros_like(s_scr)

    row = lax.broadcasted_iota(jnp.int32, (c, w), 0)
    u = lax.broadcasted_iota(jnp.int32, (c, w), 1) % HEAD_DIM
    strict = row > u
    incl = row >= u
    eye = jnp.where(row == u, 1.0, 0.0)
    level_masks = []
    s = 1
    while s < c:
        same = (row // (2 * s)) == (u // (2 * s))
        level_masks.append(same & ((row % (2 * s)) >= s) & ((u % (2 * s)) < s))
        s *= 2
    same_head = (lax.broadcasted_iota(jnp.int32, (w, w), 0) // HEAD_DIM
                 == lax.broadcasted_iota(jnp.int32, (w, w), 1) // HEAD_DIM)
    tri = jnp.where(lax.broadcasted_iota(jnp.int32, (c, c), 0)
                    >= lax.broadcasted_iota(jnp.int32, (c, c), 1), 1.0, 0.0).astype(BF16)

    def bdiag(x):
        xb = x.astype(BF16)
        tiled = jnp.concatenate([xb] * hpg, axis=0)
        return jnp.where(same_head, tiled, jnp.zeros_like(tiled))

    chains = [(ci, gi) for ci in range(n_chunks) for gi in range(n_groups)]
    lhs, rk_t, vbs, xcat, gam, rt32 = {}, {}, {}, {}, {}, {}
    for ci in range(n_chunks):
        rs = slice(ci * c, (ci + 1) * c)
        r = r_ref[0, rs, :]
        k = k_ref[0, rs, :]
        lw = lw_ref[0, rs, :]
        kk = kk_ref[0, rs, :]
        b = b_ref[0, rs, :]
        g_cum = _dot_exact_lhs(tri, lw)
        g_last = g_cum[c - 1:c, :]
        r_t = r * jnp.exp(g_cum)
        kk_t = kk * jnp.exp(g_cum - lw)
        e_neg = jnp.exp(-g_cum)
        b_n = b * e_neg
        k_n = k * e_neg
        e_end = jnp.exp(g_last - g_cum)
        b_e = (b * e_end).astype(BF16)
        k_e = (k * e_end).astype(BF16)
        gamma = jnp.exp(g_last)
        vb = v_ref[0, rs, :].astype(BF16)
        for gi in range(n_groups):
            gs = slice(gi * w, (gi + 1) * w)
            ch = (ci, gi)
            lhs[ch] = jnp.concatenate([r_t[:, gs].astype(BF16), kk_t[:, gs].astype(BF16)], axis=0)
            rk_t[ch] = (jnp.concatenate([bdiag(b_n[:, gs]), bdiag(k_n[:, gs])], axis=0),
                        bdiag(kk_t[:, gs]))
            vbs[ch] = vb[:, gs]
            xcat[ch] = jnp.concatenate([b_e[:, gs], k_e[:, gs]], axis=0)
            gam[ch] = gamma[:, gs]
            rt32[ch] = r_t[:, gs]

    p = {ch: _dot_nt(lhs[ch], rk_t[ch][0]) for ch in chains}
    l_b = {ch: jnp.where(strict, p[ch][c:, :w], 0.0) for ch in chains}
    l_k = {ch: jnp.where(strict, p[ch][c:, w:], 0.0).astype(BF16) for ch in chains}
    p_br = {ch: jnp.where(incl, p[ch][:c, :w], 0.0).astype(BF16) for ch in chains}
    p_kr = {ch: jnp.where(incl, p[ch][:c, w:], 0.0).astype(BF16) for ch in chains}
    v_bd = {ch: bdiag(vbs[ch]) for ch in chains}
    lkv = {ch: _dot(l_k[ch], v_bd[ch]) for ch in chains}

    t_inv = {ch: eye - jnp.where(level_masks[0], l_b[ch], 0.0) for ch in chains}
    for m in level_masks[1:]:
        tb = {ch: t_inv[ch].astype(BF16) for ch in chains}
        ct = {ch: _dot(jnp.where(m, l_b[ch], 0.0).astype(BF16), bdiag(tb[ch])) for ch in chains}
        t_inv = {ch: t_inv[ch] - _dot(tb[ch], bdiag(ct[ch])) for ch in chains}

    mm = {ch: _dot(t_inv[ch].astype(BF16),
                   jnp.concatenate([rk_t[ch][1], bdiag(lkv[ch])], axis=1)) for ch in chains}
    pm = {ch: _dot(p_br[ch], jnp.concatenate([bdiag(mm[ch][:, :w]), bdiag(mm[ch][:, w:])], axis=1))
          for ch in chains}
    pkv = {ch: _dot(p_kr[ch], v_bd[ch]) for ch in chains}
    nm = {ch: jnp.concatenate([(rt32[ch] - pm[ch][:, :w]).astype(BF16),
                               mm[ch][:, :w].astype(BF16)], axis=0) for ch in chains}
    n2 = {ch: pkv[ch] - pm[ch][:, w:] for ch in chains}

    state = [s_scr[gi] for gi in range(n_groups)]
    ys = []
    for ci in range(n_chunks):
        q = {gi: _dot_nt(nm[(ci, gi)], state[gi].astype(BF16)) for gi in range(n_groups)}
        y_parts = []
        for gi in range(n_groups):
            ch = (ci, gi)
            y_parts.append(q[gi][:c] + n2[ch])
            u_new = -(q[gi][c:] + mm[ch][:, w:])
            w_cat = jnp.concatenate([u_new.astype(BF16), vbs[ch]], axis=0)
            upd = _dot_tn(w_cat, xcat[ch])
            state[gi] = state[gi] * gam[ch] + jnp.where(same_head, upd, 0.0)
        ys.append(jnp.concatenate(y_parts, axis=1))
    for gi in range(n_groups):
        s_scr[gi] = state[gi]

    y = jnp.concatenate(ys, axis=0)
    r = r_ref[0]
    k = k_ref[0]
    v = v_ref[0]
    bd = bd_ref[...]
    inv_n = 1.0 / HEAD_DIM
    mean = _dot_exact_rhs(y, bd) * inv_n
    d = y - mean
    var = _dot_exact_rhs(d * d, bd) * inv_n
    yn = d * lax.rsqrt(var + RWKV_GN_EPS) * lnw_ref[...] + lnb_ref[...]
    bonus = _dot_exact_rhs(r * k * rk_ref[...], bd) * v
    o_ref[0] = ((yn + bonus) * g_ref[0]).astype(o_ref.dtype)


def _rwkv_scan(r, k, v, lw, kk, b, g, r_k, ln_w, ln_b, bd, n_chunks):
    bsz, t, _ = r.shape
    rows = RWKV_CHUNK * n_chunks
    const = lambda i, j: (0, 0)
    tile = pl.BlockSpec((1, rows, GROUP_WIDTH), lambda i, j: (i, j, 0))
    vec = pl.BlockSpec((1, GROUP_WIDTH), const)
    return pl.pallas_call(
        functools.partial(_rwkv_scan_kernel, n_chunks=n_chunks),
        grid=(bsz, t // rows),
        in_specs=[tile] * 7 + [vec, vec, vec, pl.BlockSpec((GROUP_WIDTH, GROUP_WIDTH), const)],
        out_specs=tile,
        out_shape=jax.ShapeDtypeStruct((bsz, t, GROUP_WIDTH), BF16),
        scratch_shapes=[pltpu.VMEM((GROUP_WIDTH // MXU_WIDTH, MXU_WIDTH, MXU_WIDTH), F32)],
        compiler_params=_cparams(("parallel", "arbitrary")),
        name="rwkv_scan",
    )(r, k, v, lw, kk, b, g, r_k, ln_w, ln_b, bd)


def _fox_attn_kernel(qt_ref, k_ref, ce_ref, vt_ref, cend_ref, og_ref, o_ref,
                     m_scr, l_scr, acc_scr, kmax_scr, *, t):
    qi = pl.program_id(2)
    n_strips = 2 * t // LANES
    qt = qt_ref[0, 0, 0]
    frow = lax.broadcasted_iota(jnp.int32, (LANES, t), 0)
    zero = jnp.zeros_like(qt)
    main = jnp.concatenate([jnp.where(frow < HEAD_DIM, qt, zero),
                            jnp.where(frow < HEAD_DIM, zero, qt)], axis=1)
    erow = lax.broadcasted_iota(jnp.int32, (LANES, 2 * t), 0)
    ecol = lax.broadcasted_iota(jnp.int32, (LANES, 2 * t), 1)
    off = jnp.where(ecol < t, 0, 3)
    extra = jnp.where((erow >= off) & (erow < off + 3), -1.0, 0.0).astype(BF16)
    q_aug = jnp.concatenate([main, extra], axis=0)

    m_scr[...] = jnp.full_like(m_scr, -jnp.inf)
    l_scr[...] = jnp.zeros_like(l_scr)
    acc_scr[...] = jnp.zeros_like(acc_scr)

    @pl.when(qi == 0)
    def _():
        hid_r = lax.broadcasted_iota(jnp.int32, (LANES, LANES), 0) // HEAD_DIM
        hid_c = lax.broadcasted_iota(jnp.int32, (LANES, LANES), 1) // HEAD_DIM
        same_head = jnp.where(hid_r == hid_c, 1.0, 0.0).astype(BF16)

        def tile_max(j, best):
            kf = k_ref[0, pl.ds(pl.multiple_of(j * t, t), t), :].astype(F32)
            sq = _dot((kf * kf).astype(BF16), same_head)
            return jnp.maximum(best, jnp.max(sq, axis=0, keepdims=True))

        best = lax.fori_loop(0, k_ref.shape[1] // t, tile_max, jnp.zeros((1, LANES), F32))
        kmax_scr[...] = jnp.sqrt(best) * NORM_SLACK

    def step(j, masked):
        start = pl.multiple_of(j * t, t)
        k_aug = jnp.concatenate([k_ref[0, pl.ds(start, t), :], ce_ref[0, pl.ds(start, t), :]],
                                axis=1)
        vt = vt_ref[0, 0, j]
        zt = _dot(k_aug, q_aug)
        m_prev = m_scr[...]
        l_prev = l_scr[...]
        acc_prev = acc_scr[...]
        m_out, l_out, acc_out = [], [], [[], []]
        for s in range(n_strips):
            head = s // (n_strips // 2)
            cs = slice(s * LANES, (s + 1) * LANES)
            z = zt[:, cs]
            if masked:
                key = lax.broadcasted_iota(jnp.int32, (t, LANES), 0)
                qry = lax.broadcasted_iota(jnp.int32, (t, LANES), 1) + (s * LANES) % t
                z = jnp.where(key <= qry, z, -jnp.inf)
            m_new = jnp.maximum(m_prev[:, cs], jnp.max(z, axis=0, keepdims=True))
            alpha = jnp.exp2(m_prev[:, cs] - m_new)
            p = jnp.exp2(z - m_new)
            l_out.append(alpha * l_prev[:, cs] + jnp.sum(p, axis=0, keepdims=True))
            m_out.append(m_new)
            hs = slice(head * HEAD_DIM, (head + 1) * HEAD_DIM)
            qs = slice((s * LANES) % t, (s * LANES) % t + LANES)
            pv = _dot(vt[hs], p.astype(BF16))
            acc_out[head].append(alpha * acc_prev[hs, qs] + pv)
        m_scr[...] = jnp.concatenate(m_out, axis=1)
        l_scr[...] = jnp.concatenate(l_out, axis=1)
        acc_scr[...] = jnp.concatenate([jnp.concatenate(acc_out[0], axis=1),
                                        jnp.concatenate(acc_out[1], axis=1)], axis=0)

    def body(j, carry):
        step(j, False)
        return carry

    step(qi, True)

    qf = qt.astype(F32)
    qsq = qf * qf
    qnorm = jnp.sqrt(jnp.concatenate(
        [jnp.sum(qsq[:HEAD_DIM], axis=0, keepdims=True),
         jnp.sum(qsq[HEAD_DIM:], axis=0, keepdims=True)], axis=1)) * NORM_SLACK
    kmax = kmax_scr[...]
    kmax2 = jnp.concatenate([jnp.broadcast_to(kmax[:, 0:1], (1, t)),
                             jnp.broadcast_to(kmax[:, HEAD_DIM:HEAD_DIM + 1], (1, t))], axis=1)
    slack = qnorm * kmax2 - m_scr[...]
    cend = cend_ref[0][:, 0, :]
    lane = lax.broadcasted_iota(jnp.int32, cend.shape, 1)
    jrow = lax.broadcasted_iota(jnp.int32, (cend.shape[0], 1), 0)
    needed = jrow < 0
    for head in range(2):
        worst = jnp.max(slack[:, head * t:(head + 1) * t], axis=1, keepdims=True)
        c_head = jnp.sum(jnp.where(lane == 2 * pl.program_id(1) + head, cend, 0.0),
                         axis=1, keepdims=True)
        needed = needed | (worst - c_head * LOG2E > ZERO_PROB_EXP)
    first = jnp.min(jnp.where(needed & (jrow < qi), jrow, qi))
    lax.fori_loop(first, qi, body, 0)

    l = l_scr[...]
    acc = acc_scr[...]
    inv_n = 1.0 / HEAD_DIM
    halves = []
    for head in range(2):
        o = acc[head * HEAD_DIM:(head + 1) * HEAD_DIM] / l[:, head * t:(head + 1) * t]
        halves.append(o * lax.rsqrt(jnp.sum(o * o, axis=0, keepdims=True) * inv_n + RMS_EPS))
    o_t = jnp.concatenate(halves, axis=0)
    o_ref[0] = (jnp.transpose(o_t) * og_ref[...]).astype(o_ref.dtype)


def _fox_attn(qt5, k, cext, vt5, cend, out_g, tile):
    bsz, t_all, _ = k.shape
    pairs = N_HEADS // 2
    tiles = t_all // tile
    kern = functools.partial(_fox_attn_kernel, t=tile)
    return pl.pallas_call(
        kern,
        grid=(bsz, pairs, tiles),
        in_specs=[
            pl.BlockSpec((1, 1, 1, LANES, tile), lambda b, p, i: (b, p, i, 0, 0)),
            pl.BlockSpec((1, t_all, LANES), lambda b, p, i: (b, 0, p)),
            pl.BlockSpec((1, t_all, LANES), lambda b, p, i: (b, 0, p)),
            pl.BlockSpec((1, 1, tiles, LANES, tile), lambda b, p, i: (b, p, 0, 0, 0)),
            pl.BlockSpec((1, tiles, 1, LANES), lambda b, p, i: (b, 0, 0, 0)),
            pl.BlockSpec((1, LANES), lambda b, p, i: (0, p)),
        ],
        out_specs=pl.BlockSpec((1, tile, LANES), lambda b, p, i: (b, i, p)),
        out_shape=jax.ShapeDtypeStruct((bsz, t_all, GROUP_WIDTH), BF16),
        scratch_shapes=[
            pltpu.VMEM((1, 2 * tile), F32),
            pltpu.VMEM((1, 2 * tile), F32),
            pltpu.VMEM((LANES, tile), F32),
            pltpu.VMEM((1, LANES), F32),
        ],
        compiler_params=_cparams(("parallel", "parallel", "arbitrary")),
        name="fox_attn",
    )(qt5, k, cext, vt5, cend, out_g)


def _outproj_router_kernel(x_ref, yr_ref, yf_ref, wo_r_ref, wo_f_ref, g_ref, rwt_ref, rb_ref,
                           x1_ref, h_ref, idx_ref, gate_ref, rank_ref, count_ref):
    @pl.when(pl.program_id(0) == 0)
    def _():
        count_ref[...] = jnp.zeros_like(count_ref)

    x1 = x_ref[...] + _dot(yr_ref[...], wo_r_ref[...]) + _dot(yf_ref[...], wo_f_ref[...])
    x1_ref[...] = x1
    h = x1 * lax.rsqrt(jnp.mean(x1 * x1, axis=-1, keepdims=True) + RMS_EPS) * g_ref[...]
    h_ref[:, 0, :] = h
    logits = lax.dot_general(rwt_ref[...], h, (((1,), (1,)), ((), ())),
                             precision=lax.Precision.HIGHEST,
                             preferred_element_type=F32) + rb_ref[...]
    eidx = lax.broadcasted_iota(jnp.int32, logits.shape, 0)
    vals, idxs, picks = [], [], []
    for _ in range(TOP_K):
        m = jnp.max(logits, axis=0, keepdims=True)
        i = jnp.min(jnp.where(logits == m, eidx, N_EXPERTS), axis=0, keepdims=True)
        vals.append(m)
        idxs.append(i)
        picks.append(eidx == i)
        logits = jnp.where(picks[-1], -jnp.inf, logits)
    es = [jnp.exp(val - vals[0]) for val in vals]
    denom = es[0] + es[1] + es[2] + es[3]
    idx_ref[...] = jnp.concatenate(idxs, axis=0)
    gate_ref[...] = jnp.concatenate([e / denom for e in es], axis=0)

    tm = logits.shape[1]
    chosen = [jnp.where(pk, 1.0, 0.0) for pk in picks]
    any_k = chosen[0] + chosen[1] + chosen[2] + chosen[3]
    before = (lax.broadcasted_iota(jnp.int32, (tm, tm), 0)
              < lax.broadcasted_iota(jnp.int32, (tm, tm), 1))
    prefix = _dot(any_k.astype(BF16), jnp.where(before, 1.0, 0.0).astype(BF16))
    seen = count_ref[:, 0:1] + prefix
    rank_ref[...] = jnp.concatenate(
        [jnp.sum(ch * seen, axis=0, keepdims=True) for ch in chosen], axis=0).astype(jnp.int32)
    count_ref[...] = count_ref[...] + jnp.sum(any_k, axis=1, keepdims=True)


def _outproj_router(x2, yr, yf, wo_r, wo_f, g, rwt, rb, tm):
    n, d = x2.shape
    const = lambda i: (0, 0)
    row = lambda i: (i, 0)
    col = lambda i: (0, i)
    return pl.pallas_call(
        _outproj_router_kernel,
        grid=(n // tm,),
        in_specs=[
            pl.BlockSpec((tm, d), row),
            pl.BlockSpec((tm, GROUP_WIDTH), row),
            pl.BlockSpec((tm, GROUP_WIDTH), row),
            pl.BlockSpec((GROUP_WIDTH, d), const),
            pl.BlockSpec((GROUP_WIDTH, d), const),
            pl.BlockSpec((1, d), const),
            pl.BlockSpec((N_EXPERTS, d), const),
            pl.BlockSpec((N_EXPERTS, 1), const),
        ],
        out_specs=[
            pl.BlockSpec((tm, d), row),
            pl.BlockSpec((tm, 1, d), lambda i: (i, 0, 0)),
            pl.BlockSpec((TOP_K, tm), col),
            pl.BlockSpec((TOP_K, tm), col),
            pl.BlockSpec((TOP_K, tm), col),
            pl.BlockSpec((N_EXPERTS, LANES), const),
        ],
        out_shape=[
            jax.ShapeDtypeStruct((n, d), F32),
            jax.ShapeDtypeStruct((n, 1, d), F32),
            jax.ShapeDtypeStruct((TOP_K, n), jnp.int32),
            jax.ShapeDtypeStruct((TOP_K, n), F32),
            jax.ShapeDtypeStruct((TOP_K, n), jnp.int32),
            jax.ShapeDtypeStruct((N_EXPERTS, LANES), F32),
        ],
        compiler_params=_cparams(("arbitrary",)),
        name="outproj_router",
    )(x2, yr, yf, wo_r, wo_f, g, rwt, rb)


def _w1_split_kernel(w_ref, perm_ref, g_ref, l_ref):
    half = MXU_WIDTH // 2
    perm = perm_ref[...]
    for grp in range(w_ref.shape[2] // MXU_WIDTH):
        blk = w_ref[0, :, grp * MXU_WIDTH:(grp + 1) * MXU_WIDTH].astype(BF16)
        r = _dot(blk, perm)
        g_ref[0, :, grp * half:(grp + 1) * half] = r[:, :half].astype(BF16)
        l_ref[0, :, grp * half:(grp + 1) * half] = r[:, half:].astype(BF16)


def _w1_split(w1, tr):
    e, d, two_f = w1.shape
    half = MXU_WIDTH // 2
    src = jnp.arange(MXU_WIDTH, dtype=jnp.int32)[:, None]
    dst = jnp.arange(MXU_WIDTH, dtype=jnp.int32)[None, :]
    perm = (src == jnp.where(dst < half, 2 * dst, 2 * (dst - half) + 1)).astype(BF16)
    out_sds = jax.ShapeDtypeStruct((e, d, two_f // 2), BF16)
    return pl.pallas_call(
        _w1_split_kernel,
        grid=(e, d // tr),
        in_specs=[
            pl.BlockSpec((1, tr, two_f), lambda i, j: (i, j, 0)),
            pl.BlockSpec((MXU_WIDTH, MXU_WIDTH), lambda i, j: (0, 0)),
        ],
        out_specs=[pl.BlockSpec((1, tr, two_f // 2), lambda i, j: (i, j, 0))] * 2,
        out_shape=[out_sds, out_sds],
        compiler_params=_cparams(("parallel", "parallel")),
        name="w1_split",
    )(w1, perm)


def _expert_kernel(be_ref, tok_a_ref, tok_b_ref, tok_a_next_ref, dst_b_prev_ref, dst_a_ref,
                   dst_b_ref, h_hbm, w1g_a, w1l_a, b1g_a, b1l_a, w2_a, b2_a,
                   w1g_b, w1l_b, b1g_b, b1l_b, w2_b, b2_b,
                   y_hbm, xbuf_a, xbuf_b, obuf_a, obuf_b, xrows, gsem, osem, *, bm, n_real_rows):
    del be_ref
    i = pl.program_id(0)

    def gather_start(idx_ref, xbuf, sem, r):
        pltpu.make_async_copy(h_hbm.at[idx_ref[0, 0, r]], xbuf.at[r], sem).start()

    def scatter_start(idx_ref, obuf, sem, r):
        pltpu.make_async_copy(obuf.at[r], y_hbm.at[idx_ref[0, 0, r]], sem).start()

    def rows_wait(buf, sem):
        pltpu.make_async_copy(buf, buf, sem).wait()

    def mlp(xbuf, w1g_ref, w1l_ref, b1g_ref, b1l_ref, w2_ref, b2_ref):
        xrows[...] = xbuf[:, 0, :]
        xb = xrows[...].astype(BF16)
        dff = w1g_ref.shape[2]
        acts = []
        for piece in range(dff // MXU_WIDTH):
            cs = slice(piece * MXU_WIDTH, (piece + 1) * MXU_WIDTH)
            glu = _dot(xb, w1g_ref[0, :, cs]) + b1g_ref[0, :, cs]
            lin = _dot(xb, w1l_ref[0, :, cs]) + b1l_ref[0, :, cs]
            glu = jnp.minimum(glu, SWIGLU_LIMIT)
            lin = jnp.clip(lin, -SWIGLU_LIMIT, SWIGLU_LIMIT)
            acts.append((glu * _sigmoid(SWIGLU_ALPHA * glu) * (lin + 1.0)).astype(BF16))
        return _dot(jnp.concatenate(acts, axis=1), w2_ref[0]) + b2_ref[0]

    @pl.when(i == 0)
    def _():
        obuf_a[...] = jnp.zeros_like(obuf_a)
        obuf_b[...] = jnp.zeros_like(obuf_b)

        def first(r, carry):
            pltpu.make_async_copy(obuf_a.at[r], y_hbm.at[n_real_rows + 2 * bm + r],
                                  osem.at[0]).start()
            gather_start(tok_a_ref, xbuf_a, gsem.at[0], r)
            return carry

        lax.fori_loop(0, bm, first, 0)

    for r in range(bm):
        gather_start(tok_b_ref, xbuf_b, gsem.at[1], r)
        scatter_start(dst_b_prev_ref, obuf_b, osem.at[1], r)
    rows_wait(xbuf_a, gsem.at[0])
    rows_wait(obuf_a, osem.at[0])
    obuf_a[:, 0, :] = mlp(xbuf_a, w1g_a, w1l_a, b1g_a, b1l_a, w2_a, b2_a)

    for r in range(bm):
        gather_start(tok_a_next_ref, xbuf_a, gsem.at[0], r)
        scatter_start(dst_a_ref, obuf_a, osem.at[0], r)
    rows_wait(xbuf_b, gsem.at[1])
    rows_wait(obuf_b, osem.at[1])
    obuf_b[:, 0, :] = mlp(xbuf_b, w1g_b, w1l_b, b1g_b, b1l_b, w2_b, b2_b)

    @pl.when(i == pl.num_programs(0) - 1)
    def _():
        def last(r, carry):
            scatter_start(dst_b_ref, obuf_b, osem.at[1], r)
            return carry

        lax.fori_loop(0, bm, last, 0)
        rows_wait(obuf_b, osem.at[1])
        rows_wait(obuf_a, osem.at[0])
        rows_wait(xbuf_a, gsem.at[0])


def _expert_mlp(block_e, tok_blocks, dst_blocks, h2, w1g, w1l, b1g, b1l, w2, b2, bm):
    n_blocks = tok_blocks.shape[0]
    assert n_blocks % 2 == 0
    n, _, d = h2.shape
    dff = w1g.shape[2]
    n_real_rows = TOP_K * n
    idx_spec = lambda fn: pl.BlockSpec((1, 1, bm), fn, memory_space=pltpu.SMEM)

    def weight_specs(which):
        wmap = lambda i, be: (be[2 * i + which], 0, 0)
        return [
            pl.BlockSpec((1, d, dff), wmap),
            pl.BlockSpec((1, d, dff), wmap),
            pl.BlockSpec((1, 1, dff), wmap),
            pl.BlockSpec((1, 1, dff), wmap),
            pl.BlockSpec((1, dff, d), wmap),
            pl.BlockSpec((1, 1, d), wmap),
        ]

    grid_spec = pltpu.PrefetchScalarGridSpec(
        num_scalar_prefetch=1,
        grid=(n_blocks // 2,),
        in_specs=[
            idx_spec(lambda i, be: (2 * i, 0, 0)),
            idx_spec(lambda i, be: (2 * i + 1, 0, 0)),
            idx_spec(lambda i, be: (jnp.minimum(2 * i + 2, n_blocks - 1), 0, 0)),
            idx_spec(lambda i, be: (2 * i, 0, 0)),
            idx_spec(lambda i, be: (2 * i + 1, 0, 0)),
            idx_spec(lambda i, be: (2 * i + 2, 0, 0)),
            pl.BlockSpec(memory_space=pl.ANY),
        ] + weight_specs(0) + weight_specs(1),
        out_specs=pl.BlockSpec(memory_space=pl.ANY),
        scratch_shapes=[
            pltpu.VMEM((bm, 1, d), F32),
            pltpu.VMEM((bm, 1, d), F32),
            pltpu.VMEM((bm, 1, d), F32),
            pltpu.VMEM((bm, 1, d), F32),
            pltpu.VMEM((bm, d), F32),
            pltpu.SemaphoreType.DMA((2,)),
            pltpu.SemaphoreType.DMA((2,)),
        ],
    )
    weights = (w1g, w1l, b1g, b1l, w2, b2)
    return pl.pallas_call(
        functools.partial(_expert_kernel, bm=bm, n_real_rows=n_real_rows),
        grid_spec=grid_spec,
        out_shape=jax.ShapeDtypeStruct((n_real_rows + 3 * bm, 1, d), F32),
        compiler_params=_cparams(("arbitrary",)),
        name="expert_mlp",
    )(block_e, tok_blocks, tok_blocks, tok_blocks, dst_blocks, dst_blocks, dst_blocks, h2,
      *weights, *weights)


def _combine_kernel(gate_ref, x1_ref, g_ref, y0_ref, y1_ref, y2_ref, y3_ref, o_ref):
    gates = gate_ref[...]
    y = x1_ref[...]
    for kk, y_ref in enumerate((y0_ref, y1_ref, y2_ref, y3_ref)):
        y = y + y_ref[:, 0, :] * gates[:, kk:kk + 1]
    o_ref[...] = y * lax.rsqrt(jnp.mean(y * y, axis=-1, keepdims=True) + RMS_EPS) * g_ref[...]


def _combine(gates_t, x1, g, y_all, tc):
    n, d = x1.shape
    tiles = n // tc
    y_spec = lambda kk: pl.BlockSpec((tc, 1, d), lambda i: (kk * tiles + i, 0, 0))
    return pl.pallas_call(
        _combine_kernel,
        grid=(tiles,),
        in_specs=[
            pl.BlockSpec((tc, TOP_K), lambda i: (i, 0)),
            pl.BlockSpec((tc, d), lambda i: (i, 0)),
            pl.BlockSpec((1, d), lambda i: (0, 0)),
        ] + [y_spec(kk) for kk in range(TOP_K)],
        out_specs=pl.BlockSpec((tc, d), lambda i: (i, 0)),
        out_shape=jax.ShapeDtypeStruct((n, d), F32),
        compiler_params=_cparams(("parallel",)),
        name="combine",
    )(gates_t, x1, g, y_all, y_all, y_all, y_all)


def _dispatch_plan(idx, rank, counts, bm):
    n = idx.shape[1]
    n_slots = TOP_K * n
    sizes = counts[:, 0].astype(jnp.int32)
    padded = (sizes + bm - 1) // bm * bm
    pad_ends = jnp.cumsum(padded)
    pad_starts = pad_ends - padded
    experts = jnp.arange(N_EXPERTS, dtype=jnp.int32)
    start_of = jnp.sum(jnp.where(idx[..., None] == experts, pad_starts, 0), axis=-1)
    pos = (start_of + rank).reshape(-1)
    n_pad = n_slots + N_EXPERTS * bm
    n_blocks = n_pad // bm
    slot_src = jnp.full((n_pad,), -1, jnp.int32).at[pos].set(
        jnp.arange(n_slots, dtype=jnp.int32), unique_indices=True)
    p = jnp.arange(n_pad, dtype=jnp.int32)
    spare = n_slots + (p // bm) % 2 * bm + p % bm
    real = slot_src >= 0
    tok_blocks = jnp.where(real, slot_src % n, 0).reshape(n_blocks, 1, bm)
    dst = jnp.where(real, slot_src, spare)
    dst_blocks = jnp.concatenate([spare[bm:2 * bm], dst]).reshape(n_blocks + 1, 1, bm)
    block_start = jnp.arange(n_blocks, dtype=jnp.int32) * bm
    block_e = jnp.minimum(jnp.sum(pad_ends[None, :] <= block_start[:, None], axis=1),
                          N_EXPERTS - 1).astype(jnp.int32)
    return tok_blocks, dst_blocks, block_e


def _block_diag_ones():
    hid = jnp.arange(GROUP_WIDTH, dtype=jnp.int32) // HEAD_DIM
    return (hid[:, None] == hid[None, :]).astype(BF16)


def _pick(n, pref):
    return pref if n % pref == 0 else n


def kernel(x, attn_norm_g, w_in, rwkv_mu, rwkv_w0, rwkv_w_up, rwkv_a0, rwkv_a_up, rwkv_g_up,
           rwkv_k_k, rwkv_k_a, rwkv_r_k, rwkv_ln_w, rwkv_ln_b, fox_f_bias, fox_out_g, w_out,
           ffn_norm_g, router_w, router_b, expert_w1, expert_b1, expert_w2, expert_b2,
           final_norm_g):
    bsz, t, d = x.shape
    n = bsz * t
    depth = w_in.shape[0]
    assert depth == 1, "the final norm is fused into the last stage of a single layer"
    bd = _block_diag_ones()
    x2 = x.reshape(n, d)
    for l in range(depth):
        w_l = w_in[l]
        w_r = w_l[:, :RWKV_IN].astype(BF16)
        w_qkv = w_l[:, RWKV_IN:RWKV_IN + 3 * GROUP_WIDTH].astype(BF16)
        w_qt = w_qkv[:, :GROUP_WIDTH].T
        w_k = w_qkv[:, GROUP_WIDTH:2 * GROUP_WIDTH]
        w_vt = w_qkv[:, 2 * GROUP_WIDTH:].T
        w_f = jnp.pad(w_l[:, RWKV_IN + 3 * GROUP_WIDTH:], ((0, 0), (0, LANES - N_HEADS))).astype(BF16)
        fb_pad = jnp.pad(fox_f_bias[l], (0, LANES - N_HEADS)).reshape(1, LANES)
        wup_pad = jnp.pad(rwkv_w_up[l], ((0, LANES - DECAY_LORA), (0, 0))).astype(BF16)
        aup_pad = jnp.pad(rwkv_a_up[l], ((DECAY_LORA, 0), (0, 0))).astype(BF16)
        gup = rwkv_g_up[l].astype(BF16)
        vec = lambda a: a.reshape(1, -1)

        u_r, qt5, k, vt5, fl = _inproj(x2, vec(attn_norm_g[l]), w_r, w_qt, w_k, w_vt, w_f,
                                       bsz, ATTN_TILE)
        cext, cend = _fox_gate(fl.reshape(bsz, t, LANES), fb_pad, _gate_piece_selectors(),
                               ATTN_TILE)
        r_, k_, v_, lw, kk, b_, g_ = _rwkv_prep(
            u_r.reshape(bsz, t, RWKV_IN), vec(rwkv_mu[l]), vec(rwkv_w0[l]), wup_pad,
            vec(rwkv_a0[l]), aup_pad, gup, vec(rwkv_k_k[l]), vec(rwkv_k_a[l]), bd, _pick(t, 256))
        y_rwkv = _rwkv_scan(r_, k_, v_, lw, kk, b_, g_, vec(rwkv_r_k[l]), vec(rwkv_ln_w[l]),
                            vec(rwkv_ln_b[l]), bd, RWKV_CHUNKS_PER_STEP)
        y_fox = _fox_attn(qt5, k.reshape(bsz, t, GROUP_WIDTH), cext, vt5, cend,
                          vec(fox_out_g[l]), ATTN_TILE)

        wo = w_out[l].astype(BF16)
        x1, h2, idx, gates, rank, counts = _outproj_router(
            x2, y_rwkv.reshape(n, GROUP_WIDTH), y_fox.reshape(n, GROUP_WIDTH),
            wo[:GROUP_WIDTH], wo[GROUP_WIDTH:], vec(ffn_norm_g[l]),
            router_w[l].T, router_b[l].reshape(N_EXPERTS, 1), _pick(n, 256))

        bm = 256
        tok_blocks, dst_blocks, block_e = _dispatch_plan(idx, rank, counts, bm)
        w1g, w1l = _w1_split(expert_w1[l], 512)
        b1 = expert_b1[l]
        b1g = b1[:, None, 0::2]
        b1l = b1[:, None, 1::2]
        y_all = _expert_mlp(block_e, tok_blocks, dst_blocks, h2, w1g, w1l, b1g, b1l,
                            expert_w2[l].astype(BF16), expert_b2[l][:, None, :], bm)
        x2 = _combine(gates.T, x1, vec(final_norm_g), y_all, _pick(n, 256))
    return x2.reshape(bsz, t, d)
```

```python
import functools

import jax
import jax.numpy as jnp
from jax import lax
from jax.experimental import pallas as pl
from jax.experimental.pallas import tpu as pltpu

F32 = jnp.float32
BF16 = jnp.bfloat16

HEAD_DIM = 64
N_HEADS = 8
GROUP_WIDTH = N_HEADS * HEAD_DIM
DECAY_LORA = 64
AAA_LORA = 64
GATE_LORA = 128
RWKV_IN = 3 * GROUP_WIDTH + DECAY_LORA + AAA_LORA + GATE_LORA
LORA_OFF = 3 * GROUP_WIDTH
N_EXPERTS = 32
TOP_K = 4
SWIGLU_ALPHA = 1.702
SWIGLU_LIMIT = 7.0
RMS_EPS = 1e-5
RWKV_GN_EPS = 64e-5
LANES = 128
RWKV_CHUNK = 64
RWKV_CHUNKS_PER_STEP = 4
ATTN_TILE = 512
MXU_WIDTH = 256
LOG2E = 1.4426950408889634
Q_SCALE = HEAD_DIM ** -0.5 * LOG2E
ZERO_PROB_EXP = -152.0
NORM_SLACK = 1.0 + 2.0 ** -6
VMEM_LIMIT = 56 * 1024 * 1024


def _cparams(semantics):
    return pltpu.CompilerParams(dimension_semantics=semantics, vmem_limit_bytes=VMEM_LIMIT)


def _dot(a, b):
    return jnp.dot(a, b, preferred_element_type=F32)


def _dot_nt(a, b):
    return lax.dot_general(a, b, (((1,), (1,)), ((), ())), preferred_element_type=F32)


def _dot_tn(a, b):
    return lax.dot_general(a, b, (((0,), (0,)), ((), ())), preferred_element_type=F32)


def _split3(x):
    hi = x.astype(BF16)
    r1 = x - hi.astype(F32)
    mid = r1.astype(BF16)
    lo = (r1 - mid.astype(F32)).astype(BF16)
    return hi, mid, lo


def _dot_exact_lhs(a_bf16, x):
    hi, mid, lo = _split3(x)
    return _dot(a_bf16, hi) + _dot(a_bf16, mid) + _dot(a_bf16, lo)


def _dot_exact_rhs(x, b_bf16):
    hi, mid, lo = _split3(x)
    return _dot(hi, b_bf16) + _dot(mid, b_bf16) + _dot(lo, b_bf16)


def _softplus(z):
    return jnp.maximum(z, 0.0) + jnp.log1p(jnp.exp(-jnp.abs(z)))


def _sigmoid(z):
    return 1.0 / (1.0 + jnp.exp(-z))


def _inproj_kernel(x_ref, g_ref, wr_ref, wqt_ref, wk_ref, wvt_ref, wf_ref,
                   ur_ref, qt_ref, k_ref, vt_ref, fl_ref):
    x = x_ref[...]
    h = x * lax.rsqrt(jnp.mean(x * x, axis=-1, keepdims=True) + RMS_EPS) * g_ref[...]
    hb = h.astype(BF16)
    ur_ref[...] = _dot(hb, wr_ref[...])
    k_ref[...] = _dot(hb, wk_ref[...]).astype(BF16)
    fl_ref[...] = _dot(hb, wf_ref[...])
    qt = (_dot_nt(wqt_ref[...], hb) * Q_SCALE).astype(BF16)
    vt = _dot_nt(wvt_ref[...], hb).astype(BF16)
    for p in range(N_HEADS // 2):
        qt_ref[0, p, 0] = qt[p * LANES:(p + 1) * LANES]
        vt_ref[0, p, 0] = vt[p * LANES:(p + 1) * LANES]


def _inproj(x2, g, w_r, w_qt, w_k, w_vt, w_f, bsz, tm):
    n, d = x2.shape
    nt = n // bsz // tm
    pairs = N_HEADS // 2
    const = lambda i: (0, 0)
    row = lambda i: (i, 0)
    fm = lambda i: (i // nt, 0, i % nt, 0, 0)
    fm_sds = jax.ShapeDtypeStruct((bsz, pairs, nt, LANES, tm), BF16)
    return pl.pallas_call(
        _inproj_kernel,
        grid=(n // tm,),
        in_specs=[
            pl.BlockSpec((tm, d), row),
            pl.BlockSpec((1, d), const),
            pl.BlockSpec(w_r.shape, const),
            pl.BlockSpec(w_qt.shape, const),
            pl.BlockSpec(w_k.shape, const),
            pl.BlockSpec(w_vt.shape, const),
            pl.BlockSpec(w_f.shape, const),
        ],
        out_specs=[
            pl.BlockSpec((tm, RWKV_IN), row),
            pl.BlockSpec((1, pairs, 1, LANES, tm), fm),
            pl.BlockSpec((tm, GROUP_WIDTH), row),
            pl.BlockSpec((1, pairs, 1, LANES, tm), fm),
            pl.BlockSpec((tm, LANES), row),
        ],
        out_shape=[
            jax.ShapeDtypeStruct((n, RWKV_IN), F32),
            fm_sds,
            jax.ShapeDtypeStruct((n, GROUP_WIDTH), BF16),
            fm_sds,
            jax.ShapeDtypeStruct((n, LANES), F32),
        ],
        compiler_params=_cparams(("parallel",)),
        name="inproj",
    )(x2, g, w_r, w_qt, w_k, w_vt, w_f)


def _fox_gate_kernel(fl_ref, fb_ref, sel_ref, c_ref, cend_ref, carry):
    tt = fl_ref.shape[1]

    @pl.when(pl.program_id(1) == 0)
    def _():
        carry[...] = jnp.zeros_like(carry)

    z = fl_ref[0] + fb_ref[...]
    log_f = jnp.minimum(z, 0.0) - jnp.log1p(jnp.exp(-jnp.abs(z)))
    ri = lax.broadcasted_iota(jnp.int32, (tt, tt), 0)
    ci = lax.broadcasted_iota(jnp.int32, (tt, tt), 1)
    tri = jnp.where(ri >= ci, 1.0, 0.0).astype(BF16)
    c = _dot_exact_lhs(tri, log_f) + carry[...]
    carry[...] = c[tt - 1:tt, :]
    cend_ref[0, 0] = c[tt - 1:tt, :]
    hi, mid, lo = _split3(c * LOG2E)
    c_ref[0] = (_dot(hi, sel_ref[0]) + _dot(mid, sel_ref[1]) + _dot(lo, sel_ref[2])).astype(BF16)


def _gate_piece_selectors():
    h = jnp.arange(LANES, dtype=jnp.int32)[:, None]
    col = jnp.arange(GROUP_WIDTH, dtype=jnp.int32)[None, :]
    sels = []
    for m in range(3):
        target = LANES * (h // 2) + 3 * (h % 2) + m
        sels.append(((col == target) & (h < N_HEADS)).astype(BF16))
    return jnp.stack(sels)


def _fox_gate(fl3, fb_pad, sel, tt):
    b, t, _ = fl3.shape
    return pl.pallas_call(
        _fox_gate_kernel,
        grid=(b, t // tt),
        in_specs=[
            pl.BlockSpec((1, tt, LANES), lambda i, j: (i, j, 0)),
            pl.BlockSpec((1, LANES), lambda i, j: (0, 0)),
            pl.BlockSpec((3, LANES, GROUP_WIDTH), lambda i, j: (0, 0, 0)),
        ],
        out_specs=[
            pl.BlockSpec((1, tt, GROUP_WIDTH), lambda i, j: (i, j, 0)),
            pl.BlockSpec((1, 1, 1, LANES), lambda i, j: (i, j, 0, 0)),
        ],
        out_shape=[
            jax.ShapeDtypeStruct((b, t, GROUP_WIDTH), BF16),
            jax.ShapeDtypeStruct((b, t // tt, 1, LANES), F32),
        ],
        scratch_shapes=[pltpu.VMEM((1, LANES), F32)],
        compiler_params=_cparams(("parallel", "arbitrary")),
        name="fox_gate",
    )(fl3, fb_pad, sel)


def _rwkv_prep_body(u_ref, mu_ref, w0_ref, wup_ref, a0_ref, aup_ref, gup_ref, kk_ref, ka_ref,
                      bd_ref, r_out, k_out, v_out, lw_out, kk_out, b_out, g_out, carry):
    tt = u_ref.shape[1]

    @pl.when(pl.program_id(1) == 0)
    def _():
        carry[...] = jnp.zeros_like(carry)

    u = u_ref[0]
    prev = pltpu.roll(u, 1, axis=0)
    row = lax.broadcasted_iota(jnp.int32, u.shape, 0)
    prev = jnp.where(row == 0, carry[...], prev)
    carry[...] = u[tt - 1:tt, :]
    us = u + (prev - u) * mu_ref[...]

    r = us[:, :GROUP_WIDTH]
    k = us[:, GROUP_WIDTH:2 * GROUP_WIDTH]
    v = us[:, 2 * GROUP_WIDTH:LORA_OFF]
    wa = us[:, LORA_OFF:LORA_OFF + LANES]
    gl = us[:, LORA_OFF + LANES:]

    w_lin = _dot(jnp.tanh(wa).astype(BF16), wup_ref[...])
    a_lin = _dot(wa.astype(BF16), aup_ref[...])
    w = -_softplus(-(w0_ref[...] + w_lin)) - 0.5
    lw_out[0] = -jnp.exp(w)
    a = _sigmoid(a0_ref[...] + a_lin)
    g_out[0] = _dot(_sigmoid(gl).astype(BF16), gup_ref[...])

    kkr = k * kk_ref[...]
    ss = _dot_exact_rhs(kkr * kkr, bd_ref[...])
    kk = kkr / jnp.maximum(jnp.sqrt(ss), 1e-12)
    r_out[0] = r
    k_out[0] = k * (1.0 + (a - 1.0) * ka_ref[...])
    v_out[0] = v
    kk_out[0] = kk
    b_out[0] = kk * a


def _rwkv_scan_body(r_ref, k_ref, v_ref, lw_ref, kk_ref, b_ref, g_ref, rk_ref, lnw_ref, lnb_ref,
                      bd_ref, o_ref, s_scr, *, n_chunks):
    c = RWKV_CHUNK
    w = MXU_WIDTH
    hpg = w // HEAD_DIM
    n_groups = GROUP_WIDTH // w

    @pl.when(pl.program_id(1) == 0)
    def _():
        s_scr[...] = jnp.zeros_like(s_scr)

    row = lax.broadcasted_iota(jnp.int32, (c, w), 0)
    u = lax.broadcasted_iota(jnp.int32, (c, w), 1) % HEAD_DIM
    strict = row > u
    incl = row >= u
    eye = jnp.where(row == u, 1.0, 0.0)
    level_masks = []
    s = 1
    while s < c:
        same = (row // (2 * s)) == (u // (2 * s))
        level_masks.append(same & ((row % (2 * s)) >= s) & ((u % (2 * s)) < s))
        s *= 2
    same_head = (lax.broadcasted_iota(jnp.int32, (w, w), 0) // HEAD_DIM
                 == lax.broadcasted_iota(jnp.int32, (w, w), 1) // HEAD_DIM)
    tri = jnp.where(lax.broadcasted_iota(jnp.int32, (c, c), 0)
                    >= lax.broadcasted_iota(jnp.int32, (c, c), 1), 1.0, 0.0).astype(BF16)

    def bdiag(x):
        xb = x.astype(BF16)
        tiled = jnp.concatenate([xb] * hpg, axis=0)
        return jnp.where(same_head, tiled, jnp.zeros_like(tiled))

    chains = [(ci, gi) for ci in range(n_chunks) for gi in range(n_groups)]
    lhs, rk_t, vbs, xcat, gam, rt32 = {}, {}, {}, {}, {}, {}
    for ci in range(n_chunks):
        rs = slice(ci * c, (ci + 1) * c)
        r = r_ref[0, rs, :]
        k = k_ref[0, rs, :]
        lw = lw_ref[0, rs, :]
        kk = kk_ref[0, rs, :]
        b = b_ref[0, rs, :]
        g_cum = _dot_exact_lhs(tri, lw)
        g_last = g_cum[c - 1:c, :]
        r_t = r * jnp.exp(g_cum)
        kk_t = kk * jnp.exp(g_cum - lw)
        e_neg = jnp.exp(-g_cum)
        b_n = b * e_neg
        k_n = k * e_neg
        e_end = jnp.exp(g_last - g_cum)
        b_e = (b * e_end).astype(BF16)
        k_e = (k * e_end).astype(BF16)
        gamma = jnp.exp(g_last)
        vb = v_ref[0, rs, :].astype(BF16)
        for gi in range(n_groups):
            gs = slice(gi * w, (gi + 1) * w)
            ch = (ci, gi)
            lhs[ch] = jnp.concatenate([r_t[:, gs].astype(BF16), kk_t[:, gs].astype(BF16)], axis=0)
            rk_t[ch] = (jnp.concatenate([bdiag(b_n[:, gs]), bdiag(k_n[:, gs])], axis=0),
                        bdiag(kk_t[:, gs]))
            vbs[ch] = vb[:, gs]
            xcat[ch] = jnp.concatenate([b_e[:, gs], k_e[:, gs]], axis=0)
            gam[ch] = gamma[:, gs]
            rt32[ch] = r_t[:, gs]

    p = {ch: _dot_nt(lhs[ch], rk_t[ch][0]) for ch in chains}
    l_b = {ch: jnp.where(strict, p[ch][c:, :w], 0.0) for ch in chains}
    l_k = {ch: jnp.where(strict, p[ch][c:, w:], 0.0).astype(BF16) for ch in chains}
    p_br = {ch: jnp.where(incl, p[ch][:c, :w], 0.0).astype(BF16) for ch in chains}
    p_kr = {ch: jnp.where(incl, p[ch][:c, w:], 0.0).astype(BF16) for ch in chains}
    v_bd = {ch: bdiag(vbs[ch]) for ch in chains}
    lkv = {ch: _dot(l_k[ch], v_bd[ch]) for ch in chains}

    t_inv = {ch: eye - jnp.where(level_masks[0], l_b[ch], 0.0) for ch in chains}
    for m in level_masks[1:]:
        tb = {ch: t_inv[ch].astype(BF16) for ch in chains}
        ct = {ch: _dot(jnp.where(m, l_b[ch], 0.0).astype(BF16), bdiag(tb[ch])) for ch in chains}
        t_inv = {ch: t_inv[ch] - _dot(tb[ch], bdiag(ct[ch])) for ch in chains}

    mm = {ch: _dot(t_inv[ch].astype(BF16),
                   jnp.concatenate([rk_t[ch][1], bdiag(lkv[ch])], axis=1)) for ch in chains}
    pm = {ch: _dot(p_br[ch], jnp.concatenate([bdiag(mm[ch][:, :w]), bdiag(mm[ch][:, w:])], axis=1))
          for ch in chains}
    pkv = {ch: _dot(p_kr[ch], v_bd[ch]) for ch in chains}
    nm = {ch: jnp.concatenate([(rt32[ch] - pm[ch][:, :w]).astype(BF16),
                               mm[ch][:, :w].astype(BF16)], axis=0) for ch in chains}
    n2 = {ch: pkv[ch] - pm[ch][:, w:] for ch in chains}

    state = [s_scr[gi] for gi in range(n_groups)]
    ys = []
    for ci in range(n_chunks):
        q = {gi: _dot_nt(nm[(ci, gi)], state[gi].astype(BF16)) for gi in range(n_groups)}
        y_parts = []
        for gi in range(n_groups):
            ch = (ci, gi)
            y_parts.append(q[gi][:c] + n2[ch])
            u_new = -(q[gi][c:] + mm[ch][:, w:])
            w_cat = jnp.concatenate([u_new.astype(BF16), vbs[ch]], axis=0)
            upd = _dot_tn(w_cat, xcat[ch])
            state[gi] = state[gi] * gam[ch] + jnp.where(same_head, upd, 0.0)
        ys.append(jnp.concatenate(y_parts, axis=1))
    for gi in range(n_groups):
        s_scr[gi] = state[gi]

    y = jnp.concatenate(ys, axis=0)
    r = r_ref[0]
    k = k_ref[0]
    v = v_ref[0]
    bd = bd_ref[...]
    inv_n = 1.0 / HEAD_DIM
    mean = _dot_exact_rhs(y, bd) * inv_n
    d = y - mean
    var = _dot_exact_rhs(d * d, bd) * inv_n
    yn = d * lax.rsqrt(var + RWKV_GN_EPS) * lnw_ref[...] + lnb_ref[...]
    bonus = _dot_exact_rhs(r * k * rk_ref[...], bd) * v
    o_ref[0] = ((yn + bonus) * g_ref[0]).astype(o_ref.dtype)


def _rwkv_kernel(u_ref, mu_ref, w0_ref, wup_ref, a0_ref, aup_ref, gup_ref, kkw_ref, ka_ref,
                 rk_ref, lnw_ref, lnb_ref, bd_ref, o_ref, carry, s_scr,
                 r_s, k_s, v_s, lw_s, kk_s, b_s, g_s, *, n_chunks):
    _rwkv_prep_body(u_ref, mu_ref, w0_ref, wup_ref, a0_ref, aup_ref, gup_ref, kkw_ref, ka_ref,
                    bd_ref, r_s, k_s, v_s, lw_s, kk_s, b_s, g_s, carry)
    _rwkv_scan_body(r_s, k_s, v_s, lw_s, kk_s, b_s, g_s, rk_ref, lnw_ref, lnb_ref, bd_ref,
                    o_ref, s_scr, n_chunks=n_chunks)


def _rwkv_mix(u3, mu, w0, wup_pad, a0, aup_pad, gup, k_k, k_a, r_k, ln_w, ln_b, bd, n_chunks):
    bsz, t, _ = u3.shape
    rows = RWKV_CHUNK * n_chunks
    const = lambda i, j: (0, 0)
    tile = lambda i, j: (i, j, 0)
    vec = pl.BlockSpec((1, GROUP_WIDTH), const)
    staged = pltpu.VMEM((1, rows, GROUP_WIDTH), F32)
    return pl.pallas_call(
        functools.partial(_rwkv_kernel, n_chunks=n_chunks),
        grid=(bsz, t // rows),
        in_specs=[
            pl.BlockSpec((1, rows, RWKV_IN), tile),
            pl.BlockSpec((1, RWKV_IN), const),
            vec,
            pl.BlockSpec((LANES, GROUP_WIDTH), const),
            vec,
            pl.BlockSpec((LANES, GROUP_WIDTH), const),
            pl.BlockSpec((GATE_LORA, GROUP_WIDTH), const),
            vec, vec, vec, vec, vec,
            pl.BlockSpec((GROUP_WIDTH, GROUP_WIDTH), const),
        ],
        out_specs=pl.BlockSpec((1, rows, GROUP_WIDTH), tile),
        out_shape=jax.ShapeDtypeStruct((bsz, t, GROUP_WIDTH), BF16),
        scratch_shapes=[
            pltpu.VMEM((1, RWKV_IN), F32),
            pltpu.VMEM((GROUP_WIDTH // MXU_WIDTH, MXU_WIDTH, MXU_WIDTH), F32),
        ] + [staged] * 7,
        compiler_params=_cparams(("parallel", "arbitrary")),
        name="rwkv_mix",
    )(u3, mu, w0, wup_pad, a0, aup_pad, gup, k_k, k_a, r_k, ln_w, ln_b, bd)


def _fox_attn_kernel(qt_ref, k_ref, ce_ref, vt_ref, cend_ref, og_ref, o_ref,
                     m_scr, l_scr, acc_scr, kmax_scr, *, t):
    qi = pl.program_id(2)
    n_strips = 2 * t // LANES
    qt = qt_ref[0, 0, 0]
    frow = lax.broadcasted_iota(jnp.int32, (LANES, t), 0)
    zero = jnp.zeros_like(qt)
    main = jnp.concatenate([jnp.where(frow < HEAD_DIM, qt, zero),
                            jnp.where(frow < HEAD_DIM, zero, qt)], axis=1)
    erow = lax.broadcasted_iota(jnp.int32, (LANES, 2 * t), 0)
    ecol = lax.broadcasted_iota(jnp.int32, (LANES, 2 * t), 1)
    off = jnp.where(ecol < t, 0, 3)
    extra = jnp.where((erow >= off) & (erow < off + 3), -1.0, 0.0).astype(BF16)
    q_aug = jnp.concatenate([main, extra], axis=0)

    m_scr[...] = jnp.full_like(m_scr, -jnp.inf)
    l_scr[...] = jnp.zeros_like(l_scr)
    acc_scr[...] = jnp.zeros_like(acc_scr)

    @pl.when(qi == 0)
    def _():
        hid_r = lax.broadcasted_iota(jnp.int32, (LANES, LANES), 0) // HEAD_DIM
        hid_c = lax.broadcasted_iota(jnp.int32, (LANES, LANES), 1) // HEAD_DIM
        same_head = jnp.where(hid_r == hid_c, 1.0, 0.0).astype(BF16)

        def tile_max(j, best):
            kf = k_ref[0, pl.ds(pl.multiple_of(j * t, t), t), :].astype(F32)
            sq = _dot((kf * kf).astype(BF16), same_head)
            return jnp.maximum(best, jnp.max(sq, axis=0, keepdims=True))

        best = lax.fori_loop(0, k_ref.shape[1] // t, tile_max, jnp.zeros((1, LANES), F32))
        kmax_scr[...] = jnp.sqrt(best) * NORM_SLACK

    def step(j, masked):
        start = pl.multiple_of(j * t, t)
        k_aug = jnp.concatenate([k_ref[0, pl.ds(start, t), :], ce_ref[0, pl.ds(start, t), :]],
                                axis=1)
        vt = vt_ref[0, 0, j]
        zt = _dot(k_aug, q_aug)
        m_prev = m_scr[...]
        l_prev = l_scr[...]
        acc_prev = acc_scr[...]
        m_out, l_out, acc_out = [], [], [[], []]
        for s in range(n_strips):
            head = s // (n_strips // 2)
            cs = slice(s * LANES, (s + 1) * LANES)
            z = zt[:, cs]
            if masked:
                key = lax.broadcasted_iota(jnp.int32, (t, LANES), 0)
                qry = lax.broadcasted_iota(jnp.int32, (t, LANES), 1) + (s * LANES) % t
                z = jnp.where(key <= qry, z, -jnp.inf)
            m_new = jnp.maximum(m_prev[:, cs], jnp.max(z, axis=0, keepdims=True))
            alpha = jnp.exp2(m_prev[:, cs] - m_new)
            p = jnp.exp2(z - m_new)
            l_out.append(alpha * l_prev[:, cs] + jnp.sum(p, axis=0, keepdims=True))
            m_out.append(m_new)
            hs = slice(head * HEAD_DIM, (head + 1) * HEAD_DIM)
            qs = slice((s * LANES) % t, (s * LANES) % t + LANES)
            pv = _dot(vt[hs], p.astype(BF16))
            acc_out[head].append(alpha * acc_prev[hs, qs] + pv)
        m_scr[...] = jnp.concatenate(m_out, axis=1)
        l_scr[...] = jnp.concatenate(l_out, axis=1)
        acc_scr[...] = jnp.concatenate([jnp.concatenate(acc_out[0], axis=1),
                                        jnp.concatenate(acc_out[1], axis=1)], axis=0)

    def body(j, carry):
        step(j, False)
        return carry

    step(qi, True)

    qf = qt.astype(F32)
    qsq = qf * qf
    qnorm = jnp.sqrt(jnp.concatenate(
        [jnp.sum(qsq[:HEAD_DIM], axis=0, keepdims=True),
         jnp.sum(qsq[HEAD_DIM:], axis=0, keepdims=True)], axis=1)) * NORM_SLACK
    kmax = kmax_scr[...]
    kmax2 = jnp.concatenate([jnp.broadcast_to(kmax[:, 0:1], (1, t)),
                             jnp.broadcast_to(kmax[:, HEAD_DIM:HEAD_DIM + 1], (1, t))], axis=1)
    slack = qnorm * kmax2 - m_scr[...]
    cend = cend_ref[0][:, 0, :]
    lane = lax.broadcasted_iota(jnp.int32, cend.shape, 1)
    jrow = lax.broadcasted_iota(jnp.int32, (cend.shape[0], 1), 0)
    needed = jrow < 0
    for head in range(2):
        worst = jnp.max(slack[:, head * t:(head + 1) * t], axis=1, keepdims=True)
        c_head = jnp.sum(jnp.where(lane == 2 * pl.program_id(1) + head, cend, 0.0),
                         axis=1, keepdims=True)
        needed = needed | (worst - c_head * LOG2E > ZERO_PROB_EXP)
    first = jnp.min(jnp.where(needed & (jrow < qi), jrow, qi))
    lax.fori_loop(first, qi, body, 0)

    l = l_scr[...]
    acc = acc_scr[...]
    inv_n = 1.0 / HEAD_DIM
    halves = []
    for head in range(2):
        o = acc[head * HEAD_DIM:(head + 1) * HEAD_DIM] / l[:, head * t:(head + 1) * t]
        halves.append(o * lax.rsqrt(jnp.sum(o * o, axis=0, keepdims=True) * inv_n + RMS_EPS))
    o_t = jnp.concatenate(halves, axis=0)
    o_ref[0] = (jnp.transpose(o_t) * og_ref[...]).astype(o_ref.dtype)


def _fox_attn(qt5, k, cext, vt5, cend, out_g, tile):
    bsz, t_all, _ = k.shape
    pairs = N_HEADS // 2
    tiles = t_all // tile
    kern = functools.partial(_fox_attn_kernel, t=tile)
    return pl.pallas_call(
        kern,
        grid=(bsz, pairs, tiles),
        in_specs=[
            pl.BlockSpec((1, 1, 1, LANES, tile), lambda b, p, i: (b, p, i, 0, 0)),
            pl.BlockSpec((1, t_all, LANES), lambda b, p, i: (b, 0, p)),
            pl.BlockSpec((1, t_all, LANES), lambda b, p, i: (b, 0, p)),
            pl.BlockSpec((1, 1, tiles, LANES, tile), lambda b, p, i: (b, p, 0, 0, 0)),
            pl.BlockSpec((1, tiles, 1, LANES), lambda b, p, i: (b, 0, 0, 0)),
            pl.BlockSpec((1, LANES), lambda b, p, i: (0, p)),
        ],
        out_specs=pl.BlockSpec((1, tile, LANES), lambda b, p, i: (b, i, p)),
        out_shape=jax.ShapeDtypeStruct((bsz, t_all, GROUP_WIDTH), BF16),
        scratch_shapes=[
            pltpu.VMEM((1, 2 * tile), F32),
            pltpu.VMEM((1, 2 * tile), F32),
            pltpu.VMEM((LANES, tile), F32),
            pltpu.VMEM((1, LANES), F32),
        ],
        compiler_params=_cparams(("parallel", "parallel", "arbitrary")),
        name="fox_attn",
    )(qt5, k, cext, vt5, cend, out_g)


def _outproj_router_kernel(x_ref, yr_ref, yf_ref, wo_r_ref, wo_f_ref, g_ref, rwt_ref, rb_ref,
                           x1_ref, h_ref, idx_ref, gate_ref, rank_ref, count_ref):
    @pl.when(pl.program_id(0) == 0)
    def _():
        count_ref[...] = jnp.zeros_like(count_ref)

    x1 = x_ref[...] + _dot(yr_ref[...], wo_r_ref[...]) + _dot(yf_ref[...], wo_f_ref[...])
    x1_ref[...] = x1
    h = x1 * lax.rsqrt(jnp.mean(x1 * x1, axis=-1, keepdims=True) + RMS_EPS) * g_ref[...]
    h_ref[:, 0, :] = h
    logits = lax.dot_general(rwt_ref[...], h, (((1,), (1,)), ((), ())),
                             precision=lax.Precision.HIGHEST,
                             preferred_element_type=F32) + rb_ref[...]
    eidx = lax.broadcasted_iota(jnp.int32, logits.shape, 0)
    vals, idxs, picks = [], [], []
    for _ in range(TOP_K):
        m = jnp.max(logits, axis=0, keepdims=True)
        i = jnp.min(jnp.where(logits == m, eidx, N_EXPERTS), axis=0, keepdims=True)
        vals.append(m)
        idxs.append(i)
        picks.append(eidx == i)
        logits = jnp.where(picks[-1], -jnp.inf, logits)
    es = [jnp.exp(val - vals[0]) for val in vals]
    denom = es[0] + es[1] + es[2] + es[3]
    idx_ref[...] = jnp.concatenate(idxs, axis=0)
    gate_ref[...] = jnp.concatenate([e / denom for e in es], axis=0)

    tm = logits.shape[1]
    chosen = [jnp.where(pk, 1.0, 0.0) for pk in picks]
    any_k = chosen[0] + chosen[1] + chosen[2] + chosen[3]
    before = (lax.broadcasted_iota(jnp.int32, (tm, tm), 0)
              < lax.broadcasted_iota(jnp.int32, (tm, tm), 1))
    prefix = _dot(any_k.astype(BF16), jnp.where(before, 1.0, 0.0).astype(BF16))
    seen = count_ref[:, 0:1] + prefix
    rank_ref[...] = jnp.concatenate(
        [jnp.sum(ch * seen, axis=0, keepdims=True) for ch in chosen], axis=0).astype(jnp.int32)
    count_ref[...] = count_ref[...] + jnp.sum(any_k, axis=1, keepdims=True)


def _outproj_router(x2, yr, yf, wo_r, wo_f, g, rwt, rb, tm):
    n, d = x2.shape
    const = lambda i: (0, 0)
    row = lambda i: (i, 0)
    col = lambda i: (0, i)
    return pl.pallas_call(
        _outproj_router_kernel,
        grid=(n // tm,),
        in_specs=[
            pl.BlockSpec((tm, d), row),
            pl.BlockSpec((tm, GROUP_WIDTH), row),
            pl.BlockSpec((tm, GROUP_WIDTH), row),
            pl.BlockSpec((GROUP_WIDTH, d), const),
            pl.BlockSpec((GROUP_WIDTH, d), const),
            pl.BlockSpec((1, d), const),
            pl.BlockSpec((N_EXPERTS, d), const),
            pl.BlockSpec((N_EXPERTS, 1), const),
        ],
        out_specs=[
            pl.BlockSpec((tm, d), row),
            pl.BlockSpec((tm, 1, d), lambda i: (i, 0, 0)),
            pl.BlockSpec((TOP_K, tm), col),
            pl.BlockSpec((TOP_K, tm), col),
            pl.BlockSpec((TOP_K, tm), col),
            pl.BlockSpec((N_EXPERTS, LANES), const),
        ],
        out_shape=[
            jax.ShapeDtypeStruct((n, d), F32),
            jax.ShapeDtypeStruct((n, 1, d), F32),
            jax.ShapeDtypeStruct((TOP_K, n), jnp.int32),
            jax.ShapeDtypeStruct((TOP_K, n), F32),
            jax.ShapeDtypeStruct((TOP_K, n), jnp.int32),
            jax.ShapeDtypeStruct((N_EXPERTS, LANES), F32),
        ],
        compiler_params=_cparams(("arbitrary",)),
        name="outproj_router",
    )(x2, yr, yf, wo_r, wo_f, g, rwt, rb)


def _w1_split_kernel(w_ref, perm_ref, g_ref, l_ref):
    half = MXU_WIDTH // 2
    perm = perm_ref[...]
    for grp in range(w_ref.shape[2] // MXU_WIDTH):
        blk = w_ref[0, :, grp * MXU_WIDTH:(grp + 1) * MXU_WIDTH].astype(BF16)
        r = _dot(blk, perm)
        g_ref[0, :, grp * half:(grp + 1) * half] = r[:, :half].astype(BF16)
        l_ref[0, :, grp * half:(grp + 1) * half] = r[:, half:].astype(BF16)


def _w1_split(w1, tr):
    e, d, two_f = w1.shape
    half = MXU_WIDTH // 2
    src = jnp.arange(MXU_WIDTH, dtype=jnp.int32)[:, None]
    dst = jnp.arange(MXU_WIDTH, dtype=jnp.int32)[None, :]
    perm = (src == jnp.where(dst < half, 2 * dst, 2 * (dst - half) + 1)).astype(BF16)
    out_sds = jax.ShapeDtypeStruct((e, d, two_f // 2), BF16)
    return pl.pallas_call(
        _w1_split_kernel,
        grid=(e, d // tr),
        in_specs=[
            pl.BlockSpec((1, tr, two_f), lambda i, j: (i, j, 0)),
            pl.BlockSpec((MXU_WIDTH, MXU_WIDTH), lambda i, j: (0, 0)),
        ],
        out_specs=[pl.BlockSpec((1, tr, two_f // 2), lambda i, j: (i, j, 0))] * 2,
        out_shape=[out_sds, out_sds],
        compiler_params=_cparams(("parallel", "parallel")),
        name="w1_split",
    )(w1, perm)


def _expert_kernel(be_ref, tok_a_ref, tok_b_ref, tok_a_next_ref, dst_b_prev_ref, dst_a_ref,
                   dst_b_ref, h_hbm, w1g_a, w1l_a, b1g_a, b1l_a, w2_a, b2_a,
                   w1g_b, w1l_b, b1g_b, b1l_b, w2_b, b2_b,
                   y_hbm, xbuf_a, xbuf_b, obuf_a, obuf_b, xrows, gsem, osem, *, bm, n_real_rows):
    del be_ref
    i = pl.program_id(0)

    def gather_start(idx_ref, xbuf, sem, r):
        pltpu.make_async_copy(h_hbm.at[idx_ref[0, 0, r]], xbuf.at[r], sem).start()

    def scatter_start(idx_ref, obuf, sem, r):
        pltpu.make_async_copy(obuf.at[r], y_hbm.at[idx_ref[0, 0, r]], sem).start()

    def rows_wait(buf, sem):
        pltpu.make_async_copy(buf, buf, sem).wait()

    def mlp(xbuf, w1g_ref, w1l_ref, b1g_ref, b1l_ref, w2_ref, b2_ref):
        xrows[...] = xbuf[:, 0, :]
        xb = xrows[...].astype(BF16)
        dff = w1g_ref.shape[2]
        acts = []
        for piece in range(dff // MXU_WIDTH):
            cs = slice(piece * MXU_WIDTH, (piece + 1) * MXU_WIDTH)
            glu = _dot(xb, w1g_ref[0, :, cs]) + b1g_ref[0, :, cs]
            lin = _dot(xb, w1l_ref[0, :, cs]) + b1l_ref[0, :, cs]
            glu = jnp.minimum(glu, SWIGLU_LIMIT)
            lin = jnp.clip(lin, -SWIGLU_LIMIT, SWIGLU_LIMIT)
            acts.append((glu * _sigmoid(SWIGLU_ALPHA * glu) * (lin + 1.0)).astype(BF16))
        return _dot(jnp.concatenate(acts, axis=1), w2_ref[0]) + b2_ref[0]

    @pl.when(i == 0)
    def _():
        obuf_a[...] = jnp.zeros_like(obuf_a)
        obuf_b[...] = jnp.zeros_like(obuf_b)

        def first(r, carry):
            pltpu.make_async_copy(obuf_a.at[r], y_hbm.at[n_real_rows + 2 * bm + r],
                                  osem.at[0]).start()
            gather_start(tok_a_ref, xbuf_a, gsem.at[0], r)
            return carry

        lax.fori_loop(0, bm, first, 0)

    for r in range(bm):
        gather_start(tok_b_ref, xbuf_b, gsem.at[1], r)
        scatter_start(dst_b_prev_ref, obuf_b, osem.at[1], r)
    rows_wait(xbuf_a, gsem.at[0])
    rows_wait(obuf_a, osem.at[0])
    obuf_a[:, 0, :] = mlp(xbuf_a, w1g_a, w1l_a, b1g_a, b1l_a, w2_a, b2_a)

    for r in range(bm):
        gather_start(tok_a_next_ref, xbuf_a, gsem.at[0], r)
        scatter_start(dst_a_ref, obuf_a, osem.at[0], r)
    rows_wait(xbuf_b, gsem.at[1])
    rows_wait(obuf_b, osem.at[1])
    obuf_b[:, 0, :] = mlp(xbuf_b, w1g_b, w1l_b, b1g_b, b1l_b, w2_b, b2_b)

    @pl.when(i == pl.num_programs(0) - 1)
    def _():
        def last(r, carry):
            scatter_start(dst_b_ref, obuf_b, osem.at[1], r)
            return carry

        lax.fori_loop(0, bm, last, 0)
        rows_wait(obuf_b, osem.at[1])
        rows_wait(obuf_a, osem.at[0])
        rows_wait(xbuf_a, gsem.at[0])


def _expert_mlp(block_e, tok_blocks, dst_blocks, h2, w1g, w1l, b1g, b1l, w2, b2, bm):
    n_blocks = tok_blocks.shape[0]
    assert n_blocks % 2 == 0
    n, _, d = h2.shape
    dff = w1g.shape[2]
    n_real_rows = TOP_K * n
    idx_spec = lambda fn: pl.BlockSpec((1, 1, bm), fn, memory_space=pltpu.SMEM)

    def weight_specs(which):
        wmap = lambda i, be: (be[2 * i + which], 0, 0)
        return [
            pl.BlockSpec((1, d, dff), wmap),
            pl.BlockSpec((1, d, dff), wmap),
            pl.BlockSpec((1, 1, dff), wmap),
            pl.BlockSpec((1, 1, dff), wmap),
            pl.BlockSpec((1, dff, d), wmap),
            pl.BlockSpec((1, 1, d), wmap),
        ]

    grid_spec = pltpu.PrefetchScalarGridSpec(
        num_scalar_prefetch=1,
        grid=(n_blocks // 2,),
        in_specs=[
            idx_spec(lambda i, be: (2 * i, 0, 0)),
            idx_spec(lambda i, be: (2 * i + 1, 0, 0)),
            idx_spec(lambda i, be: (jnp.minimum(2 * i + 2, n_blocks - 1), 0, 0)),
            idx_spec(lambda i, be: (2 * i, 0, 0)),
            idx_spec(lambda i, be: (2 * i + 1, 0, 0)),
            idx_spec(lambda i, be: (2 * i + 2, 0, 0)),
            pl.BlockSpec(memory_space=pl.ANY),
        ] + weight_specs(0) + weight_specs(1),
        out_specs=pl.BlockSpec(memory_space=pl.ANY),
        scratch_shapes=[
            pltpu.VMEM((bm, 1, d), F32),
            pltpu.VMEM((bm, 1, d), F32),
            pltpu.VMEM((bm, 1, d), F32),
            pltpu.VMEM((bm, 1, d), F32),
            pltpu.VMEM((bm, d), F32),
            pltpu.SemaphoreType.DMA((2,)),
            pltpu.SemaphoreType.DMA((2,)),
        ],
    )
    weights = (w1g, w1l, b1g, b1l, w2, b2)
    return pl.pallas_call(
        functools.partial(_expert_kernel, bm=bm, n_real_rows=n_real_rows),
        grid_spec=grid_spec,
        out_shape=jax.ShapeDtypeStruct((n_real_rows + 3 * bm, 1, d), F32),
        compiler_params=_cparams(("arbitrary",)),
        name="expert_mlp",
    )(block_e, tok_blocks, tok_blocks, tok_blocks, dst_blocks, dst_blocks, dst_blocks, h2,
      *weights, *weights)


def _combine_kernel(gate_ref, x1_ref, g_ref, y0_ref, y1_ref, y2_ref, y3_ref, o_ref):
    gates = gate_ref[...]
    y = x1_ref[...]
    for kk, y_ref in enumerate((y0_ref, y1_ref, y2_ref, y3_ref)):
        y = y + y_ref[:, 0, :] * gates[:, kk:kk + 1]
    o_ref[...] = y * lax.rsqrt(jnp.mean(y * y, axis=-1, keepdims=True) + RMS_EPS) * g_ref[...]


def _combine(gates_t, x1, g, y_all, tc):
    n, d = x1.shape
    tiles = n // tc
    y_spec = lambda kk: pl.BlockSpec((tc, 1, d), lambda i: (kk * tiles + i, 0, 0))
    return pl.pallas_call(
        _combine_kernel,
        grid=(tiles,),
        in_specs=[
            pl.BlockSpec((tc, TOP_K), lambda i: (i, 0)),
            pl.BlockSpec((tc, d), lambda i: (i, 0)),
            pl.BlockSpec((1, d), lambda i: (0, 0)),
        ] + [y_spec(kk) for kk in range(TOP_K)],
        out_specs=pl.BlockSpec((tc, d), lambda i: (i, 0)),
        out_shape=jax.ShapeDtypeStruct((n, d), F32),
        compiler_params=_cparams(("parallel",)),
        name="combine",
    )(gates_t, x1, g, y_all, y_all, y_all, y_all)


def _dispatch_plan(idx, rank, counts, bm):
    n = idx.shape[1]
    n_slots = TOP_K * n
    sizes = counts[:, 0].astype(jnp.int32)
    padded = (sizes + bm - 1) // bm * bm
    pad_ends = jnp.cumsum(padded)
    pad_starts = pad_ends - padded
    experts = jnp.arange(N_EXPERTS, dtype=jnp.int32)
    start_of = jnp.sum(jnp.where(idx[..., None] == experts, pad_starts, 0), axis=-1)
    pos = (start_of + rank).reshape(-1)
    n_pad = n_slots + N_EXPERTS * bm
    n_blocks = n_pad // bm
    slot_src = jnp.full((n_pad,), -1, jnp.int32).at[pos].set(
        jnp.arange(n_slots, dtype=jnp.int32), unique_indices=True)
    p = jnp.arange(n_pad, dtype=jnp.int32)
    spare = n_slots + (p // bm) % 2 * bm + p % bm
    real = slot_src >= 0
    tok_blocks = jnp.where(real, slot_src % n, 0).reshape(n_blocks, 1, bm)
    dst = jnp.where(real, slot_src, spare)
    dst_blocks = jnp.concatenate([spare[bm:2 * bm], dst]).reshape(n_blocks + 1, 1, bm)
    block_start = jnp.arange(n_blocks, dtype=jnp.int32) * bm
    block_e = jnp.minimum(jnp.sum(pad_ends[None, :] <= block_start[:, None], axis=1),
                          N_EXPERTS - 1).astype(jnp.int32)
    return tok_blocks, dst_blocks, block_e


def _block_diag_ones():
    hid = jnp.arange(GROUP_WIDTH, dtype=jnp.int32) // HEAD_DIM
    return (hid[:, None] == hid[None, :]).astype(BF16)


def _pick(n, pref):
    return pref if n % pref == 0 else n


def kernel(x, attn_norm_g, w_in, rwkv_mu, rwkv_w0, rwkv_w_up, rwkv_a0, rwkv_a_up, rwkv_g_up,
           rwkv_k_k, rwkv_k_a, rwkv_r_k, rwkv_ln_w, rwkv_ln_b, fox_f_bias, fox_out_g, w_out,
           ffn_norm_g, router_w, router_b, expert_w1, expert_b1, expert_w2, expert_b2,
           final_norm_g):
    bsz, t, d = x.shape
    n = bsz * t
    depth = w_in.shape[0]
    assert depth == 1, "the final norm is fused into the last stage of a single layer"
    bd = _block_diag_ones()
    x2 = x.reshape(n, d)
    for l in range(depth):
        w_l = w_in[l]
        w_r = w_l[:, :RWKV_IN].astype(BF16)
        w_qkv = w_l[:, RWKV_IN:RWKV_IN + 3 * GROUP_WIDTH].astype(BF16)
        w_qt = w_qkv[:, :GROUP_WIDTH].T
        w_k = w_qkv[:, GROUP_WIDTH:2 * GROUP_WIDTH]
        w_vt = w_qkv[:, 2 * GROUP_WIDTH:].T
        w_f = jnp.pad(w_l[:, RWKV_IN + 3 * GROUP_WIDTH:], ((0, 0), (0, LANES - N_HEADS))).astype(BF16)
        fb_pad = jnp.pad(fox_f_bias[l], (0, LANES - N_HEADS)).reshape(1, LANES)
        wup_pad = jnp.pad(rwkv_w_up[l], ((0, LANES - DECAY_LORA), (0, 0))).astype(BF16)
        aup_pad = jnp.pad(rwkv_a_up[l], ((DECAY_LORA, 0), (0, 0))).astype(BF16)
        gup = rwkv_g_up[l].astype(BF16)
        vec = lambda a: a.reshape(1, -1)

        u_r, qt5, k, vt5, fl = _inproj(x2, vec(attn_norm_g[l]), w_r, w_qt, w_k, w_vt, w_f,
                                       bsz, ATTN_TILE)
        cext, cend = _fox_gate(fl.reshape(bsz, t, LANES), fb_pad, _gate_piece_selectors(),
                               ATTN_TILE)
        y_rwkv = _rwkv_mix(
            u_r.reshape(bsz, t, RWKV_IN), vec(rwkv_mu[l]), vec(rwkv_w0[l]), wup_pad,
            vec(rwkv_a0[l]), aup_pad, gup, vec(rwkv_k_k[l]), vec(rwkv_k_a[l]), vec(rwkv_r_k[l]),
            vec(rwkv_ln_w[l]), vec(rwkv_ln_b[l]), bd, RWKV_CHUNKS_PER_STEP)
        y_fox = _fox_attn(qt5, k.reshape(bsz, t, GROUP_WIDTH), cext, vt5, cend,
                          vec(fox_out_g[l]), ATTN_TILE)

        wo = w_out[l].astype(BF16)
        x1, h2, idx, gates, rank, counts = _outproj_router(
            x2, y_rwkv.reshape(n, GROUP_WIDTH), y_fox.reshape(n, GROUP_WIDTH),
            wo[:GROUP_WIDTH], wo[GROUP_WIDTH:], vec(ffn_norm_g[l]),
            router_w[l].T, router_b[l].reshape(N_EXPERTS, 1), _pick(n, 512))

        bm = 256
        tok_blocks, dst_blocks, block_e = _dispatch_plan(idx, rank, counts, bm)
        w1g, w1l = _w1_split(expert_w1[l], 512)
        b1 = expert_b1[l]
        b1g = b1[:, None, 0::2]
        b1l = b1[:, None, 1::2]
        y_all = _expert_mlp(block_e, tok_blocks, dst_blocks, h2, w1g, w1l, b1g, b1l,
                            expert_w2[l].astype(BF16), expert_b2[l][:, None, :], bm)
        x2 = _combine(gates.T, x1, vec(final_norm_g), y_all, _pick(n, 256))
    return x2.reshape(bsz, t, d)
```

```python
import functools

import jax
import jax.numpy as jnp
from jax import lax
from jax.experimental import pallas as pl
from jax.experimental.pallas import tpu as pltpu

F32 = jnp.float32
BF16 = jnp.bfloat16

HEAD_DIM = 64
N_HEADS = 8
GROUP_WIDTH = N_HEADS * HEAD_DIM
DECAY_LORA = 64
AAA_LORA = 64
GATE_LORA = 128
RWKV_IN = 3 * GROUP_WIDTH + DECAY_LORA + AAA_LORA + GATE_LORA
LORA_OFF = 3 * GROUP_WIDTH
N_EXPERTS = 32
TOP_K = 4
SWIGLU_ALPHA = 1.702
SWIGLU_LIMIT = 7.0
RMS_EPS = 1e-5
RWKV_GN_EPS = 64e-5
LANES = 128
RWKV_CHUNK = 64
RWKV_CHUNKS_PER_STEP = 4
ATTN_TILE = 512
MXU_WIDTH = 256
LOG2E = 1.4426950408889634
Q_SCALE = HEAD_DIM ** -0.5 * LOG2E
ZERO_PROB_EXP = -152.0
NORM_SLACK = 1.0 + 2.0 ** -6
VMEM_LIMIT = 56 * 1024 * 1024


def _cparams(semantics):
    return pltpu.CompilerParams(dimension_semantics=semantics, vmem_limit_bytes=VMEM_LIMIT)


def _dot(a, b):
    return jnp.dot(a, b, preferred_element_type=F32)


def _dot_nt(a, b):
    return lax.dot_general(a, b, (((1,), (1,)), ((), ())), preferred_element_type=F32)


def _dot_tn(a, b):
    return lax.dot_general(a, b, (((0,), (0,)), ((), ())), preferred_element_type=F32)


def _split3(x):
    hi = x.astype(BF16)
    r1 = x - hi.astype(F32)
    mid = r1.astype(BF16)
    lo = (r1 - mid.astype(F32)).astype(BF16)
    return hi, mid, lo


def _dot_exact_lhs(a_bf16, x):
    hi, mid, lo = _split3(x)
    return _dot(a_bf16, hi) + _dot(a_bf16, mid) + _dot(a_bf16, lo)


def _dot_exact_rhs(x, b_bf16):
    hi, mid, lo = _split3(x)
    return _dot(hi, b_bf16) + _dot(mid, b_bf16) + _dot(lo, b_bf16)


def _softplus(z):
    return jnp.maximum(z, 0.0) + jnp.log1p(jnp.exp(-jnp.abs(z)))


def _sigmoid(z):
    return 1.0 / (1.0 + jnp.exp(-z))


def _inproj_kernel(x_ref, g_ref, wr_ref, wqt_ref, wk_ref, wvt_ref, wf_ref,
                   ur_ref, qt_ref, k_ref, vt_ref, fl_ref):
    x = x_ref[...]
    h = x * lax.rsqrt(jnp.mean(x * x, axis=-1, keepdims=True) + RMS_EPS) * g_ref[...]
    hb = h.astype(BF16)
    ur_ref[...] = _dot(hb, wr_ref[...])
    k_ref[...] = _dot(hb, wk_ref[...]).astype(BF16)
    fl_ref[...] = _dot(hb, wf_ref[...])
    qt = (_dot_nt(wqt_ref[...], hb) * Q_SCALE).astype(BF16)
    vt = _dot_nt(wvt_ref[...], hb).astype(BF16)
    for p in range(N_HEADS // 2):
        qt_ref[0, p, 0] = qt[p * LANES:(p + 1) * LANES]
        vt_ref[0, p, 0] = vt[p * LANES:(p + 1) * LANES]


def _inproj(x2, g, w_r, w_qt, w_k, w_vt, w_f, bsz, tm):
    n, d = x2.shape
    nt = n // bsz // tm
    pairs = N_HEADS // 2
    const = lambda i: (0, 0)
    row = lambda i: (i, 0)
    fm = lambda i: (i // nt, 0, i % nt, 0, 0)
    fm_sds = jax.ShapeDtypeStruct((bsz, pairs, nt, LANES, tm), BF16)
    return pl.pallas_call(
        _inproj_kernel,
        grid=(n // tm,),
        in_specs=[
            pl.BlockSpec((tm, d), row),
            pl.BlockSpec((1, d), const),
            pl.BlockSpec(w_r.shape, const),
            pl.BlockSpec(w_qt.shape, const),
            pl.BlockSpec(w_k.shape, const),
            pl.BlockSpec(w_vt.shape, const),
            pl.BlockSpec(w_f.shape, const),
        ],
        out_specs=[
            pl.BlockSpec((tm, RWKV_IN), row),
            pl.BlockSpec((1, pairs, 1, LANES, tm), fm),
            pl.BlockSpec((tm, GROUP_WIDTH), row),
            pl.BlockSpec((1, pairs, 1, LANES, tm), fm),
            pl.BlockSpec((tm, LANES), row),
        ],
        out_shape=[
            jax.ShapeDtypeStruct((n, RWKV_IN), F32),
            fm_sds,
            jax.ShapeDtypeStruct((n, GROUP_WIDTH), BF16),
            fm_sds,
            jax.ShapeDtypeStruct((n, LANES), F32),
        ],
        compiler_params=_cparams(("parallel",)),
        name="inproj",
    )(x2, g, w_r, w_qt, w_k, w_vt, w_f)


def _fox_gate_kernel(fl_ref, fb_ref, sel_ref, c_ref, cend_ref, carry):
    tt = fl_ref.shape[1]

    @pl.when(pl.program_id(1) == 0)
    def _():
        carry[...] = jnp.zeros_like(carry)

    z = fl_ref[0] + fb_ref[...]
    log_f = jnp.minimum(z, 0.0) - jnp.log1p(jnp.exp(-jnp.abs(z)))
    ri = lax.broadcasted_iota(jnp.int32, (tt, tt), 0)
    ci = lax.broadcasted_iota(jnp.int32, (tt, tt), 1)
    tri = jnp.where(ri >= ci, 1.0, 0.0).astype(BF16)
    c = _dot_exact_lhs(tri, log_f) + carry[...]
    carry[...] = c[tt - 1:tt, :]
    cend_ref[0, 0] = c[tt - 1:tt, :]
    hi, mid, lo = _split3(c * LOG2E)
    c_ref[0] = (_dot(hi, sel_ref[0]) + _dot(mid, sel_ref[1]) + _dot(lo, sel_ref[2])).astype(BF16)


def _gate_piece_selectors():
    h = jnp.arange(LANES, dtype=jnp.int32)[:, None]
    col = jnp.arange(GROUP_WIDTH, dtype=jnp.int32)[None, :]
    sels = []
    for m in range(3):
        target = LANES * (h // 2) + 3 * (h % 2) + m
        sels.append(((col == target) & (h < N_HEADS)).astype(BF16))
    return jnp.stack(sels)


def _fox_gate(fl3, fb_pad, sel, tt):
    b, t, _ = fl3.shape
    return pl.pallas_call(
        _fox_gate_kernel,
        grid=(b, t // tt),
        in_specs=[
            pl.BlockSpec((1, tt, LANES), lambda i, j: (i, j, 0)),
            pl.BlockSpec((1, LANES), lambda i, j: (0, 0)),
            pl.BlockSpec((3, LANES, GROUP_WIDTH), lambda i, j: (0, 0, 0)),
        ],
        out_specs=[
            pl.BlockSpec((1, tt, GROUP_WIDTH), lambda i, j: (i, j, 0)),
            pl.BlockSpec((1, 1, 1, LANES), lambda i, j: (i, j, 0, 0)),
        ],
        out_shape=[
            jax.ShapeDtypeStruct((b, t, GROUP_WIDTH), BF16),
            jax.ShapeDtypeStruct((b, t // tt, 1, LANES), F32),
        ],
        scratch_shapes=[pltpu.VMEM((1, LANES), F32)],
        compiler_params=_cparams(("parallel", "arbitrary")),
        name="fox_gate",
    )(fl3, fb_pad, sel)


def _rwkv_prep_body(u_ref, mu_ref, w0_ref, wup_ref, a0_ref, aup_ref, gup_ref, kk_ref, ka_ref,
                      bd_ref, r_out, k_out, v_out, lw_out, kk_out, b_out, g_out, carry):
    tt = u_ref.shape[1]

    @pl.when(pl.program_id(1) == 0)
    def _():
        carry[...] = jnp.zeros_like(carry)

    u = u_ref[0]
    prev = pltpu.roll(u, 1, axis=0)
    row = lax.broadcasted_iota(jnp.int32, u.shape, 0)
    prev = jnp.where(row == 0, carry[...], prev)
    carry[...] = u[tt - 1:tt, :]
    us = u + (prev - u) * mu_ref[...]

    r = us[:, :GROUP_WIDTH]
    k = us[:, GROUP_WIDTH:2 * GROUP_WIDTH]
    v = us[:, 2 * GROUP_WIDTH:LORA_OFF]
    wa = us[:, LORA_OFF:LORA_OFF + LANES]
    gl = us[:, LORA_OFF + LANES:]

    w_lin = _dot(jnp.tanh(wa).astype(BF16), wup_ref[...])
    a_lin = _dot(wa.astype(BF16), aup_ref[...])
    w = -_softplus(-(w0_ref[...] + w_lin)) - 0.5
    lw_out[0] = -jnp.exp(w)
    a = _sigmoid(a0_ref[...] + a_lin)
    g_out[0] = _dot(_sigmoid(gl).astype(BF16), gup_ref[...])

    kkr = k * kk_ref[...]
    ss = _dot_exact_rhs(kkr * kkr, bd_ref[...])
    kk = kkr / jnp.maximum(jnp.sqrt(ss), 1e-12)
    r_out[0] = r
    k_out[0] = k * (1.0 + (a - 1.0) * ka_ref[...])
    v_out[0] = v
    kk_out[0] = kk
    b_out[0] = kk * a


def _rwkv_scan_body(r_ref, k_ref, v_ref, lw_ref, kk_ref, b_ref, g_ref, rk_ref, lnw_ref, lnb_ref,
                      bd_ref, o_ref, s_scr, *, n_chunks):
    c = RWKV_CHUNK
    w = MXU_WIDTH
    hpg = w // HEAD_DIM
    n_groups = GROUP_WIDTH // w

    @pl.when(pl.program_id(1) == 0)
    def _():
        s_scr[...] = jnp.zeros_like(s_scr)

    row = lax.broadcasted_iota(jnp.int32, (c, w), 0)
    u = lax.broadcasted_iota(jnp.int32, (c, w), 1) % HEAD_DIM
    strict = row > u
    incl = row >= u
    eye = jnp.where(row == u, 1.0, 0.0)
    level_masks = []
    s = 1
    while s < c:
        same = (row // (2 * s)) == (u // (2 * s))
        level_masks.append(same & ((row % (2 * s)) >= s) & ((u % (2 * s)) < s))
        s *= 2
    same_head = (lax.broadcasted_iota(jnp.int32, (w, w), 0) // HEAD_DIM
                 == lax.broadcasted_iota(jnp.int32, (w, w), 1) // HEAD_DIM)
    tri = jnp.where(lax.broadcasted_iota(jnp.int32, (c, c), 0)
                    >= lax.broadcasted_iota(jnp.int32, (c, c), 1), 1.0, 0.0).astype(BF16)

    def bdiag(x):
        xb = x.astype(BF16)
        tiled = jnp.concatenate([xb] * hpg, axis=0)
        return jnp.where(same_head, tiled, jnp.zeros_like(tiled))

    chains = [(ci, gi) for ci in range(n_chunks) for gi in range(n_groups)]
    lhs, rk_t, vbs, xcat, gam, rt32 = {}, {}, {}, {}, {}, {}
    for ci in range(n_chunks):
        rs = slice(ci * c, (ci + 1) * c)
        r = r_ref[0, rs, :]
        k = k_ref[0, rs, :]
        lw = lw_ref[0, rs, :]
        kk = kk_ref[0, rs, :]
        b = b_ref[0, rs, :]
        g_cum = _dot_exact_lhs(tri, lw)
        g_last = g_cum[c - 1:c, :]
        r_t = r * jnp.exp(g_cum)
        kk_t = kk * jnp.exp(g_cum - lw)
        e_neg = jnp.exp(-g_cum)
        b_n = b * e_neg
        k_n = k * e_neg
        e_end = jnp.exp(g_last - g_cum)
        b_e = (b * e_end).astype(BF16)
        k_e = (k * e_end).astype(BF16)
        gamma = jnp.exp(g_last)
        vb = v_ref[0, rs, :].astype(BF16)
        for gi in range(n_groups):
            gs = slice(gi * w, (gi + 1) * w)
            ch = (ci, gi)
            lhs[ch] = jnp.concatenate([r_t[:, gs].astype(BF16), kk_t[:, gs].astype(BF16)], axis=0)
            rk_t[ch] = (jnp.concatenate([bdiag(b_n[:, gs]), bdiag(k_n[:, gs])], axis=0),
                        bdiag(kk_t[:, gs]))
            vbs[ch] = vb[:, gs]
            xcat[ch] = jnp.concatenate([b_e[:, gs], k_e[:, gs]], axis=0)
            gam[ch] = gamma[:, gs]
            rt32[ch] = r_t[:, gs]

    p = {ch: _dot_nt(lhs[ch], rk_t[ch][0]) for ch in chains}
    l_b = {ch: jnp.where(strict, p[ch][c:, :w], 0.0) for ch in chains}
    l_k = {ch: jnp.where(strict, p[ch][c:, w:], 0.0).astype(BF16) for ch in chains}
    p_br = {ch: jnp.where(incl, p[ch][:c, :w], 0.0).astype(BF16) for ch in chains}
    p_kr = {ch: jnp.where(incl, p[ch][:c, w:], 0.0).astype(BF16) for ch in chains}
    v_bd = {ch: bdiag(vbs[ch]) for ch in chains}
    lkv = {ch: _dot(l_k[ch], v_bd[ch]) for ch in chains}

    t_inv = {ch: eye - jnp.where(level_masks[0], l_b[ch], 0.0) for ch in chains}
    for m in level_masks[1:]:
        tb = {ch: t_inv[ch].astype(BF16) for ch in chains}
        ct = {ch: _dot(jnp.where(m, l_b[ch], 0.0).astype(BF16), bdiag(tb[ch])) for ch in chains}
        t_inv = {ch: t_inv[ch] - _dot(tb[ch], bdiag(ct[ch])) for ch in chains}

    mm = {ch: _dot(t_inv[ch].astype(BF16),
                   jnp.concatenate([rk_t[ch][1], bdiag(lkv[ch])], axis=1)) for ch in chains}
    pm = {ch: _dot(p_br[ch], jnp.concatenate([bdiag(mm[ch][:, :w]), bdiag(mm[ch][:, w:])], axis=1))
          for ch in chains}
    pkv = {ch: _dot(p_kr[ch], v_bd[ch]) for ch in chains}
    nm = {ch: jnp.concatenate([(rt32[ch] - pm[ch][:, :w]).astype(BF16),
                               mm[ch][:, :w].astype(BF16)], axis=0) for ch in chains}
    n2 = {ch: pkv[ch] - pm[ch][:, w:] for ch in chains}

    state = [s_scr[gi] for gi in range(n_groups)]
    ys = []
    for ci in range(n_chunks):
        q = {gi: _dot_nt(nm[(ci, gi)], state[gi].astype(BF16)) for gi in range(n_groups)}
        y_parts = []
        for gi in range(n_groups):
            ch = (ci, gi)
            y_parts.append(q[gi][:c] + n2[ch])
            u_new = -(q[gi][c:] + mm[ch][:, w:])
            w_cat = jnp.concatenate([u_new.astype(BF16), vbs[ch]], axis=0)
            upd = _dot_tn(w_cat, xcat[ch])
            state[gi] = state[gi] * gam[ch] + jnp.where(same_head, upd, 0.0)
        ys.append(jnp.concatenate(y_parts, axis=1))
    for gi in range(n_groups):
        s_scr[gi] = state[gi]

    y = jnp.concatenate(ys, axis=0)
    r = r_ref[0]
    k = k_ref[0]
    v = v_ref[0]
    bd = bd_ref[...]
    inv_n = 1.0 / HEAD_DIM
    mean = _dot_exact_rhs(y, bd) * inv_n
    d = y - mean
    var = _dot_exact_rhs(d * d, bd) * inv_n
    yn = d * lax.rsqrt(var + RWKV_GN_EPS) * lnw_ref[...] + lnb_ref[...]
    bonus = _dot_exact_rhs(r * k * rk_ref[...], bd) * v
    o_ref[0] = ((yn + bonus) * g_ref[0]).astype(o_ref.dtype)


def _rwkv_kernel(u_ref, mu_ref, w0_ref, wup_ref, a0_ref, aup_ref, gup_ref, kkw_ref, ka_ref,
                 rk_ref, lnw_ref, lnb_ref, bd_ref, o_ref, carry, s_scr,
                 r_s, k_s, v_s, lw_s, kk_s, b_s, g_s, *, n_chunks):
    _rwkv_prep_body(u_ref, mu_ref, w0_ref, wup_ref, a0_ref, aup_ref, gup_ref, kkw_ref, ka_ref,
                    bd_ref, r_s, k_s, v_s, lw_s, kk_s, b_s, g_s, carry)
    _rwkv_scan_body(r_s, k_s, v_s, lw_s, kk_s, b_s, g_s, rk_ref, lnw_ref, lnb_ref, bd_ref,
                    o_ref, s_scr, n_chunks=n_chunks)


def _rwkv_mix(u3, mu, w0, wup_pad, a0, aup_pad, gup, k_k, k_a, r_k, ln_w, ln_b, bd, n_chunks):
    bsz, t, _ = u3.shape
    rows = RWKV_CHUNK * n_chunks
    const = lambda i, j: (0, 0)
    tile = lambda i, j: (i, j, 0)
    vec = pl.BlockSpec((1, GROUP_WIDTH), const)
    staged = pltpu.VMEM((1, rows, GROUP_WIDTH), F32)
    return pl.pallas_call(
        functools.partial(_rwkv_kernel, n_chunks=n_chunks),
        grid=(bsz, t // rows),
        in_specs=[
            pl.BlockSpec((1, rows, RWKV_IN), tile),
            pl.BlockSpec((1, RWKV_IN), const),
            vec,
            pl.BlockSpec((LANES, GROUP_WIDTH), const),
            vec,
            pl.BlockSpec((LANES, GROUP_WIDTH), const),
            pl.BlockSpec((GATE_LORA, GROUP_WIDTH), const),
            vec, vec, vec, vec, vec,
            pl.BlockSpec((GROUP_WIDTH, GROUP_WIDTH), const),
        ],
        out_specs=pl.BlockSpec((1, rows, GROUP_WIDTH), tile),
        out_shape=jax.ShapeDtypeStruct((bsz, t, GROUP_WIDTH), BF16),
        scratch_shapes=[
            pltpu.VMEM((1, RWKV_IN), F32),
            pltpu.VMEM((GROUP_WIDTH // MXU_WIDTH, MXU_WIDTH, MXU_WIDTH), F32),
        ] + [staged] * 7,
        compiler_params=_cparams(("parallel", "arbitrary")),
        name="rwkv_mix",
    )(u3, mu, w0, wup_pad, a0, aup_pad, gup, k_k, k_a, r_k, ln_w, ln_b, bd)


def _fox_attn_kernel(qt_ref, k_ref, ce_ref, vt_ref, cend_ref, og_ref, o_ref,
                     m_scr, l_scr, acc_scr, kmax_scr, *, t):
    qi = pl.program_id(2)
    n_strips = 2 * t // LANES
    qt = qt_ref[0, 0, 0]
    frow = lax.broadcasted_iota(jnp.int32, (LANES, t), 0)
    zero = jnp.zeros_like(qt)
    main = jnp.concatenate([jnp.where(frow < HEAD_DIM, qt, zero),
                            jnp.where(frow < HEAD_DIM, zero, qt)], axis=1)
    erow = lax.broadcasted_iota(jnp.int32, (LANES, 2 * t), 0)
    ecol = lax.broadcasted_iota(jnp.int32, (LANES, 2 * t), 1)
    off = jnp.where(ecol < t, 0, 3)
    extra = jnp.where((erow >= off) & (erow < off + 3), -1.0, 0.0).astype(BF16)
    q_aug = jnp.concatenate([main, extra], axis=0)

    m_scr[...] = jnp.full_like(m_scr, -jnp.inf)
    l_scr[...] = jnp.zeros_like(l_scr)
    acc_scr[...] = jnp.zeros_like(acc_scr)

    @pl.when(qi == 0)
    def _():
        hid_r = lax.broadcasted_iota(jnp.int32, (LANES, LANES), 0) // HEAD_DIM
        hid_c = lax.broadcasted_iota(jnp.int32, (LANES, LANES), 1) // HEAD_DIM
        same_head = jnp.where(hid_r == hid_c, 1.0, 0.0).astype(BF16)

        def tile_max(j, best):
            kf = k_ref[0, pl.ds(pl.multiple_of(j * t, t), t), :].astype(F32)
            sq = _dot((kf * kf).astype(BF16), same_head)
            return jnp.maximum(best, jnp.max(sq, axis=0, keepdims=True))

        best = lax.fori_loop(0, k_ref.shape[1] // t, tile_max, jnp.zeros((1, LANES), F32))
        kmax_scr[...] = jnp.sqrt(best) * NORM_SLACK

    def step(j, masked):
        start = pl.multiple_of(j * t, t)
        k_aug = jnp.concatenate([k_ref[0, pl.ds(start, t), :], ce_ref[0, pl.ds(start, t), :]],
                                axis=1)
        vt = vt_ref[0, 0, j]
        zt = _dot(k_aug, q_aug)
        m_prev = m_scr[...]
        l_prev = l_scr[...]
        acc_prev = acc_scr[...]
        m_out, l_out, acc_out = [], [], [[], []]
        for s in range(n_strips):
            head = s // (n_strips // 2)
            cs = slice(s * LANES, (s + 1) * LANES)
            z = zt[:, cs]
            if masked:
                key = lax.broadcasted_iota(jnp.int32, (t, LANES), 0)
                qry = lax.broadcasted_iota(jnp.int32, (t, LANES), 1) + (s * LANES) % t
                z = jnp.where(key <= qry, z, -jnp.inf)
            m_new = jnp.maximum(m_prev[:, cs], jnp.max(z, axis=0, keepdims=True))
            alpha = jnp.exp2(m_prev[:, cs] - m_new)
            p = jnp.exp2(z - m_new)
            l_out.append(alpha * l_prev[:, cs] + jnp.sum(p, axis=0, keepdims=True))
            m_out.append(m_new)
            hs = slice(head * HEAD_DIM, (head + 1) * HEAD_DIM)
            qs = slice((s * LANES) % t, (s * LANES) % t + LANES)
            pv = _dot(vt[hs], p.astype(BF16))
            acc_out[head].append(alpha * acc_prev[hs, qs] + pv)
        m_scr[...] = jnp.concatenate(m_out, axis=1)
        l_scr[...] = jnp.concatenate(l_out, axis=1)
        acc_scr[...] = jnp.concatenate([jnp.concatenate(acc_out[0], axis=1),
                                        jnp.concatenate(acc_out[1], axis=1)], axis=0)

    def body(j, carry):
        step(j, False)
        return carry

    step(qi, True)

    qf = qt.astype(F32)
    qsq = qf * qf
    qnorm = jnp.sqrt(jnp.concatenate(
        [jnp.sum(qsq[:HEAD_DIM], axis=0, keepdims=True),
         jnp.sum(qsq[HEAD_DIM:], axis=0, keepdims=True)], axis=1)) * NORM_SLACK
    kmax = kmax_scr[...]
    kmax2 = jnp.concatenate([jnp.broadcast_to(kmax[:, 0:1], (1, t)),
                             jnp.broadcast_to(kmax[:, HEAD_DIM:HEAD_DIM + 1], (1, t))], axis=1)
    slack = qnorm * kmax2 - m_scr[...]
    cend = cend_ref[0][:, 0, :]
    lane = lax.broadcasted_iota(jnp.int32, cend.shape, 1)
    jrow = lax.broadcasted_iota(jnp.int32, (cend.shape[0], 1), 0)
    needed = jrow < 0
    for head in range(2):
        worst = jnp.max(slack[:, head * t:(head + 1) * t], axis=1, keepdims=True)
        c_head = jnp.sum(jnp.where(lane == 2 * pl.program_id(1) + head, cend, 0.0),
                         axis=1, keepdims=True)
        needed = needed | (worst - c_head * LOG2E > ZERO_PROB_EXP)
    first = jnp.min(jnp.where(needed & (jrow < qi), jrow, qi))
    lax.fori_loop(first, qi, body, 0)

    l = l_scr[...]
    acc = acc_scr[...]
    inv_n = 1.0 / HEAD_DIM
    halves = []
    for head in range(2):
        o = acc[head * HEAD_DIM:(head + 1) * HEAD_DIM] / l[:, head * t:(head + 1) * t]
        halves.append(o * lax.rsqrt(jnp.sum(o * o, axis=0, keepdims=True) * inv_n + RMS_EPS))
    o_t = jnp.concatenate(halves, axis=0)
    o_ref[0] = (jnp.transpose(o_t) * og_ref[...]).astype(o_ref.dtype)


def _fox_attn(qt5, k, cext, vt5, cend, out_g, tile):
    bsz, t_all, _ = k.shape
    pairs = N_HEADS // 2
    tiles = t_all // tile
    kern = functools.partial(_fox_attn_kernel, t=tile)
    return pl.pallas_call(
        kern,
        grid=(bsz, pairs, tiles),
        in_specs=[
            pl.BlockSpec((1, 1, 1, LANES, tile), lambda b, p, i: (b, p, i, 0, 0)),
            pl.BlockSpec((1, t_all, LANES), lambda b, p, i: (b, 0, p)),
            pl.BlockSpec((1, t_all, LANES), lambda b, p, i: (b, 0, p)),
            pl.BlockSpec((1, 1, tiles, LANES, tile), lambda b, p, i: (b, p, 0, 0, 0)),
            pl.BlockSpec((1, tiles, 1, LANES), lambda b, p, i: (b, 0, 0, 0)),
            pl.BlockSpec((1, LANES), lambda b, p, i: (0, p)),
        ],
        out_specs=pl.BlockSpec((1, tile, LANES), lambda b, p, i: (b, i, p)),
        out_shape=jax.ShapeDtypeStruct((bsz, t_all, GROUP_WIDTH), BF16),
        scratch_shapes=[
            pltpu.VMEM((1, 2 * tile), F32),
            pltpu.VMEM((1, 2 * tile), F32),
            pltpu.VMEM((LANES, tile), F32),
            pltpu.VMEM((1, LANES), F32),
        ],
        compiler_params=_cparams(("parallel", "parallel", "arbitrary")),
        name="fox_attn",
    )(qt5, k, cext, vt5, cend, out_g)


def _outproj_router_kernel(x_ref, yr_ref, yf_ref, wo_r_ref, wo_f_ref, g_ref, rwt_ref, rb_ref,
                           x1_ref, h_ref, idx_ref, gate_ref, rank_ref, count_ref):
    @pl.when(pl.program_id(0) == 0)
    def _():
        count_ref[...] = jnp.zeros_like(count_ref)

    x1 = x_ref[...] + _dot(yr_ref[...], wo_r_ref[...]) + _dot(yf_ref[...], wo_f_ref[...])
    x1_ref[...] = x1
    h = x1 * lax.rsqrt(jnp.mean(x1 * x1, axis=-1, keepdims=True) + RMS_EPS) * g_ref[...]
    h_ref[:, 0, :] = h
    logits = lax.dot_general(rwt_ref[...], h, (((1,), (1,)), ((), ())),
                             precision=lax.Precision.HIGHEST,
                             preferred_element_type=F32) + rb_ref[...]
    eidx = lax.broadcasted_iota(jnp.int32, logits.shape, 0)
    vals, idxs, picks = [], [], []
    for _ in range(TOP_K):
        m = jnp.max(logits, axis=0, keepdims=True)
        i = jnp.min(jnp.where(logits == m, eidx, N_EXPERTS), axis=0, keepdims=True)
        vals.append(m)
        idxs.append(i)
        picks.append(eidx == i)
        logits = jnp.where(picks[-1], -jnp.inf, logits)
    es = [jnp.exp(val - vals[0]) for val in vals]
    denom = es[0] + es[1] + es[2] + es[3]
    idx_ref[...] = jnp.concatenate(idxs, axis=0)
    gate_ref[...] = jnp.concatenate([e / denom for e in es], axis=0)

    tm = logits.shape[1]
    chosen = [jnp.where(pk, 1.0, 0.0) for pk in picks]
    any_k = chosen[0] + chosen[1] + chosen[2] + chosen[3]
    before = (lax.broadcasted_iota(jnp.int32, (tm, tm), 0)
              < lax.broadcasted_iota(jnp.int32, (tm, tm), 1))
    prefix = _dot(any_k.astype(BF16), jnp.where(before, 1.0, 0.0).astype(BF16))
    seen = count_ref[:, 0:1] + prefix
    rank_ref[...] = jnp.concatenate(
        [jnp.sum(ch * seen, axis=0, keepdims=True) for ch in chosen], axis=0).astype(jnp.int32)
    count_ref[...] = count_ref[...] + jnp.sum(any_k, axis=1, keepdims=True)


def _outproj_router(x2, yr, yf, wo_r, wo_f, g, rwt, rb, tm):
    n, d = x2.shape
    const = lambda i: (0, 0)
    row = lambda i: (i, 0)
    col = lambda i: (0, i)
    return pl.pallas_call(
        _outproj_router_kernel,
        grid=(n // tm,),
        in_specs=[
            pl.BlockSpec((tm, d), row),
            pl.BlockSpec((tm, GROUP_WIDTH), row),
            pl.BlockSpec((tm, GROUP_WIDTH), row),
            pl.BlockSpec((GROUP_WIDTH, d), const),
            pl.BlockSpec((GROUP_WIDTH, d), const),
            pl.BlockSpec((1, d), const),
            pl.BlockSpec((N_EXPERTS, d), const),
            pl.BlockSpec((N_EXPERTS, 1), const),
        ],
        out_specs=[
            pl.BlockSpec((tm, d), row),
            pl.BlockSpec((tm, 1, d), lambda i: (i, 0, 0)),
            pl.BlockSpec((TOP_K, tm), col),
            pl.BlockSpec((TOP_K, tm), col),
            pl.BlockSpec((TOP_K, tm), col),
            pl.BlockSpec((N_EXPERTS, LANES), const),
        ],
        out_shape=[
            jax.ShapeDtypeStruct((n, d), F32),
            jax.ShapeDtypeStruct((n, 1, d), F32),
            jax.ShapeDtypeStruct((TOP_K, n), jnp.int32),
            jax.ShapeDtypeStruct((TOP_K, n), F32),
            jax.ShapeDtypeStruct((TOP_K, n), jnp.int32),
            jax.ShapeDtypeStruct((N_EXPERTS, LANES), F32),
        ],
        compiler_params=_cparams(("arbitrary",)),
        name="outproj_router",
    )(x2, yr, yf, wo_r, wo_f, g, rwt, rb)


def _w1_split_kernel(w_ref, perm_ref, g_ref, l_ref):
    half = MXU_WIDTH // 2
    perm = perm_ref[...]
    for grp in range(w_ref.shape[2] // MXU_WIDTH):
        blk = w_ref[0, :, grp * MXU_WIDTH:(grp + 1) * MXU_WIDTH].astype(BF16)
        r = _dot(blk, perm)
        g_ref[0, :, grp * half:(grp + 1) * half] = r[:, :half].astype(BF16)
        l_ref[0, :, grp * half:(grp + 1) * half] = r[:, half:].astype(BF16)


def _w1_split(w1, tr):
    e, d, two_f = w1.shape
    half = MXU_WIDTH // 2
    src = jnp.arange(MXU_WIDTH, dtype=jnp.int32)[:, None]
    dst = jnp.arange(MXU_WIDTH, dtype=jnp.int32)[None, :]
    perm = (src == jnp.where(dst < half, 2 * dst, 2 * (dst - half) + 1)).astype(BF16)
    out_sds = jax.ShapeDtypeStruct((e, d, two_f // 2), BF16)
    return pl.pallas_call(
        _w1_split_kernel,
        grid=(e, d // tr),
        in_specs=[
            pl.BlockSpec((1, tr, two_f), lambda i, j: (i, j, 0)),
            pl.BlockSpec((MXU_WIDTH, MXU_WIDTH), lambda i, j: (0, 0)),
        ],
        out_specs=[pl.BlockSpec((1, tr, two_f // 2), lambda i, j: (i, j, 0))] * 2,
        out_shape=[out_sds, out_sds],
        compiler_params=_cparams(("parallel", "parallel")),
        name="w1_split",
    )(w1, perm)


def _expert_kernel(be_ref, tok_a_ref, tok_b_ref, tok_a_next_ref, dst_b_prev_ref, dst_a_ref,
                   dst_b_ref, h_hbm, w1g_a, w1l_a, b1g_a, b1l_a, w2_a, b2_a,
                   w1g_b, w1l_b, b1g_b, b1l_b, w2_b, b2_b,
                   y_hbm, xbuf_a, xbuf_b, obuf_a, obuf_b, gsem, osem, *, bm, n_real_rows):
    del be_ref
    i = pl.program_id(0)

    def gather_start(idx_ref, xbuf, sem, r):
        pltpu.make_async_copy(h_hbm.at[idx_ref[0, 0, r]], xbuf.at[pl.ds(r, 1), :], sem).start()

    def scatter_start(idx_ref, obuf, sem, r):
        pltpu.make_async_copy(obuf.at[pl.ds(r, 1), :], y_hbm.at[idx_ref[0, 0, r]], sem).start()

    def rows_wait(buf, sem):
        pltpu.make_async_copy(buf, buf, sem).wait()

    def mlp(xbuf, w1g_ref, w1l_ref, b1g_ref, b1l_ref, w2_ref, b2_ref):
        xb = xbuf[...].astype(BF16)
        dff = w1g_ref.shape[2]
        acts = []
        for piece in range(dff // MXU_WIDTH):
            cs = slice(piece * MXU_WIDTH, (piece + 1) * MXU_WIDTH)
            glu = _dot(xb, w1g_ref[0, :, cs]) + b1g_ref[0, :, cs]
            lin = _dot(xb, w1l_ref[0, :, cs]) + b1l_ref[0, :, cs]
            glu = jnp.minimum(glu, SWIGLU_LIMIT)
            lin = jnp.clip(lin, -SWIGLU_LIMIT, SWIGLU_LIMIT)
            acts.append((glu * _sigmoid(SWIGLU_ALPHA * glu) * (lin + 1.0)).astype(BF16))
        return _dot(jnp.concatenate(acts, axis=1), w2_ref[0]) + b2_ref[0]

    @pl.when(i == 0)
    def _():
        obuf_a[...] = jnp.zeros_like(obuf_a)
        obuf_b[...] = jnp.zeros_like(obuf_b)

        def first(r, carry):
            pltpu.make_async_copy(obuf_a.at[pl.ds(r, 1), :], y_hbm.at[n_real_rows + 2 * bm + r],
                                  osem.at[0]).start()
            gather_start(tok_a_ref, xbuf_a, gsem.at[0], r)
            return carry

        lax.fori_loop(0, bm, first, 0)

    for r in range(bm):
        gather_start(tok_b_ref, xbuf_b, gsem.at[1], r)
        scatter_start(dst_b_prev_ref, obuf_b, osem.at[1], r)
    rows_wait(xbuf_a, gsem.at[0])
    rows_wait(obuf_a, osem.at[0])
    obuf_a[...] = mlp(xbuf_a, w1g_a, w1l_a, b1g_a, b1l_a, w2_a, b2_a)

    for r in range(bm):
        gather_start(tok_a_next_ref, xbuf_a, gsem.at[0], r)
        scatter_start(dst_a_ref, obuf_a, osem.at[0], r)
    rows_wait(xbuf_b, gsem.at[1])
    rows_wait(obuf_b, osem.at[1])
    obuf_b[...] = mlp(xbuf_b, w1g_b, w1l_b, b1g_b, b1l_b, w2_b, b2_b)

    @pl.when(i == pl.num_programs(0) - 1)
    def _():
        def last(r, carry):
            scatter_start(dst_b_ref, obuf_b, osem.at[1], r)
            return carry

        lax.fori_loop(0, bm, last, 0)
        rows_wait(obuf_b, osem.at[1])
        rows_wait(obuf_a, osem.at[0])
        rows_wait(xbuf_a, gsem.at[0])


def _expert_mlp(block_e, tok_blocks, dst_blocks, h2, w1g, w1l, b1g, b1l, w2, b2, bm):
    n_blocks = tok_blocks.shape[0]
    assert n_blocks % 2 == 0
    n, _, d = h2.shape
    dff = w1g.shape[2]
    n_real_rows = TOP_K * n
    idx_spec = lambda fn: pl.BlockSpec((1, 1, bm), fn, memory_space=pltpu.SMEM)

    def weight_specs(which):
        wmap = lambda i, be: (be[2 * i + which], 0, 0)
        return [
            pl.BlockSpec((1, d, dff), wmap),
            pl.BlockSpec((1, d, dff), wmap),
            pl.BlockSpec((1, 1, dff), wmap),
            pl.BlockSpec((1, 1, dff), wmap),
            pl.BlockSpec((1, dff, d), wmap),
            pl.BlockSpec((1, 1, d), wmap),
        ]

    grid_spec = pltpu.PrefetchScalarGridSpec(
        num_scalar_prefetch=1,
        grid=(n_blocks // 2,),
        in_specs=[
            idx_spec(lambda i, be: (2 * i, 0, 0)),
            idx_spec(lambda i, be: (2 * i + 1, 0, 0)),
            idx_spec(lambda i, be: (jnp.minimum(2 * i + 2, n_blocks - 1), 0, 0)),
            idx_spec(lambda i, be: (2 * i, 0, 0)),
            idx_spec(lambda i, be: (2 * i + 1, 0, 0)),
            idx_spec(lambda i, be: (2 * i + 2, 0, 0)),
            pl.BlockSpec(memory_space=pl.ANY),
        ] + weight_specs(0) + weight_specs(1),
        out_specs=pl.BlockSpec(memory_space=pl.ANY),
        scratch_shapes=[
            pltpu.VMEM((bm, d), F32),
            pltpu.VMEM((bm, d), F32),
            pltpu.VMEM((bm, d), F32),
            pltpu.VMEM((bm, d), F32),
            pltpu.SemaphoreType.DMA((2,)),
            pltpu.SemaphoreType.DMA((2,)),
        ],
    )
    weights = (w1g, w1l, b1g, b1l, w2, b2)
    return pl.pallas_call(
        functools.partial(_expert_kernel, bm=bm, n_real_rows=n_real_rows),
        grid_spec=grid_spec,
        out_shape=jax.ShapeDtypeStruct((n_real_rows + 3 * bm, 1, d), F32),
        compiler_params=_cparams(("arbitrary",)),
        name="expert_mlp",
    )(block_e, tok_blocks, tok_blocks, tok_blocks, dst_blocks, dst_blocks, dst_blocks, h2,
      *weights, *weights)


def _combine_kernel(gate_ref, x1_ref, g_ref, y0_ref, y1_ref, y2_ref, y3_ref, o_ref):
    gates = gate_ref[...]
    y = x1_ref[...]
    for kk, y_ref in enumerate((y0_ref, y1_ref, y2_ref, y3_ref)):
        y = y + y_ref[:, 0, :] * gates[:, kk:kk + 1]
    o_ref[...] = y * lax.rsqrt(jnp.mean(y * y, axis=-1, keepdims=True) + RMS_EPS) * g_ref[...]


def _combine(gates_t, x1, g, y_all, tc):
    n, d = x1.shape
    tiles = n // tc
    y_spec = lambda kk: pl.BlockSpec((tc, 1, d), lambda i: (kk * tiles + i, 0, 0))
    return pl.pallas_call(
        _combine_kernel,
        grid=(tiles,),
        in_specs=[
            pl.BlockSpec((tc, TOP_K), lambda i: (i, 0)),
            pl.BlockSpec((tc, d), lambda i: (i, 0)),
            pl.BlockSpec((1, d), lambda i: (0, 0)),
        ] + [y_spec(kk) for kk in range(TOP_K)],
        out_specs=pl.BlockSpec((tc, d), lambda i: (i, 0)),
        out_shape=jax.ShapeDtypeStruct((n, d), F32),
        compiler_params=_cparams(("parallel",)),
        name="combine",
    )(gates_t, x1, g, y_all, y_all, y_all, y_all)


def _slot_sources_kernel(pos_ref, out_ref, *, chunk):
    i = pl.program_id(0)

    @pl.when(i == 0)
    def _():
        def init(p, carry):
            out_ref[p] = -1
            return carry

        lax.fori_loop(0, out_ref.shape[0], init, 0, unroll=32)

    base = i * chunk

    def place(s, carry):
        out_ref[pos_ref[0, 0, s]] = base + s
        return carry

    lax.fori_loop(0, chunk, place, 0, unroll=16)


def _slot_sources(pos, n_pad):
    n_slots = pos.shape[0]
    chunk = _pick(n_slots, 8192)
    return pl.pallas_call(
        functools.partial(_slot_sources_kernel, chunk=chunk),
        grid=(n_slots // chunk,),
        in_specs=[pl.BlockSpec((1, 1, chunk), lambda i: (i, 0, 0), memory_space=pltpu.SMEM)],
        out_specs=pl.BlockSpec(memory_space=pltpu.SMEM),
        out_shape=jax.ShapeDtypeStruct((n_pad,), jnp.int32),
        compiler_params=_cparams(("arbitrary",)),
        name="slot_sources",
    )(pos.reshape(n_slots // chunk, 1, chunk))


def _dispatch_plan(idx, rank, counts, bm):
    n = idx.shape[1]
    n_slots = TOP_K * n
    sizes = counts[:, 0].astype(jnp.int32)
    padded = (sizes + bm - 1) // bm * bm
    pad_ends = jnp.cumsum(padded)
    pad_starts = pad_ends - padded
    experts = jnp.arange(N_EXPERTS, dtype=jnp.int32)
    start_of = jnp.sum(jnp.where(idx[..., None] == experts, pad_starts, 0), axis=-1)
    pos = (start_of + rank).reshape(-1)
    n_pad = n_slots + N_EXPERTS * bm
    n_blocks = n_pad // bm
    slot_src = _slot_sources(pos, n_pad)
    p = jnp.arange(n_pad, dtype=jnp.int32)
    spare = n_slots + (p // bm) % 2 * bm + p % bm
    real = slot_src >= 0
    tok_blocks = jnp.where(real, slot_src % n, 0).reshape(n_blocks, 1, bm)
    dst = jnp.where(real, slot_src, spare)
    dst_blocks = jnp.concatenate([spare[bm:2 * bm], dst]).reshape(n_blocks + 1, 1, bm)
    block_start = jnp.arange(n_blocks, dtype=jnp.int32) * bm
    block_e = jnp.minimum(jnp.sum(pad_ends[None, :] <= block_start[:, None], axis=1),
                          N_EXPERTS - 1).astype(jnp.int32)
    return tok_blocks, dst_blocks, block_e


def _block_diag_ones():
    hid = jnp.arange(GROUP_WIDTH, dtype=jnp.int32) // HEAD_DIM
    return (hid[:, None] == hid[None, :]).astype(BF16)


def _pick(n, pref):
    return pref if n % pref == 0 else n


def kernel(x, attn_norm_g, w_in, rwkv_mu, rwkv_w0, rwkv_w_up, rwkv_a0, rwkv_a_up, rwkv_g_up,
           rwkv_k_k, rwkv_k_a, rwkv_r_k, rwkv_ln_w, rwkv_ln_b, fox_f_bias, fox_out_g, w_out,
           ffn_norm_g, router_w, router_b, expert_w1, expert_b1, expert_w2, expert_b2,
           final_norm_g):
    bsz, t, d = x.shape
    n = bsz * t
    depth = w_in.shape[0]
    assert depth == 1, "the final norm is fused into the last stage of a single layer"
    bd = _block_diag_ones()
    x2 = x.reshape(n, d)
    for l in range(depth):
        w_l = w_in[l]
        w_r = w_l[:, :RWKV_IN].astype(BF16)
        w_qkv = w_l[:, RWKV_IN:RWKV_IN + 3 * GROUP_WIDTH].astype(BF16)
        w_qt = w_qkv[:, :GROUP_WIDTH].T
        w_k = w_qkv[:, GROUP_WIDTH:2 * GROUP_WIDTH]
        w_vt = w_qkv[:, 2 * GROUP_WIDTH:].T
        w_f = jnp.pad(w_l[:, RWKV_IN + 3 * GROUP_WIDTH:], ((0, 0), (0, LANES - N_HEADS))).astype(BF16)
        fb_pad = jnp.pad(fox_f_bias[l], (0, LANES - N_HEADS)).reshape(1, LANES)
        wup_pad = jnp.pad(rwkv_w_up[l], ((0, LANES - DECAY_LORA), (0, 0))).astype(BF16)
        aup_pad = jnp.pad(rwkv_a_up[l], ((DECAY_LORA, 0), (0, 0))).astype(BF16)
        gup = rwkv_g_up[l].astype(BF16)
        vec = lambda a: a.reshape(1, -1)

        u_r, qt5, k, vt5, fl = _inproj(x2, vec(attn_norm_g[l]), w_r, w_qt, w_k, w_vt, w_f,
                                       bsz, ATTN_TILE)
        cext, cend = _fox_gate(fl.reshape(bsz, t, LANES), fb_pad, _gate_piece_selectors(),
                               ATTN_TILE)
        y_rwkv = _rwkv_mix(
            u_r.reshape(bsz, t, RWKV_IN), vec(rwkv_mu[l]), vec(rwkv_w0[l]), wup_pad,
            vec(rwkv_a0[l]), aup_pad, gup, vec(rwkv_k_k[l]), vec(rwkv_k_a[l]), vec(rwkv_r_k[l]),
            vec(rwkv_ln_w[l]), vec(rwkv_ln_b[l]), bd, RWKV_CHUNKS_PER_STEP)
        y_fox = _fox_attn(qt5, k.reshape(bsz, t, GROUP_WIDTH), cext, vt5, cend,
                          vec(fox_out_g[l]), ATTN_TILE)

        wo = w_out[l].astype(BF16)
        x1, h2, idx, gates, rank, counts = _outproj_router(
            x2, y_rwkv.reshape(n, GROUP_WIDTH), y_fox.reshape(n, GROUP_WIDTH),
            wo[:GROUP_WIDTH], wo[GROUP_WIDTH:], vec(ffn_norm_g[l]),
            router_w[l].T, router_b[l].reshape(N_EXPERTS, 1), _pick(n, 512))

        bm = 256
        tok_blocks, dst_blocks, block_e = _dispatch_plan(idx, rank, counts, bm)
        w1g, w1l = _w1_split(expert_w1[l], 512)
        b1 = expert_b1[l]
        b1g = b1[:, None, 0::2]
        b1l = b1[:, None, 1::2]
        y_all = _expert_mlp(block_e, tok_blocks, dst_blocks, h2, w1g, w1l, b1g, b1l,
                            expert_w2[l].astype(BF16), expert_b2[l][:, None, :], bm)
        x2 = _combine(gates.T, x1, vec(final_norm_g), y_all, _pick(n, 256))
    return x2.reshape(bsz, t, d)
```

```python
import functools

import jax
import jax.numpy as jnp
from jax import lax
from jax.experimental import pallas as pl
from jax.experimental.pallas import tpu as pltpu

F32 = jnp.float32
BF16 = jnp.bfloat16

HEAD_DIM = 64
N_HEADS = 8
GROUP_WIDTH = N_HEADS * HEAD_DIM
DECAY_LORA = 64
AAA_LORA = 64
GATE_LORA = 128
RWKV_IN = 3 * GROUP_WIDTH + DECAY_LORA + AAA_LORA + GATE_LORA
LORA_OFF = 3 * GROUP_WIDTH
N_EXPERTS = 32
TOP_K = 4
SWIGLU_ALPHA = 1.702
SWIGLU_LIMIT = 7.0
RMS_EPS = 1e-5
RWKV_GN_EPS = 64e-5
LANES = 128
RWKV_CHUNK = 64
RWKV_CHUNKS_PER_STEP = 4
ATTN_TILE = 512
MXU_WIDTH = 256
LOG2E = 1.4426950408889634
Q_SCALE = HEAD_DIM ** -0.5 * LOG2E
ZERO_PROB_EXP = -152.0
NORM_SLACK = 1.0 + 2.0 ** -6
VMEM_LIMIT = 56 * 1024 * 1024


def _cparams(semantics):
    return pltpu.CompilerParams(dimension_semantics=semantics, vmem_limit_bytes=VMEM_LIMIT)


def _dot(a, b):
    return jnp.dot(a, b, preferred_element_type=F32)


def _dot_nt(a, b):
    return lax.dot_general(a, b, (((1,), (1,)), ((), ())), preferred_element_type=F32)


def _dot_tn(a, b):
    return lax.dot_general(a, b, (((0,), (0,)), ((), ())), preferred_element_type=F32)


def _split3(x):
    hi = x.astype(BF16)
    r1 = x - hi.astype(F32)
    mid = r1.astype(BF16)
    lo = (r1 - mid.astype(F32)).astype(BF16)
    return hi, mid, lo


def _dot_exact_lhs(a_bf16, x):
    hi, mid, lo = _split3(x)
    return _dot(a_bf16, hi) + _dot(a_bf16, mid) + _dot(a_bf16, lo)


def _dot_exact_rhs(x, b_bf16):
    hi, mid, lo = _split3(x)
    return _dot(hi, b_bf16) + _dot(mid, b_bf16) + _dot(lo, b_bf16)


def _softplus(z):
    return jnp.maximum(z, 0.0) + jnp.log1p(jnp.exp(-jnp.abs(z)))


def _sigmoid(z):
    return 1.0 / (1.0 + jnp.exp(-z))


def _inproj_kernel(x_ref, g_ref, wr_ref, wqt_ref, wk_ref, wvt_ref, wf_ref,
                   ur_ref, qt_ref, k_ref, vt_ref, fl_ref):
    x = x_ref[...]
    h = x * lax.rsqrt(jnp.mean(x * x, axis=-1, keepdims=True) + RMS_EPS) * g_ref[...]
    hb = h.astype(BF16)
    ur_ref[...] = _dot(hb, wr_ref[...])
    k_ref[...] = _dot(hb, wk_ref[...]).astype(BF16)
    fl_ref[...] = _dot(hb, wf_ref[...])
    qt = (_dot_nt(wqt_ref[...], hb) * Q_SCALE).astype(BF16)
    vt = _dot_nt(wvt_ref[...], hb).astype(BF16)
    for p in range(N_HEADS // 2):
        qt_ref[0, p, 0] = qt[p * LANES:(p + 1) * LANES]
        vt_ref[0, p, 0] = vt[p * LANES:(p + 1) * LANES]


def _inproj(x2, g, w_r, w_qt, w_k, w_vt, w_f, bsz, tm):
    n, d = x2.shape
    nt = n // bsz // tm
    pairs = N_HEADS // 2
    const = lambda i: (0, 0)
    row = lambda i: (i, 0)
    fm = lambda i: (i // nt, 0, i % nt, 0, 0)
    fm_sds = jax.ShapeDtypeStruct((bsz, pairs, nt, LANES, tm), BF16)
    return pl.pallas_call(
        _inproj_kernel,
        grid=(n // tm,),
        in_specs=[
            pl.BlockSpec((tm, d), row),
            pl.BlockSpec((1, d), const),
            pl.BlockSpec(w_r.shape, const),
            pl.BlockSpec(w_qt.shape, const),
            pl.BlockSpec(w_k.shape, const),
            pl.BlockSpec(w_vt.shape, const),
            pl.BlockSpec(w_f.shape, const),
        ],
        out_specs=[
            pl.BlockSpec((tm, RWKV_IN), row),
            pl.BlockSpec((1, pairs, 1, LANES, tm), fm),
            pl.BlockSpec((tm, GROUP_WIDTH), row),
            pl.BlockSpec((1, pairs, 1, LANES, tm), fm),
            pl.BlockSpec((tm, LANES), row),
        ],
        out_shape=[
            jax.ShapeDtypeStruct((n, RWKV_IN), F32),
            fm_sds,
            jax.ShapeDtypeStruct((n, GROUP_WIDTH), BF16),
            fm_sds,
            jax.ShapeDtypeStruct((n, LANES), F32),
        ],
        compiler_params=_cparams(("parallel",)),
        name="inproj",
    )(x2, g, w_r, w_qt, w_k, w_vt, w_f)


def _fox_gate_kernel(fl_ref, fb_ref, sel_ref, c_ref, cend_ref, carry):
    tt = fl_ref.shape[1]

    @pl.when(pl.program_id(1) == 0)
    def _():
        carry[...] = jnp.zeros_like(carry)

    z = fl_ref[0] + fb_ref[...]
    log_f = jnp.minimum(z, 0.0) - jnp.log1p(jnp.exp(-jnp.abs(z)))
    ri = lax.broadcasted_iota(jnp.int32, (tt, tt), 0)
    ci = lax.broadcasted_iota(jnp.int32, (tt, tt), 1)
    tri = jnp.where(ri >= ci, 1.0, 0.0).astype(BF16)
    c = _dot_exact_lhs(tri, log_f) + carry[...]
    carry[...] = c[tt - 1:tt, :]
    cend_ref[0, 0] = c[tt - 1:tt, :]
    hi, mid, lo = _split3(c * LOG2E)
    c_ref[0] = (_dot(hi, sel_ref[0]) + _dot(mid, sel_ref[1]) + _dot(lo, sel_ref[2])).astype(BF16)


def _gate_piece_selectors():
    h = jnp.arange(LANES, dtype=jnp.int32)[:, None]
    col = jnp.arange(GROUP_WIDTH, dtype=jnp.int32)[None, :]
    sels = []
    for m in range(3):
        target = LANES * (h // 2) + 3 * (h % 2) + m
        sels.append(((col == target) & (h < N_HEADS)).astype(BF16))
    return jnp.stack(sels)


def _fox_gate(fl3, fb_pad, sel, tt):
    b, t, _ = fl3.shape
    return pl.pallas_call(
        _fox_gate_kernel,
        grid=(b, t // tt),
        in_specs=[
            pl.BlockSpec((1, tt, LANES), lambda i, j: (i, j, 0)),
            pl.BlockSpec((1, LANES), lambda i, j: (0, 0)),
            pl.BlockSpec((3, LANES, GROUP_WIDTH), lambda i, j: (0, 0, 0)),
        ],
        out_specs=[
            pl.BlockSpec((1, tt, GROUP_WIDTH), lambda i, j: (i, j, 0)),
            pl.BlockSpec((1, 1, 1, LANES), lambda i, j: (i, j, 0, 0)),
        ],
        out_shape=[
            jax.ShapeDtypeStruct((b, t, GROUP_WIDTH), BF16),
            jax.ShapeDtypeStruct((b, t // tt, 1, LANES), F32),
        ],
        scratch_shapes=[pltpu.VMEM((1, LANES), F32)],
        compiler_params=_cparams(("parallel", "arbitrary")),
        name="fox_gate",
    )(fl3, fb_pad, sel)


def _rwkv_prep_body(u_ref, mu_ref, w0_ref, wup_ref, a0_ref, aup_ref, gup_ref, kk_ref, ka_ref,
                      bd_ref, r_out, k_out, v_out, lw_out, kk_out, b_out, g_out, carry):
    tt = u_ref.shape[1]

    @pl.when(pl.program_id(1) == 0)
    def _():
        carry[...] = jnp.zeros_like(carry)

    u = u_ref[0]
    prev = pltpu.roll(u, 1, axis=0)
    row = lax.broadcasted_iota(jnp.int32, u.shape, 0)
    prev = jnp.where(row == 0, carry[...], prev)
    carry[...] = u[tt - 1:tt, :]
    us = u + (prev - u) * mu_ref[...]

    r = us[:, :GROUP_WIDTH]
    k = us[:, GROUP_WIDTH:2 * GROUP_WIDTH]
    v = us[:, 2 * GROUP_WIDTH:LORA_OFF]
    wa = us[:, LORA_OFF:LORA_OFF + LANES]
    gl = us[:, LORA_OFF + LANES:]

    w_lin = _dot(jnp.tanh(wa).astype(BF16), wup_ref[...])
    a_lin = _dot(wa.astype(BF16), aup_ref[...])
    w = -_softplus(-(w0_ref[...] + w_lin)) - 0.5
    lw_out[0] = -jnp.exp(w)
    a = _sigmoid(a0_ref[...] + a_lin)
    g_out[0] = _dot(_sigmoid(gl).astype(BF16), gup_ref[...])

    kkr = k * kk_ref[...]
    ss = _dot_exact_rhs(kkr * kkr, bd_ref[...])
    kk = kkr / jnp.maximum(jnp.sqrt(ss), 1e-12)
    r_out[0] = r
    k_out[0] = k * (1.0 + (a - 1.0) * ka_ref[...])
    v_out[0] = v
    kk_out[0] = kk
    b_out[0] = kk * a


def _rwkv_scan_body(r_ref, k_ref, v_ref, lw_ref, kk_ref, b_ref, g_ref, rk_ref, lnw_ref, lnb_ref,
                      bd_ref, o_ref, s_scr, *, n_chunks):
    c = RWKV_CHUNK
    w = MXU_WIDTH
    hpg = w // HEAD_DIM
    n_groups = GROUP_WIDTH // w

    @pl.when(pl.program_id(1) == 0)
    def _():
        s_scr[...] = jnp.zeros_like(s_scr)

    row = lax.broadcasted_iota(jnp.int32, (c, w), 0)
    u = lax.broadcasted_iota(jnp.int32, (c, w), 1) % HEAD_DIM
    strict = row > u
    incl = row >= u
    eye = jnp.where(row == u, 1.0, 0.0)
    level_masks = []
    s = 1
    while s < c:
        same = (row // (2 * s)) == (u // (2 * s))
        level_masks.append(same & ((row % (2 * s)) >= s) & ((u % (2 * s)) < s))
        s *= 2
    same_head = (lax.broadcasted_iota(jnp.int32, (w, w), 0) // HEAD_DIM
                 == lax.broadcasted_iota(jnp.int32, (w, w), 1) // HEAD_DIM)
    tri = jnp.where(lax.broadcasted_iota(jnp.int32, (c, c), 0)
                    >= lax.broadcasted_iota(jnp.int32, (c, c), 1), 1.0, 0.0).astype(BF16)

    def bdiag(x):
        xb = x.astype(BF16)
        tiled = jnp.concatenate([xb] * hpg, axis=0)
        return jnp.where(same_head, tiled, jnp.zeros_like(tiled))

    chains = [(ci, gi) for ci in range(n_chunks) for gi in range(n_groups)]
    lhs, rk_t, vbs, xcat, gam, rt32 = {}, {}, {}, {}, {}, {}
    for ci in range(n_chunks):
        rs = slice(ci * c, (ci + 1) * c)
        r = r_ref[0, rs, :]
        k = k_ref[0, rs, :]
        lw = lw_ref[0, rs, :]
        kk = kk_ref[0, rs, :]
        b = b_ref[0, rs, :]
        g_cum = _dot_exact_lhs(tri, lw)
        g_last = g_cum[c - 1:c, :]
        r_t = r * jnp.exp(g_cum)
        kk_t = kk * jnp.exp(g_cum - lw)
        e_neg = jnp.exp(-g_cum)
        b_n = b * e_neg
        k_n = k * e_neg
        e_end = jnp.exp(g_last - g_cum)
        b_e = (b * e_end).astype(BF16)
        k_e = (k * e_end).astype(BF16)
        gamma = jnp.exp(g_last)
        vb = v_ref[0, rs, :].astype(BF16)
        for gi in range(n_groups):
            gs = slice(gi * w, (gi + 1) * w)
            ch = (ci, gi)
            lhs[ch] = jnp.concatenate([r_t[:, gs].astype(BF16), kk_t[:, gs].astype(BF16)], axis=0)
            rk_t[ch] = (jnp.concatenate([bdiag(b_n[:, gs]), bdiag(k_n[:, gs])], axis=0),
                        bdiag(kk_t[:, gs]))
            vbs[ch] = vb[:, gs]
            xcat[ch] = jnp.concatenate([b_e[:, gs], k_e[:, gs]], axis=0)
            gam[ch] = gamma[:, gs]
            rt32[ch] = r_t[:, gs]

    p = {ch: _dot_nt(lhs[ch], rk_t[ch][0]) for ch in chains}
    l_b = {ch: jnp.where(strict, p[ch][c:, :w], 0.0) for ch in chains}
    l_k = {ch: jnp.where(strict, p[ch][c:, w:], 0.0).astype(BF16) for ch in chains}
    p_br = {ch: jnp.where(incl, p[ch][:c, :w], 0.0).astype(BF16) for ch in chains}
    p_kr = {ch: jnp.where(incl, p[ch][:c, w:], 0.0).astype(BF16) for ch in chains}
    v_bd = {ch: bdiag(vbs[ch]) for ch in chains}
    lkv = {ch: _dot(l_k[ch], v_bd[ch]) for ch in chains}

    t_inv = {ch: eye - jnp.where(level_masks[0], l_b[ch], 0.0) for ch in chains}
    for m in level_masks[1:]:
        tb = {ch: t_inv[ch].astype(BF16) for ch in chains}
        ct = {ch: _dot(jnp.where(m, l_b[ch], 0.0).astype(BF16), bdiag(tb[ch])) for ch in chains}
        t_inv = {ch: t_inv[ch] - _dot(tb[ch], bdiag(ct[ch])) for ch in chains}

    mm = {ch: _dot(t_inv[ch].astype(BF16),
                   jnp.concatenate([rk_t[ch][1], bdiag(lkv[ch])], axis=1)) for ch in chains}
    pm = {ch: _dot(p_br[ch], jnp.concatenate([bdiag(mm[ch][:, :w]), bdiag(mm[ch][:, w:])], axis=1))
          for ch in chains}
    pkv = {ch: _dot(p_kr[ch], v_bd[ch]) for ch in chains}
    nm = {ch: jnp.concatenate([(rt32[ch] - pm[ch][:, :w]).astype(BF16),
                               mm[ch][:, :w].astype(BF16)], axis=0) for ch in chains}
    n2 = {ch: pkv[ch] - pm[ch][:, w:] for ch in chains}

    state = [s_scr[gi] for gi in range(n_groups)]
    ys = []
    for ci in range(n_chunks):
        q = {gi: _dot_nt(nm[(ci, gi)], state[gi].astype(BF16)) for gi in range(n_groups)}
        y_parts = []
        for gi in range(n_groups):
            ch = (ci, gi)
            y_parts.append(q[gi][:c] + n2[ch])
            u_new = -(q[gi][c:] + mm[ch][:, w:])
            w_cat = jnp.concatenate([u_new.astype(BF16), vbs[ch]], axis=0)
            upd = _dot_tn(w_cat, xcat[ch])
            state[gi] = state[gi] * gam[ch] + jnp.where(same_head, upd, 0.0)
        ys.append(jnp.concatenate(y_parts, axis=1))
    for gi in range(n_groups):
        s_scr[gi] = state[gi]

    y = jnp.concatenate(ys, axis=0)
    r = r_ref[0]
    k = k_ref[0]
    v = v_ref[0]
    bd = bd_ref[...]
    inv_n = 1.0 / HEAD_DIM
    mean = _dot_exact_rhs(y, bd) * inv_n
    d = y - mean
    var = _dot_exact_rhs(d * d, bd) * inv_n
    yn = d * lax.rsqrt(var + RWKV_GN_EPS) * lnw_ref[...] + lnb_ref[...]
    bonus = _dot_exact_rhs(r * k * rk_ref[...], bd) * v
    o_ref[0] = ((yn + bonus) * g_ref[0]).astype(o_ref.dtype)


def _rwkv_kernel(u_ref, mu_ref, w0_ref, wup_ref, a0_ref, aup_ref, gup_ref, kkw_ref, ka_ref,
                 rk_ref, lnw_ref, lnb_ref, bd_ref, o_ref, carry, s_scr,
                 r_s, k_s, v_s, lw_s, kk_s, b_s, g_s, *, n_chunks):
    _rwkv_prep_body(u_ref, mu_ref, w0_ref, wup_ref, a0_ref, aup_ref, gup_ref, kkw_ref, ka_ref,
                    bd_ref, r_s, k_s, v_s, lw_s, kk_s, b_s, g_s, carry)
    _rwkv_scan_body(r_s, k_s, v_s, lw_s, kk_s, b_s, g_s, rk_ref, lnw_ref, lnb_ref, bd_ref,
                    o_ref, s_scr, n_chunks=n_chunks)


def _rwkv_mix(u3, mu, w0, wup_pad, a0, aup_pad, gup, k_k, k_a, r_k, ln_w, ln_b, bd, n_chunks):
    bsz, t, _ = u3.shape
    rows = RWKV_CHUNK * n_chunks
    const = lambda i, j: (0, 0)
    tile = lambda i, j: (i, j, 0)
    vec = pl.BlockSpec((1, GROUP_WIDTH), const)
    staged = pltpu.VMEM((1, rows, GROUP_WIDTH), F32)
    return pl.pallas_call(
        functools.partial(_rwkv_kernel, n_chunks=n_chunks),
        grid=(bsz, t // rows),
        in_specs=[
            pl.BlockSpec((1, rows, RWKV_IN), tile),
            pl.BlockSpec((1, RWKV_IN), const),
            vec,
            pl.BlockSpec((LANES, GROUP_WIDTH), const),
            vec,
            pl.BlockSpec((LANES, GROUP_WIDTH), const),
            pl.BlockSpec((GATE_LORA, GROUP_WIDTH), const),
            vec, vec, vec, vec, vec,
            pl.BlockSpec((GROUP_WIDTH, GROUP_WIDTH), const),
        ],
        out_specs=pl.BlockSpec((1, rows, GROUP_WIDTH), tile),
        out_shape=jax.ShapeDtypeStruct((bsz, t, GROUP_WIDTH), BF16),
        scratch_shapes=[
            pltpu.VMEM((1, RWKV_IN), F32),
            pltpu.VMEM((GROUP_WIDTH // MXU_WIDTH, MXU_WIDTH, MXU_WIDTH), F32),
        ] + [staged] * 7,
        compiler_params=_cparams(("parallel", "arbitrary")),
        name="rwkv_mix",
    )(u3, mu, w0, wup_pad, a0, aup_pad, gup, k_k, k_a, r_k, ln_w, ln_b, bd)


def _fox_attn_kernel(qt_ref, k_ref, ce_ref, vt_ref, cend_ref, og_ref, o_ref,
                     m_scr, l_scr, acc_scr, kmax_scr, *, t):
    qi = pl.program_id(2)
    n_strips = 2 * t // LANES
    qt = qt_ref[0, 0, 0]
    frow = lax.broadcasted_iota(jnp.int32, (LANES, t), 0)
    zero = jnp.zeros_like(qt)
    main = jnp.concatenate([jnp.where(frow < HEAD_DIM, qt, zero),
                            jnp.where(frow < HEAD_DIM, zero, qt)], axis=1)
    erow = lax.broadcasted_iota(jnp.int32, (LANES, 2 * t), 0)
    ecol = lax.broadcasted_iota(jnp.int32, (LANES, 2 * t), 1)
    off = jnp.where(ecol < t, 0, 3)
    extra = jnp.where((erow >= off) & (erow < off + 3), -1.0, 0.0).astype(BF16)
    q_aug = jnp.concatenate([main, extra], axis=0)

    m_scr[...] = jnp.full_like(m_scr, -jnp.inf)
    l_scr[...] = jnp.zeros_like(l_scr)
    acc_scr[...] = jnp.zeros_like(acc_scr)

    @pl.when(qi == 0)
    def _():
        hid_r = lax.broadcasted_iota(jnp.int32, (LANES, LANES), 0) // HEAD_DIM
        hid_c = lax.broadcasted_iota(jnp.int32, (LANES, LANES), 1) // HEAD_DIM
        same_head = jnp.where(hid_r == hid_c, 1.0, 0.0).astype(BF16)

        def tile_max(j, best):
            kf = k_ref[0, pl.ds(pl.multiple_of(j * t, t), t), :].astype(F32)
            sq = _dot((kf * kf).astype(BF16), same_head)
            return jnp.maximum(best, jnp.max(sq, axis=0, keepdims=True))

        best = lax.fori_loop(0, k_ref.shape[1] // t, tile_max, jnp.zeros((1, LANES), F32))
        kmax_scr[...] = jnp.sqrt(best) * NORM_SLACK

    def step(j, masked):
        start = pl.multiple_of(j * t, t)
        k_aug = jnp.concatenate([k_ref[0, pl.ds(start, t), :], ce_ref[0, pl.ds(start, t), :]],
                                axis=1)
        vt = vt_ref[0, 0, j]
        zt = _dot(k_aug, q_aug)
        m_prev = m_scr[...]
        l_prev = l_scr[...]
        acc_prev = acc_scr[...]
        m_out, l_out, acc_out = [], [], [[], []]
        for s in range(n_strips):
            head = s // (n_strips // 2)
            cs = slice(s * LANES, (s + 1) * LANES)
            z = zt[:, cs]
            if masked:
                key = lax.broadcasted_iota(jnp.int32, (t, LANES), 0)
                qry = lax.broadcasted_iota(jnp.int32, (t, LANES), 1) + (s * LANES) % t
                z = jnp.where(key <= qry, z, -jnp.inf)
            m_new = jnp.maximum(m_prev[:, cs], jnp.max(z, axis=0, keepdims=True))
            alpha = jnp.exp2(m_prev[:, cs] - m_new)
            p = jnp.exp2(z - m_new)
            l_out.append(alpha * l_prev[:, cs] + jnp.sum(p, axis=0, keepdims=True))
            m_out.append(m_new)
            hs = slice(head * HEAD_DIM, (head + 1) * HEAD_DIM)
            qs = slice((s * LANES) % t, (s * LANES) % t + LANES)
            pv = _dot(vt[hs], p.astype(BF16))
            acc_out[head].append(alpha * acc_prev[hs, qs] + pv)
        m_scr[...] = jnp.concatenate(m_out, axis=1)
        l_scr[...] = jnp.concatenate(l_out, axis=1)
        acc_scr[...] = jnp.concatenate([jnp.concatenate(acc_out[0], axis=1),
                                        jnp.concatenate(acc_out[1], axis=1)], axis=0)

    def body(j, carry):
        step(j, False)
        return carry

    step(qi, True)

    qf = qt.astype(F32)
    qsq = qf * qf
    qnorm = jnp.sqrt(jnp.concatenate(
        [jnp.sum(qsq[:HEAD_DIM], axis=0, keepdims=True),
         jnp.sum(qsq[HEAD_DIM:], axis=0, keepdims=True)], axis=1)) * NORM_SLACK
    kmax = kmax_scr[...]
    kmax2 = jnp.concatenate([jnp.broadcast_to(kmax[:, 0:1], (1, t)),
                             jnp.broadcast_to(kmax[:, HEAD_DIM:HEAD_DIM + 1], (1, t))], axis=1)
    slack = qnorm * kmax2 - m_scr[...]
    cend = cend_ref[0][:, 0, :]
    lane = lax.broadcasted_iota(jnp.int32, cend.shape, 1)
    jrow = lax.broadcasted_iota(jnp.int32, (cend.shape[0], 1), 0)
    needed = jrow < 0
    for head in range(2):
        worst = jnp.max(slack[:, head * t:(head + 1) * t], axis=1, keepdims=True)
        c_head = jnp.sum(jnp.where(lane == 2 * pl.program_id(1) + head, cend, 0.0),
                         axis=1, keepdims=True)
        needed = needed | (worst - c_head * LOG2E > ZERO_PROB_EXP)
    first = jnp.min(jnp.where(needed & (jrow < qi), jrow, qi))
    lax.fori_loop(first, qi, body, 0)

    l = l_scr[...]
    acc = acc_scr[...]
    inv_n = 1.0 / HEAD_DIM
    halves = []
    for head in range(2):
        o = acc[head * HEAD_DIM:(head + 1) * HEAD_DIM] / l[:, head * t:(head + 1) * t]
        halves.append(o * lax.rsqrt(jnp.sum(o * o, axis=0, keepdims=True) * inv_n + RMS_EPS))
    o_t = jnp.concatenate(halves, axis=0)
    o_ref[0] = (jnp.transpose(o_t) * og_ref[...]).astype(o_ref.dtype)


def _fox_attn(qt5, k, cext, vt5, cend, out_g, tile):
    bsz, t_all, _ = k.shape
    pairs = N_HEADS // 2
    tiles = t_all // tile
    kern = functools.partial(_fox_attn_kernel, t=tile)
    return pl.pallas_call(
        kern,
        grid=(bsz, pairs, tiles),
        in_specs=[
            pl.BlockSpec((1, 1, 1, LANES, tile), lambda b, p, i: (b, p, i, 0, 0)),
            pl.BlockSpec((1, t_all, LANES), lambda b, p, i: (b, 0, p)),
            pl.BlockSpec((1, t_all, LANES), lambda b, p, i: (b, 0, p)),
            pl.BlockSpec((1, 1, tiles, LANES, tile), lambda b, p, i: (b, p, 0, 0, 0)),
            pl.BlockSpec((1, tiles, 1, LANES), lambda b, p, i: (b, 0, 0, 0)),
            pl.BlockSpec((1, LANES), lambda b, p, i: (0, p)),
        ],
        out_specs=pl.BlockSpec((1, tile, LANES), lambda b, p, i: (b, i, p)),
        out_shape=jax.ShapeDtypeStruct((bsz, t_all, GROUP_WIDTH), BF16),
        scratch_shapes=[
            pltpu.VMEM((1, 2 * tile), F32),
            pltpu.VMEM((1, 2 * tile), F32),
            pltpu.VMEM((LANES, tile), F32),
            pltpu.VMEM((1, LANES), F32),
        ],
        compiler_params=_cparams(("parallel", "parallel", "arbitrary")),
        name="fox_attn",
    )(qt5, k, cext, vt5, cend, out_g)


def _outproj_router_kernel(x_ref, yr_ref, yf_ref, wo_r_ref, wo_f_ref, g_ref, rwt_ref, rb_ref,
                           x1_ref, h_ref, idx_ref, gate_ref, rank_ref, count_ref):
    @pl.when(pl.program_id(0) == 0)
    def _():
        count_ref[...] = jnp.zeros_like(count_ref)

    x1 = x_ref[...] + _dot(yr_ref[...], wo_r_ref[...]) + _dot(yf_ref[...], wo_f_ref[...])
    x1_ref[...] = x1
    h = x1 * lax.rsqrt(jnp.mean(x1 * x1, axis=-1, keepdims=True) + RMS_EPS) * g_ref[...]
    h_ref[:, 0, :] = h
    logits = lax.dot_general(rwt_ref[...], h, (((1,), (1,)), ((), ())),
                             precision=lax.Precision.HIGHEST,
                             preferred_element_type=F32) + rb_ref[...]
    eidx = lax.broadcasted_iota(jnp.int32, logits.shape, 0)
    vals, idxs, picks = [], [], []
    for _ in range(TOP_K):
        m = jnp.max(logits, axis=0, keepdims=True)
        i = jnp.min(jnp.where(logits == m, eidx, N_EXPERTS), axis=0, keepdims=True)
        vals.append(m)
        idxs.append(i)
        picks.append(eidx == i)
        logits = jnp.where(picks[-1], -jnp.inf, logits)
    es = [jnp.exp(val - vals[0]) for val in vals]
    denom = es[0] + es[1] + es[2] + es[3]
    idx_ref[...] = jnp.concatenate(idxs, axis=0)
    gate_ref[...] = jnp.concatenate([e / denom for e in es], axis=0)

    tm = logits.shape[1]
    chosen = [jnp.where(pk, 1.0, 0.0) for pk in picks]
    any_k = chosen[0] + chosen[1] + chosen[2] + chosen[3]
    before = (lax.broadcasted_iota(jnp.int32, (tm, tm), 0)
              < lax.broadcasted_iota(jnp.int32, (tm, tm), 1))
    prefix = _dot(any_k.astype(BF16), jnp.where(before, 1.0, 0.0).astype(BF16))
    seen = count_ref[:, 0:1] + prefix
    rank_ref[...] = jnp.concatenate(
        [jnp.sum(ch * seen, axis=0, keepdims=True) for ch in chosen], axis=0).astype(jnp.int32)
    count_ref[...] = count_ref[...] + jnp.sum(any_k, axis=1, keepdims=True)


def _outproj_router(x2, yr, yf, wo_r, wo_f, g, rwt, rb, tm):
    n, d = x2.shape
    const = lambda i: (0, 0)
    row = lambda i: (i, 0)
    col = lambda i: (0, i)
    return pl.pallas_call(
        _outproj_router_kernel,
        grid=(n // tm,),
        in_specs=[
            pl.BlockSpec((tm, d), row),
            pl.BlockSpec((tm, GROUP_WIDTH), row),
            pl.BlockSpec((tm, GROUP_WIDTH), row),
            pl.BlockSpec((GROUP_WIDTH, d), const),
            pl.BlockSpec((GROUP_WIDTH, d), const),
            pl.BlockSpec((1, d), const),
            pl.BlockSpec((N_EXPERTS, d), const),
            pl.BlockSpec((N_EXPERTS, 1), const),
        ],
        out_specs=[
            pl.BlockSpec((tm, d), row),
            pl.BlockSpec((tm, 1, d), lambda i: (i, 0, 0)),
            pl.BlockSpec((TOP_K, tm), col),
            pl.BlockSpec((TOP_K, tm), col),
            pl.BlockSpec((TOP_K, tm), col),
            pl.BlockSpec((N_EXPERTS, LANES), const),
        ],
        out_shape=[
            jax.ShapeDtypeStruct((n, d), F32),
            jax.ShapeDtypeStruct((n, 1, d), F32),
            jax.ShapeDtypeStruct((TOP_K, n), jnp.int32),
            jax.ShapeDtypeStruct((TOP_K, n), F32),
            jax.ShapeDtypeStruct((TOP_K, n), jnp.int32),
            jax.ShapeDtypeStruct((N_EXPERTS, LANES), F32),
        ],
        compiler_params=_cparams(("arbitrary",)),
        name="outproj_router",
    )(x2, yr, yf, wo_r, wo_f, g, rwt, rb)


def _w1_split_kernel(w_ref, perm_ref, g_ref, l_ref):
    half = MXU_WIDTH // 2
    perm = perm_ref[...]
    for grp in range(w_ref.shape[2] // MXU_WIDTH):
        blk = w_ref[0, :, grp * MXU_WIDTH:(grp + 1) * MXU_WIDTH].astype(BF16)
        r = _dot(blk, perm)
        g_ref[0, :, grp * half:(grp + 1) * half] = r[:, :half].astype(BF16)
        l_ref[0, :, grp * half:(grp + 1) * half] = r[:, half:].astype(BF16)


def _w1_split(w1, tr):
    e, d, two_f = w1.shape
    half = MXU_WIDTH // 2
    src = jnp.arange(MXU_WIDTH, dtype=jnp.int32)[:, None]
    dst = jnp.arange(MXU_WIDTH, dtype=jnp.int32)[None, :]
    perm = (src == jnp.where(dst < half, 2 * dst, 2 * (dst - half) + 1)).astype(BF16)
    out_sds = jax.ShapeDtypeStruct((e, d, two_f // 2), BF16)
    return pl.pallas_call(
        _w1_split_kernel,
        grid=(e, d // tr),
        in_specs=[
            pl.BlockSpec((1, tr, two_f), lambda i, j: (i, j, 0)),
            pl.BlockSpec((MXU_WIDTH, MXU_WIDTH), lambda i, j: (0, 0)),
        ],
        out_specs=[pl.BlockSpec((1, tr, two_f // 2), lambda i, j: (i, j, 0))] * 2,
        out_shape=[out_sds, out_sds],
        compiler_params=_cparams(("parallel", "parallel")),
        name="w1_split",
    )(w1, perm)


def _expert_kernel(be_ref, tok_a_ref, tok_b_ref, tok_a_next_ref, dst_b_prev_ref, dst_a_ref,
                   dst_b_ref, h_hbm, w1g_a, w1l_a, b1g_a, b1l_a, w2_a, b2_a,
                   w1g_b, w1l_b, b1g_b, b1l_b, w2_b, b2_b,
                   y_hbm, xbuf_a, xbuf_b, obuf_a, obuf_b, xrows, gsem, osem, *, bm, n_real_rows):
    del be_ref
    i = pl.program_id(0)

    def gather_start(idx_ref, xbuf, sem, r, priority=0):
        pltpu.make_async_copy(h_hbm.at[idx_ref[0, 0, r]], xbuf.at[r], sem).start(priority=priority)

    def scatter_start(idx_ref, obuf, sem, r, priority=0):
        pltpu.make_async_copy(obuf.at[r], y_hbm.at[idx_ref[0, 0, r]], sem).start(priority=priority)

    def rows_wait(buf, sem):
        pltpu.make_async_copy(buf, buf, sem).wait()

    def mlp(xbuf, w1g_ref, w1l_ref, b1g_ref, b1l_ref, w2_ref, b2_ref):
        xrows[...] = xbuf[:, 0, :]
        xb = xrows[...].astype(BF16)
        dff = w1g_ref.shape[2]
        acts = []
        for piece in range(dff // MXU_WIDTH):
            cs = slice(piece * MXU_WIDTH, (piece + 1) * MXU_WIDTH)
            glu = _dot(xb, w1g_ref[0, :, cs]) + b1g_ref[0, :, cs]
            lin = _dot(xb, w1l_ref[0, :, cs]) + b1l_ref[0, :, cs]
            glu = jnp.minimum(glu, SWIGLU_LIMIT)
            lin = jnp.clip(lin, -SWIGLU_LIMIT, SWIGLU_LIMIT)
            acts.append((glu * _sigmoid(SWIGLU_ALPHA * glu) * (lin + 1.0)).astype(BF16))
        return _dot(jnp.concatenate(acts, axis=1), w2_ref[0]) + b2_ref[0]

    @pl.when(i == 0)
    def _():
        obuf_a[...] = jnp.zeros_like(obuf_a)
        obuf_b[...] = jnp.zeros_like(obuf_b)

        def first(r, carry):
            pltpu.make_async_copy(obuf_a.at[r], y_hbm.at[n_real_rows + 2 * bm + r],
                                  osem.at[0]).start()
            gather_start(tok_a_ref, xbuf_a, gsem.at[0], r)
            return carry

        lax.fori_loop(0, bm, first, 0)

    for r in range(bm):
        gather_start(tok_b_ref, xbuf_b, gsem.at[1], r, r % 2)
        scatter_start(dst_b_prev_ref, obuf_b, osem.at[1], r, r % 2)
    rows_wait(xbuf_a, gsem.at[0])
    rows_wait(obuf_a, osem.at[0])
    obuf_a[:, 0, :] = mlp(xbuf_a, w1g_a, w1l_a, b1g_a, b1l_a, w2_a, b2_a)

    for r in range(bm):
        gather_start(tok_a_next_ref, xbuf_a, gsem.at[0], r, r % 2)
        scatter_start(dst_a_ref, obuf_a, osem.at[0], r, r % 2)
    rows_wait(xbuf_b, gsem.at[1])
    rows_wait(obuf_b, osem.at[1])
    obuf_b[:, 0, :] = mlp(xbuf_b, w1g_b, w1l_b, b1g_b, b1l_b, w2_b, b2_b)

    @pl.when(i == pl.num_programs(0) - 1)
    def _():
        def last(r, carry):
            scatter_start(dst_b_ref, obuf_b, osem.at[1], r)
            return carry

        lax.fori_loop(0, bm, last, 0)
        rows_wait(obuf_b, osem.at[1])
        rows_wait(obuf_a, osem.at[0])
        rows_wait(xbuf_a, gsem.at[0])


def _expert_mlp(block_e, tok_blocks, dst_blocks, h2, w1g, w1l, b1g, b1l, w2, b2, bm):
    n_blocks = tok_blocks.shape[0]
    assert n_blocks % 2 == 0
    n, _, d = h2.shape
    dff = w1g.shape[2]
    n_real_rows = TOP_K * n
    idx_spec = lambda fn: pl.BlockSpec((1, 1, bm), fn, memory_space=pltpu.SMEM)

    def weight_specs(which):
        wmap = lambda i, be: (be[2 * i + which], 0, 0)
        return [
            pl.BlockSpec((1, d, dff), wmap),
            pl.BlockSpec((1, d, dff), wmap),
            pl.BlockSpec((1, 1, dff), wmap),
            pl.BlockSpec((1, 1, dff), wmap),
            pl.BlockSpec((1, dff, d), wmap),
            pl.BlockSpec((1, 1, d), wmap),
        ]

    grid_spec = pltpu.PrefetchScalarGridSpec(
        num_scalar_prefetch=1,
        grid=(n_blocks // 2,),
        in_specs=[
            idx_spec(lambda i, be: (2 * i, 0, 0)),
            idx_spec(lambda i, be: (2 * i + 1, 0, 0)),
            idx_spec(lambda i, be: (jnp.minimum(2 * i + 2, n_blocks - 1), 0, 0)),
            idx_spec(lambda i, be: (2 * i, 0, 0)),
            idx_spec(lambda i, be: (2 * i + 1, 0, 0)),
            idx_spec(lambda i, be: (2 * i + 2, 0, 0)),
            pl.BlockSpec(memory_space=pl.ANY),
        ] + weight_specs(0) + weight_specs(1),
        out_specs=pl.BlockSpec(memory_space=pl.ANY),
        scratch_shapes=[
            pltpu.VMEM((bm, 1, d), F32),
            pltpu.VMEM((bm, 1, d), F32),
            pltpu.VMEM((bm, 1, d), F32),
            pltpu.VMEM((bm, 1, d), F32),
            pltpu.VMEM((bm, d), F32),
            pltpu.SemaphoreType.DMA((2,)),
            pltpu.SemaphoreType.DMA((2,)),
        ],
    )
    weights = (w1g, w1l, b1g, b1l, w2, b2)
    return pl.pallas_call(
        functools.partial(_expert_kernel, bm=bm, n_real_rows=n_real_rows),
        grid_spec=grid_spec,
        out_shape=jax.ShapeDtypeStruct((n_real_rows + 3 * bm, 1, d), F32),
        compiler_params=_cparams(("arbitrary",)),
        name="expert_mlp",
    )(block_e, tok_blocks, tok_blocks, tok_blocks, dst_blocks, dst_blocks, dst_blocks, h2,
      *weights, *weights)


def _combine_kernel(gate_ref, x1_ref, g_ref, y0_ref, y1_ref, y2_ref, y3_ref, o_ref):
    gates = gate_ref[...]
    y = x1_ref[...]
    for kk, y_ref in enumerate((y0_ref, y1_ref, y2_ref, y3_ref)):
        y = y + y_ref[:, 0, :] * gates[:, kk:kk + 1]
    o_ref[...] = y * lax.rsqrt(jnp.mean(y * y, axis=-1, keepdims=True) + RMS_EPS) * g_ref[...]


def _combine(gates_t, x1, g, y_all, tc):
    n, d = x1.shape
    tiles = n // tc
    y_spec = lambda kk: pl.BlockSpec((tc, 1, d), lambda i: (kk * tiles + i, 0, 0))
    return pl.pallas_call(
        _combine_kernel,
        grid=(tiles,),
        in_specs=[
            pl.BlockSpec((tc, TOP_K), lambda i: (i, 0)),
            pl.BlockSpec((tc, d), lambda i: (i, 0)),
            pl.BlockSpec((1, d), lambda i: (0, 0)),
        ] + [y_spec(kk) for kk in range(TOP_K)],
        out_specs=pl.BlockSpec((tc, d), lambda i: (i, 0)),
        out_shape=jax.ShapeDtypeStruct((n, d), F32),
        compiler_params=_cparams(("parallel",)),
        name="combine",
    )(gates_t, x1, g, y_all, y_all, y_all, y_all)


def _slot_sources_kernel(pos_ref, out_ref, *, chunk):
    i = pl.program_id(0)

    @pl.when(i == 0)
    def _():
        def init(p, carry):
            out_ref[p] = -1
            return carry

        lax.fori_loop(0, out_ref.shape[0], init, 0, unroll=32)

    base = i * chunk

    def place(s, carry):
        out_ref[pos_ref[0, 0, s]] = base + s
        return carry

    lax.fori_loop(0, chunk, place, 0, unroll=16)


def _slot_sources(pos, n_pad):
    n_slots = pos.shape[0]
    chunk = _pick(n_slots, 8192)
    return pl.pallas_call(
        functools.partial(_slot_sources_kernel, chunk=chunk),
        grid=(n_slots // chunk,),
        in_specs=[pl.BlockSpec((1, 1, chunk), lambda i: (i, 0, 0), memory_space=pltpu.SMEM)],
        out_specs=pl.BlockSpec(memory_space=pltpu.SMEM),
        out_shape=jax.ShapeDtypeStruct((n_pad,), jnp.int32),
        compiler_params=_cparams(("arbitrary",)),
        name="slot_sources",
    )(pos.reshape(n_slots // chunk, 1, chunk))


def _dispatch_plan(idx, rank, counts, bm):
    n = idx.shape[1]
    n_slots = TOP_K * n
    sizes = counts[:, 0].astype(jnp.int32)
    padded = (sizes + bm - 1) // bm * bm
    pad_ends = jnp.cumsum(padded)
    pad_starts = pad_ends - padded
    experts = jnp.arange(N_EXPERTS, dtype=jnp.int32)
    start_of = jnp.sum(jnp.where(idx[..., None] == experts, pad_starts, 0), axis=-1)
    pos = (start_of + rank).reshape(-1)
    n_pad = n_slots + N_EXPERTS * bm
    n_blocks = n_pad // bm
    slot_src = _slot_sources(pos, n_pad)
    p = jnp.arange(n_pad, dtype=jnp.int32)
    spare = n_slots + (p // bm) % 2 * bm + p % bm
    real = slot_src >= 0
    tok_blocks = jnp.where(real, slot_src % n, 0).reshape(n_blocks, 1, bm)
    dst = jnp.where(real, slot_src, spare)
    dst_blocks = jnp.concatenate([spare[bm:2 * bm], dst]).reshape(n_blocks + 1, 1, bm)
    block_start = jnp.arange(n_blocks, dtype=jnp.int32) * bm
    block_e = jnp.minimum(jnp.sum(pad_ends[None, :] <= block_start[:, None], axis=1),
                          N_EXPERTS - 1).astype(jnp.int32)
    return tok_blocks, dst_blocks, block_e


def _block_diag_ones():
    hid = jnp.arange(GROUP_WIDTH, dtype=jnp.int32) // HEAD_DIM
    return (hid[:, None] == hid[None, :]).astype(BF16)


def _pick(n, pref):
    return pref if n % pref == 0 else n


def kernel(x, attn_norm_g, w_in, rwkv_mu, rwkv_w0, rwkv_w_up, rwkv_a0, rwkv_a_up, rwkv_g_up,
           rwkv_k_k, rwkv_k_a, rwkv_r_k, rwkv_ln_w, rwkv_ln_b, fox_f_bias, fox_out_g, w_out,
           ffn_norm_g, router_w, router_b, expert_w1, expert_b1, expert_w2, expert_b2,
           final_norm_g):
    bsz, t, d = x.shape
    n = bsz * t
    depth = w_in.shape[0]
    assert depth == 1, "the final norm is fused into the last stage of a single layer"
    bd = _block_diag_ones()
    x2 = x.reshape(n, d)
    for l in range(depth):
        w_l = w_in[l]
        w_r = w_l[:, :RWKV_IN].astype(BF16)
        w_qkv = w_l[:, RWKV_IN:RWKV_IN + 3 * GROUP_WIDTH].astype(BF16)
        w_qt = w_qkv[:, :GROUP_WIDTH].T
        w_k = w_qkv[:, GROUP_WIDTH:2 * GROUP_WIDTH]
        w_vt = w_qkv[:, 2 * GROUP_WIDTH:].T
        w_f = jnp.pad(w_l[:, RWKV_IN + 3 * GROUP_WIDTH:], ((0, 0), (0, LANES - N_HEADS))).astype(BF16)
        fb_pad = jnp.pad(fox_f_bias[l], (0, LANES - N_HEADS)).reshape(1, LANES)
        wup_pad = jnp.pad(rwkv_w_up[l], ((0, LANES - DECAY_LORA), (0, 0))).astype(BF16)
        aup_pad = jnp.pad(rwkv_a_up[l], ((DECAY_LORA, 0), (0, 0))).astype(BF16)
        gup = rwkv_g_up[l].astype(BF16)
        vec = lambda a: a.reshape(1, -1)

        u_r, qt5, k, vt5, fl = _inproj(x2, vec(attn_norm_g[l]), w_r, w_qt, w_k, w_vt, w_f,
                                       bsz, ATTN_TILE)
        cext, cend = _fox_gate(fl.reshape(bsz, t, LANES), fb_pad, _gate_piece_selectors(),
                               ATTN_TILE)
        y_rwkv = _rwkv_mix(
            u_r.reshape(bsz, t, RWKV_IN), vec(rwkv_mu[l]), vec(rwkv_w0[l]), wup_pad,
            vec(rwkv_a0[l]), aup_pad, gup, vec(rwkv_k_k[l]), vec(rwkv_k_a[l]), vec(rwkv_r_k[l]),
            vec(rwkv_ln_w[l]), vec(rwkv_ln_b[l]), bd, RWKV_CHUNKS_PER_STEP)
        y_fox = _fox_attn(qt5, k.reshape(bsz, t, GROUP_WIDTH), cext, vt5, cend,
                          vec(fox_out_g[l]), ATTN_TILE)

        wo = w_out[l].astype(BF16)
        x1, h2, idx, gates, rank, counts = _outproj_router(
            x2, y_rwkv.reshape(n, GROUP_WIDTH), y_fox.reshape(n, GROUP_WIDTH),
            wo[:GROUP_WIDTH], wo[GROUP_WIDTH:], vec(ffn_norm_g[l]),
            router_w[l].T, router_b[l].reshape(N_EXPERTS, 1), _pick(n, 512))

        bm = 256
        tok_blocks, dst_blocks, block_e = _dispatch_plan(idx, rank, counts, bm)
        w1g, w1l = _w1_split(expert_w1[l], 512)
        b1 = expert_b1[l]
        b1g = b1[:, None, 0::2]
        b1l = b1[:, None, 1::2]
        y_all = _expert_mlp(block_e, tok_blocks, dst_blocks, h2, w1g, w1l, b1g, b1l,
                            expert_w2[l].astype(BF16), expert_b2[l][:, None, :], bm)
        x2 = _combine(gates.T, x1, vec(final_norm_g), y_all, _pick(n, 256))
    return x2.reshape(bsz, t, d)
```

```python
import functools

import jax
import jax.numpy as jnp
from jax import lax
from jax.experimental import pallas as pl
from jax.experimental.pallas import tpu as pltpu

F32 = jnp.float32
BF16 = jnp.bfloat16

HEAD_DIM = 64
N_HEADS = 8
GROUP_WIDTH = N_HEADS * HEAD_DIM
DECAY_LORA = 64
AAA_LORA = 64
GATE_LORA = 128
RWKV_IN = 3 * GROUP_WIDTH + DECAY_LORA + AAA_LORA + GATE_LORA
LORA_OFF = 3 * GROUP_WIDTH
N_EXPERTS = 32
TOP_K = 4
SWIGLU_ALPHA = 1.702
SWIGLU_LIMIT = 7.0
RMS_EPS = 1e-5
RWKV_GN_EPS = 64e-5
LANES = 128
RWKV_CHUNK = 64
RWKV_CHUNKS_PER_STEP = 4
ATTN_TILE = 512
MXU_WIDTH = 256
LOG2E = 1.4426950408889634
Q_SCALE = HEAD_DIM ** -0.5 * LOG2E
ZERO_PROB_EXP = -152.0
NORM_SLACK = 1.0 + 2.0 ** -6
VMEM_LIMIT = 56 * 1024 * 1024


def _cparams(semantics):
    return pltpu.CompilerParams(dimension_semantics=semantics, vmem_limit_bytes=VMEM_LIMIT)


def _dot(a, b):
    return jnp.dot(a, b, preferred_element_type=F32)


def _dot_nt(a, b):
    return lax.dot_general(a, b, (((1,), (1,)), ((), ())), preferred_element_type=F32)


def _dot_tn(a, b):
    return lax.dot_general(a, b, (((0,), (0,)), ((), ())), preferred_element_type=F32)


def _split3(x):
    hi = x.astype(BF16)
    r1 = x - hi.astype(F32)
    mid = r1.astype(BF16)
    lo = (r1 - mid.astype(F32)).astype(BF16)
    return hi, mid, lo


def _dot_exact_lhs(a_bf16, x):
    hi, mid, lo = _split3(x)
    return _dot(a_bf16, hi) + _dot(a_bf16, mid) + _dot(a_bf16, lo)


def _head_sums(x, same_head):
    w = same_head.shape[0]
    parts = []
    for g in range(x.shape[1] // w):
        xs = x[:, g * w:(g + 1) * w]
        hi = xs.astype(BF16)
        lo = (xs - hi.astype(F32)).astype(BF16)
        parts.append(_dot(hi, same_head) + _dot(lo, same_head))
    return jnp.concatenate(parts, axis=1)


def _softplus(z):
    return jnp.maximum(z, 0.0) + jnp.log1p(jnp.exp(-jnp.abs(z)))


def _sigmoid(z):
    return 1.0 / (1.0 + jnp.exp(-z))


def _inproj_kernel(x_ref, g_ref, wr_ref, wqt_ref, wk_ref, wvt_ref, wf_ref,
                   ur_ref, qt_ref, k_ref, vt_ref, fl_ref):
    x = x_ref[...]
    h = x * lax.rsqrt(jnp.mean(x * x, axis=-1, keepdims=True) + RMS_EPS) * g_ref[...]
    hb = h.astype(BF16)
    ur_ref[...] = _dot(hb, wr_ref[...])
    k_ref[...] = _dot(hb, wk_ref[...]).astype(BF16)
    fl_ref[...] = _dot(hb, wf_ref[...])
    qt = (_dot_nt(wqt_ref[...], hb) * Q_SCALE).astype(BF16)
    vt = _dot_nt(wvt_ref[...], hb).astype(BF16)
    for p in range(N_HEADS // 2):
        qt_ref[0, p, 0] = qt[p * LANES:(p + 1) * LANES]
        vt_ref[0, p, 0] = vt[p * LANES:(p + 1) * LANES]


def _inproj(x2, g, w_r, w_qt, w_k, w_vt, w_f, bsz, tm):
    n, d = x2.shape
    nt = n // bsz // tm
    pairs = N_HEADS // 2
    const = lambda i: (0, 0)
    row = lambda i: (i, 0)
    fm = lambda i: (i // nt, 0, i % nt, 0, 0)
    fm_sds = jax.ShapeDtypeStruct((bsz, pairs, nt, LANES, tm), BF16)
    return pl.pallas_call(
        _inproj_kernel,
        grid=(n // tm,),
        in_specs=[
            pl.BlockSpec((tm, d), row),
            pl.BlockSpec((1, d), const),
            pl.BlockSpec(w_r.shape, const),
            pl.BlockSpec(w_qt.shape, const),
            pl.BlockSpec(w_k.shape, const),
            pl.BlockSpec(w_vt.shape, const),
            pl.BlockSpec(w_f.shape, const),
        ],
        out_specs=[
            pl.BlockSpec((tm, RWKV_IN), row),
            pl.BlockSpec((1, pairs, 1, LANES, tm), fm),
            pl.BlockSpec((tm, GROUP_WIDTH), row),
            pl.BlockSpec((1, pairs, 1, LANES, tm), fm),
            pl.BlockSpec((tm, LANES), row),
        ],
        out_shape=[
            jax.ShapeDtypeStruct((n, RWKV_IN), F32),
            fm_sds,
            jax.ShapeDtypeStruct((n, GROUP_WIDTH), BF16),
            fm_sds,
            jax.ShapeDtypeStruct((n, LANES), F32),
        ],
        compiler_params=_cparams(("parallel",)),
        name="inproj",
    )(x2, g, w_r, w_qt, w_k, w_vt, w_f)


def _fox_gate_kernel(fl_ref, fb_ref, sel_ref, c_ref, cend_ref, carry):
    tt = fl_ref.shape[1]

    @pl.when(pl.program_id(1) == 0)
    def _():
        carry[...] = jnp.zeros_like(carry)

    z = fl_ref[0] + fb_ref[...]
    log_f = jnp.minimum(z, 0.0) - jnp.log1p(jnp.exp(-jnp.abs(z)))
    ri = lax.broadcasted_iota(jnp.int32, (tt, tt), 0)
    ci = lax.broadcasted_iota(jnp.int32, (tt, tt), 1)
    tri = jnp.where(ri >= ci, 1.0, 0.0).astype(BF16)
    c = _dot_exact_lhs(tri, log_f) + carry[...]
    carry[...] = c[tt - 1:tt, :]
    cend_ref[0, 0] = c[tt - 1:tt, :]
    hi, mid, lo = _split3(c * LOG2E)
    c_ref[0] = (_dot(hi, sel_ref[0]) + _dot(mid, sel_ref[1]) + _dot(lo, sel_ref[2])).astype(BF16)


def _gate_piece_selectors():
    h = jnp.arange(LANES, dtype=jnp.int32)[:, None]
    col = jnp.arange(GROUP_WIDTH, dtype=jnp.int32)[None, :]
    sels = []
    for m in range(3):
        target = LANES * (h // 2) + 3 * (h % 2) + m
        sels.append(((col == target) & (h < N_HEADS)).astype(BF16))
    return jnp.stack(sels)


def _fox_gate(fl3, fb_pad, sel, tt):
    b, t, _ = fl3.shape
    return pl.pallas_call(
        _fox_gate_kernel,
        grid=(b, t // tt),
        in_specs=[
            pl.BlockSpec((1, tt, LANES), lambda i, j: (i, j, 0)),
            pl.BlockSpec((1, LANES), lambda i, j: (0, 0)),
            pl.BlockSpec((3, LANES, GROUP_WIDTH), lambda i, j: (0, 0, 0)),
        ],
        out_specs=[
            pl.BlockSpec((1, tt, GROUP_WIDTH), lambda i, j: (i, j, 0)),
            pl.BlockSpec((1, 1, 1, LANES), lambda i, j: (i, j, 0, 0)),
        ],
        out_shape=[
            jax.ShapeDtypeStruct((b, t, GROUP_WIDTH), BF16),
            jax.ShapeDtypeStruct((b, t // tt, 1, LANES), F32),
        ],
        scratch_shapes=[pltpu.VMEM((1, LANES), F32)],
        compiler_params=_cparams(("parallel", "arbitrary")),
        name="fox_gate",
    )(fl3, fb_pad, sel)


def _rwkv_prep_body(u_ref, mu_ref, w0_ref, wup_ref, a0_ref, aup_ref, gup_ref, kk_ref, ka_ref,
                      bd_ref, r_out, k_out, v_out, lw_out, kk_out, b_out, g_out, carry):
    tt = u_ref.shape[1]

    @pl.when(pl.program_id(1) == 0)
    def _():
        carry[...] = jnp.zeros_like(carry)

    u = u_ref[0]
    prev = pltpu.roll(u, 1, axis=0)
    row = lax.broadcasted_iota(jnp.int32, u.shape, 0)
    prev = jnp.where(row == 0, carry[...], prev)
    carry[...] = u[tt - 1:tt, :]
    us = u + (prev - u) * mu_ref[...]

    r = us[:, :GROUP_WIDTH]
    k = us[:, GROUP_WIDTH:2 * GROUP_WIDTH]
    v = us[:, 2 * GROUP_WIDTH:LORA_OFF]
    wa = us[:, LORA_OFF:LORA_OFF + LANES]
    gl = us[:, LORA_OFF + LANES:]

    w_lin = _dot(jnp.tanh(wa).astype(BF16), wup_ref[...])
    a_lin = _dot(wa.astype(BF16), aup_ref[...])
    w = -_softplus(-(w0_ref[...] + w_lin)) - 0.5
    lw_out[0] = -jnp.exp(w)
    a = _sigmoid(a0_ref[...] + a_lin)
    g_out[0] = _dot(_sigmoid(gl).astype(BF16), gup_ref[...])

    kkr = k * kk_ref[...]
    ss = _head_sums(kkr * kkr, bd_ref[...])
    kk = kkr / jnp.maximum(jnp.sqrt(ss), 1e-12)
    r_out[0] = r
    k_out[0] = k * (1.0 + (a - 1.0) * ka_ref[...])
    v_out[0] = v
    kk_out[0] = kk
    b_out[0] = kk * a


def _rwkv_scan_body(r_ref, k_ref, v_ref, lw_ref, kk_ref, b_ref, g_ref, rk_ref, lnw_ref, lnb_ref,
                      bd_ref, o_ref, s_scr, *, n_chunks):
    c = RWKV_CHUNK
    w = MXU_WIDTH
    hpg = w // HEAD_DIM
    n_groups = GROUP_WIDTH // w

    @pl.when(pl.program_id(1) == 0)
    def _():
        s_scr[...] = jnp.zeros_like(s_scr)

    row = lax.broadcasted_iota(jnp.int32, (c, w), 0)
    u = lax.broadcasted_iota(jnp.int32, (c, w), 1) % HEAD_DIM
    strict = row > u
    incl = row >= u
    eye = jnp.where(row == u, 1.0, 0.0)
    level_masks = []
    s = 1
    while s < c:
        same = (row // (2 * s)) == (u // (2 * s))
        level_masks.append(same & ((row % (2 * s)) >= s) & ((u % (2 * s)) < s))
        s *= 2
    same_head = (lax.broadcasted_iota(jnp.int32, (w, w), 0) // HEAD_DIM
                 == lax.broadcasted_iota(jnp.int32, (w, w), 1) // HEAD_DIM)
    tri = jnp.where(lax.broadcasted_iota(jnp.int32, (c, c), 0)
                    >= lax.broadcasted_iota(jnp.int32, (c, c), 1), 1.0, 0.0).astype(BF16)

    def bdiag(x):
        xb = x.astype(BF16)
        tiled = jnp.concatenate([xb] * hpg, axis=0)
        return jnp.where(same_head, tiled, jnp.zeros_like(tiled))

    chains = [(ci, gi) for ci in range(n_chunks) for gi in range(n_groups)]
    lhs, rk_t, vbs, xcat, xneg, gam, rt32 = {}, {}, {}, {}, {}, {}, {}
    for ci in range(n_chunks):
        rs = slice(ci * c, (ci + 1) * c)
        r = r_ref[0, rs, :]
        k = k_ref[0, rs, :]
        lw = lw_ref[0, rs, :]
        kk = kk_ref[0, rs, :]
        b = b_ref[0, rs, :]
        g_cum = _dot_exact_lhs(tri, lw)
        g_last = g_cum[c - 1:c, :]
        r_t = r * jnp.exp(g_cum)
        kk_t = kk * jnp.exp(g_cum - lw)
        e_neg = jnp.exp(-g_cum)
        b_n = b * e_neg
        k_n = k * e_neg
        e_end = jnp.exp(g_last - g_cum)
        b_e = (b * e_end).astype(BF16)
        k_e = (k * e_end).astype(BF16)
        gamma = jnp.exp(g_last)
        vb = v_ref[0, rs, :].astype(BF16)
        for gi in range(n_groups):
            gs = slice(gi * w, (gi + 1) * w)
            ch = (ci, gi)
            lhs[ch] = jnp.concatenate([r_t[:, gs].astype(BF16), kk_t[:, gs].astype(BF16)], axis=0)
            rk_t[ch] = (jnp.concatenate([bdiag(b_n[:, gs]), bdiag(k_n[:, gs])], axis=0),
                        bdiag(kk_t[:, gs]))
            vbs[ch] = vb[:, gs]
            xcat[ch] = jnp.concatenate([b_e[:, gs], k_e[:, gs]], axis=0)
            xneg[ch] = jnp.concatenate([-b_e[:, gs], k_e[:, gs]], axis=0)
            gam[ch] = gamma[:, gs]
            rt32[ch] = r_t[:, gs]

    p = {ch: _dot_nt(lhs[ch], rk_t[ch][0]) for ch in chains}
    l_b = {ch: jnp.where(strict, p[ch][c:, :w], 0.0) for ch in chains}
    l_k = {ch: jnp.where(strict, p[ch][c:, w:], 0.0).astype(BF16) for ch in chains}
    p_br = {ch: jnp.where(incl, p[ch][:c, :w], 0.0).astype(BF16) for ch in chains}
    p_kr = {ch: jnp.where(incl, p[ch][:c, w:], 0.0).astype(BF16) for ch in chains}
    v_bd = {ch: bdiag(vbs[ch]) for ch in chains}
    lkv = {ch: _dot(l_k[ch], v_bd[ch]) for ch in chains}

    t_inv = {ch: eye - jnp.where(level_masks[0], l_b[ch], 0.0) for ch in chains}
    for m in level_masks[1:]:
        tb = {ch: t_inv[ch].astype(BF16) for ch in chains}
        ct = {ch: _dot(jnp.where(m, l_b[ch], 0.0).astype(BF16), bdiag(tb[ch])) for ch in chains}
        t_inv = {ch: t_inv[ch] - _dot(tb[ch], bdiag(ct[ch])) for ch in chains}

    mm = {ch: _dot(t_inv[ch].astype(BF16),
                   jnp.concatenate([rk_t[ch][1], bdiag(lkv[ch])], axis=1)) for ch in chains}
    pm = {ch: _dot(p_br[ch], jnp.concatenate([bdiag(mm[ch][:, :w]), bdiag(mm[ch][:, w:])], axis=1))
          for ch in chains}
    pkv = {ch: _dot(p_kr[ch], v_bd[ch]) for ch in chains}
    n1 = {ch: (rt32[ch] - pm[ch][:, :w]).astype(BF16) for ch in chains}
    n2 = {ch: pkv[ch] - pm[ch][:, w:] for ch in chains}
    omega = {ch: jnp.where(same_head, _dot_tn(mm[ch][:, :w].astype(BF16), xcat[ch][:c]), 0.0)
             .astype(BF16) for ch in chains}
    psi = {ch: jnp.where(same_head, _dot_tn(
        jnp.concatenate([mm[ch][:, w:].astype(BF16), vbs[ch]], axis=0), xneg[ch]), 0.0)
           for ch in chains}

    state = [s_scr[gi] for gi in range(n_groups)]
    ys = []
    for ci in range(n_chunks):
        sb = [state[gi].astype(BF16) for gi in range(n_groups)]
        ys.append(jnp.concatenate(
            [_dot_nt(n1[(ci, gi)], sb[gi]) + n2[(ci, gi)] for gi in range(n_groups)], axis=1))
        state = [state[gi] * gam[(ci, gi)] - _dot(sb[gi], omega[(ci, gi)]) + psi[(ci, gi)]
                 for gi in range(n_groups)]
    for gi in range(n_groups):
        s_scr[gi] = state[gi]

    y = jnp.concatenate(ys, axis=0)
    r = r_ref[0]
    k = k_ref[0]
    v = v_ref[0]
    bd = bd_ref[...]
    inv_n = 1.0 / HEAD_DIM
    mean = _head_sums(y, bd) * inv_n
    d = y - mean
    var = _head_sums(d * d, bd) * inv_n
    yn = d * lax.rsqrt(var + RWKV_GN_EPS) * lnw_ref[...] + lnb_ref[...]
    bonus = _head_sums(r * k * rk_ref[...], bd) * v
    o_ref[0] = ((yn + bonus) * g_ref[0]).astype(o_ref.dtype)


def _rwkv_kernel(u_ref, mu_ref, w0_ref, wup_ref, a0_ref, aup_ref, gup_ref, kkw_ref, ka_ref,
                 rk_ref, lnw_ref, lnb_ref, bd_ref, o_ref, carry, s_scr,
                 r_s, k_s, v_s, lw_s, kk_s, b_s, g_s, *, n_chunks):
    _rwkv_prep_body(u_ref, mu_ref, w0_ref, wup_ref, a0_ref, aup_ref, gup_ref, kkw_ref, ka_ref,
                    bd_ref, r_s, k_s, v_s, lw_s, kk_s, b_s, g_s, carry)
    _rwkv_scan_body(r_s, k_s, v_s, lw_s, kk_s, b_s, g_s, rk_ref, lnw_ref, lnb_ref, bd_ref,
                    o_ref, s_scr, n_chunks=n_chunks)


def _rwkv_mix(u3, mu, w0, wup_pad, a0, aup_pad, gup, k_k, k_a, r_k, ln_w, ln_b, bd, n_chunks):
    bsz, t, _ = u3.shape
    rows = RWKV_CHUNK * n_chunks
    const = lambda i, j: (0, 0)
    tile = lambda i, j: (i, j, 0)
    vec = pl.BlockSpec((1, GROUP_WIDTH), const)
    staged = pltpu.VMEM((1, rows, GROUP_WIDTH), F32)
    return pl.pallas_call(
        functools.partial(_rwkv_kernel, n_chunks=n_chunks),
        grid=(bsz, t // rows),
        in_specs=[
            pl.BlockSpec((1, rows, RWKV_IN), tile),
            pl.BlockSpec((1, RWKV_IN), const),
            vec,
            pl.BlockSpec((LANES, GROUP_WIDTH), const),
            vec,
            pl.BlockSpec((LANES, GROUP_WIDTH), const),
            pl.BlockSpec((GATE_LORA, GROUP_WIDTH), const),
            vec, vec, vec, vec, vec,
            pl.BlockSpec((MXU_WIDTH, MXU_WIDTH), const),
        ],
        out_specs=pl.BlockSpec((1, rows, GROUP_WIDTH), tile),
        out_shape=jax.ShapeDtypeStruct((bsz, t, GROUP_WIDTH), BF16),
        scratch_shapes=[
            pltpu.VMEM((1, RWKV_IN), F32),
            pltpu.VMEM((GROUP_WIDTH // MXU_WIDTH, MXU_WIDTH, MXU_WIDTH), F32),
        ] + [staged] * 7,
        compiler_params=_cparams(("parallel", "arbitrary")),
        name="rwkv_mix",
    )(u3, mu, w0, wup_pad, a0, aup_pad, gup, k_k, k_a, r_k, ln_w, ln_b, bd)


def _fox_attn_kernel(qt_ref, k_ref, ce_ref, vt_ref, cend_ref, og_ref, o_ref,
                     m_scr, l_scr, acc_scr, kmax_scr, *, t):
    qi = pl.program_id(2)
    n_strips = 2 * t // LANES
    qt = qt_ref[0, 0, 0]
    frow = lax.broadcasted_iota(jnp.int32, (LANES, t), 0)
    zero = jnp.zeros_like(qt)
    main = jnp.concatenate([jnp.where(frow < HEAD_DIM, qt, zero),
                            jnp.where(frow < HEAD_DIM, zero, qt)], axis=1)
    erow = lax.broadcasted_iota(jnp.int32, (LANES, 2 * t), 0)
    ecol = lax.broadcasted_iota(jnp.int32, (LANES, 2 * t), 1)
    off = jnp.where(ecol < t, 0, 3)
    extra = jnp.where((erow >= off) & (erow < off + 3), -1.0, 0.0).astype(BF16)
    q_aug = jnp.concatenate([main, extra], axis=0)

    m_scr[...] = jnp.full_like(m_scr, -jnp.inf)
    l_scr[...] = jnp.zeros_like(l_scr)
    acc_scr[...] = jnp.zeros_like(acc_scr)

    @pl.when(qi == 0)
    def _():
        hid_r = lax.broadcasted_iota(jnp.int32, (LANES, LANES), 0) // HEAD_DIM
        hid_c = lax.broadcasted_iota(jnp.int32, (LANES, LANES), 1) // HEAD_DIM
        same_head = jnp.where(hid_r == hid_c, 1.0, 0.0).astype(BF16)

        def tile_max(j, best):
            kf = k_ref[0, pl.ds(pl.multiple_of(j * t, t), t), :].astype(F32)
            sq = _dot((kf * kf).astype(BF16), same_head)
            return jnp.maximum(best, jnp.max(sq, axis=0, keepdims=True))

        best = lax.fori_loop(0, k_ref.shape[1] // t, tile_max, jnp.zeros((1, LANES), F32))
        kmax_scr[...] = jnp.sqrt(best) * NORM_SLACK

    def step(j, masked):
        start = pl.multiple_of(j * t, t)
        k_aug = jnp.concatenate([k_ref[0, pl.ds(start, t), :], ce_ref[0, pl.ds(start, t), :]],
                                axis=1)
        vt = vt_ref[0, 0, j]
        zt = _dot(k_aug, q_aug)
        m_prev = m_scr[...]
        l_prev = l_scr[...]
        acc_prev = acc_scr[...]
        m_out, l_out, acc_out = [], [], [[], []]
        for s in range(n_strips):
            head = s // (n_strips // 2)
            cs = slice(s * LANES, (s + 1) * LANES)
            z = zt[:, cs]
            if masked:
                key = lax.broadcasted_iota(jnp.int32, (t, LANES), 0)
                qry = lax.broadcasted_iota(jnp.int32, (t, LANES), 1) + (s * LANES) % t
                z = jnp.where(key <= qry, z, -jnp.inf)
            m_new = jnp.maximum(m_prev[:, cs], jnp.max(z, axis=0, keepdims=True))
            alpha = jnp.exp2(m_prev[:, cs] - m_new)
            p = jnp.exp2(z - m_new)
            l_out.append(alpha * l_prev[:, cs] + jnp.sum(p, axis=0, keepdims=True))
            m_out.append(m_new)
            hs = slice(head * HEAD_DIM, (head + 1) * HEAD_DIM)
            qs = slice((s * LANES) % t, (s * LANES) % t + LANES)
            pv = _dot(vt[hs], p.astype(BF16))
            acc_out[head].append(alpha * acc_prev[hs, qs] + pv)
        m_scr[...] = jnp.concatenate(m_out, axis=1)
        l_scr[...] = jnp.concatenate(l_out, axis=1)
        acc_scr[...] = jnp.concatenate([jnp.concatenate(acc_out[0], axis=1),
                                        jnp.concatenate(acc_out[1], axis=1)], axis=0)

    def body(j, carry):
        step(j, False)
        return carry

    step(qi, True)

    qf = qt.astype(F32)
    qsq = qf * qf
    qnorm = jnp.sqrt(jnp.concatenate(
        [jnp.sum(qsq[:HEAD_DIM], axis=0, keepdims=True),
         jnp.sum(qsq[HEAD_DIM:], axis=0, keepdims=True)], axis=1)) * NORM_SLACK
    kmax = kmax_scr[...]
    kmax2 = jnp.concatenate([jnp.broadcast_to(kmax[:, 0:1], (1, t)),
                             jnp.broadcast_to(kmax[:, HEAD_DIM:HEAD_DIM + 1], (1, t))], axis=1)
    slack = qnorm * kmax2 - m_scr[...]
    cend = cend_ref[0][:, 0, :]
    lane = lax.broadcasted_iota(jnp.int32, cend.shape, 1)
    jrow = lax.broadcasted_iota(jnp.int32, (cend.shape[0], 1), 0)
    needed = jrow < 0
    for head in range(2):
        worst = jnp.max(slack[:, head * t:(head + 1) * t], axis=1, keepdims=True)
        c_head = jnp.sum(jnp.where(lane == 2 * pl.program_id(1) + head, cend, 0.0),
                         axis=1, keepdims=True)
        needed = needed | (worst - c_head * LOG2E > ZERO_PROB_EXP)
    first = jnp.min(jnp.where(needed & (jrow < qi), jrow, qi))
    lax.fori_loop(first, qi, body, 0)

    l = l_scr[...]
    acc = acc_scr[...]
    inv_n = 1.0 / HEAD_DIM
    halves = []
    for head in range(2):
        o = acc[head * HEAD_DIM:(head + 1) * HEAD_DIM] / l[:, head * t:(head + 1) * t]
        halves.append(o * lax.rsqrt(jnp.sum(o * o, axis=0, keepdims=True) * inv_n + RMS_EPS))
    o_t = jnp.concatenate(halves, axis=0)
    o_ref[0] = (jnp.transpose(o_t) * og_ref[...]).astype(o_ref.dtype)


def _fox_attn(qt5, k, cext, vt5, cend, out_g, tile):
    bsz, t_all, _ = k.shape
    pairs = N_HEADS // 2
    tiles = t_all // tile
    kern = functools.partial(_fox_attn_kernel, t=tile)
    return pl.pallas_call(
        kern,
        grid=(bsz, pairs, tiles),
        in_specs=[
            pl.BlockSpec((1, 1, 1, LANES, tile), lambda b, p, i: (b, p, i, 0, 0)),
            pl.BlockSpec((1, t_all, LANES), lambda b, p, i: (b, 0, p)),
            pl.BlockSpec((1, t_all, LANES), lambda b, p, i: (b, 0, p)),
            pl.BlockSpec((1, 1, tiles, LANES, tile), lambda b, p, i: (b, p, 0, 0, 0)),
            pl.BlockSpec((1, tiles, 1, LANES), lambda b, p, i: (b, 0, 0, 0)),
            pl.BlockSpec((1, LANES), lambda b, p, i: (0, p)),
        ],
        out_specs=pl.BlockSpec((1, tile, LANES), lambda b, p, i: (b, i, p)),
        out_shape=jax.ShapeDtypeStruct((bsz, t_all, GROUP_WIDTH), BF16),
        scratch_shapes=[
            pltpu.VMEM((1, 2 * tile), F32),
            pltpu.VMEM((1, 2 * tile), F32),
            pltpu.VMEM((LANES, tile), F32),
            pltpu.VMEM((1, LANES), F32),
        ],
        compiler_params=_cparams(("parallel", "parallel", "arbitrary")),
        name="fox_attn",
    )(qt5, k, cext, vt5, cend, out_g)


def _outproj_router_kernel(x_ref, yr_ref, yf_ref, wo_r_ref, wo_f_ref, g_ref, rwt_ref, rb_ref,
                           x1_ref, h_ref, idx_ref, gate_ref, rank_ref, count_ref):
    @pl.when(pl.program_id(0) == 0)
    def _():
        count_ref[...] = jnp.zeros_like(count_ref)

    x1 = x_ref[...] + _dot(yr_ref[...], wo_r_ref[...]) + _dot(yf_ref[...], wo_f_ref[...])
    x1_ref[...] = x1
    h = x1 * lax.rsqrt(jnp.mean(x1 * x1, axis=-1, keepdims=True) + RMS_EPS) * g_ref[...]
    h_ref[:, 0, :] = h
    logits = lax.dot_general(rwt_ref[...], h, (((1,), (1,)), ((), ())),
                             precision=lax.Precision.HIGHEST,
                             preferred_element_type=F32) + rb_ref[...]
    eidx = lax.broadcasted_iota(jnp.int32, logits.shape, 0)
    vals, idxs, picks = [], [], []
    for _ in range(TOP_K):
        m = jnp.max(logits, axis=0, keepdims=True)
        i = jnp.min(jnp.where(logits == m, eidx, N_EXPERTS), axis=0, keepdims=True)
        vals.append(m)
        idxs.append(i)
        picks.append(eidx == i)
        logits = jnp.where(picks[-1], -jnp.inf, logits)
    es = [jnp.exp(val - vals[0]) for val in vals]
    denom = es[0] + es[1] + es[2] + es[3]
    idx_ref[...] = jnp.concatenate(idxs, axis=0)
    gate_ref[...] = jnp.concatenate([e / denom for e in es], axis=0)

    tm = logits.shape[1]
    chosen = [jnp.where(pk, 1.0, 0.0) for pk in picks]
    any_k = chosen[0] + chosen[1] + chosen[2] + chosen[3]
    before = (lax.broadcasted_iota(jnp.int32, (tm, tm), 0)
              < lax.broadcasted_iota(jnp.int32, (tm, tm), 1))
    prefix = _dot(any_k.astype(BF16), jnp.where(before, 1.0, 0.0).astype(BF16))
    seen = count_ref[:, 0:1] + prefix
    rank_ref[...] = jnp.concatenate(
        [jnp.sum(ch * seen, axis=0, keepdims=True) for ch in chosen], axis=0).astype(jnp.int32)
    count_ref[...] = count_ref[...] + jnp.sum(any_k, axis=1, keepdims=True)


def _outproj_router(x2, yr, yf, wo_r, wo_f, g, rwt, rb, tm):
    n, d = x2.shape
    const = lambda i: (0, 0)
    row = lambda i: (i, 0)
    col = lambda i: (0, i)
    return pl.pallas_call(
        _outproj_router_kernel,
        grid=(n // tm,),
        in_specs=[
            pl.BlockSpec((tm, d), row),
            pl.BlockSpec((tm, GROUP_WIDTH), row),
            pl.BlockSpec((tm, GROUP_WIDTH), row),
            pl.BlockSpec((GROUP_WIDTH, d), const),
            pl.BlockSpec((GROUP_WIDTH, d), const),
            pl.BlockSpec((1, d), const),
            pl.BlockSpec((N_EXPERTS, d), const),
            pl.BlockSpec((N_EXPERTS, 1), const),
        ],
        out_specs=[
            pl.BlockSpec((tm, d), row),
            pl.BlockSpec((tm, 1, d), lambda i: (i, 0, 0)),
            pl.BlockSpec((TOP_K, tm), col),
            pl.BlockSpec((TOP_K, tm), col),
            pl.BlockSpec((TOP_K, tm), col),
            pl.BlockSpec((N_EXPERTS, LANES), const),
        ],
        out_shape=[
            jax.ShapeDtypeStruct((n, d), F32),
            jax.ShapeDtypeStruct((n, 1, d), F32),
            jax.ShapeDtypeStruct((TOP_K, n), jnp.int32),
            jax.ShapeDtypeStruct((TOP_K, n), F32),
            jax.ShapeDtypeStruct((TOP_K, n), jnp.int32),
            jax.ShapeDtypeStruct((N_EXPERTS, LANES), F32),
        ],
        compiler_params=_cparams(("arbitrary",)),
        name="outproj_router",
    )(x2, yr, yf, wo_r, wo_f, g, rwt, rb)


def _w1_split_kernel(w_ref, perm_ref, g_ref, l_ref):
    half = MXU_WIDTH // 2
    perm = perm_ref[...]
    for grp in range(w_ref.shape[2] // MXU_WIDTH):
        blk = w_ref[0, :, grp * MXU_WIDTH:(grp + 1) * MXU_WIDTH].astype(BF16)
        r = _dot(blk, perm)
        g_ref[0, :, grp * half:(grp + 1) * half] = r[:, :half].astype(BF16)
        l_ref[0, :, grp * half:(grp + 1) * half] = r[:, half:].astype(BF16)


def _w1_split(w1, tr):
    e, d, two_f = w1.shape
    half = MXU_WIDTH // 2
    src = jnp.arange(MXU_WIDTH, dtype=jnp.int32)[:, None]
    dst = jnp.arange(MXU_WIDTH, dtype=jnp.int32)[None, :]
    perm = (src == jnp.where(dst < half, 2 * dst, 2 * (dst - half) + 1)).astype(BF16)
    out_sds = jax.ShapeDtypeStruct((e, d, two_f // 2), BF16)
    return pl.pallas_call(
        _w1_split_kernel,
        grid=(e, d // tr),
        in_specs=[
            pl.BlockSpec((1, tr, two_f), lambda i, j: (i, j, 0)),
            pl.BlockSpec((MXU_WIDTH, MXU_WIDTH), lambda i, j: (0, 0)),
        ],
        out_specs=[pl.BlockSpec((1, tr, two_f // 2), lambda i, j: (i, j, 0))] * 2,
        out_shape=[out_sds, out_sds],
        compiler_params=_cparams(("parallel", "parallel")),
        name="w1_split",
    )(w1, perm)


def _expert_kernel(be_ref, tok_a_ref, tok_b_ref, tok_a_next_ref, dst_b_prev_ref, dst_a_ref,
                   dst_b_ref, h_hbm, w1g_a, w1l_a, b1g_a, b1l_a, w2_a, b2_a,
                   w1g_b, w1l_b, b1g_b, b1l_b, w2_b, b2_b,
                   y_hbm, xbuf_a, xbuf_b, obuf_a, obuf_b, xrows, gsem, osem, *, bm, n_real_rows):
    del be_ref
    i = pl.program_id(0)

    def gather_start(idx_ref, xbuf, sem, r, priority=0):
        pltpu.make_async_copy(h_hbm.at[idx_ref[0, 0, r]], xbuf.at[r], sem).start(priority=priority)

    def scatter_start(idx_ref, obuf, sem, r, priority=0):
        pltpu.make_async_copy(obuf.at[r], y_hbm.at[idx_ref[0, 0, r]], sem).start(priority=priority)

    def rows_wait(buf, sem):
        pltpu.make_async_copy(buf, buf, sem).wait()

    def mlp(xbuf, w1g_ref, w1l_ref, b1g_ref, b1l_ref, w2_ref, b2_ref):
        xrows[...] = xbuf[:, 0, :]
        xb = xrows[...].astype(BF16)
        dff = w1g_ref.shape[2]
        acts = []
        for piece in range(dff // MXU_WIDTH):
            cs = slice(piece * MXU_WIDTH, (piece + 1) * MXU_WIDTH)
            glu = _dot(xb, w1g_ref[0, :, cs]) + b1g_ref[0, :, cs]
            lin = _dot(xb, w1l_ref[0, :, cs]) + b1l_ref[0, :, cs]
            glu = jnp.minimum(glu, SWIGLU_LIMIT)
            lin = jnp.clip(lin, -SWIGLU_LIMIT, SWIGLU_LIMIT)
            acts.append((glu * _sigmoid(SWIGLU_ALPHA * glu) * (lin + 1.0)).astype(BF16))
        return _dot(jnp.concatenate(acts, axis=1), w2_ref[0]) + b2_ref[0]

    @pl.when(i == 0)
    def _():
        obuf_a[...] = jnp.zeros_like(obuf_a)
        obuf_b[...] = jnp.zeros_like(obuf_b)

        def first(r, carry):
            pltpu.make_async_copy(obuf_a.at[r], y_hbm.at[n_real_rows + 2 * bm + r],
                                  osem.at[0]).start()
            gather_start(tok_a_ref, xbuf_a, gsem.at[0], r)
            return carry

        lax.fori_loop(0, bm, first, 0)

    for r in range(bm):
        gather_start(tok_b_ref, xbuf_b, gsem.at[1], r, r % 2)
        scatter_start(dst_b_prev_ref, obuf_b, osem.at[1], r, r % 2)
    rows_wait(xbuf_a, gsem.at[0])
    rows_wait(obuf_a, osem.at[0])
    obuf_a[:, 0, :] = mlp(xbuf_a, w1g_a, w1l_a, b1g_a, b1l_a, w2_a, b2_a)

    for r in range(bm):
        gather_start(tok_a_next_ref, xbuf_a, gsem.at[0], r, r % 2)
        scatter_start(dst_a_ref, obuf_a, osem.at[0], r, r % 2)
    rows_wait(xbuf_b, gsem.at[1])
    rows_wait(obuf_b, osem.at[1])
    obuf_b[:, 0, :] = mlp(xbuf_b, w1g_b, w1l_b, b1g_b, b1l_b, w2_b, b2_b)

    @pl.when(i == pl.num_programs(0) - 1)
    def _():
        def last(r, carry):
            scatter_start(dst_b_ref, obuf_b, osem.at[1], r)
            return carry

        lax.fori_loop(0, bm, last, 0)
        rows_wait(obuf_b, osem.at[1])
        rows_wait(obuf_a, osem.at[0])
        rows_wait(xbuf_a, gsem.at[0])


def _expert_mlp(block_e, tok_blocks, dst_blocks, h2, w1g, w1l, b1g, b1l, w2, b2, bm):
    n_blocks = tok_blocks.shape[0]
    assert n_blocks % 2 == 0
    n, _, d = h2.shape
    dff = w1g.shape[2]
    n_real_rows = TOP_K * n
    idx_spec = lambda fn: pl.BlockSpec((1, 1, bm), fn, memory_space=pltpu.SMEM)

    def weight_specs(which):
        wmap = lambda i, be: (be[2 * i + which], 0, 0)
        return [
            pl.BlockSpec((1, d, dff), wmap),
            pl.BlockSpec((1, d, dff), wmap),
            pl.BlockSpec((1, 1, dff), wmap),
            pl.BlockSpec((1, 1, dff), wmap),
            pl.BlockSpec((1, dff, d), wmap),
            pl.BlockSpec((1, 1, d), wmap),
        ]

    grid_spec = pltpu.PrefetchScalarGridSpec(
        num_scalar_prefetch=1,
        grid=(n_blocks // 2,),
        in_specs=[
            idx_spec(lambda i, be: (2 * i, 0, 0)),
            idx_spec(lambda i, be: (2 * i + 1, 0, 0)),
            idx_spec(lambda i, be: (jnp.minimum(2 * i + 2, n_blocks - 1), 0, 0)),
            idx_spec(lambda i, be: (2 * i, 0, 0)),
            idx_spec(lambda i, be: (2 * i + 1, 0, 0)),
            idx_spec(lambda i, be: (2 * i + 2, 0, 0)),
            pl.BlockSpec(memory_space=pl.ANY),
        ] + weight_specs(0) + weight_specs(1),
        out_specs=pl.BlockSpec(memory_space=pl.ANY),
        scratch_shapes=[
            pltpu.VMEM((bm, 1, d), F32),
            pltpu.VMEM((bm, 1, d), F32),
            pltpu.VMEM((bm, 1, d), F32),
            pltpu.VMEM((bm, 1, d), F32),
            pltpu.VMEM((bm, d), F32),
            pltpu.SemaphoreType.DMA((2,)),
            pltpu.SemaphoreType.DMA((2,)),
        ],
    )
    weights = (w1g, w1l, b1g, b1l, w2, b2)
    return pl.pallas_call(
        functools.partial(_expert_kernel, bm=bm, n_real_rows=n_real_rows),
        grid_spec=grid_spec,
        out_shape=jax.ShapeDtypeStruct((n_real_rows + 3 * bm, 1, d), F32),
        compiler_params=_cparams(("arbitrary",)),
        name="expert_mlp",
    )(block_e, tok_blocks, tok_blocks, tok_blocks, dst_blocks, dst_blocks, dst_blocks, h2,
      *weights, *weights)


def _combine_kernel(gate_ref, x1_ref, g_ref, y0_ref, y1_ref, y2_ref, y3_ref, o_ref):
    gates = gate_ref[...]
    y = x1_ref[...]
    for kk, y_ref in enumerate((y0_ref, y1_ref, y2_ref, y3_ref)):
        y = y + y_ref[:, 0, :] * gates[:, kk:kk + 1]
    o_ref[...] = y * lax.rsqrt(jnp.mean(y * y, axis=-1, keepdims=True) + RMS_EPS) * g_ref[...]


def _combine(gates_t, x1, g, y_all, tc):
    n, d = x1.shape
    tiles = n // tc
    y_spec = lambda kk: pl.BlockSpec((tc, 1, d), lambda i: (kk * tiles + i, 0, 0))
    return pl.pallas_call(
        _combine_kernel,
        grid=(tiles,),
        in_specs=[
            pl.BlockSpec((tc, TOP_K), lambda i: (i, 0)),
            pl.BlockSpec((tc, d), lambda i: (i, 0)),
            pl.BlockSpec((1, d), lambda i: (0, 0)),
        ] + [y_spec(kk) for kk in range(TOP_K)],
        out_specs=pl.BlockSpec((tc, d), lambda i: (i, 0)),
        out_shape=jax.ShapeDtypeStruct((n, d), F32),
        compiler_params=_cparams(("parallel",)),
        name="combine",
    )(gates_t, x1, g, y_all, y_all, y_all, y_all)


def _slot_sources_kernel(pos_ref, out_ref, *, chunk):
    i = pl.program_id(0)

    @pl.when(i == 0)
    def _():
        def init(p, carry):
            out_ref[p] = -1
            return carry

        lax.fori_loop(0, out_ref.shape[0], init, 0, unroll=32)

    base = i * chunk

    def place(s, carry):
        out_ref[pos_ref[0, 0, s]] = base + s
        return carry

    lax.fori_loop(0, chunk, place, 0, unroll=16)


def _slot_sources(pos, n_pad):
    n_slots = pos.shape[0]
    chunk = _pick(n_slots, 8192)
    return pl.pallas_call(
        functools.partial(_slot_sources_kernel, chunk=chunk),
        grid=(n_slots // chunk,),
        in_specs=[pl.BlockSpec((1, 1, chunk), lambda i: (i, 0, 0), memory_space=pltpu.SMEM)],
        out_specs=pl.BlockSpec(memory_space=pltpu.SMEM),
        out_shape=jax.ShapeDtypeStruct((n_pad,), jnp.int32),
        compiler_params=_cparams(("arbitrary",)),
        name="slot_sources",
    )(pos.reshape(n_slots // chunk, 1, chunk))


def _dispatch_plan(idx, rank, counts, bm):
    n = idx.shape[1]
    n_slots = TOP_K * n
    sizes = counts[:, 0].astype(jnp.int32)
    padded = (sizes + bm - 1) // bm * bm
    pad_ends = jnp.cumsum(padded)
    pad_starts = pad_ends - padded
    experts = jnp.arange(N_EXPERTS, dtype=jnp.int32)
    start_of = jnp.sum(jnp.where(idx[..., None] == experts, pad_starts, 0), axis=-1)
    pos = (start_of + rank).reshape(-1)
    n_pad = n_slots + N_EXPERTS * bm
    n_blocks = n_pad // bm
    slot_src = _slot_sources(pos, n_pad)
    p = jnp.arange(n_pad, dtype=jnp.int32)
    spare = n_slots + (p // bm) % 2 * bm + p % bm
    real = slot_src >= 0
    tok_blocks = jnp.where(real, slot_src % n, 0).reshape(n_blocks, 1, bm)
    dst = jnp.where(real, slot_src, spare)
    dst_blocks = jnp.concatenate([spare[bm:2 * bm], dst]).reshape(n_blocks + 1, 1, bm)
    block_start = jnp.arange(n_blocks, dtype=jnp.int32) * bm
    block_e = jnp.minimum(jnp.sum(pad_ends[None, :] <= block_start[:, None], axis=1),
                          N_EXPERTS - 1).astype(jnp.int32)
    return tok_blocks, dst_blocks, block_e


def _block_diag_ones():
    hid = jnp.arange(MXU_WIDTH, dtype=jnp.int32) // HEAD_DIM
    return (hid[:, None] == hid[None, :]).astype(BF16)


def _pick(n, pref):
    return pref if n % pref == 0 else n


def kernel(x, attn_norm_g, w_in, rwkv_mu, rwkv_w0, rwkv_w_up, rwkv_a0, rwkv_a_up, rwkv_g_up,
           rwkv_k_k, rwkv_k_a, rwkv_r_k, rwkv_ln_w, rwkv_ln_b, fox_f_bias, fox_out_g, w_out,
           ffn_norm_g, router_w, router_b, expert_w1, expert_b1, expert_w2, expert_b2,
           final_norm_g):
    bsz, t, d = x.shape
    n = bsz * t
    depth = w_in.shape[0]
    assert depth == 1, "the final norm is fused into the last stage of a single layer"
    bd = _block_diag_ones()
    x2 = x.reshape(n, d)
    for l in range(depth):
        w_l = w_in[l]
        w_r = w_l[:, :RWKV_IN].astype(BF16)
        w_qkv = w_l[:, RWKV_IN:RWKV_IN + 3 * GROUP_WIDTH].astype(BF16)
        w_qt = w_qkv[:, :GROUP_WIDTH].T
        w_k = w_qkv[:, GROUP_WIDTH:2 * GROUP_WIDTH]
        w_vt = w_qkv[:, 2 * GROUP_WIDTH:].T
        w_f = jnp.pad(w_l[:, RWKV_IN + 3 * GROUP_WIDTH:], ((0, 0), (0, LANES - N_HEADS))).astype(BF16)
        fb_pad = jnp.pad(fox_f_bias[l], (0, LANES - N_HEADS)).reshape(1, LANES)
        wup_pad = jnp.pad(rwkv_w_up[l], ((0, LANES - DECAY_LORA), (0, 0))).astype(BF16)
        aup_pad = jnp.pad(rwkv_a_up[l], ((DECAY_LORA, 0), (0, 0))).astype(BF16)
        gup = rwkv_g_up[l].astype(BF16)
        vec = lambda a: a.reshape(1, -1)

        u_r, qt5, k, vt5, fl = _inproj(x2, vec(attn_norm_g[l]), w_r, w_qt, w_k, w_vt, w_f,
                                       bsz, ATTN_TILE)
        cext, cend = _fox_gate(fl.reshape(bsz, t, LANES), fb_pad, _gate_piece_selectors(),
                               ATTN_TILE)
        y_rwkv = _rwkv_mix(
            u_r.reshape(bsz, t, RWKV_IN), vec(rwkv_mu[l]), vec(rwkv_w0[l]), wup_pad,
            vec(rwkv_a0[l]), aup_pad, gup, vec(rwkv_k_k[l]), vec(rwkv_k_a[l]), vec(rwkv_r_k[l]),
            vec(rwkv_ln_w[l]), vec(rwkv_ln_b[l]), bd, RWKV_CHUNKS_PER_STEP)
        y_fox = _fox_attn(qt5, k.reshape(bsz, t, GROUP_WIDTH), cext, vt5, cend,
                          vec(fox_out_g[l]), ATTN_TILE)

        wo = w_out[l].astype(BF16)
        x1, h2, idx, gates, rank, counts = _outproj_router(
            x2, y_rwkv.reshape(n, GROUP_WIDTH), y_fox.reshape(n, GROUP_WIDTH),
            wo[:GROUP_WIDTH], wo[GROUP_WIDTH:], vec(ffn_norm_g[l]),
            router_w[l].T, router_b[l].reshape(N_EXPERTS, 1), _pick(n, 512))

        bm = 256
        tok_blocks, dst_blocks, block_e = _dispatch_plan(idx, rank, counts, bm)
        w1g, w1l = _w1_split(expert_w1[l], 512)
        b1 = expert_b1[l]
        b1g = b1[:, None, 0::2]
        b1l = b1[:, None, 1::2]
        y_all = _expert_mlp(block_e, tok_blocks, dst_blocks, h2, w1g, w1l, b1g, b1l,
                            expert_w2[l].astype(BF16), expert_b2[l][:, None, :], bm)
        x2 = _combine(gates.T, x1, vec(final_norm_g), y_all, _pick(n, 256))
    return x2.reshape(bsz, t, d)
```

```python
import functools

import jax
import jax.numpy as jnp
from jax import lax
from jax.experimental import pallas as pl
from jax.experimental.pallas import tpu as pltpu

F32 = jnp.float32
BF16 = jnp.bfloat16

HEAD_DIM = 64
N_HEADS = 8
GROUP_WIDTH = N_HEADS * HEAD_DIM
DECAY_LORA = 64
AAA_LORA = 64
GATE_LORA = 128
RWKV_IN = 3 * GROUP_WIDTH + DECAY_LORA + AAA_LORA + GATE_LORA
LORA_OFF = 3 * GROUP_WIDTH
N_EXPERTS = 32
TOP_K = 4
SWIGLU_ALPHA = 1.702
SWIGLU_LIMIT = 7.0
RMS_EPS = 1e-5
RWKV_GN_EPS = 64e-5
LANES = 128
SUBLANES = 8
RWKV_CHUNK = 64
RWKV_CHUNKS_PER_STEP = 4
ATTN_TILE = 512
MXU_WIDTH = 256
LOG2E = 1.4426950408889634
Q_SCALE = HEAD_DIM ** -0.5 * LOG2E
ZERO_PROB_EXP = -152.0
NORM_SLACK = 1.0 + 2.0 ** -6
VMEM_LIMIT = 56 * 1024 * 1024


def _cparams(semantics):
    return pltpu.CompilerParams(dimension_semantics=semantics, vmem_limit_bytes=VMEM_LIMIT)


def _dot(a, b):
    return jnp.dot(a, b, preferred_element_type=F32)


def _dot_nt(a, b):
    return lax.dot_general(a, b, (((1,), (1,)), ((), ())), preferred_element_type=F32)


def _dot_tn(a, b):
    return lax.dot_general(a, b, (((0,), (0,)), ((), ())), preferred_element_type=F32)


def _split3(x):
    hi = x.astype(BF16)
    r1 = x - hi.astype(F32)
    mid = r1.astype(BF16)
    lo = (r1 - mid.astype(F32)).astype(BF16)
    return hi, mid, lo


def _dot_exact_lhs(a_bf16, x):
    hi, mid, lo = _split3(x)
    return _dot(a_bf16, hi) + _dot(a_bf16, mid) + _dot(a_bf16, lo)


def _head_sums(x, same_head):
    w = same_head.shape[0]
    parts = []
    for g in range(x.shape[1] // w):
        xs = x[:, g * w:(g + 1) * w]
        hi = xs.astype(BF16)
        lo = (xs - hi.astype(F32)).astype(BF16)
        parts.append(_dot(hi, same_head) + _dot(lo, same_head))
    return jnp.concatenate(parts, axis=1)


def _softplus(z):
    return jnp.maximum(z, 0.0) + jnp.log1p(jnp.exp(-jnp.abs(z)))


def _sigmoid(z):
    return 1.0 / (1.0 + jnp.exp(-z))


def _inproj_kernel(x_ref, g_ref, wr_ref, wqt_ref, wk_ref, wvt_ref, wf_ref,
                   ur_ref, qt_ref, k_ref, vt_ref, fl_ref):
    x = x_ref[...]
    h = x * lax.rsqrt(jnp.mean(x * x, axis=-1, keepdims=True) + RMS_EPS) * g_ref[...]
    hb = h.astype(BF16)
    ur_ref[...] = _dot(hb, wr_ref[...])
    k_ref[...] = _dot(hb, wk_ref[...]).astype(BF16)
    fl_ref[...] = _dot(hb, wf_ref[...])
    qt = (_dot_nt(wqt_ref[...], hb) * Q_SCALE).astype(BF16)
    vt = _dot_nt(wvt_ref[...], hb).astype(BF16)
    for p in range(N_HEADS // 2):
        qt_ref[0, p, 0] = qt[p * LANES:(p + 1) * LANES]
        vt_ref[0, p, 0] = vt[p * LANES:(p + 1) * LANES]


def _inproj(x2, g, w_r, w_qt, w_k, w_vt, w_f, bsz, tm):
    n, d = x2.shape
    nt = n // bsz // tm
    pairs = N_HEADS // 2
    const = lambda i: (0, 0)
    row = lambda i: (i, 0)
    fm = lambda i: (i // nt, 0, i % nt, 0, 0)
    fm_sds = jax.ShapeDtypeStruct((bsz, pairs, nt, LANES, tm), BF16)
    return pl.pallas_call(
        _inproj_kernel,
        grid=(n // tm,),
        in_specs=[
            pl.BlockSpec((tm, d), row),
            pl.BlockSpec((1, d), const),
            pl.BlockSpec(w_r.shape, const),
            pl.BlockSpec(w_qt.shape, const),
            pl.BlockSpec(w_k.shape, const),
            pl.BlockSpec(w_vt.shape, const),
            pl.BlockSpec(w_f.shape, const),
        ],
        out_specs=[
            pl.BlockSpec((tm, RWKV_IN), row),
            pl.BlockSpec((1, pairs, 1, LANES, tm), fm),
            pl.BlockSpec((tm, GROUP_WIDTH), row),
            pl.BlockSpec((1, pairs, 1, LANES, tm), fm),
            pl.BlockSpec((tm, LANES), row),
        ],
        out_shape=[
            jax.ShapeDtypeStruct((n, RWKV_IN), F32),
            fm_sds,
            jax.ShapeDtypeStruct((n, GROUP_WIDTH), BF16),
            fm_sds,
            jax.ShapeDtypeStruct((n, LANES), F32),
        ],
        compiler_params=_cparams(("parallel",)),
        name="inproj",
    )(x2, g, w_r, w_qt, w_k, w_vt, w_f)


def _fox_gate_kernel(fl_ref, fb_ref, sel_ref, c_ref, cend_ref, carry):
    tt = fl_ref.shape[1]

    @pl.when(pl.program_id(1) == 0)
    def _():
        carry[...] = jnp.zeros_like(carry)

    z = fl_ref[0] + fb_ref[...]
    log_f = jnp.minimum(z, 0.0) - jnp.log1p(jnp.exp(-jnp.abs(z)))
    ri = lax.broadcasted_iota(jnp.int32, (tt, tt), 0)
    ci = lax.broadcasted_iota(jnp.int32, (tt, tt), 1)
    tri = jnp.where(ri >= ci, 1.0, 0.0).astype(BF16)
    c = _dot_exact_lhs(tri, log_f) + carry[...]
    carry[...] = c[tt - 1:tt, :]
    cend_ref[0, 0] = c[tt - 1:tt, :]
    hi, mid, lo = _split3(c * LOG2E)
    c_ref[0] = (_dot(hi, sel_ref[0]) + _dot(mid, sel_ref[1]) + _dot(lo, sel_ref[2])).astype(BF16)


def _gate_piece_selectors():
    h = jnp.arange(LANES, dtype=jnp.int32)[:, None]
    col = jnp.arange(GROUP_WIDTH, dtype=jnp.int32)[None, :]
    sels = []
    for m in range(3):
        target = LANES * (h // 2) + 3 * (h % 2) + m
        sels.append(((col == target) & (h < N_HEADS)).astype(BF16))
    return jnp.stack(sels)


def _fox_gate(fl3, fb_pad, sel, tt):
    b, t, _ = fl3.shape
    return pl.pallas_call(
        _fox_gate_kernel,
        grid=(b, t // tt),
        in_specs=[
            pl.BlockSpec((1, tt, LANES), lambda i, j: (i, j, 0)),
            pl.BlockSpec((1, LANES), lambda i, j: (0, 0)),
            pl.BlockSpec((3, LANES, GROUP_WIDTH), lambda i, j: (0, 0, 0)),
        ],
        out_specs=[
            pl.BlockSpec((1, tt, GROUP_WIDTH), lambda i, j: (i, j, 0)),
            pl.BlockSpec((1, 1, 1, LANES), lambda i, j: (i, j, 0, 0)),
        ],
        out_shape=[
            jax.ShapeDtypeStruct((b, t, GROUP_WIDTH), BF16),
            jax.ShapeDtypeStruct((b, t // tt, 1, LANES), F32),
        ],
        scratch_shapes=[pltpu.VMEM((1, LANES), F32)],
        compiler_params=_cparams(("parallel", "arbitrary")),
        name="fox_gate",
    )(fl3, fb_pad, sel)


def _rwkv_prep_body(u_ref, mu_ref, w0_ref, wup_ref, a0_ref, aup_ref, gup_ref, kk_ref, ka_ref,
                      bd_ref, r_out, k_out, v_out, lw_out, kk_out, b_out, g_out, carry):
    tt = u_ref.shape[1]

    @pl.when(pl.program_id(1) == 0)
    def _():
        carry[...] = jnp.zeros_like(carry)

    u = u_ref[0]
    prev = pltpu.roll(u, 1, axis=0)
    row = lax.broadcasted_iota(jnp.int32, u.shape, 0)
    prev = jnp.where(row == 0, carry[...], prev)
    carry[...] = u[tt - 1:tt, :]
    us = u + (prev - u) * mu_ref[...]

    r = us[:, :GROUP_WIDTH]
    k = us[:, GROUP_WIDTH:2 * GROUP_WIDTH]
    v = us[:, 2 * GROUP_WIDTH:LORA_OFF]
    wa = us[:, LORA_OFF:LORA_OFF + LANES]
    gl = us[:, LORA_OFF + LANES:]

    w_lin = _dot(jnp.tanh(wa).astype(BF16), wup_ref[...])
    a_lin = _dot(wa.astype(BF16), aup_ref[...])
    w = -_softplus(-(w0_ref[...] + w_lin)) - 0.5
    lw_out[0] = -jnp.exp(w)
    a = _sigmoid(a0_ref[...] + a_lin)
    g_out[0] = _dot(_sigmoid(gl).astype(BF16), gup_ref[...])

    kkr = k * kk_ref[...]
    ss = _head_sums(kkr * kkr, bd_ref[...])
    kk = kkr / jnp.maximum(jnp.sqrt(ss), 1e-12)
    r_out[0] = r
    k_out[0] = k * (1.0 + (a - 1.0) * ka_ref[...])
    v_out[0] = v
    kk_out[0] = kk
    b_out[0] = kk * a


def _rwkv_scan_body(r_ref, k_ref, v_ref, lw_ref, kk_ref, b_ref, g_ref, rk_ref, lnw_ref, lnb_ref,
                      bd_ref, o_ref, s_scr, *, n_chunks):
    c = RWKV_CHUNK
    w = MXU_WIDTH
    hpg = w // HEAD_DIM
    n_groups = GROUP_WIDTH // w

    @pl.when(pl.program_id(1) == 0)
    def _():
        s_scr[...] = jnp.zeros_like(s_scr)

    row = lax.broadcasted_iota(jnp.int32, (c, w), 0)
    u = lax.broadcasted_iota(jnp.int32, (c, w), 1) % HEAD_DIM
    strict = row > u
    incl = row >= u
    eye = jnp.where(row == u, 1.0, 0.0)
    level_masks = []
    s = 1
    while s < c:
        same = (row // (2 * s)) == (u // (2 * s))
        level_masks.append(same & ((row % (2 * s)) >= s) & ((u % (2 * s)) < s))
        s *= 2
    same_head = (lax.broadcasted_iota(jnp.int32, (w, w), 0) // HEAD_DIM
                 == lax.broadcasted_iota(jnp.int32, (w, w), 1) // HEAD_DIM)
    tri = jnp.where(lax.broadcasted_iota(jnp.int32, (c, c), 0)
                    >= lax.broadcasted_iota(jnp.int32, (c, c), 1), 1.0, 0.0).astype(BF16)

    def bdiag(x):
        xb = x.astype(BF16)
        tiled = jnp.concatenate([xb] * hpg, axis=0)
        return jnp.where(same_head, tiled, jnp.zeros_like(tiled))

    chains = [(ci, gi) for ci in range(n_chunks) for gi in range(n_groups)]
    lhs, rk_t, vbs, xcat, xneg, gam, rt32 = {}, {}, {}, {}, {}, {}, {}
    for ci in range(n_chunks):
        rs = slice(ci * c, (ci + 1) * c)
        r = r_ref[0, rs, :]
        k = k_ref[0, rs, :]
        lw = lw_ref[0, rs, :]
        kk = kk_ref[0, rs, :]
        b = b_ref[0, rs, :]
        g_cum = _dot_exact_lhs(tri, lw)
        g_last = g_cum[c - 1:c, :]
        r_t = r * jnp.exp(g_cum)
        kk_t = kk * jnp.exp(g_cum - lw)
        e_neg = jnp.exp(-g_cum)
        b_n = b * e_neg
        k_n = k * e_neg
        e_end = jnp.exp(g_last - g_cum)
        b_e = (b * e_end).astype(BF16)
        k_e = (k * e_end).astype(BF16)
        gamma = jnp.exp(g_last)
        vb = v_ref[0, rs, :].astype(BF16)
        for gi in range(n_groups):
            gs = slice(gi * w, (gi + 1) * w)
            ch = (ci, gi)
            lhs[ch] = jnp.concatenate([r_t[:, gs].astype(BF16), kk_t[:, gs].astype(BF16)], axis=0)
            rk_t[ch] = (jnp.concatenate([bdiag(b_n[:, gs]), bdiag(k_n[:, gs])], axis=0),
                        bdiag(kk_t[:, gs]))
            vbs[ch] = vb[:, gs]
            xcat[ch] = jnp.concatenate([b_e[:, gs], k_e[:, gs]], axis=0)
            xneg[ch] = jnp.concatenate([-b_e[:, gs], k_e[:, gs]], axis=0)
            gam[ch] = gamma[:, gs]
            rt32[ch] = r_t[:, gs]

    p = {ch: _dot_nt(lhs[ch], rk_t[ch][0]) for ch in chains}
    l_b = {ch: jnp.where(strict, p[ch][c:, :w], 0.0) for ch in chains}
    l_k = {ch: jnp.where(strict, p[ch][c:, w:], 0.0).astype(BF16) for ch in chains}
    p_br = {ch: jnp.where(incl, p[ch][:c, :w], 0.0).astype(BF16) for ch in chains}
    p_kr = {ch: jnp.where(incl, p[ch][:c, w:], 0.0).astype(BF16) for ch in chains}
    v_bd = {ch: bdiag(vbs[ch]) for ch in chains}
    lkv = {ch: _dot(l_k[ch], v_bd[ch]) for ch in chains}

    t_inv = {ch: eye - jnp.where(level_masks[0], l_b[ch], 0.0) for ch in chains}
    for m in level_masks[1:]:
        tb = {ch: t_inv[ch].astype(BF16) for ch in chains}
        ct = {ch: _dot(jnp.where(m, l_b[ch], 0.0).astype(BF16), bdiag(tb[ch])) for ch in chains}
        t_inv = {ch: t_inv[ch] - _dot(tb[ch], bdiag(ct[ch])) for ch in chains}

    mm = {ch: _dot(t_inv[ch].astype(BF16),
                   jnp.concatenate([rk_t[ch][1], bdiag(lkv[ch])], axis=1)) for ch in chains}
    pm = {ch: _dot(p_br[ch], jnp.concatenate([bdiag(mm[ch][:, :w]), bdiag(mm[ch][:, w:])], axis=1))
          for ch in chains}
    pkv = {ch: _dot(p_kr[ch], v_bd[ch]) for ch in chains}
    n1 = {ch: (rt32[ch] - pm[ch][:, :w]).astype(BF16) for ch in chains}
    n2 = {ch: pkv[ch] - pm[ch][:, w:] for ch in chains}
    omega = {ch: jnp.where(same_head, _dot_tn(mm[ch][:, :w].astype(BF16), xcat[ch][:c]), 0.0)
             .astype(BF16) for ch in chains}
    psi = {ch: jnp.where(same_head, _dot_tn(
        jnp.concatenate([mm[ch][:, w:].astype(BF16), vbs[ch]], axis=0), xneg[ch]), 0.0)
           for ch in chains}

    state = [s_scr[gi] for gi in range(n_groups)]
    ys = []
    for ci in range(n_chunks):
        sb = [state[gi].astype(BF16) for gi in range(n_groups)]
        ys.append(jnp.concatenate(
            [_dot_nt(n1[(ci, gi)], sb[gi]) + n2[(ci, gi)] for gi in range(n_groups)], axis=1))
        state = [state[gi] * gam[(ci, gi)] - _dot(sb[gi], omega[(ci, gi)]) + psi[(ci, gi)]
                 for gi in range(n_groups)]
    for gi in range(n_groups):
        s_scr[gi] = state[gi]

    y = jnp.concatenate(ys, axis=0)
    r = r_ref[0]
    k = k_ref[0]
    v = v_ref[0]
    bd = bd_ref[...]
    inv_n = 1.0 / HEAD_DIM
    mean = _head_sums(y, bd) * inv_n
    d = y - mean
    var = _head_sums(d * d, bd) * inv_n
    yn = d * lax.rsqrt(var + RWKV_GN_EPS) * lnw_ref[...] + lnb_ref[...]
    bonus = _head_sums(r * k * rk_ref[...], bd) * v
    o_ref[0] = ((yn + bonus) * g_ref[0]).astype(o_ref.dtype)


def _rwkv_kernel(u_ref, mu_ref, w0_ref, wup_ref, a0_ref, aup_ref, gup_ref, kkw_ref, ka_ref,
                 rk_ref, lnw_ref, lnb_ref, bd_ref, o_ref, carry, s_scr,
                 r_s, k_s, v_s, lw_s, kk_s, b_s, g_s, *, n_chunks):
    _rwkv_prep_body(u_ref, mu_ref, w0_ref, wup_ref, a0_ref, aup_ref, gup_ref, kkw_ref, ka_ref,
                    bd_ref, r_s, k_s, v_s, lw_s, kk_s, b_s, g_s, carry)
    _rwkv_scan_body(r_s, k_s, v_s, lw_s, kk_s, b_s, g_s, rk_ref, lnw_ref, lnb_ref, bd_ref,
                    o_ref, s_scr, n_chunks=n_chunks)


def _rwkv_mix(u3, mu, w0, wup_pad, a0, aup_pad, gup, k_k, k_a, r_k, ln_w, ln_b, bd, n_chunks):
    bsz, t, _ = u3.shape
    rows = RWKV_CHUNK * n_chunks
    const = lambda i, j: (0, 0)
    tile = lambda i, j: (i, j, 0)
    vec = pl.BlockSpec((1, GROUP_WIDTH), const)
    staged = pltpu.VMEM((1, rows, GROUP_WIDTH), F32)
    return pl.pallas_call(
        functools.partial(_rwkv_kernel, n_chunks=n_chunks),
        grid=(bsz, t // rows),
        in_specs=[
            pl.BlockSpec((1, rows, RWKV_IN), tile),
            pl.BlockSpec((1, RWKV_IN), const),
            vec,
            pl.BlockSpec((LANES, GROUP_WIDTH), const),
            vec,
            pl.BlockSpec((LANES, GROUP_WIDTH), const),
            pl.BlockSpec((GATE_LORA, GROUP_WIDTH), const),
            vec, vec, vec, vec, vec,
            pl.BlockSpec((MXU_WIDTH, MXU_WIDTH), const),
        ],
        out_specs=pl.BlockSpec((1, rows, GROUP_WIDTH), tile),
        out_shape=jax.ShapeDtypeStruct((bsz, t, GROUP_WIDTH), BF16),
        scratch_shapes=[
            pltpu.VMEM((1, RWKV_IN), F32),
            pltpu.VMEM((GROUP_WIDTH // MXU_WIDTH, MXU_WIDTH, MXU_WIDTH), F32),
        ] + [staged] * 7,
        compiler_params=_cparams(("parallel", "arbitrary")),
        name="rwkv_mix",
    )(u3, mu, w0, wup_pad, a0, aup_pad, gup, k_k, k_a, r_k, ln_w, ln_b, bd)


def _fox_attn_kernel(qt_ref, k_ref, ce_ref, vt_ref, cend_ref, og_ref, o_ref,
                     m_scr, l_scr, acc_scr, kmax_scr, *, t):
    qi = pl.program_id(2)
    n_strips = 2 * t // LANES
    qt = qt_ref[0, 0, 0]
    frow = lax.broadcasted_iota(jnp.int32, (LANES, t), 0)
    zero = jnp.zeros_like(qt)
    main = jnp.concatenate([jnp.where(frow < HEAD_DIM, qt, zero),
                            jnp.where(frow < HEAD_DIM, zero, qt)], axis=1)
    erow = lax.broadcasted_iota(jnp.int32, (LANES, 2 * t), 0)
    ecol = lax.broadcasted_iota(jnp.int32, (LANES, 2 * t), 1)
    off = jnp.where(ecol < t, 0, 3)
    extra = jnp.where((erow >= off) & (erow < off + 3), -1.0, 0.0).astype(BF16)
    q_aug = jnp.concatenate([main, extra], axis=0)

    m_scr[...] = jnp.full_like(m_scr, -jnp.inf)
    l_scr[...] = jnp.zeros_like(l_scr)
    acc_scr[...] = jnp.zeros_like(acc_scr)

    @pl.when(qi == 0)
    def _():
        hid_r = lax.broadcasted_iota(jnp.int32, (LANES, LANES), 0) // HEAD_DIM
        hid_c = lax.broadcasted_iota(jnp.int32, (LANES, LANES), 1) // HEAD_DIM
        same_head = jnp.where(hid_r == hid_c, 1.0, 0.0).astype(BF16)

        def tile_max(j, best):
            kf = k_ref[0, pl.ds(pl.multiple_of(j * t, t), t), :].astype(F32)
            sq = _dot((kf * kf).astype(BF16), same_head)
            return jnp.maximum(best, jnp.max(sq, axis=0, keepdims=True))

        best = lax.fori_loop(0, k_ref.shape[1] // t, tile_max, jnp.zeros((1, LANES), F32))
        kmax_scr[...] = jnp.sqrt(best) * NORM_SLACK

    def step(j, masked):
        start = pl.multiple_of(j * t, t)
        k_aug = jnp.concatenate([k_ref[0, pl.ds(start, t), :], ce_ref[0, pl.ds(start, t), :]],
                                axis=1)
        vt = vt_ref[0, 0, j]
        zt = _dot(k_aug, q_aug)
        m_prev = m_scr[...]
        l_prev = l_scr[...]
        acc_prev = acc_scr[...]
        m_out, l_out, acc_out = [], [], [[], []]
        for s in range(n_strips):
            head = s // (n_strips // 2)
            cs = slice(s * LANES, (s + 1) * LANES)
            z = zt[:, cs]
            if masked:
                key = lax.broadcasted_iota(jnp.int32, (t, LANES), 0)
                qry = lax.broadcasted_iota(jnp.int32, (t, LANES), 1) + (s * LANES) % t
                z = jnp.where(key <= qry, z, -jnp.inf)
            m_new = jnp.maximum(m_prev[:, cs], jnp.max(z, axis=0, keepdims=True))
            alpha = jnp.exp2(m_prev[:, cs] - m_new)
            p = jnp.exp2(z - m_new)
            l_out.append(alpha * l_prev[:, cs] + jnp.sum(p, axis=0, keepdims=True))
            m_out.append(m_new)
            hs = slice(head * HEAD_DIM, (head + 1) * HEAD_DIM)
            qs = slice((s * LANES) % t, (s * LANES) % t + LANES)
            pv = _dot(vt[hs], p.astype(BF16))
            acc_out[head].append(alpha * acc_prev[hs, qs] + pv)
        m_scr[...] = jnp.concatenate(m_out, axis=1)
        l_scr[...] = jnp.concatenate(l_out, axis=1)
        acc_scr[...] = jnp.concatenate([jnp.concatenate(acc_out[0], axis=1),
                                        jnp.concatenate(acc_out[1], axis=1)], axis=0)

    def body(j, carry):
        step(j, False)
        return carry

    step(qi, True)

    qf = qt.astype(F32)
    qsq = qf * qf
    qnorm = jnp.sqrt(jnp.concatenate(
        [jnp.sum(qsq[:HEAD_DIM], axis=0, keepdims=True),
         jnp.sum(qsq[HEAD_DIM:], axis=0, keepdims=True)], axis=1)) * NORM_SLACK
    kmax = kmax_scr[...]
    kmax2 = jnp.concatenate([jnp.broadcast_to(kmax[:, 0:1], (1, t)),
                             jnp.broadcast_to(kmax[:, HEAD_DIM:HEAD_DIM + 1], (1, t))], axis=1)
    slack = qnorm * kmax2 - m_scr[...]
    cend = cend_ref[0][:, 0, :]
    lane = lax.broadcasted_iota(jnp.int32, cend.shape, 1)
    jrow = lax.broadcasted_iota(jnp.int32, (cend.shape[0], 1), 0)
    needed = jrow < 0
    for head in range(2):
        worst = jnp.max(slack[:, head * t:(head + 1) * t], axis=1, keepdims=True)
        c_head = jnp.sum(jnp.where(lane == 2 * pl.program_id(1) + head, cend, 0.0),
                         axis=1, keepdims=True)
        needed = needed | (worst - c_head * LOG2E > ZERO_PROB_EXP)
    first = jnp.min(jnp.where(needed & (jrow < qi), jrow, qi))
    lax.fori_loop(first, qi, body, 0)

    l = l_scr[...]
    acc = acc_scr[...]
    inv_n = 1.0 / HEAD_DIM
    halves = []
    for head in range(2):
        o = acc[head * HEAD_DIM:(head + 1) * HEAD_DIM] / l[:, head * t:(head + 1) * t]
        halves.append(o * lax.rsqrt(jnp.sum(o * o, axis=0, keepdims=True) * inv_n + RMS_EPS))
    o_t = jnp.concatenate(halves, axis=0)
    o_ref[0] = (jnp.transpose(o_t) * og_ref[...]).astype(o_ref.dtype)


def _fox_attn(qt5, k, cext, vt5, cend, out_g, tile):
    bsz, t_all, _ = k.shape
    pairs = N_HEADS // 2
    tiles = t_all // tile
    kern = functools.partial(_fox_attn_kernel, t=tile)
    return pl.pallas_call(
        kern,
        grid=(bsz, pairs, tiles),
        in_specs=[
            pl.BlockSpec((1, 1, 1, LANES, tile), lambda b, p, i: (b, p, i, 0, 0)),
            pl.BlockSpec((1, t_all, LANES), lambda b, p, i: (b, 0, p)),
            pl.BlockSpec((1, t_all, LANES), lambda b, p, i: (b, 0, p)),
            pl.BlockSpec((1, 1, tiles, LANES, tile), lambda b, p, i: (b, p, 0, 0, 0)),
            pl.BlockSpec((1, tiles, 1, LANES), lambda b, p, i: (b, 0, 0, 0)),
            pl.BlockSpec((1, LANES), lambda b, p, i: (0, p)),
        ],
        out_specs=pl.BlockSpec((1, tile, LANES), lambda b, p, i: (b, i, p)),
        out_shape=jax.ShapeDtypeStruct((bsz, t_all, GROUP_WIDTH), BF16),
        scratch_shapes=[
            pltpu.VMEM((1, 2 * tile), F32),
            pltpu.VMEM((1, 2 * tile), F32),
            pltpu.VMEM((LANES, tile), F32),
            pltpu.VMEM((1, LANES), F32),
        ],
        compiler_params=_cparams(("parallel", "parallel", "arbitrary")),
        name="fox_attn",
    )(qt5, k, cext, vt5, cend, out_g)


def _outproj_router_kernel(x_ref, yr_ref, yf_ref, wo_r_ref, wo_f_ref, g_ref, rwt_ref, rb_ref,
                           x1_ref, h_ref, idx_ref, gate_ref, rank_ref, count_ref):
    @pl.when(pl.program_id(0) == 0)
    def _():
        count_ref[...] = jnp.zeros_like(count_ref)

    x1 = x_ref[...] + _dot(yr_ref[...], wo_r_ref[...]) + _dot(yf_ref[...], wo_f_ref[...])
    x1_ref[...] = x1
    h = x1 * lax.rsqrt(jnp.mean(x1 * x1, axis=-1, keepdims=True) + RMS_EPS) * g_ref[...]
    h_ref[:, 0, :] = h
    logits = lax.dot_general(rwt_ref[...], h, (((1,), (1,)), ((), ())),
                             precision=lax.Precision.HIGHEST,
                             preferred_element_type=F32) + rb_ref[...]
    eidx = lax.broadcasted_iota(jnp.int32, logits.shape, 0)
    vals, idxs, picks = [], [], []
    for _ in range(TOP_K):
        m = jnp.max(logits, axis=0, keepdims=True)
        i = jnp.min(jnp.where(logits == m, eidx, N_EXPERTS), axis=0, keepdims=True)
        vals.append(m)
        idxs.append(i)
        picks.append(eidx == i)
        logits = jnp.where(picks[-1], -jnp.inf, logits)
    es = [jnp.exp(val - vals[0]) for val in vals]
    denom = es[0] + es[1] + es[2] + es[3]
    idx_ref[...] = jnp.concatenate(idxs, axis=0)
    gate_ref[...] = jnp.concatenate([e / denom for e in es], axis=0)

    tm = logits.shape[1]
    chosen = [jnp.where(pk, 1.0, 0.0) for pk in picks]
    any_k = chosen[0] + chosen[1] + chosen[2] + chosen[3]
    before = (lax.broadcasted_iota(jnp.int32, (tm, tm), 0)
              < lax.broadcasted_iota(jnp.int32, (tm, tm), 1))
    prefix = _dot(any_k.astype(BF16), jnp.where(before, 1.0, 0.0).astype(BF16))
    seen = count_ref[:, 0:1] + prefix
    rank_ref[...] = jnp.concatenate(
        [jnp.sum(ch * seen, axis=0, keepdims=True) for ch in chosen], axis=0).astype(jnp.int32)
    count_ref[...] = count_ref[...] + jnp.sum(any_k, axis=1, keepdims=True)


def _outproj_router(x2, yr, yf, wo_r, wo_f, g, rwt, rb, tm):
    n, d = x2.shape
    const = lambda i: (0, 0)
    row = lambda i: (i, 0)
    col = lambda i: (0, i)
    return pl.pallas_call(
        _outproj_router_kernel,
        grid=(n // tm,),
        in_specs=[
            pl.BlockSpec((tm, d), row),
            pl.BlockSpec((tm, GROUP_WIDTH), row),
            pl.BlockSpec((tm, GROUP_WIDTH), row),
            pl.BlockSpec((GROUP_WIDTH, d), const),
            pl.BlockSpec((GROUP_WIDTH, d), const),
            pl.BlockSpec((1, d), const),
            pl.BlockSpec((N_EXPERTS, d), const),
            pl.BlockSpec((N_EXPERTS, 1), const),
        ],
        out_specs=[
            pl.BlockSpec((tm, d), row),
            pl.BlockSpec((tm, 1, d), lambda i: (i, 0, 0)),
            pl.BlockSpec((TOP_K, tm), col),
            pl.BlockSpec((TOP_K, tm), col),
            pl.BlockSpec((TOP_K, tm), col),
            pl.BlockSpec((N_EXPERTS, LANES), const),
        ],
        out_shape=[
            jax.ShapeDtypeStruct((n, d), F32),
            jax.ShapeDtypeStruct((n, 1, d), F32),
            jax.ShapeDtypeStruct((TOP_K, n), jnp.int32),
            jax.ShapeDtypeStruct((TOP_K, n), F32),
            jax.ShapeDtypeStruct((TOP_K, n), jnp.int32),
            jax.ShapeDtypeStruct((N_EXPERTS, LANES), F32),
        ],
        compiler_params=_cparams(("arbitrary",)),
        name="outproj_router",
    )(x2, yr, yf, wo_r, wo_f, g, rwt, rb)


def _w1_split_kernel(w_ref, perm_ref, g_ref, l_ref):
    half = MXU_WIDTH // 2
    perm = perm_ref[...]
    for grp in range(w_ref.shape[2] // MXU_WIDTH):
        blk = w_ref[0, :, grp * MXU_WIDTH:(grp + 1) * MXU_WIDTH].astype(BF16)
        r = _dot(blk, perm)
        g_ref[0, :, grp * half:(grp + 1) * half] = r[:, :half].astype(BF16)
        l_ref[0, :, grp * half:(grp + 1) * half] = r[:, half:].astype(BF16)


def _w1_split(w1, tr):
    e, d, two_f = w1.shape
    half = MXU_WIDTH // 2
    src = jnp.arange(MXU_WIDTH, dtype=jnp.int32)[:, None]
    dst = jnp.arange(MXU_WIDTH, dtype=jnp.int32)[None, :]
    perm = (src == jnp.where(dst < half, 2 * dst, 2 * (dst - half) + 1)).astype(BF16)
    out_sds = jax.ShapeDtypeStruct((e, d, two_f // 2), BF16)
    return pl.pallas_call(
        _w1_split_kernel,
        grid=(e, d // tr),
        in_specs=[
            pl.BlockSpec((1, tr, two_f), lambda i, j: (i, j, 0)),
            pl.BlockSpec((MXU_WIDTH, MXU_WIDTH), lambda i, j: (0, 0)),
        ],
        out_specs=[pl.BlockSpec((1, tr, two_f // 2), lambda i, j: (i, j, 0))] * 2,
        out_shape=[out_sds, out_sds],
        compiler_params=_cparams(("parallel", "parallel")),
        name="w1_split",
    )(w1, perm)


def _expert_kernel(be_ref, tok_a_ref, tok_b_ref, tok_a_next_ref, dst_b_prev_ref, dst_a_ref,
                   dst_b_ref, h_hbm, w1g_a, w1l_a, b1g_a, b1l_a, w2_a, b2_a,
                   w1g_b, w1l_b, b1g_b, b1l_b, w2_b, b2_b,
                   y_hbm, xbuf_a, xbuf_b, obuf_a, obuf_b, xrows, gsem, osem, *, bm, n_real_rows):
    del be_ref
    i = pl.program_id(0)

    def gather_start(idx_ref, xbuf, sem, r, priority=0):
        pltpu.make_async_copy(h_hbm.at[idx_ref[0, 0, r]], xbuf.at[r], sem).start(priority=priority)

    def scatter_start(idx_ref, obuf, sem, r, priority=0):
        pltpu.make_async_copy(obuf.at[r], y_hbm.at[idx_ref[0, 0, r]], sem).start(priority=priority)

    def rows_wait(buf, sem):
        rows = buf.at[pl.ds(0, bm)]
        pltpu.make_async_copy(rows, rows, sem).wait()

    def mlp(xbuf, w1g_ref, w1l_ref, b1g_ref, b1l_ref, w2_ref, b2_ref, start_gather, next_xbuf):
        xrows[...] = xbuf[pl.ds(0, bm), 0, :]
        xb = xrows[...].astype(BF16)
        dff = w1g_ref.shape[2]
        n_pieces = dff // MXU_WIDTH
        per_piece = -(-bm // (n_pieces - 1))
        acts = []
        for piece in range(n_pieces):
            cs = slice(piece * MXU_WIDTH, (piece + 1) * MXU_WIDTH)
            for r in range(piece * per_piece, min((piece + 1) * per_piece, bm)):
                start_gather(r)
            zero = next_xbuf[bm, :, cs]
            glu = _dot(xb, w1g_ref[0, :, cs]) + (b1g_ref[0, :, cs] + zero)
            lin = _dot(xb, w1l_ref[0, :, cs]) + b1l_ref[0, :, cs]
            glu = jnp.minimum(glu, SWIGLU_LIMIT)
            lin = jnp.clip(lin, -SWIGLU_LIMIT, SWIGLU_LIMIT)
            acts.append((glu * _sigmoid(SWIGLU_ALPHA * glu) * (lin + 1.0)).astype(BF16))
        return _dot(jnp.concatenate(acts, axis=1), w2_ref[0]) + b2_ref[0]

    @pl.when(i == 0)
    def _():
        obuf_a[...] = jnp.zeros_like(obuf_a)
        obuf_b[...] = jnp.zeros_like(obuf_b)
        spare = jnp.zeros((SUBLANES, 1, xbuf_a.shape[2]), F32)
        xbuf_a[pl.ds(bm, SUBLANES)] = spare
        xbuf_b[pl.ds(bm, SUBLANES)] = spare

        def first(r, carry):
            pltpu.make_async_copy(obuf_a.at[r], y_hbm.at[n_real_rows + 2 * bm + r],
                                  osem.at[0]).start()
            gather_start(tok_a_ref, xbuf_a, gsem.at[0], r)
            return carry

        lax.fori_loop(0, bm, first, 0)

    def gather_b(r):
        gather_start(tok_b_ref, xbuf_b, gsem.at[1], r, r % 2)

    def gather_a_next(r):
        gather_start(tok_a_next_ref, xbuf_a, gsem.at[0], r, r % 2)

    for r in range(bm):
        scatter_start(dst_b_prev_ref, obuf_b, osem.at[1], r, r % 2)
    rows_wait(xbuf_a, gsem.at[0])
    rows_wait(obuf_a, osem.at[0])
    obuf_a[:, 0, :] = mlp(xbuf_a, w1g_a, w1l_a, b1g_a, b1l_a, w2_a, b2_a, gather_b, xbuf_b)

    for r in range(bm):
        scatter_start(dst_a_ref, obuf_a, osem.at[0], r, r % 2)
    rows_wait(xbuf_b, gsem.at[1])
    rows_wait(obuf_b, osem.at[1])
    obuf_b[:, 0, :] = mlp(xbuf_b, w1g_b, w1l_b, b1g_b, b1l_b, w2_b, b2_b, gather_a_next, xbuf_a)

    @pl.when(i == pl.num_programs(0) - 1)
    def _():
        def last(r, carry):
            scatter_start(dst_b_ref, obuf_b, osem.at[1], r)
            return carry

        lax.fori_loop(0, bm, last, 0)
        rows_wait(obuf_b, osem.at[1])
        rows_wait(obuf_a, osem.at[0])
        rows_wait(xbuf_a, gsem.at[0])


def _expert_mlp(block_e, tok_blocks, dst_blocks, h2, w1g, w1l, b1g, b1l, w2, b2, bm):
    n_blocks = tok_blocks.shape[0]
    assert n_blocks % 2 == 0
    n, _, d = h2.shape
    dff = w1g.shape[2]
    n_real_rows = TOP_K * n
    idx_spec = lambda fn: pl.BlockSpec((1, 1, bm), fn, memory_space=pltpu.SMEM)

    def weight_specs(which):
        wmap = lambda i, be: (be[2 * i + which], 0, 0)
        return [
            pl.BlockSpec((1, d, dff), wmap),
            pl.BlockSpec((1, d, dff), wmap),
            pl.BlockSpec((1, 1, dff), wmap),
            pl.BlockSpec((1, 1, dff), wmap),
            pl.BlockSpec((1, dff, d), wmap),
            pl.BlockSpec((1, 1, d), wmap),
        ]

    grid_spec = pltpu.PrefetchScalarGridSpec(
        num_scalar_prefetch=1,
        grid=(n_blocks // 2,),
        in_specs=[
            idx_spec(lambda i, be: (2 * i, 0, 0)),
            idx_spec(lambda i, be: (2 * i + 1, 0, 0)),
            idx_spec(lambda i, be: (jnp.minimum(2 * i + 2, n_blocks - 1), 0, 0)),
            idx_spec(lambda i, be: (2 * i, 0, 0)),
            idx_spec(lambda i, be: (2 * i + 1, 0, 0)),
            idx_spec(lambda i, be: (2 * i + 2, 0, 0)),
            pl.BlockSpec(memory_space=pl.ANY),
        ] + weight_specs(0) + weight_specs(1),
        out_specs=pl.BlockSpec(memory_space=pl.ANY),
        scratch_shapes=[
            pltpu.VMEM((bm + SUBLANES, 1, d), F32),
            pltpu.VMEM((bm + SUBLANES, 1, d), F32),
            pltpu.VMEM((bm, 1, d), F32),
            pltpu.VMEM((bm, 1, d), F32),
            pltpu.VMEM((bm, d), F32),
            pltpu.SemaphoreType.DMA((2,)),
            pltpu.SemaphoreType.DMA((2,)),
        ],
    )
    weights = (w1g, w1l, b1g, b1l, w2, b2)
    return pl.pallas_call(
        functools.partial(_expert_kernel, bm=bm, n_real_rows=n_real_rows),
        grid_spec=grid_spec,
        out_shape=jax.ShapeDtypeStruct((n_real_rows + 3 * bm, 1, d), F32),
        compiler_params=_cparams(("arbitrary",)),
        name="expert_mlp",
    )(block_e, tok_blocks, tok_blocks, tok_blocks, dst_blocks, dst_blocks, dst_blocks, h2,
      *weights, *weights)


def _combine_kernel(gate_ref, x1_ref, g_ref, y0_ref, y1_ref, y2_ref, y3_ref, o_ref):
    gates = gate_ref[...]
    y = x1_ref[...]
    for kk, y_ref in enumerate((y0_ref, y1_ref, y2_ref, y3_ref)):
        y = y + y_ref[:, 0, :] * gates[:, kk:kk + 1]
    o_ref[...] = y * lax.rsqrt(jnp.mean(y * y, axis=-1, keepdims=True) + RMS_EPS) * g_ref[...]


def _combine(gates_t, x1, g, y_all, tc):
    n, d = x1.shape
    tiles = n // tc
    y_spec = lambda kk: pl.BlockSpec((tc, 1, d), lambda i: (kk * tiles + i, 0, 0))
    return pl.pallas_call(
        _combine_kernel,
        grid=(tiles,),
        in_specs=[
            pl.BlockSpec((tc, TOP_K), lambda i: (i, 0)),
            pl.BlockSpec((tc, d), lambda i: (i, 0)),
            pl.BlockSpec((1, d), lambda i: (0, 0)),
        ] + [y_spec(kk) for kk in range(TOP_K)],
        out_specs=pl.BlockSpec((tc, d), lambda i: (i, 0)),
        out_shape=jax.ShapeDtypeStruct((n, d), F32),
        compiler_params=_cparams(("parallel",)),
        name="combine",
    )(gates_t, x1, g, y_all, y_all, y_all, y_all)


def _slot_sources_kernel(pos_ref, out_ref, *, chunk):
    i = pl.program_id(0)

    @pl.when(i == 0)
    def _():
        def init(p, carry):
            out_ref[p] = -1
            return carry

        lax.fori_loop(0, out_ref.shape[0], init, 0, unroll=32)

    base = i * chunk

    def place(s, carry):
        out_ref[pos_ref[0, 0, s]] = base + s
        return carry

    lax.fori_loop(0, chunk, place, 0, unroll=16)


def _slot_sources(pos, n_pad):
    n_slots = pos.shape[0]
    chunk = _pick(n_slots, 8192)
    return pl.pallas_call(
        functools.partial(_slot_sources_kernel, chunk=chunk),
        grid=(n_slots // chunk,),
        in_specs=[pl.BlockSpec((1, 1, chunk), lambda i: (i, 0, 0), memory_space=pltpu.SMEM)],
        out_specs=pl.BlockSpec(memory_space=pltpu.SMEM),
        out_shape=jax.ShapeDtypeStruct((n_pad,), jnp.int32),
        compiler_params=_cparams(("arbitrary",)),
        name="slot_sources",
    )(pos.reshape(n_slots // chunk, 1, chunk))


def _dispatch_plan(idx, rank, counts, bm):
    n = idx.shape[1]
    n_slots = TOP_K * n
    sizes = counts[:, 0].astype(jnp.int32)
    padded = (sizes + bm - 1) // bm * bm
    pad_ends = jnp.cumsum(padded)
    pad_starts = pad_ends - padded
    experts = jnp.arange(N_EXPERTS, dtype=jnp.int32)
    start_of = jnp.sum(jnp.where(idx[..., None] == experts, pad_starts, 0), axis=-1)
    pos = (start_of + rank).reshape(-1)
    n_pad = n_slots + N_EXPERTS * bm
    n_blocks = n_pad // bm
    slot_src = _slot_sources(pos, n_pad)
    p = jnp.arange(n_pad, dtype=jnp.int32)
    spare = n_slots + (p // bm) % 2 * bm + p % bm
    real = slot_src >= 0
    tok_blocks = jnp.where(real, slot_src % n, 0).reshape(n_blocks, 1, bm)
    dst = jnp.where(real, slot_src, spare)
    dst_blocks = jnp.concatenate([spare[bm:2 * bm], dst]).reshape(n_blocks + 1, 1, bm)
    block_start = jnp.arange(n_blocks, dtype=jnp.int32) * bm
    block_e = jnp.minimum(jnp.sum(pad_ends[None, :] <= block_start[:, None], axis=1),
                          N_EXPERTS - 1).astype(jnp.int32)
    return tok_blocks, dst_blocks, block_e


def _block_diag_ones():
    hid = jnp.arange(MXU_WIDTH, dtype=jnp.int32) // HEAD_DIM
    return (hid[:, None] == hid[None, :]).astype(BF16)


def _pick(n, pref):
    return pref if n % pref == 0 else n


def kernel(x, attn_norm_g, w_in, rwkv_mu, rwkv_w0, rwkv_w_up, rwkv_a0, rwkv_a_up, rwkv_g_up,
           rwkv_k_k, rwkv_k_a, rwkv_r_k, rwkv_ln_w, rwkv_ln_b, fox_f_bias, fox_out_g, w_out,
           ffn_norm_g, router_w, router_b, expert_w1, expert_b1, expert_w2, expert_b2,
           final_norm_g):
    bsz, t, d = x.shape
    n = bsz * t
    depth = w_in.shape[0]
    assert depth == 1, "the final norm is fused into the last stage of a single layer"
    bd = _block_diag_ones()
    x2 = x.reshape(n, d)
    for l in range(depth):
        w_l = w_in[l]
        w_r = w_l[:, :RWKV_IN].astype(BF16)
        w_qkv = w_l[:, RWKV_IN:RWKV_IN + 3 * GROUP_WIDTH].astype(BF16)
        w_qt = w_qkv[:, :GROUP_WIDTH].T
        w_k = w_qkv[:, GROUP_WIDTH:2 * GROUP_WIDTH]
        w_vt = w_qkv[:, 2 * GROUP_WIDTH:].T
        w_f = jnp.pad(w_l[:, RWKV_IN + 3 * GROUP_WIDTH:], ((0, 0), (0, LANES - N_HEADS))).astype(BF16)
        fb_pad = jnp.pad(fox_f_bias[l], (0, LANES - N_HEADS)).reshape(1, LANES)
        wup_pad = jnp.pad(rwkv_w_up[l], ((0, LANES - DECAY_LORA), (0, 0))).astype(BF16)
        aup_pad = jnp.pad(rwkv_a_up[l], ((DECAY_LORA, 0), (0, 0))).astype(BF16)
        gup = rwkv_g_up[l].astype(BF16)
        vec = lambda a: a.reshape(1, -1)

        u_r, qt5, k, vt5, fl = _inproj(x2, vec(attn_norm_g[l]), w_r, w_qt, w_k, w_vt, w_f,
                                       bsz, ATTN_TILE)
        cext, cend = _fox_gate(fl.reshape(bsz, t, LANES), fb_pad, _gate_piece_selectors(),
                               ATTN_TILE)
        y_rwkv = _rwkv_mix(
            u_r.reshape(bsz, t, RWKV_IN), vec(rwkv_mu[l]), vec(rwkv_w0[l]), wup_pad,
            vec(rwkv_a0[l]), aup_pad, gup, vec(rwkv_k_k[l]), vec(rwkv_k_a[l]), vec(rwkv_r_k[l]),
            vec(rwkv_ln_w[l]), vec(rwkv_ln_b[l]), bd, RWKV_CHUNKS_PER_STEP)
        y_fox = _fox_attn(qt5, k.reshape(bsz, t, GROUP_WIDTH), cext, vt5, cend,
                          vec(fox_out_g[l]), ATTN_TILE)

        wo = w_out[l].astype(BF16)
        x1, h2, idx, gates, rank, counts = _outproj_router(
            x2, y_rwkv.reshape(n, GROUP_WIDTH), y_fox.reshape(n, GROUP_WIDTH),
            wo[:GROUP_WIDTH], wo[GROUP_WIDTH:], vec(ffn_norm_g[l]),
            router_w[l].T, router_b[l].reshape(N_EXPERTS, 1), _pick(n, 512))

        bm = 256
        tok_blocks, dst_blocks, block_e = _dispatch_plan(idx, rank, counts, bm)
        w1g, w1l = _w1_split(expert_w1[l], 512)
        b1 = expert_b1[l]
        b1g = b1[:, None, 0::2]
        b1l = b1[:, None, 1::2]
        y_all = _expert_mlp(block_e, tok_blocks, dst_blocks, h2, w1g, w1l, b1g, b1l,
                            expert_w2[l].astype(BF16), expert_b2[l][:, None, :], bm)
        x2 = _combine(gates.T, x1, vec(final_norm_g), y_all, _pick(n, 256))
    return x2.reshape(bsz, t, d)
```

```python
import functools

import jax
import jax.numpy as jnp
from jax import lax
from jax.experimental import pallas as pl
from jax.experimental.pallas import tpu as pltpu

F32 = jnp.float32
BF16 = jnp.bfloat16

HEAD_DIM = 64
N_HEADS = 8
GROUP_WIDTH = N_HEADS * HEAD_DIM
DECAY_LORA = 64
AAA_LORA = 64
GATE_LORA = 128
RWKV_IN = 3 * GROUP_WIDTH + DECAY_LORA + AAA_LORA + GATE_LORA
LORA_OFF = 3 * GROUP_WIDTH
N_EXPERTS = 32
TOP_K = 4
SWIGLU_ALPHA = 1.702
SWIGLU_LIMIT = 7.0
RMS_EPS = 1e-5
RWKV_GN_EPS = 64e-5
LANES = 128
SUBLANES = 8
RWKV_CHUNK = 64
RWKV_CHUNKS_PER_STEP = 8
ATTN_TILE = 512
MXU_WIDTH = 256
LOG2E = 1.4426950408889634
Q_SCALE = HEAD_DIM ** -0.5 * LOG2E
ZERO_PROB_EXP = -152.0
NORM_SLACK = 1.0 + 2.0 ** -6
VMEM_LIMIT = 56 * 1024 * 1024


def _cparams(semantics):
    return pltpu.CompilerParams(dimension_semantics=semantics, vmem_limit_bytes=VMEM_LIMIT)


def _dot(a, b):
    return jnp.dot(a, b, preferred_element_type=F32)


def _dot_nt(a, b):
    return lax.dot_general(a, b, (((1,), (1,)), ((), ())), preferred_element_type=F32)


def _dot_tn(a, b):
    return lax.dot_general(a, b, (((0,), (0,)), ((), ())), preferred_element_type=F32)


def _split3(x):
    hi = x.astype(BF16)
    r1 = x - hi.astype(F32)
    mid = r1.astype(BF16)
    lo = (r1 - mid.astype(F32)).astype(BF16)
    return hi, mid, lo


def _dot_exact_lhs(a_bf16, x):
    hi, mid, lo = _split3(x)
    return _dot(a_bf16, hi) + _dot(a_bf16, mid) + _dot(a_bf16, lo)


def _head_sums(x, same_head):
    w = same_head.shape[0]
    parts = []
    for g in range(x.shape[1] // w):
        xs = x[:, g * w:(g + 1) * w]
        hi = xs.astype(BF16)
        lo = (xs - hi.astype(F32)).astype(BF16)
        parts.append(_dot(hi, same_head) + _dot(lo, same_head))
    return jnp.concatenate(parts, axis=1)


def _softplus(z):
    return jnp.maximum(z, 0.0) + jnp.log1p(jnp.exp(-jnp.abs(z)))


def _sigmoid(z):
    return 1.0 / (1.0 + jnp.exp(-z))


def _inproj_kernel(x_ref, g_ref, wr_ref, wqt_ref, wk_ref, wvt_ref, wf_ref,
                   ur_ref, qt_ref, k_ref, vt_ref, fl_ref):
    x = x_ref[...]
    h = x * lax.rsqrt(jnp.mean(x * x, axis=-1, keepdims=True) + RMS_EPS) * g_ref[...]
    hb = h.astype(BF16)
    ur_ref[...] = _dot(hb, wr_ref[...])
    k_ref[...] = _dot(hb, wk_ref[...]).astype(BF16)
    fl_ref[...] = _dot(hb, wf_ref[...])
    qt = (_dot_nt(wqt_ref[...], hb) * Q_SCALE).astype(BF16)
    vt = _dot_nt(wvt_ref[...], hb).astype(BF16)
    for p in range(N_HEADS // 2):
        qt_ref[0, p, 0] = qt[p * LANES:(p + 1) * LANES]
        vt_ref[0, p, 0] = vt[p * LANES:(p + 1) * LANES]


def _inproj(x2, g, w_r, w_qt, w_k, w_vt, w_f, bsz, tm):
    n, d = x2.shape
    nt = n // bsz // tm
    pairs = N_HEADS // 2
    const = lambda i: (0, 0)
    row = lambda i: (i, 0)
    fm = lambda i: (i // nt, 0, i % nt, 0, 0)
    fm_sds = jax.ShapeDtypeStruct((bsz, pairs, nt, LANES, tm), BF16)
    return pl.pallas_call(
        _inproj_kernel,
        grid=(n // tm,),
        in_specs=[
            pl.BlockSpec((tm, d), row),
            pl.BlockSpec((1, d), const),
            pl.BlockSpec(w_r.shape, const),
            pl.BlockSpec(w_qt.shape, const),
            pl.BlockSpec(w_k.shape, const),
            pl.BlockSpec(w_vt.shape, const),
            pl.BlockSpec(w_f.shape, const),
        ],
        out_specs=[
            pl.BlockSpec((tm, RWKV_IN), row),
            pl.BlockSpec((1, pairs, 1, LANES, tm), fm),
            pl.BlockSpec((tm, GROUP_WIDTH), row),
            pl.BlockSpec((1, pairs, 1, LANES, tm), fm),
            pl.BlockSpec((tm, LANES), row),
        ],
        out_shape=[
            jax.ShapeDtypeStruct((n, RWKV_IN), F32),
            fm_sds,
            jax.ShapeDtypeStruct((n, GROUP_WIDTH), BF16),
            fm_sds,
            jax.ShapeDtypeStruct((n, LANES), F32),
        ],
        compiler_params=_cparams(("parallel",)),
        name="inproj",
    )(x2, g, w_r, w_qt, w_k, w_vt, w_f)


def _fox_gate_kernel(fl_ref, fb_ref, sel_ref, c_ref, cend_ref, carry):
    tt = fl_ref.shape[1]

    @pl.when(pl.program_id(1) == 0)
    def _():
        carry[...] = jnp.zeros_like(carry)

    z = fl_ref[0] + fb_ref[...]
    log_f = jnp.minimum(z, 0.0) - jnp.log1p(jnp.exp(-jnp.abs(z)))
    ri = lax.broadcasted_iota(jnp.int32, (tt, tt), 0)
    ci = lax.broadcasted_iota(jnp.int32, (tt, tt), 1)
    tri = jnp.where(ri >= ci, 1.0, 0.0).astype(BF16)
    c = _dot_exact_lhs(tri, log_f) + carry[...]
    carry[...] = c[tt - 1:tt, :]
    cend_ref[0, 0] = c[tt - 1:tt, :]
    hi, mid, lo = _split3(c * LOG2E)
    c_ref[0] = (_dot(hi, sel_ref[0]) + _dot(mid, sel_ref[1]) + _dot(lo, sel_ref[2])).astype(BF16)


def _gate_piece_selectors():
    h = jnp.arange(LANES, dtype=jnp.int32)[:, None]
    col = jnp.arange(GROUP_WIDTH, dtype=jnp.int32)[None, :]
    sels = []
    for m in range(3):
        target = LANES * (h // 2) + 3 * (h % 2) + m
        sels.append(((col == target) & (h < N_HEADS)).astype(BF16))
    return jnp.stack(sels)


def _fox_gate(fl3, fb_pad, sel, tt):
    b, t, _ = fl3.shape
    return pl.pallas_call(
        _fox_gate_kernel,
        grid=(b, t // tt),
        in_specs=[
            pl.BlockSpec((1, tt, LANES), lambda i, j: (i, j, 0)),
            pl.BlockSpec((1, LANES), lambda i, j: (0, 0)),
            pl.BlockSpec((3, LANES, GROUP_WIDTH), lambda i, j: (0, 0, 0)),
        ],
        out_specs=[
            pl.BlockSpec((1, tt, GROUP_WIDTH), lambda i, j: (i, j, 0)),
            pl.BlockSpec((1, 1, 1, LANES), lambda i, j: (i, j, 0, 0)),
        ],
        out_shape=[
            jax.ShapeDtypeStruct((b, t, GROUP_WIDTH), BF16),
            jax.ShapeDtypeStruct((b, t // tt, 1, LANES), F32),
        ],
        scratch_shapes=[pltpu.VMEM((1, LANES), F32)],
        compiler_params=_cparams(("parallel", "arbitrary")),
        name="fox_gate",
    )(fl3, fb_pad, sel)


def _rwkv_prep_body(u_ref, mu_ref, w0_ref, wup_ref, a0_ref, aup_ref, gup_ref, kk_ref, ka_ref,
                      bd_ref, r_out, k_out, v_out, lw_out, kk_out, b_out, g_out, carry):
    tt = u_ref.shape[1]

    @pl.when(pl.program_id(1) == 0)
    def _():
        carry[...] = jnp.zeros_like(carry)

    u = u_ref[0]
    prev = pltpu.roll(u, 1, axis=0)
    row = lax.broadcasted_iota(jnp.int32, u.shape, 0)
    prev = jnp.where(row == 0, carry[...], prev)
    carry[...] = u[tt - 1:tt, :]
    us = u + (prev - u) * mu_ref[...]

    r = us[:, :GROUP_WIDTH]
    k = us[:, GROUP_WIDTH:2 * GROUP_WIDTH]
    v = us[:, 2 * GROUP_WIDTH:LORA_OFF]
    wa = us[:, LORA_OFF:LORA_OFF + LANES]
    gl = us[:, LORA_OFF + LANES:]

    w_lin = _dot(jnp.tanh(wa).astype(BF16), wup_ref[...])
    a_lin = _dot(wa.astype(BF16), aup_ref[...])
    w = -_softplus(-(w0_ref[...] + w_lin)) - 0.5
    lw_out[0] = -jnp.exp(w)
    a = _sigmoid(a0_ref[...] + a_lin)
    g_out[0] = _dot(_sigmoid(gl).astype(BF16), gup_ref[...])

    kkr = k * kk_ref[...]
    ss = _head_sums(kkr * kkr, bd_ref[...])
    kk = kkr / jnp.maximum(jnp.sqrt(ss), 1e-12)
    r_out[0] = r
    k_out[0] = k * (1.0 + (a - 1.0) * ka_ref[...])
    v_out[0] = v
    kk_out[0] = kk
    b_out[0] = kk * a


def _rwkv_scan_body(r_ref, k_ref, v_ref, lw_ref, kk_ref, b_ref, g_ref, rk_ref, lnw_ref, lnb_ref,
                      bd_ref, o_ref, s_scr, *, n_chunks):
    c = RWKV_CHUNK
    w = MXU_WIDTH
    hpg = w // HEAD_DIM
    n_groups = GROUP_WIDTH // w

    @pl.when(pl.program_id(1) == 0)
    def _():
        s_scr[...] = jnp.zeros_like(s_scr)

    row = lax.broadcasted_iota(jnp.int32, (c, w), 0)
    u = lax.broadcasted_iota(jnp.int32, (c, w), 1) % HEAD_DIM
    strict = row > u
    incl = row >= u
    eye = jnp.where(row == u, 1.0, 0.0)
    level_masks = []
    s = 1
    while s < c:
        same = (row // (2 * s)) == (u // (2 * s))
        level_masks.append(same & ((row % (2 * s)) >= s) & ((u % (2 * s)) < s))
        s *= 2
    same_head = (lax.broadcasted_iota(jnp.int32, (w, w), 0) // HEAD_DIM
                 == lax.broadcasted_iota(jnp.int32, (w, w), 1) // HEAD_DIM)
    tri = jnp.where(lax.broadcasted_iota(jnp.int32, (c, c), 0)
                    >= lax.broadcasted_iota(jnp.int32, (c, c), 1), 1.0, 0.0).astype(BF16)

    def bdiag(x):
        xb = x.astype(BF16)
        tiled = jnp.concatenate([xb] * hpg, axis=0)
        return jnp.where(same_head, tiled, jnp.zeros_like(tiled))

    chains = [(ci, gi) for ci in range(n_chunks) for gi in range(n_groups)]
    lhs, rk_t, vbs, xcat, xneg, gam, rt32 = {}, {}, {}, {}, {}, {}, {}
    for ci in range(n_chunks):
        rs = slice(ci * c, (ci + 1) * c)
        r = r_ref[0, rs, :]
        k = k_ref[0, rs, :]
        lw = lw_ref[0, rs, :]
        kk = kk_ref[0, rs, :]
        b = b_ref[0, rs, :]
        g_cum = _dot_exact_lhs(tri, lw)
        g_last = g_cum[c - 1:c, :]
        r_t = r * jnp.exp(g_cum)
        kk_t = kk * jnp.exp(g_cum - lw)
        e_neg = jnp.exp(-g_cum)
        b_n = b * e_neg
        k_n = k * e_neg
        e_end = jnp.exp(g_last - g_cum)
        b_e = (b * e_end).astype(BF16)
        k_e = (k * e_end).astype(BF16)
        gamma = jnp.exp(g_last)
        vb = v_ref[0, rs, :].astype(BF16)
        for gi in range(n_groups):
            gs = slice(gi * w, (gi + 1) * w)
            ch = (ci, gi)
            lhs[ch] = jnp.concatenate([r_t[:, gs].astype(BF16), kk_t[:, gs].astype(BF16)], axis=0)
            rk_t[ch] = (jnp.concatenate([bdiag(b_n[:, gs]), bdiag(k_n[:, gs])], axis=0),
                        bdiag(kk_t[:, gs]))
            vbs[ch] = vb[:, gs]
            xcat[ch] = jnp.concatenate([b_e[:, gs], k_e[:, gs]], axis=0)
            xneg[ch] = jnp.concatenate([-b_e[:, gs], k_e[:, gs]], axis=0)
            gam[ch] = gamma[:, gs]
            rt32[ch] = r_t[:, gs]

    p = {ch: _dot_nt(lhs[ch], rk_t[ch][0]) for ch in chains}
    l_b = {ch: jnp.where(strict, p[ch][c:, :w], 0.0) for ch in chains}
    l_k = {ch: jnp.where(strict, p[ch][c:, w:], 0.0).astype(BF16) for ch in chains}
    p_br = {ch: jnp.where(incl, p[ch][:c, :w], 0.0).astype(BF16) for ch in chains}
    p_kr = {ch: jnp.where(incl, p[ch][:c, w:], 0.0).astype(BF16) for ch in chains}
    v_bd = {ch: bdiag(vbs[ch]) for ch in chains}
    lkv = {ch: _dot(l_k[ch], v_bd[ch]) for ch in chains}

    t_inv = {ch: eye - jnp.where(level_masks[0], l_b[ch], 0.0) for ch in chains}
    for m in level_masks[1:]:
        tb = {ch: t_inv[ch].astype(BF16) for ch in chains}
        ct = {ch: _dot(jnp.where(m, l_b[ch], 0.0).astype(BF16), bdiag(tb[ch])) for ch in chains}
        t_inv = {ch: t_inv[ch] - _dot(tb[ch], bdiag(ct[ch])) for ch in chains}

    mm = {ch: _dot(t_inv[ch].astype(BF16),
                   jnp.concatenate([rk_t[ch][1], bdiag(lkv[ch])], axis=1)) for ch in chains}
    pm = {ch: _dot(p_br[ch], jnp.concatenate([bdiag(mm[ch][:, :w]), bdiag(mm[ch][:, w:])], axis=1))
          for ch in chains}
    pkv = {ch: _dot(p_kr[ch], v_bd[ch]) for ch in chains}
    n1 = {ch: (rt32[ch] - pm[ch][:, :w]).astype(BF16) for ch in chains}
    n2 = {ch: pkv[ch] - pm[ch][:, w:] for ch in chains}
    omega = {ch: jnp.where(same_head, _dot_tn(mm[ch][:, :w].astype(BF16), xcat[ch][:c]), 0.0)
             .astype(BF16) for ch in chains}
    psi = {ch: jnp.where(same_head, _dot_tn(
        jnp.concatenate([mm[ch][:, w:].astype(BF16), vbs[ch]], axis=0), xneg[ch]), 0.0)
           for ch in chains}

    state = [s_scr[gi] for gi in range(n_groups)]
    ys = []
    for ci in range(n_chunks):
        sb = [state[gi].astype(BF16) for gi in range(n_groups)]
        ys.append(jnp.concatenate(
            [_dot_nt(n1[(ci, gi)], sb[gi]) + n2[(ci, gi)] for gi in range(n_groups)], axis=1))
        state = [state[gi] * gam[(ci, gi)] - _dot(sb[gi], omega[(ci, gi)]) + psi[(ci, gi)]
                 for gi in range(n_groups)]
    for gi in range(n_groups):
        s_scr[gi] = state[gi]

    y = jnp.concatenate(ys, axis=0)
    r = r_ref[0]
    k = k_ref[0]
    v = v_ref[0]
    bd = bd_ref[...]
    inv_n = 1.0 / HEAD_DIM
    mean = _head_sums(y, bd) * inv_n
    d = y - mean
    var = _head_sums(d * d, bd) * inv_n
    yn = d * lax.rsqrt(var + RWKV_GN_EPS) * lnw_ref[...] + lnb_ref[...]
    bonus = _head_sums(r * k * rk_ref[...], bd) * v
    o_ref[0] = ((yn + bonus) * g_ref[0]).astype(o_ref.dtype)


def _rwkv_kernel(u_ref, mu_ref, w0_ref, wup_ref, a0_ref, aup_ref, gup_ref, kkw_ref, ka_ref,
                 rk_ref, lnw_ref, lnb_ref, bd_ref, o_ref, carry, s_scr,
                 r_s, k_s, v_s, lw_s, kk_s, b_s, g_s, *, n_chunks):
    _rwkv_prep_body(u_ref, mu_ref, w0_ref, wup_ref, a0_ref, aup_ref, gup_ref, kkw_ref, ka_ref,
                    bd_ref, r_s, k_s, v_s, lw_s, kk_s, b_s, g_s, carry)
    _rwkv_scan_body(r_s, k_s, v_s, lw_s, kk_s, b_s, g_s, rk_ref, lnw_ref, lnb_ref, bd_ref,
                    o_ref, s_scr, n_chunks=n_chunks)


def _rwkv_mix(u3, mu, w0, wup_pad, a0, aup_pad, gup, k_k, k_a, r_k, ln_w, ln_b, bd, n_chunks):
    bsz, t, _ = u3.shape
    rows = RWKV_CHUNK * n_chunks
    const = lambda i, j: (0, 0)
    tile = lambda i, j: (i, j, 0)
    vec = pl.BlockSpec((1, GROUP_WIDTH), const)
    staged = pltpu.VMEM((1, rows, GROUP_WIDTH), F32)
    return pl.pallas_call(
        functools.partial(_rwkv_kernel, n_chunks=n_chunks),
        grid=(bsz, t // rows),
        in_specs=[
            pl.BlockSpec((1, rows, RWKV_IN), tile),
            pl.BlockSpec((1, RWKV_IN), const),
            vec,
            pl.BlockSpec((LANES, GROUP_WIDTH), const),
            vec,
            pl.BlockSpec((LANES, GROUP_WIDTH), const),
            pl.BlockSpec((GATE_LORA, GROUP_WIDTH), const),
            vec, vec, vec, vec, vec,
            pl.BlockSpec((MXU_WIDTH, MXU_WIDTH), const),
        ],
        out_specs=pl.BlockSpec((1, rows, GROUP_WIDTH), tile),
        out_shape=jax.ShapeDtypeStruct((bsz, t, GROUP_WIDTH), BF16),
        scratch_shapes=[
            pltpu.VMEM((1, RWKV_IN), F32),
            pltpu.VMEM((GROUP_WIDTH // MXU_WIDTH, MXU_WIDTH, MXU_WIDTH), F32),
        ] + [staged] * 7,
        compiler_params=_cparams(("parallel", "arbitrary")),
        name="rwkv_mix",
    )(u3, mu, w0, wup_pad, a0, aup_pad, gup, k_k, k_a, r_k, ln_w, ln_b, bd)


def _fox_attn_kernel(qt_ref, k_ref, ce_ref, vt_ref, cend_ref, og_ref, o_ref,
                     m_scr, l_scr, acc_scr, kmax_scr, *, t):
    qi = pl.program_id(2)
    n_strips = 2 * t // LANES
    qt = qt_ref[0, 0, 0]
    frow = lax.broadcasted_iota(jnp.int32, (LANES, t), 0)
    zero = jnp.zeros_like(qt)
    main = jnp.concatenate([jnp.where(frow < HEAD_DIM, qt, zero),
                            jnp.where(frow < HEAD_DIM, zero, qt)], axis=1)
    erow = lax.broadcasted_iota(jnp.int32, (LANES, 2 * t), 0)
    ecol = lax.broadcasted_iota(jnp.int32, (LANES, 2 * t), 1)
    off = jnp.where(ecol < t, 0, 3)
    extra = jnp.where((erow >= off) & (erow < off + 3), -1.0, 0.0).astype(BF16)
    q_aug = jnp.concatenate([main, extra], axis=0)

    m_scr[...] = jnp.full_like(m_scr, -jnp.inf)
    l_scr[...] = jnp.zeros_like(l_scr)
    acc_scr[...] = jnp.zeros_like(acc_scr)

    @pl.when(qi == 0)
    def _():
        hid_r = lax.broadcasted_iota(jnp.int32, (LANES, LANES), 0) // HEAD_DIM
        hid_c = lax.broadcasted_iota(jnp.int32, (LANES, LANES), 1) // HEAD_DIM
        same_head = jnp.where(hid_r == hid_c, 1.0, 0.0).astype(BF16)

        def tile_max(j, best):
            kf = k_ref[0, pl.ds(pl.multiple_of(j * t, t), t), :].astype(F32)
            sq = _dot((kf * kf).astype(BF16), same_head)
            return jnp.maximum(best, jnp.max(sq, axis=0, keepdims=True))

        best = lax.fori_loop(0, k_ref.shape[1] // t, tile_max, jnp.zeros((1, LANES), F32))
        kmax_scr[...] = jnp.sqrt(best) * NORM_SLACK

    def step(j, masked):
        start = pl.multiple_of(j * t, t)
        k_aug = jnp.concatenate([k_ref[0, pl.ds(start, t), :], ce_ref[0, pl.ds(start, t), :]],
                                axis=1)
        vt = vt_ref[0, 0, j]
        zt = _dot(k_aug, q_aug)
        m_prev = m_scr[...]
        l_prev = l_scr[...]
        acc_prev = acc_scr[...]
        m_out, l_out, acc_out = [], [], [[], []]
        for s in range(n_strips):
            head = s // (n_strips // 2)
            cs = slice(s * LANES, (s + 1) * LANES)
            z = zt[:, cs]
            if masked:
                key = lax.broadcasted_iota(jnp.int32, (t, LANES), 0)
                qry = lax.broadcasted_iota(jnp.int32, (t, LANES), 1) + (s * LANES) % t
                z = jnp.where(key <= qry, z, -jnp.inf)
            m_new = jnp.maximum(m_prev[:, cs], jnp.max(z, axis=0, keepdims=True))
            alpha = jnp.exp2(m_prev[:, cs] - m_new)
            p = jnp.exp2(z - m_new)
            l_out.append(alpha * l_prev[:, cs] + jnp.sum(p, axis=0, keepdims=True))
            m_out.append(m_new)
            hs = slice(head * HEAD_DIM, (head + 1) * HEAD_DIM)
            qs = slice((s * LANES) % t, (s * LANES) % t + LANES)
            pv = _dot(vt[hs], p.astype(BF16))
            acc_out[head].append(alpha * acc_prev[hs, qs] + pv)
        m_scr[...] = jnp.concatenate(m_out, axis=1)
        l_scr[...] = jnp.concatenate(l_out, axis=1)
        acc_scr[...] = jnp.concatenate([jnp.concatenate(acc_out[0], axis=1),
                                        jnp.concatenate(acc_out[1], axis=1)], axis=0)

    def body(j, carry):
        step(j, False)
        return carry

    step(qi, True)

    qf = qt.astype(F32)
    qsq = qf * qf
    qnorm = jnp.sqrt(jnp.concatenate(
        [jnp.sum(qsq[:HEAD_DIM], axis=0, keepdims=True),
         jnp.sum(qsq[HEAD_DIM:], axis=0, keepdims=True)], axis=1)) * NORM_SLACK
    kmax = kmax_scr[...]
    kmax2 = jnp.concatenate([jnp.broadcast_to(kmax[:, 0:1], (1, t)),
                             jnp.broadcast_to(kmax[:, HEAD_DIM:HEAD_DIM + 1], (1, t))], axis=1)
    slack = qnorm * kmax2 - m_scr[...]
    cend = cend_ref[0][:, 0, :]
    lane = lax.broadcasted_iota(jnp.int32, cend.shape, 1)
    jrow = lax.broadcasted_iota(jnp.int32, (cend.shape[0], 1), 0)
    needed = jrow < 0
    for head in range(2):
        worst = jnp.max(slack[:, head * t:(head + 1) * t], axis=1, keepdims=True)
        c_head = jnp.sum(jnp.where(lane == 2 * pl.program_id(1) + head, cend, 0.0),
                         axis=1, keepdims=True)
        needed = needed | (worst - c_head * LOG2E > ZERO_PROB_EXP)
    first = jnp.min(jnp.where(needed & (jrow < qi), jrow, qi))
    lax.fori_loop(first, qi, body, 0)

    l = l_scr[...]
    acc = acc_scr[...]
    inv_n = 1.0 / HEAD_DIM
    halves = []
    for head in range(2):
        o = acc[head * HEAD_DIM:(head + 1) * HEAD_DIM] / l[:, head * t:(head + 1) * t]
        halves.append(o * lax.rsqrt(jnp.sum(o * o, axis=0, keepdims=True) * inv_n + RMS_EPS))
    o_t = jnp.concatenate(halves, axis=0)
    o_ref[0] = (jnp.transpose(o_t) * og_ref[...]).astype(o_ref.dtype)


def _fox_attn(qt5, k, cext, vt5, cend, out_g, tile):
    bsz, t_all, _ = k.shape
    pairs = N_HEADS // 2
    tiles = t_all // tile
    kern = functools.partial(_fox_attn_kernel, t=tile)
    return pl.pallas_call(
        kern,
        grid=(bsz, pairs, tiles),
        in_specs=[
            pl.BlockSpec((1, 1, 1, LANES, tile), lambda b, p, i: (b, p, i, 0, 0)),
            pl.BlockSpec((1, t_all, LANES), lambda b, p, i: (b, 0, p)),
            pl.BlockSpec((1, t_all, LANES), lambda b, p, i: (b, 0, p)),
            pl.BlockSpec((1, 1, tiles, LANES, tile), lambda b, p, i: (b, p, 0, 0, 0)),
            pl.BlockSpec((1, tiles, 1, LANES), lambda b, p, i: (b, 0, 0, 0)),
            pl.BlockSpec((1, LANES), lambda b, p, i: (0, p)),
        ],
        out_specs=pl.BlockSpec((1, tile, LANES), lambda b, p, i: (b, i, p)),
        out_shape=jax.ShapeDtypeStruct((bsz, t_all, GROUP_WIDTH), BF16),
        scratch_shapes=[
            pltpu.VMEM((1, 2 * tile), F32),
            pltpu.VMEM((1, 2 * tile), F32),
            pltpu.VMEM((LANES, tile), F32),
            pltpu.VMEM((1, LANES), F32),
        ],
        compiler_params=_cparams(("parallel", "parallel", "arbitrary")),
        name="fox_attn",
    )(qt5, k, cext, vt5, cend, out_g)


def _outproj_router_kernel(x_ref, yr_ref, yf_ref, wo_r_ref, wo_f_ref, g_ref, rwt_ref, rb_ref,
                           x1_ref, h_ref, idx_ref, gate_ref, rank_ref, count_ref):
    @pl.when(pl.program_id(0) == 0)
    def _():
        count_ref[...] = jnp.zeros_like(count_ref)

    x1 = x_ref[...] + _dot(yr_ref[...], wo_r_ref[...]) + _dot(yf_ref[...], wo_f_ref[...])
    x1_ref[...] = x1
    h = x1 * lax.rsqrt(jnp.mean(x1 * x1, axis=-1, keepdims=True) + RMS_EPS) * g_ref[...]
    h_ref[:, 0, :] = h
    logits = lax.dot_general(rwt_ref[...], h, (((1,), (1,)), ((), ())),
                             precision=lax.Precision.HIGHEST,
                             preferred_element_type=F32) + rb_ref[...]
    eidx = lax.broadcasted_iota(jnp.int32, logits.shape, 0)
    vals, idxs, picks = [], [], []
    for _ in range(TOP_K):
        m = jnp.max(logits, axis=0, keepdims=True)
        i = jnp.min(jnp.where(logits == m, eidx, N_EXPERTS), axis=0, keepdims=True)
        vals.append(m)
        idxs.append(i)
        picks.append(eidx == i)
        logits = jnp.where(picks[-1], -jnp.inf, logits)
    es = [jnp.exp(val - vals[0]) for val in vals]
    denom = es[0] + es[1] + es[2] + es[3]
    idx_ref[...] = jnp.concatenate(idxs, axis=0)
    gate_ref[...] = jnp.concatenate([e / denom for e in es], axis=0)

    tm = logits.shape[1]
    chosen = [jnp.where(pk, 1.0, 0.0) for pk in picks]
    any_k = chosen[0] + chosen[1] + chosen[2] + chosen[3]
    before = (lax.broadcasted_iota(jnp.int32, (tm, tm), 0)
              < lax.broadcasted_iota(jnp.int32, (tm, tm), 1))
    prefix = _dot(any_k.astype(BF16), jnp.where(before, 1.0, 0.0).astype(BF16))
    seen = count_ref[:, 0:1] + prefix
    rank_ref[...] = jnp.concatenate(
        [jnp.sum(ch * seen, axis=0, keepdims=True) for ch in chosen], axis=0).astype(jnp.int32)
    count_ref[...] = count_ref[...] + jnp.sum(any_k, axis=1, keepdims=True)


def _outproj_router(x2, yr, yf, wo_r, wo_f, g, rwt, rb, tm):
    n, d = x2.shape
    const = lambda i: (0, 0)
    row = lambda i: (i, 0)
    col = lambda i: (0, i)
    return pl.pallas_call(
        _outproj_router_kernel,
        grid=(n // tm,),
        in_specs=[
            pl.BlockSpec((tm, d), row),
            pl.BlockSpec((tm, GROUP_WIDTH), row),
            pl.BlockSpec((tm, GROUP_WIDTH), row),
            pl.BlockSpec((GROUP_WIDTH, d), const),
            pl.BlockSpec((GROUP_WIDTH, d), const),
            pl.BlockSpec((1, d), const),
            pl.BlockSpec((N_EXPERTS, d), const),
            pl.BlockSpec((N_EXPERTS, 1), const),
        ],
        out_specs=[
            pl.BlockSpec((tm, d), row),
            pl.BlockSpec((tm, 1, d), lambda i: (i, 0, 0)),
            pl.BlockSpec((TOP_K, tm), col),
            pl.BlockSpec((TOP_K, tm), col),
            pl.BlockSpec((TOP_K, tm), col),
            pl.BlockSpec((N_EXPERTS, LANES), const),
        ],
        out_shape=[
            jax.ShapeDtypeStruct((n, d), F32),
            jax.ShapeDtypeStruct((n, 1, d), F32),
            jax.ShapeDtypeStruct((TOP_K, n), jnp.int32),
            jax.ShapeDtypeStruct((TOP_K, n), F32),
            jax.ShapeDtypeStruct((TOP_K, n), jnp.int32),
            jax.ShapeDtypeStruct((N_EXPERTS, LANES), F32),
        ],
        compiler_params=_cparams(("arbitrary",)),
        name="outproj_router",
    )(x2, yr, yf, wo_r, wo_f, g, rwt, rb)


def _w1_split_kernel(w_ref, perm_ref, g_ref, l_ref):
    half = MXU_WIDTH // 2
    perm = perm_ref[...]
    for grp in range(w_ref.shape[2] // MXU_WIDTH):
        blk = w_ref[0, :, grp * MXU_WIDTH:(grp + 1) * MXU_WIDTH].astype(BF16)
        r = _dot(blk, perm)
        g_ref[0, :, grp * half:(grp + 1) * half] = r[:, :half].astype(BF16)
        l_ref[0, :, grp * half:(grp + 1) * half] = r[:, half:].astype(BF16)


def _w1_split(w1, tr):
    e, d, two_f = w1.shape
    half = MXU_WIDTH // 2
    src = jnp.arange(MXU_WIDTH, dtype=jnp.int32)[:, None]
    dst = jnp.arange(MXU_WIDTH, dtype=jnp.int32)[None, :]
    perm = (src == jnp.where(dst < half, 2 * dst, 2 * (dst - half) + 1)).astype(BF16)
    out_sds = jax.ShapeDtypeStruct((e, d, two_f // 2), BF16)
    return pl.pallas_call(
        _w1_split_kernel,
        grid=(e, d // tr),
        in_specs=[
            pl.BlockSpec((1, tr, two_f), lambda i, j: (i, j, 0)),
            pl.BlockSpec((MXU_WIDTH, MXU_WIDTH), lambda i, j: (0, 0)),
        ],
        out_specs=[pl.BlockSpec((1, tr, two_f // 2), lambda i, j: (i, j, 0))] * 2,
        out_shape=[out_sds, out_sds],
        compiler_params=_cparams(("parallel", "parallel")),
        name="w1_split",
    )(w1, perm)


def _expert_kernel(be_ref, tok_a_ref, tok_b_ref, tok_a_next_ref, dst_b_prev_ref, dst_a_ref,
                   dst_b_ref, h_hbm, w1g_a, w1l_a, b1g_a, b1l_a, w2_a, b2_a,
                   w1g_b, w1l_b, b1g_b, b1l_b, w2_b, b2_b,
                   y_hbm, xbuf_a, xbuf_b, obuf_a, obuf_b, xrows, gsem, osem, *, bm, n_real_rows):
    del be_ref
    i = pl.program_id(0)

    def gather_start(idx_ref, xbuf, sem, r, priority=0):
        pltpu.make_async_copy(h_hbm.at[idx_ref[0, 0, r]], xbuf.at[r], sem).start(priority=priority)

    def scatter_start(idx_ref, obuf, sem, r, priority=0):
        pltpu.make_async_copy(obuf.at[r], y_hbm.at[idx_ref[0, 0, r]], sem).start(priority=priority)

    def rows_wait(buf, sem):
        rows = buf.at[pl.ds(0, bm)]
        pltpu.make_async_copy(rows, rows, sem).wait()

    def mlp(xbuf, w1g_ref, w1l_ref, b1g_ref, b1l_ref, w2_ref, b2_ref, start_gather, next_xbuf):
        xrows[...] = xbuf[pl.ds(0, bm), 0, :]
        xb = xrows[...].astype(BF16)
        dff = w1g_ref.shape[2]
        n_pieces = dff // MXU_WIDTH
        per_piece = -(-bm // (n_pieces - 1))
        acts = []
        for piece in range(n_pieces):
            cs = slice(piece * MXU_WIDTH, (piece + 1) * MXU_WIDTH)
            for r in range(piece * per_piece, min((piece + 1) * per_piece, bm)):
                start_gather(r)
            zero = next_xbuf[bm, :, cs]
            glu = _dot(xb, w1g_ref[0, :, cs]) + (b1g_ref[0, :, cs] + zero)
            lin = _dot(xb, w1l_ref[0, :, cs]) + b1l_ref[0, :, cs]
            glu = jnp.minimum(glu, SWIGLU_LIMIT)
            lin = jnp.clip(lin, -SWIGLU_LIMIT, SWIGLU_LIMIT)
            acts.append((glu * _sigmoid(SWIGLU_ALPHA * glu) * (lin + 1.0)).astype(BF16))
        return _dot(jnp.concatenate(acts, axis=1), w2_ref[0]) + b2_ref[0]

    @pl.when(i == 0)
    def _():
        obuf_a[...] = jnp.zeros_like(obuf_a)
        obuf_b[...] = jnp.zeros_like(obuf_b)
        spare = jnp.zeros((SUBLANES, 1, xbuf_a.shape[2]), F32)
        xbuf_a[pl.ds(bm, SUBLANES)] = spare
        xbuf_b[pl.ds(bm, SUBLANES)] = spare

        def first(r, carry):
            pltpu.make_async_copy(obuf_a.at[r], y_hbm.at[n_real_rows + 2 * bm + r],
                                  osem.at[0]).start()
            gather_start(tok_a_ref, xbuf_a, gsem.at[0], r)
            return carry

        lax.fori_loop(0, bm, first, 0)

    def gather_b(r):
        gather_start(tok_b_ref, xbuf_b, gsem.at[1], r, r % 2)

    def gather_a_next(r):
        gather_start(tok_a_next_ref, xbuf_a, gsem.at[0], r, r % 2)

    for r in range(bm):
        scatter_start(dst_b_prev_ref, obuf_b, osem.at[1], r, r % 2)
    rows_wait(xbuf_a, gsem.at[0])
    rows_wait(obuf_a, osem.at[0])
    obuf_a[:, 0, :] = mlp(xbuf_a, w1g_a, w1l_a, b1g_a, b1l_a, w2_a, b2_a, gather_b, xbuf_b)

    for r in range(bm):
        scatter_start(dst_a_ref, obuf_a, osem.at[0], r, r % 2)
    rows_wait(xbuf_b, gsem.at[1])
    rows_wait(obuf_b, osem.at[1])
    obuf_b[:, 0, :] = mlp(xbuf_b, w1g_b, w1l_b, b1g_b, b1l_b, w2_b, b2_b, gather_a_next, xbuf_a)

    @pl.when(i == pl.num_programs(0) - 1)
    def _():
        def last(r, carry):
            scatter_start(dst_b_ref, obuf_b, osem.at[1], r)
            return carry

        lax.fori_loop(0, bm, last, 0)
        rows_wait(obuf_b, osem.at[1])
        rows_wait(obuf_a, osem.at[0])
        rows_wait(xbuf_a, gsem.at[0])


def _expert_mlp(block_e, tok_blocks, dst_blocks, h2, w1g, w1l, b1g, b1l, w2, b2, bm):
    n_blocks = tok_blocks.shape[0]
    assert n_blocks % 2 == 0
    n, _, d = h2.shape
    dff = w1g.shape[2]
    n_real_rows = TOP_K * n
    idx_spec = lambda fn: pl.BlockSpec((1, 1, bm), fn, memory_space=pltpu.SMEM)

    def weight_specs(which):
        wmap = lambda i, be: (be[2 * i + which], 0, 0)
        return [
            pl.BlockSpec((1, d, dff), wmap),
            pl.BlockSpec((1, d, dff), wmap),
            pl.BlockSpec((1, 1, dff), wmap),
            pl.BlockSpec((1, 1, dff), wmap),
            pl.BlockSpec((1, dff, d), wmap),
            pl.BlockSpec((1, 1, d), wmap),
        ]

    grid_spec = pltpu.PrefetchScalarGridSpec(
        num_scalar_prefetch=1,
        grid=(n_blocks // 2,),
        in_specs=[
            idx_spec(lambda i, be: (2 * i, 0, 0)),
            idx_spec(lambda i, be: (2 * i + 1, 0, 0)),
            idx_spec(lambda i, be: (jnp.minimum(2 * i + 2, n_blocks - 1), 0, 0)),
            idx_spec(lambda i, be: (2 * i, 0, 0)),
            idx_spec(lambda i, be: (2 * i + 1, 0, 0)),
            idx_spec(lambda i, be: (2 * i + 2, 0, 0)),
            pl.BlockSpec(memory_space=pl.ANY),
        ] + weight_specs(0) + weight_specs(1),
        out_specs=pl.BlockSpec(memory_space=pl.ANY),
        scratch_shapes=[
            pltpu.VMEM((bm + SUBLANES, 1, d), F32),
            pltpu.VMEM((bm + SUBLANES, 1, d), F32),
            pltpu.VMEM((bm, 1, d), F32),
            pltpu.VMEM((bm, 1, d), F32),
            pltpu.VMEM((bm, d), F32),
            pltpu.SemaphoreType.DMA((2,)),
            pltpu.SemaphoreType.DMA((2,)),
        ],
    )
    weights = (w1g, w1l, b1g, b1l, w2, b2)
    return pl.pallas_call(
        functools.partial(_expert_kernel, bm=bm, n_real_rows=n_real_rows),
        grid_spec=grid_spec,
        out_shape=jax.ShapeDtypeStruct((n_real_rows + 3 * bm, 1, d), F32),
        compiler_params=_cparams(("arbitrary",)),
        name="expert_mlp",
    )(block_e, tok_blocks, tok_blocks, tok_blocks, dst_blocks, dst_blocks, dst_blocks, h2,
      *weights, *weights)


def _combine_kernel(gate_ref, x1_ref, g_ref, y0_ref, y1_ref, y2_ref, y3_ref, o_ref):
    gates = gate_ref[...]
    y = x1_ref[...]
    for kk, y_ref in enumerate((y0_ref, y1_ref, y2_ref, y3_ref)):
        y = y + y_ref[:, 0, :] * gates[:, kk:kk + 1]
    o_ref[...] = y * lax.rsqrt(jnp.mean(y * y, axis=-1, keepdims=True) + RMS_EPS) * g_ref[...]


def _combine(gates_t, x1, g, y_all, tc):
    n, d = x1.shape
    tiles = n // tc
    y_spec = lambda kk: pl.BlockSpec((tc, 1, d), lambda i: (kk * tiles + i, 0, 0))
    return pl.pallas_call(
        _combine_kernel,
        grid=(tiles,),
        in_specs=[
            pl.BlockSpec((tc, TOP_K), lambda i: (i, 0)),
            pl.BlockSpec((tc, d), lambda i: (i, 0)),
            pl.BlockSpec((1, d), lambda i: (0, 0)),
        ] + [y_spec(kk) for kk in range(TOP_K)],
        out_specs=pl.BlockSpec((tc, d), lambda i: (i, 0)),
        out_shape=jax.ShapeDtypeStruct((n, d), F32),
        compiler_params=_cparams(("parallel",)),
        name="combine",
    )(gates_t, x1, g, y_all, y_all, y_all, y_all)


def _slot_sources_kernel(gaps_ref, pos_ref, out_ref, *, chunk):
    i = pl.program_id(0)

    @pl.when(i == 0)
    def _():
        def init(p, carry):
            out_ref[p] = -1
            return carry

        for e in range(gaps_ref.shape[1]):
            lax.fori_loop(gaps_ref[0, e], gaps_ref[1, e], init, 0)

    base = i * chunk

    def place(s, carry):
        out_ref[pos_ref[0, 0, s]] = base + s
        return carry

    lax.fori_loop(0, chunk, place, 0, unroll=16)


def _slot_sources(pos, gaps, n_pad):
    n_slots = pos.shape[0]
    chunk = _pick(n_slots, 8192)
    return pl.pallas_call(
        functools.partial(_slot_sources_kernel, chunk=chunk),
        grid=(n_slots // chunk,),
        in_specs=[
            pl.BlockSpec(memory_space=pltpu.SMEM),
            pl.BlockSpec((1, 1, chunk), lambda i: (i, 0, 0), memory_space=pltpu.SMEM),
        ],
        out_specs=pl.BlockSpec(memory_space=pltpu.SMEM),
        out_shape=jax.ShapeDtypeStruct((n_pad,), jnp.int32),
        compiler_params=_cparams(("arbitrary",)),
        name="slot_sources",
    )(gaps, pos.reshape(n_slots // chunk, 1, chunk))


def _dispatch_plan(idx, rank, counts, bm):
    n = idx.shape[1]
    n_slots = TOP_K * n
    sizes = counts[:, 0].astype(jnp.int32)
    padded = (sizes + bm - 1) // bm * bm
    pad_ends = jnp.cumsum(padded)
    pad_starts = pad_ends - padded
    experts = jnp.arange(N_EXPERTS, dtype=jnp.int32)
    start_of = jnp.sum(jnp.where(idx[..., None] == experts, pad_starts, 0), axis=-1)
    pos = (start_of + rank).reshape(-1)
    n_pad = n_slots + N_EXPERTS * bm
    n_blocks = n_pad // bm
    gap_lo = jnp.concatenate([pad_starts + sizes, pad_ends[-1:]])
    gap_hi = jnp.concatenate([pad_ends, jnp.full((1,), n_pad, jnp.int32)])
    slot_src = _slot_sources(pos, jnp.stack([gap_lo, gap_hi]).astype(jnp.int32), n_pad)
    p = jnp.arange(n_pad, dtype=jnp.int32)
    spare = n_slots + (p // bm) % 2 * bm + p % bm
    real = slot_src >= 0
    tok_blocks = jnp.where(real, slot_src % n, 0).reshape(n_blocks, 1, bm)
    dst = jnp.where(real, slot_src, spare)
    dst_blocks = jnp.concatenate([spare[bm:2 * bm], dst]).reshape(n_blocks + 1, 1, bm)
    block_start = jnp.arange(n_blocks, dtype=jnp.int32) * bm
    block_e = jnp.minimum(jnp.sum(pad_ends[None, :] <= block_start[:, None], axis=1),
                          N_EXPERTS - 1).astype(jnp.int32)
    return tok_blocks, dst_blocks, block_e


def _block_diag_ones():
    hid = jnp.arange(MXU_WIDTH, dtype=jnp.int32) // HEAD_DIM
    return (hid[:, None] == hid[None, :]).astype(BF16)


def _pick(n, pref):
    return pref if n % pref == 0 else n


def kernel(x, attn_norm_g, w_in, rwkv_mu, rwkv_w0, rwkv_w_up, rwkv_a0, rwkv_a_up, rwkv_g_up,
           rwkv_k_k, rwkv_k_a, rwkv_r_k, rwkv_ln_w, rwkv_ln_b, fox_f_bias, fox_out_g, w_out,
           ffn_norm_g, router_w, router_b, expert_w1, expert_b1, expert_w2, expert_b2,
           final_norm_g):
    bsz, t, d = x.shape
    n = bsz * t
    depth = w_in.shape[0]
    assert depth == 1, "the final norm is fused into the last stage of a single layer"
    bd = _block_diag_ones()
    x2 = x.reshape(n, d)
    for l in range(depth):
        w_l = w_in[l]
        w_r = w_l[:, :RWKV_IN].astype(BF16)
        w_qkv = w_l[:, RWKV_IN:RWKV_IN + 3 * GROUP_WIDTH].astype(BF16)
        w_qt = w_qkv[:, :GROUP_WIDTH].T
        w_k = w_qkv[:, GROUP_WIDTH:2 * GROUP_WIDTH]
        w_vt = w_qkv[:, 2 * GROUP_WIDTH:].T
        w_f = jnp.pad(w_l[:, RWKV_IN + 3 * GROUP_WIDTH:], ((0, 0), (0, LANES - N_HEADS))).astype(BF16)
        fb_pad = jnp.pad(fox_f_bias[l], (0, LANES - N_HEADS)).reshape(1, LANES)
        wup_pad = jnp.pad(rwkv_w_up[l], ((0, LANES - DECAY_LORA), (0, 0))).astype(BF16)
        aup_pad = jnp.pad(rwkv_a_up[l], ((DECAY_LORA, 0), (0, 0))).astype(BF16)
        gup = rwkv_g_up[l].astype(BF16)
        vec = lambda a: a.reshape(1, -1)

        u_r, qt5, k, vt5, fl = _inproj(x2, vec(attn_norm_g[l]), w_r, w_qt, w_k, w_vt, w_f,
                                       bsz, ATTN_TILE)
        cext, cend = _fox_gate(fl.reshape(bsz, t, LANES), fb_pad, _gate_piece_selectors(),
                               ATTN_TILE)
        y_rwkv = _rwkv_mix(
            u_r.reshape(bsz, t, RWKV_IN), vec(rwkv_mu[l]), vec(rwkv_w0[l]), wup_pad,
            vec(rwkv_a0[l]), aup_pad, gup, vec(rwkv_k_k[l]), vec(rwkv_k_a[l]), vec(rwkv_r_k[l]),
            vec(rwkv_ln_w[l]), vec(rwkv_ln_b[l]), bd, RWKV_CHUNKS_PER_STEP)
        y_fox = _fox_attn(qt5, k.reshape(bsz, t, GROUP_WIDTH), cext, vt5, cend,
                          vec(fox_out_g[l]), ATTN_TILE)

        wo = w_out[l].astype(BF16)
        x1, h2, idx, gates, rank, counts = _outproj_router(
            x2, y_rwkv.reshape(n, GROUP_WIDTH), y_fox.reshape(n, GROUP_WIDTH),
            wo[:GROUP_WIDTH], wo[GROUP_WIDTH:], vec(ffn_norm_g[l]),
            router_w[l].T, router_b[l].reshape(N_EXPERTS, 1), _pick(n, 512))

        bm = 256
        tok_blocks, dst_blocks, block_e = _dispatch_plan(idx, rank, counts, bm)
        w1g, w1l = _w1_split(expert_w1[l], 512)
        b1 = expert_b1[l]
        b1g = b1[:, None, 0::2]
        b1l = b1[:, None, 1::2]
        y_all = _expert_mlp(block_e, tok_blocks, dst_blocks, h2, w1g, w1l, b1g, b1l,
                            expert_w2[l].astype(BF16), expert_b2[l][:, None, :], bm)
        x2 = _combine(gates.T, x1, vec(final_norm_g), y_all, _pick(n, 256))
    return x2.reshape(bsz, t, d)
```

```python
import functools

import jax
import jax.numpy as jnp
from jax import lax
from jax.experimental import pallas as pl
from jax.experimental.pallas import tpu as pltpu

F32 = jnp.float32
BF16 = jnp.bfloat16

HEAD_DIM = 64
N_HEADS = 8
GROUP_WIDTH = N_HEADS * HEAD_DIM
DECAY_LORA = 64
AAA_LORA = 64
GATE_LORA = 128
RWKV_IN = 3 * GROUP_WIDTH + DECAY_LORA + AAA_LORA + GATE_LORA
LORA_OFF = 3 * GROUP_WIDTH
N_EXPERTS = 32
TOP_K = 4
SWIGLU_ALPHA = 1.702
SWIGLU_LIMIT = 7.0
RMS_EPS = 1e-5
RWKV_GN_EPS = 64e-5
LANES = 128
SUBLANES = 8
MXU_WIDTH = 256
RWKV_CHUNK = 64
RWKV_CHUNKS_PER_STEP = 8
ATTN_TILE = 512
ROUTER_TILE = 512
EXPERT_BLOCK = 256
W1_SPLIT_ROWS = 512
COMBINE_TILE = 256
SLOT_CHUNK = 8192
LOG2E = 1.4426950408889634
Q_SCALE = HEAD_DIM ** -0.5 * LOG2E
ZERO_PROB_EXP = -152.0
NORM_SLACK = 1.0 + 2.0 ** -6
VMEM_LIMIT = 56 * 1024 * 1024


def _cparams(semantics):
    return pltpu.CompilerParams(dimension_semantics=semantics, vmem_limit_bytes=VMEM_LIMIT)


def _dot(a, b):
    return jnp.dot(a, b, preferred_element_type=F32)


def _dot_nt(a, b):
    return lax.dot_general(a, b, (((1,), (1,)), ((), ())), preferred_element_type=F32)


def _dot_tn(a, b):
    return lax.dot_general(a, b, (((0,), (0,)), ((), ())), preferred_element_type=F32)


def _split3(x):
    hi = x.astype(BF16)
    r1 = x - hi.astype(F32)
    mid = r1.astype(BF16)
    lo = (r1 - mid.astype(F32)).astype(BF16)
    return hi, mid, lo


def _dot_exact_lhs(a_bf16, x):
    hi, mid, lo = _split3(x)
    return _dot(a_bf16, hi) + _dot(a_bf16, mid) + _dot(a_bf16, lo)


def _head_sums(x, same_head):
    w = same_head.shape[0]
    parts = []
    for g in range(x.shape[1] // w):
        xs = x[:, g * w:(g + 1) * w]
        hi = xs.astype(BF16)
        lo = (xs - hi.astype(F32)).astype(BF16)
        parts.append(_dot(hi, same_head) + _dot(lo, same_head))
    return jnp.concatenate(parts, axis=1)


def _softplus(z):
    return jnp.maximum(z, 0.0) + jnp.log1p(jnp.exp(-jnp.abs(z)))


def _sigmoid(z):
    return 1.0 / (1.0 + jnp.exp(-z))


def _inproj_kernel(x_ref, g_ref, wr_ref, wqt_ref, wk_ref, wvt_ref, wf_ref,
                   ur_ref, qt_ref, k_ref, vt_ref, fl_ref):
    x = x_ref[...]
    h = x * lax.rsqrt(jnp.mean(x * x, axis=-1, keepdims=True) + RMS_EPS) * g_ref[...]
    hb = h.astype(BF16)
    ur_ref[...] = _dot(hb, wr_ref[...])
    k_ref[...] = _dot(hb, wk_ref[...]).astype(BF16)
    fl_ref[...] = _dot(hb, wf_ref[...])
    qt = (_dot_nt(wqt_ref[...], hb) * Q_SCALE).astype(BF16)
    vt = _dot_nt(wvt_ref[...], hb).astype(BF16)
    for p in range(N_HEADS // 2):
        qt_ref[0, p, 0] = qt[p * LANES:(p + 1) * LANES]
        vt_ref[0, p, 0] = vt[p * LANES:(p + 1) * LANES]


def _inproj(x2, g, w_r, w_qt, w_k, w_vt, w_f, bsz, tm):
    n, d = x2.shape
    nt = n // bsz // tm
    pairs = N_HEADS // 2
    const = lambda i: (0, 0)
    row = lambda i: (i, 0)
    fm = lambda i: (i // nt, 0, i % nt, 0, 0)
    fm_sds = jax.ShapeDtypeStruct((bsz, pairs, nt, LANES, tm), BF16)
    return pl.pallas_call(
        _inproj_kernel,
        grid=(n // tm,),
        in_specs=[
            pl.BlockSpec((tm, d), row),
            pl.BlockSpec((1, d), const),
            pl.BlockSpec(w_r.shape, const),
            pl.BlockSpec(w_qt.shape, const),
            pl.BlockSpec(w_k.shape, const),
            pl.BlockSpec(w_vt.shape, const),
            pl.BlockSpec(w_f.shape, const),
        ],
        out_specs=[
            pl.BlockSpec((tm, RWKV_IN), row),
            pl.BlockSpec((1, pairs, 1, LANES, tm), fm),
            pl.BlockSpec((tm, GROUP_WIDTH), row),
            pl.BlockSpec((1, pairs, 1, LANES, tm), fm),
            pl.BlockSpec((tm, LANES), row),
        ],
        out_shape=[
            jax.ShapeDtypeStruct((n, RWKV_IN), F32),
            fm_sds,
            jax.ShapeDtypeStruct((n, GROUP_WIDTH), BF16),
            fm_sds,
            jax.ShapeDtypeStruct((n, LANES), F32),
        ],
        compiler_params=_cparams(("parallel",)),
        name="inproj",
    )(x2, g, w_r, w_qt, w_k, w_vt, w_f)


def _fox_gate_kernel(fl_ref, fb_ref, sel_ref, c_ref, cend_ref, carry):
    tt = fl_ref.shape[1]

    @pl.when(pl.program_id(1) == 0)
    def _():
        carry[...] = jnp.zeros_like(carry)

    z = fl_ref[0] + fb_ref[...]
    log_f = jnp.minimum(z, 0.0) - jnp.log1p(jnp.exp(-jnp.abs(z)))
    ri = lax.broadcasted_iota(jnp.int32, (tt, tt), 0)
    ci = lax.broadcasted_iota(jnp.int32, (tt, tt), 1)
    tri = jnp.where(ri >= ci, 1.0, 0.0).astype(BF16)
    c = _dot_exact_lhs(tri, log_f) + carry[...]
    carry[...] = c[tt - 1:tt, :]
    cend_ref[0, 0] = c[tt - 1:tt, :]
    hi, mid, lo = _split3(c * LOG2E)
    c_ref[0] = (_dot(hi, sel_ref[0]) + _dot(mid, sel_ref[1]) + _dot(lo, sel_ref[2])).astype(BF16)


def _gate_piece_selectors():
    h = jnp.arange(LANES, dtype=jnp.int32)[:, None]
    col = jnp.arange(GROUP_WIDTH, dtype=jnp.int32)[None, :]
    sels = []
    for m in range(3):
        target = LANES * (h // 2) + 3 * (h % 2) + m
        sels.append(((col == target) & (h < N_HEADS)).astype(BF16))
    return jnp.stack(sels)


def _fox_gate(fl3, fb_pad, sel, tt):
    b, t, _ = fl3.shape
    return pl.pallas_call(
        _fox_gate_kernel,
        grid=(b, t // tt),
        in_specs=[
            pl.BlockSpec((1, tt, LANES), lambda i, j: (i, j, 0)),
            pl.BlockSpec((1, LANES), lambda i, j: (0, 0)),
            pl.BlockSpec((3, LANES, GROUP_WIDTH), lambda i, j: (0, 0, 0)),
        ],
        out_specs=[
            pl.BlockSpec((1, tt, GROUP_WIDTH), lambda i, j: (i, j, 0)),
            pl.BlockSpec((1, 1, 1, LANES), lambda i, j: (i, j, 0, 0)),
        ],
        out_shape=[
            jax.ShapeDtypeStruct((b, t, GROUP_WIDTH), BF16),
            jax.ShapeDtypeStruct((b, t // tt, 1, LANES), F32),
        ],
        scratch_shapes=[pltpu.VMEM((1, LANES), F32)],
        compiler_params=_cparams(("parallel", "arbitrary")),
        name="fox_gate",
    )(fl3, fb_pad, sel)


def _rwkv_prep_body(u_ref, mu_ref, w0_ref, wup_ref, a0_ref, aup_ref, gup_ref, kk_ref, ka_ref,
                      bd_ref, r_out, k_out, v_out, lw_out, kk_out, b_out, g_out, carry):
    tt = u_ref.shape[1]

    @pl.when(pl.program_id(1) == 0)
    def _():
        carry[...] = jnp.zeros_like(carry)

    u = u_ref[0]
    prev = pltpu.roll(u, 1, axis=0)
    row = lax.broadcasted_iota(jnp.int32, u.shape, 0)
    prev = jnp.where(row == 0, carry[...], prev)
    carry[...] = u[tt - 1:tt, :]
    us = u + (prev - u) * mu_ref[...]

    r = us[:, :GROUP_WIDTH]
    k = us[:, GROUP_WIDTH:2 * GROUP_WIDTH]
    v = us[:, 2 * GROUP_WIDTH:LORA_OFF]
    wa = us[:, LORA_OFF:LORA_OFF + LANES]
    gl = us[:, LORA_OFF + LANES:]

    w_lin = _dot(jnp.tanh(wa).astype(BF16), wup_ref[...])
    a_lin = _dot(wa.astype(BF16), aup_ref[...])
    w = -_softplus(-(w0_ref[...] + w_lin)) - 0.5
    lw_out[0] = -jnp.exp(w)
    a = _sigmoid(a0_ref[...] + a_lin)
    g_out[0] = _dot(_sigmoid(gl).astype(BF16), gup_ref[...])

    kkr = k * kk_ref[...]
    ss = _head_sums(kkr * kkr, bd_ref[...])
    kk = kkr / jnp.maximum(jnp.sqrt(ss), 1e-12)
    r_out[0] = r
    k_out[0] = k * (1.0 + (a - 1.0) * ka_ref[...])
    v_out[0] = v
    kk_out[0] = kk
    b_out[0] = kk * a


def _rwkv_scan_body(r_ref, k_ref, v_ref, lw_ref, kk_ref, b_ref, g_ref, rk_ref, lnw_ref, lnb_ref,
                      bd_ref, o_ref, s_scr, *, n_chunks):
    c = RWKV_CHUNK
    w = MXU_WIDTH
    hpg = w // HEAD_DIM
    n_groups = GROUP_WIDTH // w

    @pl.when(pl.program_id(1) == 0)
    def _():
        s_scr[...] = jnp.zeros_like(s_scr)

    row = lax.broadcasted_iota(jnp.int32, (c, w), 0)
    u = lax.broadcasted_iota(jnp.int32, (c, w), 1) % HEAD_DIM
    strict = row > u
    incl = row >= u
    eye = jnp.where(row == u, 1.0, 0.0)
    level_masks = []
    s = 1
    while s < c:
        same = (row // (2 * s)) == (u // (2 * s))
        level_masks.append(same & ((row % (2 * s)) >= s) & ((u % (2 * s)) < s))
        s *= 2
    same_head = (lax.broadcasted_iota(jnp.int32, (w, w), 0) // HEAD_DIM
                 == lax.broadcasted_iota(jnp.int32, (w, w), 1) // HEAD_DIM)
    tri = jnp.where(lax.broadcasted_iota(jnp.int32, (c, c), 0)
                    >= lax.broadcasted_iota(jnp.int32, (c, c), 1), 1.0, 0.0).astype(BF16)

    def bdiag(x):
        xb = x.astype(BF16)
        tiled = jnp.concatenate([xb] * hpg, axis=0)
        return jnp.where(same_head, tiled, jnp.zeros_like(tiled))

    chains = [(ci, gi) for ci in range(n_chunks) for gi in range(n_groups)]
    lhs, rk_t, vbs, xcat, xneg, gam, rt32 = {}, {}, {}, {}, {}, {}, {}
    for ci in range(n_chunks):
        rs = slice(ci * c, (ci + 1) * c)
        r = r_ref[0, rs, :]
        k = k_ref[0, rs, :]
        lw = lw_ref[0, rs, :]
        kk = kk_ref[0, rs, :]
        b = b_ref[0, rs, :]
        g_cum = _dot_exact_lhs(tri, lw)
        g_last = g_cum[c - 1:c, :]
        r_t = r * jnp.exp(g_cum)
        kk_t = kk * jnp.exp(g_cum - lw)
        e_neg = jnp.exp(-g_cum)
        b_n = b * e_neg
        k_n = k * e_neg
        e_end = jnp.exp(g_last - g_cum)
        b_e = (b * e_end).astype(BF16)
        k_e = (k * e_end).astype(BF16)
        gamma = jnp.exp(g_last)
        vb = v_ref[0, rs, :].astype(BF16)
        for gi in range(n_groups):
            gs = slice(gi * w, (gi + 1) * w)
            ch = (ci, gi)
            lhs[ch] = jnp.concatenate([r_t[:, gs].astype(BF16), kk_t[:, gs].astype(BF16)], axis=0)
            rk_t[ch] = (jnp.concatenate([bdiag(b_n[:, gs]), bdiag(k_n[:, gs])], axis=0),
                        bdiag(kk_t[:, gs]))
            vbs[ch] = vb[:, gs]
            xcat[ch] = jnp.concatenate([b_e[:, gs], k_e[:, gs]], axis=0)
            xneg[ch] = jnp.concatenate([-b_e[:, gs], k_e[:, gs]], axis=0)
            gam[ch] = gamma[:, gs]
            rt32[ch] = r_t[:, gs]

    p = {ch: _dot_nt(lhs[ch], rk_t[ch][0]) for ch in chains}
    l_b = {ch: jnp.where(strict, p[ch][c:, :w], 0.0) for ch in chains}
    l_k = {ch: jnp.where(strict, p[ch][c:, w:], 0.0).astype(BF16) for ch in chains}
    p_br = {ch: jnp.where(incl, p[ch][:c, :w], 0.0).astype(BF16) for ch in chains}
    p_kr = {ch: jnp.where(incl, p[ch][:c, w:], 0.0).astype(BF16) for ch in chains}
    v_bd = {ch: bdiag(vbs[ch]) for ch in chains}
    lkv = {ch: _dot(l_k[ch], v_bd[ch]) for ch in chains}

    t_inv = {ch: eye - jnp.where(level_masks[0], l_b[ch], 0.0) for ch in chains}
    for m in level_masks[1:]:
        tb = {ch: t_inv[ch].astype(BF16) for ch in chains}
        ct = {ch: _dot(jnp.where(m, l_b[ch], 0.0).astype(BF16), bdiag(tb[ch])) for ch in chains}
        t_inv = {ch: t_inv[ch] - _dot(tb[ch], bdiag(ct[ch])) for ch in chains}

    mm = {ch: _dot(t_inv[ch].astype(BF16),
                   jnp.concatenate([rk_t[ch][1], bdiag(lkv[ch])], axis=1)) for ch in chains}
    pm = {ch: _dot(p_br[ch], jnp.concatenate([bdiag(mm[ch][:, :w]), bdiag(mm[ch][:, w:])], axis=1))
          for ch in chains}
    pkv = {ch: _dot(p_kr[ch], v_bd[ch]) for ch in chains}
    n1 = {ch: (rt32[ch] - pm[ch][:, :w]).astype(BF16) for ch in chains}
    n2 = {ch: pkv[ch] - pm[ch][:, w:] for ch in chains}
    omega = {ch: jnp.where(same_head, _dot_tn(mm[ch][:, :w].astype(BF16), xcat[ch][:c]), 0.0)
             .astype(BF16) for ch in chains}
    psi = {ch: jnp.where(same_head, _dot_tn(
        jnp.concatenate([mm[ch][:, w:].astype(BF16), vbs[ch]], axis=0), xneg[ch]), 0.0)
           for ch in chains}

    state = [s_scr[gi] for gi in range(n_groups)]
    ys = []
    for ci in range(n_chunks):
        sb = [state[gi].astype(BF16) for gi in range(n_groups)]
        ys.append(jnp.concatenate(
            [_dot_nt(n1[(ci, gi)], sb[gi]) + n2[(ci, gi)] for gi in range(n_groups)], axis=1))
        state = [state[gi] * gam[(ci, gi)] - _dot(sb[gi], omega[(ci, gi)]) + psi[(ci, gi)]
                 for gi in range(n_groups)]
    for gi in range(n_groups):
        s_scr[gi] = state[gi]

    y = jnp.concatenate(ys, axis=0)
    r = r_ref[0]
    k = k_ref[0]
    v = v_ref[0]
    bd = bd_ref[...]
    inv_n = 1.0 / HEAD_DIM
    mean = _head_sums(y, bd) * inv_n
    d = y - mean
    var = _head_sums(d * d, bd) * inv_n
    yn = d * lax.rsqrt(var + RWKV_GN_EPS) * lnw_ref[...] + lnb_ref[...]
    bonus = _head_sums(r * k * rk_ref[...], bd) * v
    o_ref[0] = ((yn + bonus) * g_ref[0]).astype(o_ref.dtype)


def _rwkv_kernel(u_ref, mu_ref, w0_ref, wup_ref, a0_ref, aup_ref, gup_ref, kkw_ref, ka_ref,
                 rk_ref, lnw_ref, lnb_ref, bd_ref, o_ref, carry, s_scr,
                 r_s, k_s, v_s, lw_s, kk_s, b_s, g_s, *, n_chunks):
    _rwkv_prep_body(u_ref, mu_ref, w0_ref, wup_ref, a0_ref, aup_ref, gup_ref, kkw_ref, ka_ref,
                    bd_ref, r_s, k_s, v_s, lw_s, kk_s, b_s, g_s, carry)
    _rwkv_scan_body(r_s, k_s, v_s, lw_s, kk_s, b_s, g_s, rk_ref, lnw_ref, lnb_ref, bd_ref,
                    o_ref, s_scr, n_chunks=n_chunks)


def _rwkv_mix(u3, mu, w0, wup_pad, a0, aup_pad, gup, k_k, k_a, r_k, ln_w, ln_b, bd, n_chunks):
    bsz, t, _ = u3.shape
    rows = RWKV_CHUNK * n_chunks
    const = lambda i, j: (0, 0)
    tile = lambda i, j: (i, j, 0)
    vec = pl.BlockSpec((1, GROUP_WIDTH), const)
    staged = pltpu.VMEM((1, rows, GROUP_WIDTH), F32)
    return pl.pallas_call(
        functools.partial(_rwkv_kernel, n_chunks=n_chunks),
        grid=(bsz, t // rows),
        in_specs=[
            pl.BlockSpec((1, rows, RWKV_IN), tile),
            pl.BlockSpec((1, RWKV_IN), const),
            vec,
            pl.BlockSpec((LANES, GROUP_WIDTH), const),
            vec,
            pl.BlockSpec((LANES, GROUP_WIDTH), const),
            pl.BlockSpec((GATE_LORA, GROUP_WIDTH), const),
            vec, vec, vec, vec, vec,
            pl.BlockSpec((MXU_WIDTH, MXU_WIDTH), const),
        ],
        out_specs=pl.BlockSpec((1, rows, GROUP_WIDTH), tile),
        out_shape=jax.ShapeDtypeStruct((bsz, t, GROUP_WIDTH), BF16),
        scratch_shapes=[
            pltpu.VMEM((1, RWKV_IN), F32),
            pltpu.VMEM((GROUP_WIDTH // MXU_WIDTH, MXU_WIDTH, MXU_WIDTH), F32),
        ] + [staged] * 7,
        compiler_params=_cparams(("parallel", "arbitrary")),
        name="rwkv_mix",
    )(u3, mu, w0, wup_pad, a0, aup_pad, gup, k_k, k_a, r_k, ln_w, ln_b, bd)


def _fox_attn_kernel(qt_ref, k_ref, ce_ref, vt_ref, cend_ref, og_ref, o_ref,
                     m_scr, l_scr, acc_scr, kmax_scr, *, t):
    qi = pl.program_id(2)
    n_strips = 2 * t // LANES
    qt = qt_ref[0, 0, 0]
    frow = lax.broadcasted_iota(jnp.int32, (LANES, t), 0)
    zero = jnp.zeros_like(qt)
    main = jnp.concatenate([jnp.where(frow < HEAD_DIM, qt, zero),
                            jnp.where(frow < HEAD_DIM, zero, qt)], axis=1)
    erow = lax.broadcasted_iota(jnp.int32, (LANES, 2 * t), 0)
    ecol = lax.broadcasted_iota(jnp.int32, (LANES, 2 * t), 1)
    off = jnp.where(ecol < t, 0, 3)
    extra = jnp.where((erow >= off) & (erow < off + 3), -1.0, 0.0).astype(BF16)
    q_aug = jnp.concatenate([main, extra], axis=0)

    m_scr[...] = jnp.full_like(m_scr, -jnp.inf)
    l_scr[...] = jnp.zeros_like(l_scr)
    acc_scr[...] = jnp.zeros_like(acc_scr)

    @pl.when(qi == 0)
    def _():
        hid_r = lax.broadcasted_iota(jnp.int32, (LANES, LANES), 0) // HEAD_DIM
        hid_c = lax.broadcasted_iota(jnp.int32, (LANES, LANES), 1) // HEAD_DIM
        same_head = jnp.where(hid_r == hid_c, 1.0, 0.0).astype(BF16)

        def tile_max(j, best):
            kf = k_ref[0, pl.ds(pl.multiple_of(j * t, t), t), :].astype(F32)
            sq = _dot((kf * kf).astype(BF16), same_head)
            return jnp.maximum(best, jnp.max(sq, axis=0, keepdims=True))

        best = lax.fori_loop(0, k_ref.shape[1] // t, tile_max, jnp.zeros((1, LANES), F32))
        kmax_scr[...] = jnp.sqrt(best) * NORM_SLACK

    def step(j, masked):
        start = pl.multiple_of(j * t, t)
        k_aug = jnp.concatenate([k_ref[0, pl.ds(start, t), :], ce_ref[0, pl.ds(start, t), :]],
                                axis=1)
        vt = vt_ref[0, 0, j]
        zt = _dot(k_aug, q_aug)
        m_prev = m_scr[...]
        l_prev = l_scr[...]
        acc_prev = acc_scr[...]
        m_out, l_out, acc_out = [], [], [[], []]
        for s in range(n_strips):
            head = s // (n_strips // 2)
            cs = slice(s * LANES, (s + 1) * LANES)
            z = zt[:, cs]
            if masked:
                key = lax.broadcasted_iota(jnp.int32, (t, LANES), 0)
                qry = lax.broadcasted_iota(jnp.int32, (t, LANES), 1) + (s * LANES) % t
                z = jnp.where(key <= qry, z, -jnp.inf)
            m_new = jnp.maximum(m_prev[:, cs], jnp.max(z, axis=0, keepdims=True))
            alpha = jnp.exp2(m_prev[:, cs] - m_new)
            p = jnp.exp2(z - m_new)
            l_out.append(alpha * l_prev[:, cs] + jnp.sum(p, axis=0, keepdims=True))
            m_out.append(m_new)
            hs = slice(head * HEAD_DIM, (head + 1) * HEAD_DIM)
            qs = slice((s * LANES) % t, (s * LANES) % t + LANES)
            pv = _dot(vt[hs], p.astype(BF16))
            acc_out[head].append(alpha * acc_prev[hs, qs] + pv)
        m_scr[...] = jnp.concatenate(m_out, axis=1)
        l_scr[...] = jnp.concatenate(l_out, axis=1)
        acc_scr[...] = jnp.concatenate([jnp.concatenate(acc_out[0], axis=1),
                                        jnp.concatenate(acc_out[1], axis=1)], axis=0)

    def body(j, carry):
        step(j, False)
        return carry

    step(qi, True)

    qf = qt.astype(F32)
    qsq = qf * qf
    qnorm = jnp.sqrt(jnp.concatenate(
        [jnp.sum(qsq[:HEAD_DIM], axis=0, keepdims=True),
         jnp.sum(qsq[HEAD_DIM:], axis=0, keepdims=True)], axis=1)) * NORM_SLACK
    kmax = kmax_scr[...]
    kmax2 = jnp.concatenate([jnp.broadcast_to(kmax[:, 0:1], (1, t)),
                             jnp.broadcast_to(kmax[:, HEAD_DIM:HEAD_DIM + 1], (1, t))], axis=1)
    slack = qnorm * kmax2 - m_scr[...]
    cend = cend_ref[0][:, 0, :]
    lane = lax.broadcasted_iota(jnp.int32, cend.shape, 1)
    jrow = lax.broadcasted_iota(jnp.int32, (cend.shape[0], 1), 0)
    needed = jrow < 0
    for head in range(2):
        worst = jnp.max(slack[:, head * t:(head + 1) * t], axis=1, keepdims=True)
        c_head = jnp.sum(jnp.where(lane == 2 * pl.program_id(1) + head, cend, 0.0),
                         axis=1, keepdims=True)
        needed = needed | (worst - c_head * LOG2E > ZERO_PROB_EXP)
    first = jnp.min(jnp.where(needed & (jrow < qi), jrow, qi))
    lax.fori_loop(first, qi, body, 0)

    l = l_scr[...]
    acc = acc_scr[...]
    inv_n = 1.0 / HEAD_DIM
    halves = []
    for head in range(2):
        o = acc[head * HEAD_DIM:(head + 1) * HEAD_DIM] / l[:, head * t:(head + 1) * t]
        halves.append(o * lax.rsqrt(jnp.sum(o * o, axis=0, keepdims=True) * inv_n + RMS_EPS))
    o_t = jnp.concatenate(halves, axis=0)
    o_ref[0] = (jnp.transpose(o_t) * og_ref[...]).astype(o_ref.dtype)


def _fox_attn(qt5, k, cext, vt5, cend, out_g, tile):
    bsz, t_all, _ = k.shape
    pairs = N_HEADS // 2
    tiles = t_all // tile
    kern = functools.partial(_fox_attn_kernel, t=tile)
    return pl.pallas_call(
        kern,
        grid=(bsz, pairs, tiles),
        in_specs=[
            pl.BlockSpec((1, 1, 1, LANES, tile), lambda b, p, i: (b, p, i, 0, 0)),
            pl.BlockSpec((1, t_all, LANES), lambda b, p, i: (b, 0, p)),
            pl.BlockSpec((1, t_all, LANES), lambda b, p, i: (b, 0, p)),
            pl.BlockSpec((1, 1, tiles, LANES, tile), lambda b, p, i: (b, p, 0, 0, 0)),
            pl.BlockSpec((1, tiles, 1, LANES), lambda b, p, i: (b, 0, 0, 0)),
            pl.BlockSpec((1, LANES), lambda b, p, i: (0, p)),
        ],
        out_specs=pl.BlockSpec((1, tile, LANES), lambda b, p, i: (b, i, p)),
        out_shape=jax.ShapeDtypeStruct((bsz, t_all, GROUP_WIDTH), BF16),
        scratch_shapes=[
            pltpu.VMEM((1, 2 * tile), F32),
            pltpu.VMEM((1, 2 * tile), F32),
            pltpu.VMEM((LANES, tile), F32),
            pltpu.VMEM((1, LANES), F32),
        ],
        compiler_params=_cparams(("parallel", "parallel", "arbitrary")),
        name="fox_attn",
    )(qt5, k, cext, vt5, cend, out_g)


def _outproj_router_kernel(x_ref, yr_ref, yf_ref, wo_r_ref, wo_f_ref, g_ref, rwt_ref, rb_ref,
                           x1_ref, h_ref, idx_ref, gate_ref, rank_ref, count_ref):
    @pl.when(pl.program_id(0) == 0)
    def _():
        count_ref[...] = jnp.zeros_like(count_ref)

    x1 = x_ref[...] + _dot(yr_ref[...], wo_r_ref[...]) + _dot(yf_ref[...], wo_f_ref[...])
    x1_ref[...] = x1
    h = x1 * lax.rsqrt(jnp.mean(x1 * x1, axis=-1, keepdims=True) + RMS_EPS) * g_ref[...]
    h_ref[:, 0, :] = h
    logits = lax.dot_general(rwt_ref[...], h, (((1,), (1,)), ((), ())),
                             precision=lax.Precision.HIGHEST,
                             preferred_element_type=F32) + rb_ref[...]
    eidx = lax.broadcasted_iota(jnp.int32, logits.shape, 0)
    vals, idxs, picks = [], [], []
    for _ in range(TOP_K):
        m = jnp.max(logits, axis=0, keepdims=True)
        i = jnp.min(jnp.where(logits == m, eidx, N_EXPERTS), axis=0, keepdims=True)
        vals.append(m)
        idxs.append(i)
        picks.append(eidx == i)
        logits = jnp.where(picks[-1], -jnp.inf, logits)
    es = [jnp.exp(val - vals[0]) for val in vals]
    denom = es[0] + es[1] + es[2] + es[3]
    idx_ref[...] = jnp.concatenate(idxs, axis=0)
    gate_ref[...] = jnp.concatenate([e / denom for e in es], axis=0)

    tm = logits.shape[1]
    chosen = [jnp.where(pk, 1.0, 0.0) for pk in picks]
    any_k = chosen[0] + chosen[1] + chosen[2] + chosen[3]
    before = (lax.broadcasted_iota(jnp.int32, (tm, tm), 0)
              < lax.broadcasted_iota(jnp.int32, (tm, tm), 1))
    prefix = _dot(any_k.astype(BF16), jnp.where(before, 1.0, 0.0).astype(BF16))
    seen = count_ref[:, 0:1] + prefix
    rank_ref[...] = jnp.concatenate(
        [jnp.sum(ch * seen, axis=0, keepdims=True) for ch in chosen], axis=0).astype(jnp.int32)
    count_ref[...] = count_ref[...] + jnp.sum(any_k, axis=1, keepdims=True)


def _outproj_router(x2, yr, yf, wo_r, wo_f, g, rwt, rb, tm):
    n, d = x2.shape
    const = lambda i: (0, 0)
    row = lambda i: (i, 0)
    col = lambda i: (0, i)
    return pl.pallas_call(
        _outproj_router_kernel,
        grid=(n // tm,),
        in_specs=[
            pl.BlockSpec((tm, d), row),
            pl.BlockSpec((tm, GROUP_WIDTH), row),
            pl.BlockSpec((tm, GROUP_WIDTH), row),
            pl.BlockSpec((GROUP_WIDTH, d), const),
            pl.BlockSpec((GROUP_WIDTH, d), const),
            pl.BlockSpec((1, d), const),
            pl.BlockSpec((N_EXPERTS, d), const),
            pl.BlockSpec((N_EXPERTS, 1), const),
        ],
        out_specs=[
            pl.BlockSpec((tm, d), row),
            pl.BlockSpec((tm, 1, d), lambda i: (i, 0, 0)),
            pl.BlockSpec((TOP_K, tm), col),
            pl.BlockSpec((TOP_K, tm), col),
            pl.BlockSpec((TOP_K, tm), col),
            pl.BlockSpec((N_EXPERTS, LANES), const),
        ],
        out_shape=[
            jax.ShapeDtypeStruct((n, d), F32),
            jax.ShapeDtypeStruct((n, 1, d), F32),
            jax.ShapeDtypeStruct((TOP_K, n), jnp.int32),
            jax.ShapeDtypeStruct((TOP_K, n), F32),
            jax.ShapeDtypeStruct((TOP_K, n), jnp.int32),
            jax.ShapeDtypeStruct((N_EXPERTS, LANES), F32),
        ],
        compiler_params=_cparams(("arbitrary",)),
        name="outproj_router",
    )(x2, yr, yf, wo_r, wo_f, g, rwt, rb)


def _w1_split_kernel(w_ref, perm_ref, g_ref, l_ref):
    half = MXU_WIDTH // 2
    perm = perm_ref[...]
    for grp in range(w_ref.shape[2] // MXU_WIDTH):
        blk = w_ref[0, :, grp * MXU_WIDTH:(grp + 1) * MXU_WIDTH].astype(BF16)
        r = _dot(blk, perm)
        g_ref[0, :, grp * half:(grp + 1) * half] = r[:, :half].astype(BF16)
        l_ref[0, :, grp * half:(grp + 1) * half] = r[:, half:].astype(BF16)


def _w1_split(w1, tr):
    e, d, two_f = w1.shape
    half = MXU_WIDTH // 2
    src = jnp.arange(MXU_WIDTH, dtype=jnp.int32)[:, None]
    dst = jnp.arange(MXU_WIDTH, dtype=jnp.int32)[None, :]
    perm = (src == jnp.where(dst < half, 2 * dst, 2 * (dst - half) + 1)).astype(BF16)
    out_sds = jax.ShapeDtypeStruct((e, d, two_f // 2), BF16)
    return pl.pallas_call(
        _w1_split_kernel,
        grid=(e, d // tr),
        in_specs=[
            pl.BlockSpec((1, tr, two_f), lambda i, j: (i, j, 0)),
            pl.BlockSpec((MXU_WIDTH, MXU_WIDTH), lambda i, j: (0, 0)),
        ],
        out_specs=[pl.BlockSpec((1, tr, two_f // 2), lambda i, j: (i, j, 0))] * 2,
        out_shape=[out_sds, out_sds],
        compiler_params=_cparams(("parallel", "parallel")),
        name="w1_split",
    )(w1, perm)


def _expert_kernel(be_ref, tok_a_ref, tok_b_ref, tok_a_next_ref, dst_b_prev_ref, dst_a_ref,
                   dst_b_ref, h_hbm, w1g_a, w1l_a, b1g_a, b1l_a, w2_a, b2_a,
                   w1g_b, w1l_b, b1g_b, b1l_b, w2_b, b2_b,
                   y_hbm, xbuf_a, xbuf_b, obuf_a, obuf_b, xrows, gsem, osem, *, bm, n_real_rows):
    del be_ref
    i = pl.program_id(0)

    def gather_start(idx_ref, xbuf, sem, r, priority=0):
        pltpu.make_async_copy(h_hbm.at[idx_ref[0, 0, r]], xbuf.at[r], sem).start(priority=priority)

    def scatter_start(idx_ref, obuf, sem, r, priority=0):
        pltpu.make_async_copy(obuf.at[r], y_hbm.at[idx_ref[0, 0, r]], sem).start(priority=priority)

    def rows_wait(buf, sem):
        rows = buf.at[pl.ds(0, bm)]
        pltpu.make_async_copy(rows, rows, sem).wait()

    def mlp(xbuf, w1g_ref, w1l_ref, b1g_ref, b1l_ref, w2_ref, b2_ref, start_gather, next_xbuf):
        xrows[...] = xbuf[pl.ds(0, bm), 0, :]
        xb = xrows[...].astype(BF16)
        dff = w1g_ref.shape[2]
        n_pieces = dff // MXU_WIDTH
        per_piece = -(-bm // (n_pieces - 2))
        acts = []
        for piece in range(n_pieces):
            cs = slice(piece * MXU_WIDTH, (piece + 1) * MXU_WIDTH)
            for r in range(piece * per_piece, min((piece + 1) * per_piece, bm)):
                start_gather(r)
            zero = next_xbuf[bm, :, cs]
            glu = _dot(xb, w1g_ref[0, :, cs]) + (b1g_ref[0, :, cs] + zero)
            lin = _dot(xb, w1l_ref[0, :, cs]) + b1l_ref[0, :, cs]
            glu = jnp.minimum(glu, SWIGLU_LIMIT)
            lin = jnp.clip(lin, -SWIGLU_LIMIT, SWIGLU_LIMIT)
            acts.append((glu * _sigmoid(SWIGLU_ALPHA * glu) * (lin + 1.0)).astype(BF16))
        return _dot(jnp.concatenate(acts, axis=1), w2_ref[0]) + b2_ref[0]

    @pl.when(i == 0)
    def _():
        obuf_a[...] = jnp.zeros_like(obuf_a)
        obuf_b[...] = jnp.zeros_like(obuf_b)
        spare = jnp.zeros((SUBLANES, 1, xbuf_a.shape[2]), F32)
        xbuf_a[pl.ds(bm, SUBLANES)] = spare
        xbuf_b[pl.ds(bm, SUBLANES)] = spare

        def first(r, carry):
            pltpu.make_async_copy(obuf_a.at[r], y_hbm.at[n_real_rows + 2 * bm + r],
                                  osem.at[0]).start()
            gather_start(tok_a_ref, xbuf_a, gsem.at[0], r)
            return carry

        lax.fori_loop(0, bm, first, 0)

    def gather_b(r):
        gather_start(tok_b_ref, xbuf_b, gsem.at[1], r, r % 2)

    def gather_a_next(r):
        gather_start(tok_a_next_ref, xbuf_a, gsem.at[0], r, r % 2)

    for r in range(bm):
        scatter_start(dst_b_prev_ref, obuf_b, osem.at[1], r, r % 2)
    rows_wait(xbuf_a, gsem.at[0])
    rows_wait(obuf_a, osem.at[0])
    obuf_a[:, 0, :] = mlp(xbuf_a, w1g_a, w1l_a, b1g_a, b1l_a, w2_a, b2_a, gather_b, xbuf_b)

    for r in range(bm):
        scatter_start(dst_a_ref, obuf_a, osem.at[0], r, r % 2)
    rows_wait(xbuf_b, gsem.at[1])
    rows_wait(obuf_b, osem.at[1])
    obuf_b[:, 0, :] = mlp(xbuf_b, w1g_b, w1l_b, b1g_b, b1l_b, w2_b, b2_b, gather_a_next, xbuf_a)

    @pl.when(i == pl.num_programs(0) - 1)
    def _():
        def last(r, carry):
            scatter_start(dst_b_ref, obuf_b, osem.at[1], r)
            return carry

        lax.fori_loop(0, bm, last, 0)
        rows_wait(obuf_b, osem.at[1])
        rows_wait(obuf_a, osem.at[0])
        rows_wait(xbuf_a, gsem.at[0])


def _expert_mlp(block_e, tok_blocks, dst_blocks, h2, w1g, w1l, b1g, b1l, w2, b2, bm):
    n_blocks = tok_blocks.shape[0]
    assert n_blocks % 2 == 0
    n, _, d = h2.shape
    dff = w1g.shape[2]
    n_real_rows = TOP_K * n
    idx_spec = lambda fn: pl.BlockSpec((1, 1, bm), fn, memory_space=pltpu.SMEM)

    def weight_specs(which):
        wmap = lambda i, be: (be[2 * i + which], 0, 0)
        return [
            pl.BlockSpec((1, d, dff), wmap),
            pl.BlockSpec((1, d, dff), wmap),
            pl.BlockSpec((1, 1, dff), wmap),
            pl.BlockSpec((1, 1, dff), wmap),
            pl.BlockSpec((1, dff, d), wmap),
            pl.BlockSpec((1, 1, d), wmap),
        ]

    grid_spec = pltpu.PrefetchScalarGridSpec(
        num_scalar_prefetch=1,
        grid=(n_blocks // 2,),
        in_specs=[
            idx_spec(lambda i, be: (2 * i, 0, 0)),
            idx_spec(lambda i, be: (2 * i + 1, 0, 0)),
            idx_spec(lambda i, be: (jnp.minimum(2 * i + 2, n_blocks - 1), 0, 0)),
            idx_spec(lambda i, be: (2 * i, 0, 0)),
            idx_spec(lambda i, be: (2 * i + 1, 0, 0)),
            idx_spec(lambda i, be: (2 * i + 2, 0, 0)),
            pl.BlockSpec(memory_space=pl.ANY),
        ] + weight_specs(0) + weight_specs(1),
        out_specs=pl.BlockSpec(memory_space=pl.ANY),
        scratch_shapes=[
            pltpu.VMEM((bm + SUBLANES, 1, d), F32),
            pltpu.VMEM((bm + SUBLANES, 1, d), F32),
            pltpu.VMEM((bm, 1, d), F32),
            pltpu.VMEM((bm, 1, d), F32),
            pltpu.VMEM((bm, d), F32),
            pltpu.SemaphoreType.DMA((2,)),
            pltpu.SemaphoreType.DMA((2,)),
        ],
    )
    weights = (w1g, w1l, b1g, b1l, w2, b2)
    return pl.pallas_call(
        functools.partial(_expert_kernel, bm=bm, n_real_rows=n_real_rows),
        grid_spec=grid_spec,
        out_shape=jax.ShapeDtypeStruct((n_real_rows + 3 * bm, 1, d), F32),
        compiler_params=_cparams(("arbitrary",)),
        name="expert_mlp",
    )(block_e, tok_blocks, tok_blocks, tok_blocks, dst_blocks, dst_blocks, dst_blocks, h2,
      *weights, *weights)


def _combine_kernel(gate_ref, x1_ref, g_ref, y0_ref, y1_ref, y2_ref, y3_ref, o_ref):
    gates = gate_ref[...]
    y = x1_ref[...]
    for kk, y_ref in enumerate((y0_ref, y1_ref, y2_ref, y3_ref)):
        y = y + y_ref[:, 0, :] * gates[:, kk:kk + 1]
    o_ref[...] = y * lax.rsqrt(jnp.mean(y * y, axis=-1, keepdims=True) + RMS_EPS) * g_ref[...]


def _combine(gates_t, x1, g, y_all, tc):
    n, d = x1.shape
    tiles = n // tc
    y_spec = lambda kk: pl.BlockSpec((tc, 1, d), lambda i: (kk * tiles + i, 0, 0))
    return pl.pallas_call(
        _combine_kernel,
        grid=(tiles,),
        in_specs=[
            pl.BlockSpec((tc, TOP_K), lambda i: (i, 0)),
            pl.BlockSpec((tc, d), lambda i: (i, 0)),
            pl.BlockSpec((1, d), lambda i: (0, 0)),
        ] + [y_spec(kk) for kk in range(TOP_K)],
        out_specs=pl.BlockSpec((tc, d), lambda i: (i, 0)),
        out_shape=jax.ShapeDtypeStruct((n, d), F32),
        compiler_params=_cparams(("parallel",)),
        name="combine",
    )(gates_t, x1, g, y_all, y_all, y_all, y_all)


def _slot_sources_kernel(gaps_ref, pos_ref, out_ref, *, chunk):
    i = pl.program_id(0)

    @pl.when(i == 0)
    def _():
        def init(p, carry):
            out_ref[p] = -1
            return carry

        for e in range(gaps_ref.shape[1]):
            lax.fori_loop(gaps_ref[0, e], gaps_ref[1, e], init, 0)

    base = i * chunk

    def place(s, carry):
        out_ref[pos_ref[0, 0, s]] = base + s
        return carry

    lax.fori_loop(0, chunk, place, 0, unroll=16)


def _slot_sources(pos, gaps, n_pad):
    n_slots = pos.shape[0]
    chunk = _pick(n_slots, SLOT_CHUNK)
    return pl.pallas_call(
        functools.partial(_slot_sources_kernel, chunk=chunk),
        grid=(n_slots // chunk,),
        in_specs=[
            pl.BlockSpec(memory_space=pltpu.SMEM),
            pl.BlockSpec((1, 1, chunk), lambda i: (i, 0, 0), memory_space=pltpu.SMEM),
        ],
        out_specs=pl.BlockSpec(memory_space=pltpu.SMEM),
        out_shape=jax.ShapeDtypeStruct((n_pad,), jnp.int32),
        compiler_params=_cparams(("arbitrary",)),
        name="slot_sources",
    )(gaps, pos.reshape(n_slots // chunk, 1, chunk))


def _dispatch_plan(idx, rank, counts, bm):
    n = idx.shape[1]
    n_slots = TOP_K * n
    sizes = counts[:, 0].astype(jnp.int32)
    padded = (sizes + bm - 1) // bm * bm
    pad_ends = jnp.cumsum(padded)
    pad_starts = pad_ends - padded
    experts = jnp.arange(N_EXPERTS, dtype=jnp.int32)
    start_of = jnp.sum(jnp.where(idx[..., None] == experts, pad_starts, 0), axis=-1)
    pos = (start_of + rank).reshape(-1)
    n_pad = n_slots + N_EXPERTS * bm
    n_blocks = n_pad // bm
    gap_lo = jnp.concatenate([pad_starts + sizes, pad_ends[-1:]])
    gap_hi = jnp.concatenate([pad_ends, jnp.full((1,), n_pad, jnp.int32)])
    slot_src = _slot_sources(pos, jnp.stack([gap_lo, gap_hi]).astype(jnp.int32), n_pad)
    p = jnp.arange(n_pad, dtype=jnp.int32)
    spare = n_slots + (p // bm) % 2 * bm + p % bm
    real = slot_src >= 0
    tok_blocks = jnp.where(real, slot_src % n, 0).reshape(n_blocks, 1, bm)
    dst = jnp.where(real, slot_src, spare)
    dst_blocks = jnp.concatenate([spare[bm:2 * bm], dst]).reshape(n_blocks + 1, 1, bm)
    block_start = jnp.arange(n_blocks, dtype=jnp.int32) * bm
    block_e = jnp.minimum(jnp.sum(pad_ends[None, :] <= block_start[:, None], axis=1),
                          N_EXPERTS - 1).astype(jnp.int32)
    return tok_blocks, dst_blocks, block_e


def _block_diag_ones():
    hid = jnp.arange(MXU_WIDTH, dtype=jnp.int32) // HEAD_DIM
    return (hid[:, None] == hid[None, :]).astype(BF16)


def _pick(n, pref):
    return pref if n % pref == 0 else n


def kernel(x, attn_norm_g, w_in, rwkv_mu, rwkv_w0, rwkv_w_up, rwkv_a0, rwkv_a_up, rwkv_g_up,
           rwkv_k_k, rwkv_k_a, rwkv_r_k, rwkv_ln_w, rwkv_ln_b, fox_f_bias, fox_out_g, w_out,
           ffn_norm_g, router_w, router_b, expert_w1, expert_b1, expert_w2, expert_b2,
           final_norm_g):
    bsz, t, d = x.shape
    n = bsz * t
    depth = w_in.shape[0]
    assert depth == 1, "the final norm is fused into the last stage of a single layer"
    bd = _block_diag_ones()
    x2 = x.reshape(n, d)
    for l in range(depth):
        w_l = w_in[l]
        w_r = w_l[:, :RWKV_IN].astype(BF16)
        w_qkv = w_l[:, RWKV_IN:RWKV_IN + 3 * GROUP_WIDTH].astype(BF16)
        w_qt = w_qkv[:, :GROUP_WIDTH].T
        w_k = w_qkv[:, GROUP_WIDTH:2 * GROUP_WIDTH]
        w_vt = w_qkv[:, 2 * GROUP_WIDTH:].T
        w_f = jnp.pad(w_l[:, RWKV_IN + 3 * GROUP_WIDTH:], ((0, 0), (0, LANES - N_HEADS))).astype(BF16)
        fb_pad = jnp.pad(fox_f_bias[l], (0, LANES - N_HEADS)).reshape(1, LANES)
        wup_pad = jnp.pad(rwkv_w_up[l], ((0, LANES - DECAY_LORA), (0, 0))).astype(BF16)
        aup_pad = jnp.pad(rwkv_a_up[l], ((DECAY_LORA, 0), (0, 0))).astype(BF16)
        gup = rwkv_g_up[l].astype(BF16)
        vec = lambda a: a.reshape(1, -1)

        u_r, qt5, k, vt5, fl = _inproj(x2, vec(attn_norm_g[l]), w_r, w_qt, w_k, w_vt, w_f,
                                       bsz, ATTN_TILE)
        cext, cend = _fox_gate(fl.reshape(bsz, t, LANES), fb_pad, _gate_piece_selectors(),
                               ATTN_TILE)
        y_rwkv = _rwkv_mix(
            u_r.reshape(bsz, t, RWKV_IN), vec(rwkv_mu[l]), vec(rwkv_w0[l]), wup_pad,
            vec(rwkv_a0[l]), aup_pad, gup, vec(rwkv_k_k[l]), vec(rwkv_k_a[l]), vec(rwkv_r_k[l]),
            vec(rwkv_ln_w[l]), vec(rwkv_ln_b[l]), bd, RWKV_CHUNKS_PER_STEP)
        y_fox = _fox_attn(qt5, k.reshape(bsz, t, GROUP_WIDTH), cext, vt5, cend,
                          vec(fox_out_g[l]), ATTN_TILE)

        wo = w_out[l].astype(BF16)
        x1, h2, idx, gates, rank, counts = _outproj_router(
            x2, y_rwkv.reshape(n, GROUP_WIDTH), y_fox.reshape(n, GROUP_WIDTH),
            wo[:GROUP_WIDTH], wo[GROUP_WIDTH:], vec(ffn_norm_g[l]),
            router_w[l].T, router_b[l].reshape(N_EXPERTS, 1), _pick(n, ROUTER_TILE))

        bm = EXPERT_BLOCK
        tok_blocks, dst_blocks, block_e = _dispatch_plan(idx, rank, counts, bm)
        w1g, w1l = _w1_split(expert_w1[l], W1_SPLIT_ROWS)
        b1 = expert_b1[l]
        b1g = b1[:, None, 0::2]
        b1l = b1[:, None, 1::2]
        y_all = _expert_mlp(block_e, tok_blocks, dst_blocks, h2, w1g, w1l, b1g, b1l,
                            expert_w2[l].astype(BF16), expert_b2[l][:, None, :], bm)
        x2 = _combine(gates.T, x1, vec(final_norm_g), y_all, _pick(n, COMBINE_TILE))
    return x2.reshape(bsz, t, d)
```

```python
import functools

import jax
import jax.numpy as jnp
from jax import lax
from jax.experimental import pallas as pl
from jax.experimental.pallas import tpu as pltpu

F32 = jnp.float32
BF16 = jnp.bfloat16

HEAD_DIM = 64
N_HEADS = 8
GROUP_WIDTH = N_HEADS * HEAD_DIM
DECAY_LORA = 64
AAA_LORA = 64
GATE_LORA = 128
RWKV_IN = 3 * GROUP_WIDTH + DECAY_LORA + AAA_LORA + GATE_LORA
LORA_OFF = 3 * GROUP_WIDTH
N_EXPERTS = 32
TOP_K = 4
SWIGLU_ALPHA = 1.702
SWIGLU_LIMIT = 7.0
RMS_EPS = 1e-5
RWKV_GN_EPS = 64e-5
LANES = 128
SUBLANES = 8
MXU_WIDTH = 256
RWKV_CHUNK = 64
RWKV_CHUNKS_PER_STEP = 8
ATTN_TILE = 512
ROUTER_TILE = 512
EXPERT_BLOCK = 256
W1_SPLIT_ROWS = 512
COMBINE_TILE = 256
SLOT_CHUNK = 8192
LOG2E = 1.4426950408889634
Q_SCALE = HEAD_DIM ** -0.5 * LOG2E
ZERO_PROB_EXP = -152.0
NORM_SLACK = 1.0 + 2.0 ** -6
VMEM_LIMIT = 56 * 1024 * 1024


def _cparams(semantics):
    return pltpu.CompilerParams(dimension_semantics=semantics, vmem_limit_bytes=VMEM_LIMIT)


def _dot(a, b):
    return jnp.dot(a, b, preferred_element_type=F32)


def _dot_nt(a, b):
    return lax.dot_general(a, b, (((1,), (1,)), ((), ())), preferred_element_type=F32)


def _dot_tn(a, b):
    return lax.dot_general(a, b, (((0,), (0,)), ((), ())), preferred_element_type=F32)


def _split3(x):
    hi = x.astype(BF16)
    r1 = x - hi.astype(F32)
    mid = r1.astype(BF16)
    lo = (r1 - mid.astype(F32)).astype(BF16)
    return hi, mid, lo


def _dot_exact_lhs(a_bf16, x):
    hi, mid, lo = _split3(x)
    return _dot(a_bf16, hi) + _dot(a_bf16, mid) + _dot(a_bf16, lo)


def _head_sums(x, same_head):
    w = same_head.shape[0]
    parts = []
    for g in range(x.shape[1] // w):
        xs = x[:, g * w:(g + 1) * w]
        hi = xs.astype(BF16)
        lo = (xs - hi.astype(F32)).astype(BF16)
        parts.append(_dot(hi, same_head) + _dot(lo, same_head))
    return jnp.concatenate(parts, axis=1)


def _softplus(z):
    return jnp.maximum(z, 0.0) + jnp.log1p(jnp.exp(-jnp.abs(z)))


def _sigmoid(z):
    return 1.0 / (1.0 + jnp.exp(-z))


def _inproj_kernel(x_ref, g_ref, wr_ref, wqt_ref, wk_ref, wvt_ref, wf_ref,
                   ur_ref, qt_ref, k_ref, vt_ref, fl_ref):
    x = x_ref[...]
    h = x * lax.rsqrt(jnp.mean(x * x, axis=-1, keepdims=True) + RMS_EPS) * g_ref[...]
    hb = h.astype(BF16)
    ur_ref[...] = _dot(hb, wr_ref[...])
    k_ref[...] = _dot(hb, wk_ref[...]).astype(BF16)
    fl_ref[...] = _dot(hb, wf_ref[...])
    qt = (_dot_nt(wqt_ref[...], hb) * Q_SCALE).astype(BF16)
    vt = _dot_nt(wvt_ref[...], hb).astype(BF16)
    for p in range(N_HEADS // 2):
        qt_ref[0, p, 0] = qt[p * LANES:(p + 1) * LANES]
        vt_ref[0, p, 0] = vt[p * LANES:(p + 1) * LANES]


def _inproj(x2, g, w_r, w_qt, w_k, w_vt, w_f, bsz, tm):
    n, d = x2.shape
    nt = n // bsz // tm
    pairs = N_HEADS // 2
    const = lambda i: (0, 0)
    row = lambda i: (i, 0)
    fm = lambda i: (i // nt, 0, i % nt, 0, 0)
    fm_sds = jax.ShapeDtypeStruct((bsz, pairs, nt, LANES, tm), BF16)
    return pl.pallas_call(
        _inproj_kernel,
        grid=(n // tm,),
        in_specs=[
            pl.BlockSpec((tm, d), row),
            pl.BlockSpec((1, d), const),
            pl.BlockSpec(w_r.shape, const),
            pl.BlockSpec(w_qt.shape, const),
            pl.BlockSpec(w_k.shape, const),
            pl.BlockSpec(w_vt.shape, const),
            pl.BlockSpec(w_f.shape, const),
        ],
        out_specs=[
            pl.BlockSpec((tm, RWKV_IN), row),
            pl.BlockSpec((1, pairs, 1, LANES, tm), fm),
            pl.BlockSpec((tm, GROUP_WIDTH), row),
            pl.BlockSpec((1, pairs, 1, LANES, tm), fm),
            pl.BlockSpec((tm, LANES), row),
        ],
        out_shape=[
            jax.ShapeDtypeStruct((n, RWKV_IN), F32),
            fm_sds,
            jax.ShapeDtypeStruct((n, GROUP_WIDTH), BF16),
            fm_sds,
            jax.ShapeDtypeStruct((n, LANES), F32),
        ],
        compiler_params=_cparams(("parallel",)),
        name="inproj",
    )(x2, g, w_r, w_qt, w_k, w_vt, w_f)


def _fox_gate_kernel(fl_ref, fb_ref, sel_ref, c_ref, cend_ref, carry):
    tt = fl_ref.shape[1]

    @pl.when(pl.program_id(1) == 0)
    def _():
        carry[...] = jnp.zeros_like(carry)

    z = fl_ref[0] + fb_ref[...]
    log_f = jnp.minimum(z, 0.0) - jnp.log1p(jnp.exp(-jnp.abs(z)))
    ri = lax.broadcasted_iota(jnp.int32, (tt, tt), 0)
    ci = lax.broadcasted_iota(jnp.int32, (tt, tt), 1)
    tri = jnp.where(ri >= ci, 1.0, 0.0).astype(BF16)
    c = _dot_exact_lhs(tri, log_f) + carry[...]
    carry[...] = c[tt - 1:tt, :]
    cend_ref[0, 0] = c[tt - 1:tt, :]
    hi, mid, lo = _split3(c * LOG2E)
    c_ref[0] = (_dot(hi, sel_ref[0]) + _dot(mid, sel_ref[1]) + _dot(lo, sel_ref[2])).astype(BF16)


def _gate_piece_selectors():
    h = jnp.arange(LANES, dtype=jnp.int32)[:, None]
    col = jnp.arange(GROUP_WIDTH, dtype=jnp.int32)[None, :]
    sels = []
    for m in range(3):
        target = LANES * (h // 2) + 3 * (h % 2) + m
        sels.append(((col == target) & (h < N_HEADS)).astype(BF16))
    return jnp.stack(sels)


def _fox_gate(fl3, fb_pad, sel, tt):
    b, t, _ = fl3.shape
    return pl.pallas_call(
        _fox_gate_kernel,
        grid=(b, t // tt),
        in_specs=[
            pl.BlockSpec((1, tt, LANES), lambda i, j: (i, j, 0)),
            pl.BlockSpec((1, LANES), lambda i, j: (0, 0)),
            pl.BlockSpec((3, LANES, GROUP_WIDTH), lambda i, j: (0, 0, 0)),
        ],
        out_specs=[
            pl.BlockSpec((1, tt, GROUP_WIDTH), lambda i, j: (i, j, 0)),
            pl.BlockSpec((1, 1, 1, LANES), lambda i, j: (i, j, 0, 0)),
        ],
        out_shape=[
            jax.ShapeDtypeStruct((b, t, GROUP_WIDTH), BF16),
            jax.ShapeDtypeStruct((b, t // tt, 1, LANES), F32),
        ],
        scratch_shapes=[pltpu.VMEM((1, LANES), F32)],
        compiler_params=_cparams(("parallel", "arbitrary")),
        name="fox_gate",
    )(fl3, fb_pad, sel)


def _rwkv_prep_body(u_ref, mu_ref, w0_ref, wup_ref, a0_ref, aup_ref, gup_ref, kk_ref, ka_ref,
                      bd_ref, r_out, k_out, v_out, lw_out, kk_out, b_out, g_out, carry):
    tt = u_ref.shape[1]

    @pl.when(pl.program_id(1) == 0)
    def _():
        carry[...] = jnp.zeros_like(carry)

    u = u_ref[0]
    prev = pltpu.roll(u, 1, axis=0)
    row = lax.broadcasted_iota(jnp.int32, u.shape, 0)
    prev = jnp.where(row == 0, carry[...], prev)
    carry[...] = u[tt - 1:tt, :]
    us = u + (prev - u) * mu_ref[...]

    r = us[:, :GROUP_WIDTH]
    k = us[:, GROUP_WIDTH:2 * GROUP_WIDTH]
    v = us[:, 2 * GROUP_WIDTH:LORA_OFF]
    wa = us[:, LORA_OFF:LORA_OFF + LANES]
    gl = us[:, LORA_OFF + LANES:]

    w_lin = _dot(jnp.tanh(wa).astype(BF16), wup_ref[...])
    a_lin = _dot(wa.astype(BF16), aup_ref[...])
    w = -_softplus(-(w0_ref[...] + w_lin)) - 0.5
    lw_out[0] = -jnp.exp(w)
    a = _sigmoid(a0_ref[...] + a_lin)
    g_out[0] = _dot(_sigmoid(gl).astype(BF16), gup_ref[...])

    kkr = k * kk_ref[...]
    ss = _head_sums(kkr * kkr, bd_ref[...])
    kk = kkr / jnp.maximum(jnp.sqrt(ss), 1e-12)
    r_out[0] = r
    k_out[0] = k * (1.0 + (a - 1.0) * ka_ref[...])
    v_out[0] = v
    kk_out[0] = kk
    b_out[0] = kk * a


def _rwkv_scan_body(r_ref, k_ref, v_ref, lw_ref, kk_ref, b_ref, g_ref, rk_ref, lnw_ref, lnb_ref,
                      bd_ref, o_ref, s_scr, *, n_chunks):
    c = RWKV_CHUNK
    w = MXU_WIDTH
    hpg = w // HEAD_DIM
    n_groups = GROUP_WIDTH // w

    @pl.when(pl.program_id(1) == 0)
    def _():
        s_scr[...] = jnp.zeros_like(s_scr)

    row = lax.broadcasted_iota(jnp.int32, (c, w), 0)
    u = lax.broadcasted_iota(jnp.int32, (c, w), 1) % HEAD_DIM
    strict = row > u
    incl = row >= u
    eye = jnp.where(row == u, 1.0, 0.0)
    level_masks = []
    s = 1
    while s < c:
        same = (row // (2 * s)) == (u // (2 * s))
        level_masks.append(same & ((row % (2 * s)) >= s) & ((u % (2 * s)) < s))
        s *= 2
    same_head = (lax.broadcasted_iota(jnp.int32, (w, w), 0) // HEAD_DIM
                 == lax.broadcasted_iota(jnp.int32, (w, w), 1) // HEAD_DIM)
    tri = jnp.where(lax.broadcasted_iota(jnp.int32, (c, c), 0)
                    >= lax.broadcasted_iota(jnp.int32, (c, c), 1), 1.0, 0.0).astype(BF16)

    def bdiag(x):
        xb = x.astype(BF16)
        tiled = jnp.concatenate([xb] * hpg, axis=0)
        return jnp.where(same_head, tiled, jnp.zeros_like(tiled))

    chains = [(ci, gi) for ci in range(n_chunks) for gi in range(n_groups)]
    lhs, rk_t, vbs, xcat, xneg, gam, rt32 = {}, {}, {}, {}, {}, {}, {}
    for ci in range(n_chunks):
        rs = slice(ci * c, (ci + 1) * c)
        r = r_ref[0, rs, :]
        k = k_ref[0, rs, :]
        lw = lw_ref[0, rs, :]
        kk = kk_ref[0, rs, :]
        b = b_ref[0, rs, :]
        g_cum = _dot_exact_lhs(tri, lw)
        g_last = g_cum[c - 1:c, :]
        r_t = r * jnp.exp(g_cum)
        kk_t = kk * jnp.exp(g_cum - lw)
        e_neg = jnp.exp(-g_cum)
        b_n = b * e_neg
        k_n = k * e_neg
        e_end = jnp.exp(g_last - g_cum)
        b_e = (b * e_end).astype(BF16)
        k_e = (k * e_end).astype(BF16)
        gamma = jnp.exp(g_last)
        vb = v_ref[0, rs, :].astype(BF16)
        for gi in range(n_groups):
            gs = slice(gi * w, (gi + 1) * w)
            ch = (ci, gi)
            lhs[ch] = jnp.concatenate([r_t[:, gs].astype(BF16), kk_t[:, gs].astype(BF16)], axis=0)
            rk_t[ch] = (jnp.concatenate([bdiag(b_n[:, gs]), bdiag(k_n[:, gs])], axis=0),
                        bdiag(kk_t[:, gs]))
            vbs[ch] = vb[:, gs]
            xcat[ch] = jnp.concatenate([b_e[:, gs], k_e[:, gs]], axis=0)
            xneg[ch] = jnp.concatenate([-b_e[:, gs], k_e[:, gs]], axis=0)
            gam[ch] = gamma[:, gs]
            rt32[ch] = r_t[:, gs]

    p = {ch: _dot_nt(lhs[ch], rk_t[ch][0]) for ch in chains}
    l_b = {ch: jnp.where(strict, p[ch][c:, :w], 0.0) for ch in chains}
    l_k = {ch: jnp.where(strict, p[ch][c:, w:], 0.0).astype(BF16) for ch in chains}
    p_br = {ch: jnp.where(incl, p[ch][:c, :w], 0.0).astype(BF16) for ch in chains}
    p_kr = {ch: jnp.where(incl, p[ch][:c, w:], 0.0).astype(BF16) for ch in chains}
    v_bd = {ch: bdiag(vbs[ch]) for ch in chains}
    lkv = {ch: _dot(l_k[ch], v_bd[ch]) for ch in chains}

    t_inv = {ch: eye - jnp.where(level_masks[0], l_b[ch], 0.0) for ch in chains}
    for m in level_masks[1:]:
        tb = {ch: t_inv[ch].astype(BF16) for ch in chains}
        ct = {ch: _dot(jnp.where(m, l_b[ch], 0.0).astype(BF16), bdiag(tb[ch])) for ch in chains}
        t_inv = {ch: t_inv[ch] - _dot(tb[ch], bdiag(ct[ch])) for ch in chains}

    mm = {ch: _dot(t_inv[ch].astype(BF16),
                   jnp.concatenate([rk_t[ch][1], bdiag(lkv[ch])], axis=1)) for ch in chains}
    pm = {ch: _dot(p_br[ch], jnp.concatenate([bdiag(mm[ch][:, :w]), bdiag(mm[ch][:, w:])], axis=1))
          for ch in chains}
    pkv = {ch: _dot(p_kr[ch], v_bd[ch]) for ch in chains}
    n1 = {ch: (rt32[ch] - pm[ch][:, :w]).astype(BF16) for ch in chains}
    n2 = {ch: pkv[ch] - pm[ch][:, w:] for ch in chains}
    omega = {ch: jnp.where(same_head, _dot_tn(mm[ch][:, :w].astype(BF16), xcat[ch][:c]), 0.0)
             .astype(BF16) for ch in chains}
    psi = {ch: jnp.where(same_head, _dot_tn(
        jnp.concatenate([mm[ch][:, w:].astype(BF16), vbs[ch]], axis=0), xneg[ch]), 0.0)
           for ch in chains}

    state = [s_scr[gi] for gi in range(n_groups)]
    ys = []
    for ci in range(n_chunks):
        sb = [state[gi].astype(BF16) for gi in range(n_groups)]
        ys.append(jnp.concatenate(
            [_dot_nt(n1[(ci, gi)], sb[gi]) + n2[(ci, gi)] for gi in range(n_groups)], axis=1))
        state = [state[gi] * gam[(ci, gi)] - _dot(sb[gi], omega[(ci, gi)]) + psi[(ci, gi)]
                 for gi in range(n_groups)]
    for gi in range(n_groups):
        s_scr[gi] = state[gi]

    y = jnp.concatenate(ys, axis=0)
    r = r_ref[0]
    k = k_ref[0]
    v = v_ref[0]
    bd = bd_ref[...]
    inv_n = 1.0 / HEAD_DIM
    mean = _head_sums(y, bd) * inv_n
    d = y - mean
    var = _head_sums(d * d, bd) * inv_n
    yn = d * lax.rsqrt(var + RWKV_GN_EPS) * lnw_ref[...] + lnb_ref[...]
    bonus = _head_sums(r * k * rk_ref[...], bd) * v
    o_ref[0] = ((yn + bonus) * g_ref[0]).astype(o_ref.dtype)


def _rwkv_kernel(u_ref, mu_ref, w0_ref, wup_ref, a0_ref, aup_ref, gup_ref, kkw_ref, ka_ref,
                 rk_ref, lnw_ref, lnb_ref, bd_ref, o_ref, carry, s_scr,
                 r_s, k_s, v_s, lw_s, kk_s, b_s, g_s, *, n_chunks):
    _rwkv_prep_body(u_ref, mu_ref, w0_ref, wup_ref, a0_ref, aup_ref, gup_ref, kkw_ref, ka_ref,
                    bd_ref, r_s, k_s, v_s, lw_s, kk_s, b_s, g_s, carry)
    _rwkv_scan_body(r_s, k_s, v_s, lw_s, kk_s, b_s, g_s, rk_ref, lnw_ref, lnb_ref, bd_ref,
                    o_ref, s_scr, n_chunks=n_chunks)


def _rwkv_mix(u3, mu, w0, wup_pad, a0, aup_pad, gup, k_k, k_a, r_k, ln_w, ln_b, bd, n_chunks):
    bsz, t, _ = u3.shape
    rows = RWKV_CHUNK * n_chunks
    const = lambda i, j: (0, 0)
    tile = lambda i, j: (i, j, 0)
    vec = pl.BlockSpec((1, GROUP_WIDTH), const)
    staged = pltpu.VMEM((1, rows, GROUP_WIDTH), F32)
    return pl.pallas_call(
        functools.partial(_rwkv_kernel, n_chunks=n_chunks),
        grid=(bsz, t // rows),
        in_specs=[
            pl.BlockSpec((1, rows, RWKV_IN), tile),
            pl.BlockSpec((1, RWKV_IN), const),
            vec,
            pl.BlockSpec((LANES, GROUP_WIDTH), const),
            vec,
            pl.BlockSpec((LANES, GROUP_WIDTH), const),
            pl.BlockSpec((GATE_LORA, GROUP_WIDTH), const),
            vec, vec, vec, vec, vec,
            pl.BlockSpec((MXU_WIDTH, MXU_WIDTH), const),
        ],
        out_specs=pl.BlockSpec((1, rows, GROUP_WIDTH), tile),
        out_shape=jax.ShapeDtypeStruct((bsz, t, GROUP_WIDTH), BF16),
        scratch_shapes=[
            pltpu.VMEM((1, RWKV_IN), F32),
            pltpu.VMEM((GROUP_WIDTH // MXU_WIDTH, MXU_WIDTH, MXU_WIDTH), F32),
        ] + [staged] * 7,
        compiler_params=_cparams(("parallel", "arbitrary")),
        name="rwkv_mix",
    )(u3, mu, w0, wup_pad, a0, aup_pad, gup, k_k, k_a, r_k, ln_w, ln_b, bd)


def _fox_attn_kernel(qt_ref, k_ref, ce_ref, vt_ref, cend_ref, og_ref, o_ref,
                     m_scr, l_scr, acc_scr, kmax_scr, *, t):
    qi = pl.program_id(2)
    n_strips = 2 * t // LANES
    qt = qt_ref[0, 0, 0]
    frow = lax.broadcasted_iota(jnp.int32, (LANES, t), 0)
    zero = jnp.zeros_like(qt)
    main = jnp.concatenate([jnp.where(frow < HEAD_DIM, qt, zero),
                            jnp.where(frow < HEAD_DIM, zero, qt)], axis=1)
    erow = lax.broadcasted_iota(jnp.int32, (LANES, 2 * t), 0)
    ecol = lax.broadcasted_iota(jnp.int32, (LANES, 2 * t), 1)
    off = jnp.where(ecol < t, 0, 3)
    extra = jnp.where((erow >= off) & (erow < off + 3), -1.0, 0.0).astype(BF16)
    q_aug = jnp.concatenate([main, extra], axis=0)

    m_scr[...] = jnp.full_like(m_scr, -jnp.inf)
    l_scr[...] = jnp.zeros_like(l_scr)
    acc_scr[...] = jnp.zeros_like(acc_scr)

    @pl.when(qi == 0)
    def _():
        hid_r = lax.broadcasted_iota(jnp.int32, (LANES, LANES), 0) // HEAD_DIM
        hid_c = lax.broadcasted_iota(jnp.int32, (LANES, LANES), 1) // HEAD_DIM
        same_head = jnp.where(hid_r == hid_c, 1.0, 0.0).astype(BF16)

        def tile_max(j, best):
            kf = k_ref[0, pl.ds(pl.multiple_of(j * t, t), t), :].astype(F32)
            sq = _dot((kf * kf).astype(BF16), same_head)
            return jnp.maximum(best, jnp.max(sq, axis=0, keepdims=True))

        best = lax.fori_loop(0, k_ref.shape[1] // t, tile_max, jnp.zeros((1, LANES), F32))
        kmax_scr[...] = jnp.sqrt(best) * NORM_SLACK

    def step(j, masked):
        start = pl.multiple_of(j * t, t)
        k_aug = jnp.concatenate([k_ref[0, pl.ds(start, t), :], ce_ref[0, pl.ds(start, t), :]],
                                axis=1)
        vt = vt_ref[0, 0, j]
        zt = _dot(k_aug, q_aug)
        m_prev = m_scr[...]
        l_prev = l_scr[...]
        acc_prev = acc_scr[...]
        m_out, l_out, acc_out = [], [], [[], []]
        for s in range(n_strips):
            head = s // (n_strips // 2)
            cs = slice(s * LANES, (s + 1) * LANES)
            z = zt[:, cs]
            if masked:
                key = lax.broadcasted_iota(jnp.int32, (t, LANES), 0)
                qry = lax.broadcasted_iota(jnp.int32, (t, LANES), 1) + (s * LANES) % t
                z = jnp.where(key <= qry, z, -jnp.inf)
            m_new = jnp.maximum(m_prev[:, cs], jnp.max(z, axis=0, keepdims=True))
            alpha = jnp.exp2(m_prev[:, cs] - m_new)
            p = jnp.exp2(z - m_new)
            l_out.append(alpha * l_prev[:, cs] + jnp.sum(p, axis=0, keepdims=True))
            m_out.append(m_new)
            hs = slice(head * HEAD_DIM, (head + 1) * HEAD_DIM)
            qs = slice((s * LANES) % t, (s * LANES) % t + LANES)
            pv = _dot(vt[hs], p.astype(BF16))
            acc_out[head].append(alpha * acc_prev[hs, qs] + pv)
        m_scr[...] = jnp.concatenate(m_out, axis=1)
        l_scr[...] = jnp.concatenate(l_out, axis=1)
        acc_scr[...] = jnp.concatenate([jnp.concatenate(acc_out[0], axis=1),
                                        jnp.concatenate(acc_out[1], axis=1)], axis=0)

    def body(j, carry):
        step(j, False)
        return carry

    step(qi, True)

    qf = qt.astype(F32)
    qsq = qf * qf
    qnorm = jnp.sqrt(jnp.concatenate(
        [jnp.sum(qsq[:HEAD_DIM], axis=0, keepdims=True),
         jnp.sum(qsq[HEAD_DIM:], axis=0, keepdims=True)], axis=1)) * NORM_SLACK
    kmax = kmax_scr[...]
    kmax2 = jnp.concatenate([jnp.broadcast_to(kmax[:, 0:1], (1, t)),
                             jnp.broadcast_to(kmax[:, HEAD_DIM:HEAD_DIM + 1], (1, t))], axis=1)
    slack = qnorm * kmax2 - m_scr[...]
    cend = cend_ref[0][:, 0, :]
    lane = lax.broadcasted_iota(jnp.int32, cend.shape, 1)
    jrow = lax.broadcasted_iota(jnp.int32, (cend.shape[0], 1), 0)
    needed = jrow < 0
    for head in range(2):
        worst = jnp.max(slack[:, head * t:(head + 1) * t], axis=1, keepdims=True)
        c_head = jnp.sum(jnp.where(lane == 2 * pl.program_id(1) + head, cend, 0.0),
                         axis=1, keepdims=True)
        needed = needed | (worst - c_head * LOG2E > ZERO_PROB_EXP)
    first = jnp.min(jnp.where(needed & (jrow < qi), jrow, qi))
    lax.fori_loop(first, qi, body, 0)

    l = l_scr[...]
    acc = acc_scr[...]
    inv_n = 1.0 / HEAD_DIM
    halves = []
    for head in range(2):
        o = acc[head * HEAD_DIM:(head + 1) * HEAD_DIM] / l[:, head * t:(head + 1) * t]
        halves.append(o * lax.rsqrt(jnp.sum(o * o, axis=0, keepdims=True) * inv_n + RMS_EPS))
    o_t = jnp.concatenate(halves, axis=0)
    o_ref[0] = (jnp.transpose(o_t) * og_ref[...]).astype(o_ref.dtype)


def _fox_attn(qt5, k, cext, vt5, cend, out_g, tile):
    bsz, t_all, _ = k.shape
    pairs = N_HEADS // 2
    tiles = t_all // tile
    kern = functools.partial(_fox_attn_kernel, t=tile)
    return pl.pallas_call(
        kern,
        grid=(bsz, pairs, tiles),
        in_specs=[
            pl.BlockSpec((1, 1, 1, LANES, tile), lambda b, p, i: (b, p, i, 0, 0)),
            pl.BlockSpec((1, t_all, LANES), lambda b, p, i: (b, 0, p)),
            pl.BlockSpec((1, t_all, LANES), lambda b, p, i: (b, 0, p)),
            pl.BlockSpec((1, 1, tiles, LANES, tile), lambda b, p, i: (b, p, 0, 0, 0)),
            pl.BlockSpec((1, tiles, 1, LANES), lambda b, p, i: (b, 0, 0, 0)),
            pl.BlockSpec((1, LANES), lambda b, p, i: (0, p)),
        ],
        out_specs=pl.BlockSpec((1, tile, LANES), lambda b, p, i: (b, i, p)),
        out_shape=jax.ShapeDtypeStruct((bsz, t_all, GROUP_WIDTH), BF16),
        scratch_shapes=[
            pltpu.VMEM((1, 2 * tile), F32),
            pltpu.VMEM((1, 2 * tile), F32),
            pltpu.VMEM((LANES, tile), F32),
            pltpu.VMEM((1, LANES), F32),
        ],
        compiler_params=_cparams(("parallel", "parallel", "arbitrary")),
        name="fox_attn",
    )(qt5, k, cext, vt5, cend, out_g)


def _outproj_router_kernel(x_ref, yr_ref, yf_ref, wo_r_ref, wo_f_ref, g_ref, rwt_ref, rb_ref,
                           x1_ref, h_ref, idx_ref, gate_ref, rank_ref, count_ref):
    @pl.when(pl.program_id(0) == 0)
    def _():
        count_ref[...] = jnp.zeros_like(count_ref)

    x1 = x_ref[...] + _dot(yr_ref[...], wo_r_ref[...]) + _dot(yf_ref[...], wo_f_ref[...])
    x1_ref[...] = x1
    h = x1 * lax.rsqrt(jnp.mean(x1 * x1, axis=-1, keepdims=True) + RMS_EPS) * g_ref[...]
    h_ref[:, 0, :] = h
    logits = lax.dot_general(rwt_ref[...], h, (((1,), (1,)), ((), ())),
                             precision=lax.Precision.HIGHEST,
                             preferred_element_type=F32) + rb_ref[...]
    eidx = lax.broadcasted_iota(jnp.int32, logits.shape, 0)
    vals, idxs, picks = [], [], []
    for _ in range(TOP_K):
        m = jnp.max(logits, axis=0, keepdims=True)
        i = jnp.min(jnp.where(logits == m, eidx, N_EXPERTS), axis=0, keepdims=True)
        vals.append(m)
        idxs.append(i)
        picks.append(eidx == i)
        logits = jnp.where(picks[-1], -jnp.inf, logits)
    es = [jnp.exp(val - vals[0]) for val in vals]
    denom = es[0] + es[1] + es[2] + es[3]
    idx_ref[...] = jnp.concatenate(idxs, axis=0)
    gate_ref[...] = jnp.concatenate([e / denom for e in es], axis=0)

    tm = logits.shape[1]
    chosen = [jnp.where(pk, 1.0, 0.0) for pk in picks]
    any_k = chosen[0] + chosen[1] + chosen[2] + chosen[3]
    before = (lax.broadcasted_iota(jnp.int32, (tm, tm), 0)
              < lax.broadcasted_iota(jnp.int32, (tm, tm), 1))
    prefix = _dot(any_k.astype(BF16), jnp.where(before, 1.0, 0.0).astype(BF16))
    seen = count_ref[:, 0:1] + prefix
    rank_ref[...] = jnp.concatenate(
        [jnp.sum(ch * seen, axis=0, keepdims=True) for ch in chosen], axis=0).astype(jnp.int32)
    count_ref[...] = count_ref[...] + jnp.sum(any_k, axis=1, keepdims=True)


def _outproj_router(x2, yr, yf, wo_r, wo_f, g, rwt, rb, tm):
    n, d = x2.shape
    const = lambda i: (0, 0)
    row = lambda i: (i, 0)
    col = lambda i: (0, i)
    return pl.pallas_call(
        _outproj_router_kernel,
        grid=(n // tm,),
        in_specs=[
            pl.BlockSpec((tm, d), row),
            pl.BlockSpec((tm, GROUP_WIDTH), row),
            pl.BlockSpec((tm, GROUP_WIDTH), row),
            pl.BlockSpec((GROUP_WIDTH, d), const),
            pl.BlockSpec((GROUP_WIDTH, d), const),
            pl.BlockSpec((1, d), const),
            pl.BlockSpec((N_EXPERTS, d), const),
            pl.BlockSpec((N_EXPERTS, 1), const),
        ],
        out_specs=[
            pl.BlockSpec((tm, d), row),
            pl.BlockSpec((tm, 1, d), lambda i: (i, 0, 0)),
            pl.BlockSpec((TOP_K, tm), col),
            pl.BlockSpec((TOP_K, tm), col),
            pl.BlockSpec((TOP_K, tm), col),
            pl.BlockSpec((N_EXPERTS, LANES), const),
        ],
        out_shape=[
            jax.ShapeDtypeStruct((n, d), F32),
            jax.ShapeDtypeStruct((n, 1, d), F32),
            jax.ShapeDtypeStruct((TOP_K, n), jnp.int32),
            jax.ShapeDtypeStruct((TOP_K, n), F32),
            jax.ShapeDtypeStruct((TOP_K, n), jnp.int32),
            jax.ShapeDtypeStruct((N_EXPERTS, LANES), F32),
        ],
        compiler_params=_cparams(("arbitrary",)),
        name="outproj_router",
    )(x2, yr, yf, wo_r, wo_f, g, rwt, rb)


def _w1_split_kernel(w_ref, perm_ref, g_ref, l_ref):
    half = MXU_WIDTH // 2
    perm = perm_ref[...]
    for grp in range(w_ref.shape[2] // MXU_WIDTH):
        blk = w_ref[0, :, grp * MXU_WIDTH:(grp + 1) * MXU_WIDTH].astype(BF16)
        r = _dot(blk, perm)
        g_ref[0, :, grp * half:(grp + 1) * half] = r[:, :half].astype(BF16)
        l_ref[0, :, grp * half:(grp + 1) * half] = r[:, half:].astype(BF16)


def _w1_split(w1, tr):
    e, d, two_f = w1.shape
    half = MXU_WIDTH // 2
    src = jnp.arange(MXU_WIDTH, dtype=jnp.int32)[:, None]
    dst = jnp.arange(MXU_WIDTH, dtype=jnp.int32)[None, :]
    perm = (src == jnp.where(dst < half, 2 * dst, 2 * (dst - half) + 1)).astype(BF16)
    out_sds = jax.ShapeDtypeStruct((e, d, two_f // 2), BF16)
    return pl.pallas_call(
        _w1_split_kernel,
        grid=(e, d // tr),
        in_specs=[
            pl.BlockSpec((1, tr, two_f), lambda i, j: (i, j, 0)),
            pl.BlockSpec((MXU_WIDTH, MXU_WIDTH), lambda i, j: (0, 0)),
        ],
        out_specs=[pl.BlockSpec((1, tr, two_f // 2), lambda i, j: (i, j, 0))] * 2,
        out_shape=[out_sds, out_sds],
        compiler_params=_cparams(("parallel", "parallel")),
        name="w1_split",
    )(w1, perm)


def _expert_kernel(be_ref, tok_a_ref, tok_b_ref, tok_a_next_ref, dst_b_prev_ref, dst_a_ref,
                   dst_b_ref, h_hbm, w1g_a, w1l_a, b1g_a, b1l_a, w2_a, b2_a,
                   w1g_b, w1l_b, b1g_b, b1l_b, w2_b, b2_b,
                   y_hbm, xbuf_a, xbuf_b, obuf_a, obuf_b, xrows, gsem, osem, *, bm, n_real_rows):
    del be_ref
    i = pl.program_id(0)

    def gather_start(idx_ref, xbuf, sem, r, priority=0):
        pltpu.make_async_copy(h_hbm.at[idx_ref[0, 0, r]], xbuf.at[r], sem).start(priority=priority)

    def scatter_start(idx_ref, obuf, sem, r, priority=0):
        pltpu.make_async_copy(obuf.at[r], y_hbm.at[idx_ref[0, 0, r]], sem).start(priority=priority)

    def rows_wait(buf, sem):
        rows = buf.at[pl.ds(0, bm)]
        pltpu.make_async_copy(rows, rows, sem).wait()

    def mlp(xbuf, w1g_ref, w1l_ref, b1g_ref, b1l_ref, w2_ref, b2_ref, start_gather, next_xbuf):
        xrows[...] = xbuf[pl.ds(0, bm), 0, :]
        xb = xrows[...].astype(BF16)
        dff = w1g_ref.shape[2]
        n_pieces = dff // MXU_WIDTH
        per_piece = -(-bm // (n_pieces - 1))
        acts = []
        for piece in range(n_pieces):
            cs = slice(piece * MXU_WIDTH, (piece + 1) * MXU_WIDTH)
            for r in range(piece * per_piece, min((piece + 1) * per_piece, bm)):
                start_gather(r)
            zero = next_xbuf[bm, :, cs]
            glu = _dot(xb, w1g_ref[0, :, cs]) + (b1g_ref[0, :, cs] + zero)
            lin = _dot(xb, w1l_ref[0, :, cs]) + b1l_ref[0, :, cs]
            glu = jnp.minimum(glu, SWIGLU_LIMIT)
            lin = jnp.clip(lin, -SWIGLU_LIMIT, SWIGLU_LIMIT)
            acts.append((glu * _sigmoid(SWIGLU_ALPHA * glu) * (lin + 1.0)).astype(BF16))
        return _dot(jnp.concatenate(acts, axis=1), w2_ref[0]) + b2_ref[0]

    @pl.when(i == 0)
    def _():
        obuf_a[...] = jnp.zeros_like(obuf_a)
        obuf_b[...] = jnp.zeros_like(obuf_b)
        spare = jnp.zeros((SUBLANES, 1, xbuf_a.shape[2]), F32)
        xbuf_a[pl.ds(bm, SUBLANES)] = spare
        xbuf_b[pl.ds(bm, SUBLANES)] = spare

        def first(r, carry):
            pltpu.make_async_copy(obuf_a.at[r], y_hbm.at[n_real_rows + 2 * bm + r],
                                  osem.at[0]).start()
            gather_start(tok_a_ref, xbuf_a, gsem.at[0], r)
            return carry

        lax.fori_loop(0, bm, first, 0)

    def gather_b(r):
        gather_start(tok_b_ref, xbuf_b, gsem.at[1], r, r % 2)

    def gather_a_next(r):
        gather_start(tok_a_next_ref, xbuf_a, gsem.at[0], r, r % 2)

    for r in range(bm):
        scatter_start(dst_b_prev_ref, obuf_b, osem.at[1], r, r % 2)
    rows_wait(xbuf_a, gsem.at[0])
    rows_wait(obuf_a, osem.at[0])
    obuf_a[:, 0, :] = mlp(xbuf_a, w1g_a, w1l_a, b1g_a, b1l_a, w2_a, b2_a, gather_b, xbuf_b)

    for r in range(bm):
        scatter_start(dst_a_ref, obuf_a, osem.at[0], r, r % 2)
    rows_wait(xbuf_b, gsem.at[1])
    rows_wait(obuf_b, osem.at[1])
    obuf_b[:, 0, :] = mlp(xbuf_b, w1g_b, w1l_b, b1g_b, b1l_b, w2_b, b2_b, gather_a_next, xbuf_a)

    @pl.when(i == pl.num_programs(0) - 1)
    def _():
        def last(r, carry):
            scatter_start(dst_b_ref, obuf_b, osem.at[1], r)
            return carry

        lax.fori_loop(0, bm, last, 0)
        rows_wait(obuf_b, osem.at[1])
        rows_wait(obuf_a, osem.at[0])
        rows_wait(xbuf_a, gsem.at[0])


def _expert_mlp(block_e, tok_blocks, dst_blocks, h2, w1g, w1l, b1g, b1l, w2, b2, bm):
    n_blocks = tok_blocks.shape[0]
    assert n_blocks % 2 == 0
    n, _, d = h2.shape
    dff = w1g.shape[2]
    n_real_rows = TOP_K * n
    idx_spec = lambda fn: pl.BlockSpec((1, 1, bm), fn, memory_space=pltpu.SMEM)

    def weight_specs(which):
        wmap = lambda i, be: (be[2 * i + which], 0, 0)
        return [
            pl.BlockSpec((1, d, dff), wmap),
            pl.BlockSpec((1, d, dff), wmap),
            pl.BlockSpec((1, 1, dff), wmap),
            pl.BlockSpec((1, 1, dff), wmap),
            pl.BlockSpec((1, dff, d), wmap),
            pl.BlockSpec((1, 1, d), wmap),
        ]

    grid_spec = pltpu.PrefetchScalarGridSpec(
        num_scalar_prefetch=1,
        grid=(n_blocks // 2,),
        in_specs=[
            idx_spec(lambda i, be: (2 * i, 0, 0)),
            idx_spec(lambda i, be: (2 * i + 1, 0, 0)),
            idx_spec(lambda i, be: (jnp.minimum(2 * i + 2, n_blocks - 1), 0, 0)),
            idx_spec(lambda i, be: (2 * i, 0, 0)),
            idx_spec(lambda i, be: (2 * i + 1, 0, 0)),
            idx_spec(lambda i, be: (2 * i + 2, 0, 0)),
            pl.BlockSpec(memory_space=pl.ANY),
        ] + weight_specs(0) + weight_specs(1),
        out_specs=pl.BlockSpec(memory_space=pl.ANY),
        scratch_shapes=[
            pltpu.VMEM((bm + SUBLANES, 1, d), F32),
            pltpu.VMEM((bm + SUBLANES, 1, d), F32),
            pltpu.VMEM((bm, 1, d), F32),
            pltpu.VMEM((bm, 1, d), F32),
            pltpu.VMEM((bm, d), F32),
            pltpu.SemaphoreType.DMA((2,)),
            pltpu.SemaphoreType.DMA((2,)),
        ],
    )
    weights = (w1g, w1l, b1g, b1l, w2, b2)
    return pl.pallas_call(
        functools.partial(_expert_kernel, bm=bm, n_real_rows=n_real_rows),
        grid_spec=grid_spec,
        out_shape=jax.ShapeDtypeStruct((n_real_rows + 3 * bm, 1, d), F32),
        compiler_params=_cparams(("arbitrary",)),
        name="expert_mlp",
    )(block_e, tok_blocks, tok_blocks, tok_blocks, dst_blocks, dst_blocks, dst_blocks, h2,
      *weights, *weights)


def _combine_kernel(gate_ref, x1_ref, g_ref, y0_ref, y1_ref, y2_ref, y3_ref, o_ref):
    gates = gate_ref[...]
    y = x1_ref[...]
    for kk, y_ref in enumerate((y0_ref, y1_ref, y2_ref, y3_ref)):
        y = y + y_ref[:, 0, :] * gates[:, kk:kk + 1]
    o_ref[...] = y * lax.rsqrt(jnp.mean(y * y, axis=-1, keepdims=True) + RMS_EPS) * g_ref[...]


def _combine(gates_t, x1, g, y_all, tc):
    n, d = x1.shape
    tiles = n // tc
    y_spec = lambda kk: pl.BlockSpec((tc, 1, d), lambda i: (kk * tiles + i, 0, 0))
    return pl.pallas_call(
        _combine_kernel,
        grid=(tiles,),
        in_specs=[
            pl.BlockSpec((tc, TOP_K), lambda i: (i, 0)),
            pl.BlockSpec((tc, d), lambda i: (i, 0)),
            pl.BlockSpec((1, d), lambda i: (0, 0)),
        ] + [y_spec(kk) for kk in range(TOP_K)],
        out_specs=pl.BlockSpec((tc, d), lambda i: (i, 0)),
        out_shape=jax.ShapeDtypeStruct((n, d), F32),
        compiler_params=_cparams(("parallel",)),
        name="combine",
    )(gates_t, x1, g, y_all, y_all, y_all, y_all)


def _slot_sources_kernel(gaps_ref, pos_ref, out_ref, *, chunk):
    i = pl.program_id(0)

    @pl.when(i == 0)
    def _():
        def init(p, carry):
            out_ref[p] = -1
            return carry

        for e in range(gaps_ref.shape[1]):
            lax.fori_loop(gaps_ref[0, e], gaps_ref[1, e], init, 0)

    base = i * chunk

    def place(s, carry):
        out_ref[pos_ref[0, 0, s]] = base + s
        return carry

    lax.fori_loop(0, chunk, place, 0, unroll=16)


def _slot_sources(pos, gaps, n_pad):
    n_slots = pos.shape[0]
    chunk = _pick(n_slots, SLOT_CHUNK)
    return pl.pallas_call(
        functools.partial(_slot_sources_kernel, chunk=chunk),
        grid=(n_slots // chunk,),
        in_specs=[
            pl.BlockSpec(memory_space=pltpu.SMEM),
            pl.BlockSpec((1, 1, chunk), lambda i: (i, 0, 0), memory_space=pltpu.SMEM),
        ],
        out_specs=pl.BlockSpec(memory_space=pltpu.SMEM),
        out_shape=jax.ShapeDtypeStruct((n_pad,), jnp.int32),
        compiler_params=_cparams(("arbitrary",)),
        name="slot_sources",
    )(gaps, pos.reshape(n_slots // chunk, 1, chunk))


def _dispatch_plan(idx, rank, counts, bm):
    n = idx.shape[1]
    n_slots = TOP_K * n
    sizes = counts[:, 0].astype(jnp.int32)
    padded = (sizes + bm - 1) // bm * bm
    pad_ends = jnp.cumsum(padded)
    pad_starts = pad_ends - padded
    experts = jnp.arange(N_EXPERTS, dtype=jnp.int32)
    start_of = jnp.sum(jnp.where(idx[..., None] == experts, pad_starts, 0), axis=-1)
    pos = (start_of + rank).reshape(-1)
    n_pad = n_slots + N_EXPERTS * bm
    n_blocks = n_pad // bm
    gap_lo = jnp.concatenate([pad_starts + sizes, pad_ends[-1:]])
    gap_hi = jnp.concatenate([pad_ends, jnp.full((1,), n_pad, jnp.int32)])
    slot_src = _slot_sources(pos, jnp.stack([gap_lo, gap_hi]).astype(jnp.int32), n_pad)
    p = jnp.arange(n_pad, dtype=jnp.int32)
    spare = n_slots + (p // bm) % 2 * bm + p % bm
    real = slot_src >= 0
    tok_blocks = jnp.where(real, slot_src % n, 0).reshape(n_blocks, 1, bm)
    dst = jnp.where(real, slot_src, spare)
    dst_blocks = jnp.concatenate([spare[bm:2 * bm], dst]).reshape(n_blocks + 1, 1, bm)
    block_start = jnp.arange(n_blocks, dtype=jnp.int32) * bm
    block_e = jnp.minimum(jnp.sum(pad_ends[None, :] <= block_start[:, None], axis=1),
                          N_EXPERTS - 1).astype(jnp.int32)
    return tok_blocks, dst_blocks, block_e


def _block_diag_ones():
    hid = jnp.arange(MXU_WIDTH, dtype=jnp.int32) // HEAD_DIM
    return (hid[:, None] == hid[None, :]).astype(BF16)


def _pick(n, pref):
    return pref if n % pref == 0 else n


def kernel(x, attn_norm_g, w_in, rwkv_mu, rwkv_w0, rwkv_w_up, rwkv_a0, rwkv_a_up, rwkv_g_up,
           rwkv_k_k, rwkv_k_a, rwkv_r_k, rwkv_ln_w, rwkv_ln_b, fox_f_bias, fox_out_g, w_out,
           ffn_norm_g, router_w, router_b, expert_w1, expert_b1, expert_w2, expert_b2,
           final_norm_g):
    bsz, t, d = x.shape
    n = bsz * t
    depth = w_in.shape[0]
    assert depth == 1, "the final norm is fused into the last stage of a single layer"
    bd = _block_diag_ones()
    x2 = x.reshape(n, d)
    for l in range(depth):
        w_l = w_in[l]
        w_r = w_l[:, :RWKV_IN].astype(BF16)
        w_qkv = w_l[:, RWKV_IN:RWKV_IN + 3 * GROUP_WIDTH].astype(BF16)
        w_qt = w_qkv[:, :GROUP_WIDTH].T
        w_k = w_qkv[:, GROUP_WIDTH:2 * GROUP_WIDTH]
        w_vt = w_qkv[:, 2 * GROUP_WIDTH:].T
        w_f = jnp.pad(w_l[:, RWKV_IN + 3 * GROUP_WIDTH:], ((0, 0), (0, LANES - N_HEADS))).astype(BF16)
        fb_pad = jnp.pad(fox_f_bias[l], (0, LANES - N_HEADS)).reshape(1, LANES)
        wup_pad = jnp.pad(rwkv_w_up[l], ((0, LANES - DECAY_LORA), (0, 0))).astype(BF16)
        aup_pad = jnp.pad(rwkv_a_up[l], ((DECAY_LORA, 0), (0, 0))).astype(BF16)
        gup = rwkv_g_up[l].astype(BF16)
        vec = lambda a: a.reshape(1, -1)

        u_r, qt5, k, vt5, fl = _inproj(x2, vec(attn_norm_g[l]), w_r, w_qt, w_k, w_vt, w_f,
                                       bsz, ATTN_TILE)
        cext, cend = _fox_gate(fl.reshape(bsz, t, LANES), fb_pad, _gate_piece_selectors(),
                               ATTN_TILE)
        y_rwkv = _rwkv_mix(
            u_r.reshape(bsz, t, RWKV_IN), vec(rwkv_mu[l]), vec(rwkv_w0[l]), wup_pad,
            vec(rwkv_a0[l]), aup_pad, gup, vec(rwkv_k_k[l]), vec(rwkv_k_a[l]), vec(rwkv_r_k[l]),
            vec(rwkv_ln_w[l]), vec(rwkv_ln_b[l]), bd, RWKV_CHUNKS_PER_STEP)
        y_fox = _fox_attn(qt5, k.reshape(bsz, t, GROUP_WIDTH), cext, vt5, cend,
                          vec(fox_out_g[l]), ATTN_TILE)

        wo = w_out[l].astype(BF16)
        x1, h2, idx, gates, rank, counts = _outproj_router(
            x2, y_rwkv.reshape(n, GROUP_WIDTH), y_fox.reshape(n, GROUP_WIDTH),
            wo[:GROUP_WIDTH], wo[GROUP_WIDTH:], vec(ffn_norm_g[l]),
            router_w[l].T, router_b[l].reshape(N_EXPERTS, 1), _pick(n, ROUTER_TILE))

        bm = EXPERT_BLOCK
        tok_blocks, dst_blocks, block_e = _dispatch_plan(idx, rank, counts, bm)
        w1g, w1l = _w1_split(expert_w1[l], W1_SPLIT_ROWS)
        b1 = expert_b1[l]
        b1g = b1[:, None, 0::2]
        b1l = b1[:, None, 1::2]
        y_all = _expert_mlp(block_e, tok_blocks, dst_blocks, h2, w1g, w1l, b1g, b1l,
                            expert_w2[l].astype(BF16), expert_b2[l][:, None, :], bm)
        x2 = _combine(gates.T, x1, vec(final_norm_g), y_all, _pick(n, COMBINE_TILE))
    return x2.reshape(bsz, t, d)
```

```python
import functools

import jax
import jax.numpy as jnp
from jax import lax
from jax.experimental import pallas as pl
from jax.experimental.pallas import tpu as pltpu

F32 = jnp.float32
BF16 = jnp.bfloat16

HEAD_DIM = 64
N_HEADS = 8
GROUP_WIDTH = N_HEADS * HEAD_DIM
DECAY_LORA = 64
AAA_LORA = 64
GATE_LORA = 128
RWKV_IN = 3 * GROUP_WIDTH + DECAY_LORA + AAA_LORA + GATE_LORA
LORA_OFF = 3 * GROUP_WIDTH
N_EXPERTS = 32
TOP_K = 4
SWIGLU_ALPHA = 1.702
SWIGLU_LIMIT = 7.0
RMS_EPS = 1e-5
RWKV_GN_EPS = 64e-5
LANES = 128
SUBLANES = 8
MXU_WIDTH = 256
RWKV_CHUNK = 64
RWKV_CHUNKS_PER_STEP = 8
ATTN_TILE = 512
ROUTER_TILE = 512
EXPERT_BLOCK = 256
W1_SPLIT_ROWS = 512
COMBINE_TILE = 256
SLOT_CHUNK = 8192
LOG2E = 1.4426950408889634
Q_SCALE = HEAD_DIM ** -0.5 * LOG2E
ZERO_PROB_EXP = -152.0
NORM_SLACK = 1.0 + 2.0 ** -6
VMEM_LIMIT = 56 * 1024 * 1024


def _cparams(semantics):
    return pltpu.CompilerParams(dimension_semantics=semantics, vmem_limit_bytes=VMEM_LIMIT)


def _dot(a, b):
    return jnp.dot(a, b, preferred_element_type=F32)


def _dot_nt(a, b):
    return lax.dot_general(a, b, (((1,), (1,)), ((), ())), preferred_element_type=F32)


def _dot_tn(a, b):
    return lax.dot_general(a, b, (((0,), (0,)), ((), ())), preferred_element_type=F32)


def _split3(x):
    hi = x.astype(BF16)
    r1 = x - hi.astype(F32)
    mid = r1.astype(BF16)
    lo = (r1 - mid.astype(F32)).astype(BF16)
    return hi, mid, lo


def _dot_exact_lhs(a_bf16, x):
    hi, mid, lo = _split3(x)
    return _dot(a_bf16, hi) + _dot(a_bf16, mid) + _dot(a_bf16, lo)


def _head_sums(x, same_head):
    w = same_head.shape[0]
    parts = []
    for g in range(x.shape[1] // w):
        xs = x[:, g * w:(g + 1) * w]
        hi = xs.astype(BF16)
        lo = (xs - hi.astype(F32)).astype(BF16)
        parts.append(_dot(hi, same_head) + _dot(lo, same_head))
    return jnp.concatenate(parts, axis=1)


def _softplus(z):
    return jnp.maximum(z, 0.0) + jnp.log1p(jnp.exp(-jnp.abs(z)))


def _sigmoid(z):
    return 1.0 / (1.0 + jnp.exp(-z))


def _inproj_kernel(x_ref, g_ref, wr_ref, wqt_ref, wk_ref, wvt_ref, wf_ref,
                   ur_ref, qt_ref, k_ref, vt_ref, fl_ref):
    x = x_ref[...]
    h = x * lax.rsqrt(jnp.mean(x * x, axis=-1, keepdims=True) + RMS_EPS) * g_ref[...]
    hb = h.astype(BF16)
    ur_ref[...] = _dot(hb, wr_ref[...])
    k_ref[...] = _dot(hb, wk_ref[...]).astype(BF16)
    fl_ref[...] = _dot(hb, wf_ref[...])
    qt = (_dot_nt(wqt_ref[...], hb) * Q_SCALE).astype(BF16)
    vt = _dot_nt(wvt_ref[...], hb).astype(BF16)
    for p in range(N_HEADS // 2):
        qt_ref[0, p, 0] = qt[p * LANES:(p + 1) * LANES]
        vt_ref[0, p, 0] = vt[p * LANES:(p + 1) * LANES]


def _inproj(x2, g, w_r, w_qt, w_k, w_vt, w_f, bsz, tm):
    n, d = x2.shape
    nt = n // bsz // tm
    pairs = N_HEADS // 2
    const = lambda i: (0, 0)
    row = lambda i: (i, 0)
    fm = lambda i: (i // nt, 0, i % nt, 0, 0)
    fm_sds = jax.ShapeDtypeStruct((bsz, pairs, nt, LANES, tm), BF16)
    return pl.pallas_call(
        _inproj_kernel,
        grid=(n // tm,),
        in_specs=[
            pl.BlockSpec((tm, d), row),
            pl.BlockSpec((1, d), const),
            pl.BlockSpec(w_r.shape, const),
            pl.BlockSpec(w_qt.shape, const),
            pl.BlockSpec(w_k.shape, const),
            pl.BlockSpec(w_vt.shape, const),
            pl.BlockSpec(w_f.shape, const),
        ],
        out_specs=[
            pl.BlockSpec((tm, RWKV_IN), row),
            pl.BlockSpec((1, pairs, 1, LANES, tm), fm),
            pl.BlockSpec((tm, GROUP_WIDTH), row),
            pl.BlockSpec((1, pairs, 1, LANES, tm), fm),
            pl.BlockSpec((tm, LANES), row),
        ],
        out_shape=[
            jax.ShapeDtypeStruct((n, RWKV_IN), F32),
            fm_sds,
            jax.ShapeDtypeStruct((n, GROUP_WIDTH), BF16),
            fm_sds,
            jax.ShapeDtypeStruct((n, LANES), F32),
        ],
        compiler_params=_cparams(("parallel",)),
        name="inproj",
    )(x2, g, w_r, w_qt, w_k, w_vt, w_f)


def _fox_gate_kernel(fl_ref, fb_ref, sel_ref, c_ref, cend_ref, carry):
    tt = fl_ref.shape[1]

    @pl.when(pl.program_id(1) == 0)
    def _():
        carry[...] = jnp.zeros_like(carry)

    z = fl_ref[0] + fb_ref[...]
    log_f = jnp.minimum(z, 0.0) - jnp.log1p(jnp.exp(-jnp.abs(z)))
    ri = lax.broadcasted_iota(jnp.int32, (tt, tt), 0)
    ci = lax.broadcasted_iota(jnp.int32, (tt, tt), 1)
    tri = jnp.where(ri >= ci, 1.0, 0.0).astype(BF16)
    c = _dot_exact_lhs(tri, log_f) + carry[...]
    carry[...] = c[tt - 1:tt, :]
    cend_ref[0, 0] = c[tt - 1:tt, :]
    hi, mid, lo = _split3(c * LOG2E)
    c_ref[0] = (_dot(hi, sel_ref[0]) + _dot(mid, sel_ref[1]) + _dot(lo, sel_ref[2])).astype(BF16)


def _gate_piece_selectors():
    h = jnp.arange(LANES, dtype=jnp.int32)[:, None]
    col = jnp.arange(GROUP_WIDTH, dtype=jnp.int32)[None, :]
    sels = []
    for m in range(3):
        target = LANES * (h // 2) + 3 * (h % 2) + m
        sels.append(((col == target) & (h < N_HEADS)).astype(BF16))
    return jnp.stack(sels)


def _fox_gate(fl3, fb_pad, sel, tt):
    b, t, _ = fl3.shape
    return pl.pallas_call(
        _fox_gate_kernel,
        grid=(b, t // tt),
        in_specs=[
            pl.BlockSpec((1, tt, LANES), lambda i, j: (i, j, 0)),
            pl.BlockSpec((1, LANES), lambda i, j: (0, 0)),
            pl.BlockSpec((3, LANES, GROUP_WIDTH), lambda i, j: (0, 0, 0)),
        ],
        out_specs=[
            pl.BlockSpec((1, tt, GROUP_WIDTH), lambda i, j: (i, j, 0)),
            pl.BlockSpec((1, 1, 1, LANES), lambda i, j: (i, j, 0, 0)),
        ],
        out_shape=[
            jax.ShapeDtypeStruct((b, t, GROUP_WIDTH), BF16),
            jax.ShapeDtypeStruct((b, t // tt, 1, LANES), F32),
        ],
        scratch_shapes=[pltpu.VMEM((1, LANES), F32)],
        compiler_params=_cparams(("parallel", "arbitrary")),
        name="fox_gate",
    )(fl3, fb_pad, sel)


def _rwkv_prep_body(u_ref, mu_ref, w0_ref, wup_ref, a0_ref, aup_ref, gup_ref, kk_ref, ka_ref,
                      bd_ref, r_out, k_out, v_out, lw_out, kk_out, b_out, g_out, carry):
    tt = u_ref.shape[1]

    @pl.when(pl.program_id(1) == 0)
    def _():
        carry[...] = jnp.zeros_like(carry)

    u = u_ref[0]
    prev = pltpu.roll(u, 1, axis=0)
    row = lax.broadcasted_iota(jnp.int32, u.shape, 0)
    prev = jnp.where(row == 0, carry[...], prev)
    carry[...] = u[tt - 1:tt, :]
    us = u + (prev - u) * mu_ref[...]

    r = us[:, :GROUP_WIDTH]
    k = us[:, GROUP_WIDTH:2 * GROUP_WIDTH]
    v = us[:, 2 * GROUP_WIDTH:LORA_OFF]
    wa = us[:, LORA_OFF:LORA_OFF + LANES]
    gl = us[:, LORA_OFF + LANES:]

    w_lin = _dot(jnp.tanh(wa).astype(BF16), wup_ref[...])
    a_lin = _dot(wa.astype(BF16), aup_ref[...])
    w = -_softplus(-(w0_ref[...] + w_lin)) - 0.5
    lw_out[0] = -jnp.exp(w)
    a = _sigmoid(a0_ref[...] + a_lin)
    g_out[0] = _dot(_sigmoid(gl).astype(BF16), gup_ref[...])

    kkr = k * kk_ref[...]
    ss = _head_sums(kkr * kkr, bd_ref[...])
    kk = kkr / jnp.maximum(jnp.sqrt(ss), 1e-12)
    r_out[0] = r
    k_out[0] = k * (1.0 + (a - 1.0) * ka_ref[...])
    v_out[0] = v
    kk_out[0] = kk
    b_out[0] = kk * a


def _rwkv_scan_body(r_ref, k_ref, v_ref, lw_ref, kk_ref, b_ref, g_ref, rk_ref, lnw_ref, lnb_ref,
                      bd_ref, o_ref, s_scr, *, n_chunks):
    c = RWKV_CHUNK
    w = MXU_WIDTH
    hpg = w // HEAD_DIM
    n_groups = GROUP_WIDTH // w

    @pl.when(pl.program_id(1) == 0)
    def _():
        s_scr[...] = jnp.zeros_like(s_scr)

    row = lax.broadcasted_iota(jnp.int32, (c, w), 0)
    u = lax.broadcasted_iota(jnp.int32, (c, w), 1) % HEAD_DIM
    strict = row > u
    incl = row >= u
    eye = jnp.where(row == u, 1.0, 0.0)
    level_masks = []
    s = 1
    while s < c:
        same = (row // (2 * s)) == (u // (2 * s))
        level_masks.append(same & ((row % (2 * s)) >= s) & ((u % (2 * s)) < s))
        s *= 2
    same_head = (lax.broadcasted_iota(jnp.int32, (w, w), 0) // HEAD_DIM
                 == lax.broadcasted_iota(jnp.int32, (w, w), 1) // HEAD_DIM)
    tri = jnp.where(lax.broadcasted_iota(jnp.int32, (c, c), 0)
                    >= lax.broadcasted_iota(jnp.int32, (c, c), 1), 1.0, 0.0).astype(BF16)

    def bdiag(x):
        xb = x.astype(BF16)
        tiled = jnp.concatenate([xb] * hpg, axis=0)
        return jnp.where(same_head, tiled, jnp.zeros_like(tiled))

    chains = [(ci, gi) for ci in range(n_chunks) for gi in range(n_groups)]
    lhs, rk_t, vbs, xcat, xneg, gam, rt32 = {}, {}, {}, {}, {}, {}, {}
    for ci in range(n_chunks):
        rs = slice(ci * c, (ci + 1) * c)
        r = r_ref[0, rs, :]
        k = k_ref[0, rs, :]
        lw = lw_ref[0, rs, :]
        kk = kk_ref[0, rs, :]
        b = b_ref[0, rs, :]
        g_cum = _dot_exact_lhs(tri, lw)
        g_last = g_cum[c - 1:c, :]
        r_t = r * jnp.exp(g_cum)
        kk_t = kk * jnp.exp(g_cum - lw)
        e_neg = jnp.exp(-g_cum)
        b_n = b * e_neg
        k_n = k * e_neg
        e_end = jnp.exp(g_last - g_cum)
        b_e = (b * e_end).astype(BF16)
        k_e = (k * e_end).astype(BF16)
        gamma = jnp.exp(g_last)
        vb = v_ref[0, rs, :].astype(BF16)
        for gi in range(n_groups):
            gs = slice(gi * w, (gi + 1) * w)
            ch = (ci, gi)
            lhs[ch] = jnp.concatenate([r_t[:, gs].astype(BF16), kk_t[:, gs].astype(BF16)], axis=0)
            rk_t[ch] = (jnp.concatenate([bdiag(b_n[:, gs]), bdiag(k_n[:, gs])], axis=0),
                        bdiag(kk_t[:, gs]))
            vbs[ch] = vb[:, gs]
            xcat[ch] = jnp.concatenate([b_e[:, gs], k_e[:, gs]], axis=0)
            xneg[ch] = jnp.concatenate([-b_e[:, gs], k_e[:, gs]], axis=0)
            gam[ch] = gamma[:, gs]
            rt32[ch] = r_t[:, gs]

    p = {ch: _dot_nt(lhs[ch], rk_t[ch][0]) for ch in chains}
    l_b = {ch: jnp.where(strict, p[ch][c:, :w], 0.0) for ch in chains}
    l_k = {ch: jnp.where(strict, p[ch][c:, w:], 0.0).astype(BF16) for ch in chains}
    p_br = {ch: jnp.where(incl, p[ch][:c, :w], 0.0).astype(BF16) for ch in chains}
    p_kr = {ch: jnp.where(incl, p[ch][:c, w:], 0.0).astype(BF16) for ch in chains}
    v_bd = {ch: bdiag(vbs[ch]) for ch in chains}
    lkv = {ch: _dot(l_k[ch], v_bd[ch]) for ch in chains}

    t_inv = {ch: eye - jnp.where(level_masks[0], l_b[ch], 0.0) for ch in chains}
    for m in level_masks[1:]:
        tb = {ch: t_inv[ch].astype(BF16) for ch in chains}
        ct = {ch: _dot(jnp.where(m, l_b[ch], 0.0).astype(BF16), bdiag(tb[ch])) for ch in chains}
        t_inv = {ch: t_inv[ch] - _dot(tb[ch], bdiag(ct[ch])) for ch in chains}

    mm = {ch: _dot(t_inv[ch].astype(BF16),
                   jnp.concatenate([rk_t[ch][1], bdiag(lkv[ch])], axis=1)) for ch in chains}
    pm = {ch: _dot(p_br[ch], jnp.concatenate([bdiag(mm[ch][:, :w]), bdiag(mm[ch][:, w:])], axis=1))
          for ch in chains}
    pkv = {ch: _dot(p_kr[ch], v_bd[ch]) for ch in chains}
    n1 = {ch: (rt32[ch] - pm[ch][:, :w]).astype(BF16) for ch in chains}
    n2 = {ch: pkv[ch] - pm[ch][:, w:] for ch in chains}
    omega = {ch: jnp.where(same_head, _dot_tn(mm[ch][:, :w].astype(BF16), xcat[ch][:c]), 0.0)
             .astype(BF16) for ch in chains}
    psi = {ch: jnp.where(same_head, _dot_tn(
        jnp.concatenate([mm[ch][:, w:].astype(BF16), vbs[ch]], axis=0), xneg[ch]), 0.0)
           for ch in chains}

    state = [s_scr[gi] for gi in range(n_groups)]
    ys = []
    for ci in range(n_chunks):
        sb = [state[gi].astype(BF16) for gi in range(n_groups)]
        ys.append(jnp.concatenate(
            [_dot_nt(n1[(ci, gi)], sb[gi]) + n2[(ci, gi)] for gi in range(n_groups)], axis=1))
        state = [state[gi] * gam[(ci, gi)] - _dot(sb[gi], omega[(ci, gi)]) + psi[(ci, gi)]
                 for gi in range(n_groups)]
    for gi in range(n_groups):
        s_scr[gi] = state[gi]

    y = jnp.concatenate(ys, axis=0)
    r = r_ref[0]
    k = k_ref[0]
    v = v_ref[0]
    bd = bd_ref[...]
    inv_n = 1.0 / HEAD_DIM
    mean = _head_sums(y, bd) * inv_n
    d = y - mean
    var = _head_sums(d * d, bd) * inv_n
    yn = d * lax.rsqrt(var + RWKV_GN_EPS) * lnw_ref[...] + lnb_ref[...]
    bonus = _head_sums(r * k * rk_ref[...], bd) * v
    o_ref[0] = ((yn + bonus) * g_ref[0]).astype(o_ref.dtype)


def _rwkv_kernel(u_ref, mu_ref, w0_ref, wup_ref, a0_ref, aup_ref, gup_ref, kkw_ref, ka_ref,
                 rk_ref, lnw_ref, lnb_ref, bd_ref, o_ref, carry, s_scr,
                 r_s, k_s, v_s, lw_s, kk_s, b_s, g_s, *, n_chunks):
    _rwkv_prep_body(u_ref, mu_ref, w0_ref, wup_ref, a0_ref, aup_ref, gup_ref, kkw_ref, ka_ref,
                    bd_ref, r_s, k_s, v_s, lw_s, kk_s, b_s, g_s, carry)
    _rwkv_scan_body(r_s, k_s, v_s, lw_s, kk_s, b_s, g_s, rk_ref, lnw_ref, lnb_ref, bd_ref,
                    o_ref, s_scr, n_chunks=n_chunks)


def _rwkv_mix(u3, mu, w0, wup_pad, a0, aup_pad, gup, k_k, k_a, r_k, ln_w, ln_b, bd, n_chunks):
    bsz, t, _ = u3.shape
    rows = RWKV_CHUNK * n_chunks
    const = lambda i, j: (0, 0)
    tile = lambda i, j: (i, j, 0)
    vec = pl.BlockSpec((1, GROUP_WIDTH), const)
    staged = pltpu.VMEM((1, rows, GROUP_WIDTH), F32)
    return pl.pallas_call(
        functools.partial(_rwkv_kernel, n_chunks=n_chunks),
        grid=(bsz, t // rows),
        in_specs=[
            pl.BlockSpec((1, rows, RWKV_IN), tile),
            pl.BlockSpec((1, RWKV_IN), const),
            vec,
            pl.BlockSpec((LANES, GROUP_WIDTH), const),
            vec,
            pl.BlockSpec((LANES, GROUP_WIDTH), const),
            pl.BlockSpec((GATE_LORA, GROUP_WIDTH), const),
            vec, vec, vec, vec, vec,
            pl.BlockSpec((MXU_WIDTH, MXU_WIDTH), const),
        ],
        out_specs=pl.BlockSpec((1, rows, GROUP_WIDTH), tile),
        out_shape=jax.ShapeDtypeStruct((bsz, t, GROUP_WIDTH), BF16),
        scratch_shapes=[
            pltpu.VMEM((1, RWKV_IN), F32),
            pltpu.VMEM((GROUP_WIDTH // MXU_WIDTH, MXU_WIDTH, MXU_WIDTH), F32),
        ] + [staged] * 7,
        compiler_params=_cparams(("parallel", "arbitrary")),
        name="rwkv_mix",
    )(u3, mu, w0, wup_pad, a0, aup_pad, gup, k_k, k_a, r_k, ln_w, ln_b, bd)


def _fox_attn_kernel(qt_ref, k_ref, ce_ref, vt_ref, cend_ref, og_ref, o_ref,
                     m_scr, l_scr, acc_scr, kmax_scr, *, t):
    qi = pl.program_id(2)
    n_strips = 2 * t // LANES
    qt = qt_ref[0, 0, 0]
    frow = lax.broadcasted_iota(jnp.int32, (LANES, t), 0)
    zero = jnp.zeros_like(qt)
    main = jnp.concatenate([jnp.where(frow < HEAD_DIM, qt, zero),
                            jnp.where(frow < HEAD_DIM, zero, qt)], axis=1)
    erow = lax.broadcasted_iota(jnp.int32, (LANES, 2 * t), 0)
    ecol = lax.broadcasted_iota(jnp.int32, (LANES, 2 * t), 1)
    off = jnp.where(ecol < t, 0, 3)
    extra = jnp.where((erow >= off) & (erow < off + 3), -1.0, 0.0).astype(BF16)
    q_aug = jnp.concatenate([main, extra], axis=0)

    m_scr[...] = jnp.full_like(m_scr, -jnp.inf)
    l_scr[...] = jnp.zeros_like(l_scr)
    acc_scr[...] = jnp.zeros_like(acc_scr)

    @pl.when(qi == 0)
    def _():
        hid_r = lax.broadcasted_iota(jnp.int32, (LANES, LANES), 0) // HEAD_DIM
        hid_c = lax.broadcasted_iota(jnp.int32, (LANES, LANES), 1) // HEAD_DIM
        same_head = jnp.where(hid_r == hid_c, 1.0, 0.0).astype(BF16)

        def tile_max(j, best):
            kf = k_ref[0, pl.ds(pl.multiple_of(j * t, t), t), :].astype(F32)
            sq = _dot((kf * kf).astype(BF16), same_head)
            return jnp.maximum(best, jnp.max(sq, axis=0, keepdims=True))

        best = lax.fori_loop(0, k_ref.shape[1] // t, tile_max, jnp.zeros((1, LANES), F32))
        kmax_scr[...] = jnp.sqrt(best) * NORM_SLACK

    def step(j, masked):
        start = pl.multiple_of(j * t, t)
        k_aug = jnp.concatenate([k_ref[0, pl.ds(start, t), :], ce_ref[0, pl.ds(start, t), :]],
                                axis=1)
        vt = vt_ref[0, 0, j]
        zt = _dot(k_aug, q_aug)
        m_prev = m_scr[...]
        l_prev = l_scr[...]
        acc_prev = acc_scr[...]
        m_out, l_out, acc_out = [], [], [[], []]
        for s in range(n_strips):
            head = s // (n_strips // 2)
            cs = slice(s * LANES, (s + 1) * LANES)
            z = zt[:, cs]
            if masked:
                key = lax.broadcasted_iota(jnp.int32, (t, LANES), 0)
                qry = lax.broadcasted_iota(jnp.int32, (t, LANES), 1) + (s * LANES) % t
                z = jnp.where(key <= qry, z, -jnp.inf)
            m_new = jnp.maximum(m_prev[:, cs], jnp.max(z, axis=0, keepdims=True))
            alpha = jnp.exp2(m_prev[:, cs] - m_new)
            p = jnp.exp2(z - m_new)
            l_out.append(alpha * l_prev[:, cs] + jnp.sum(p, axis=0, keepdims=True))
            m_out.append(m_new)
            hs = slice(head * HEAD_DIM, (head + 1) * HEAD_DIM)
            qs = slice((s * LANES) % t, (s * LANES) % t + LANES)
            pv = _dot(vt[hs], p.astype(BF16))
            acc_out[head].append(alpha * acc_prev[hs, qs] + pv)
        m_scr[...] = jnp.concatenate(m_out, axis=1)
        l_scr[...] = jnp.concatenate(l_out, axis=1)
        acc_scr[...] = jnp.concatenate([jnp.concatenate(acc_out[0], axis=1),
                                        jnp.concatenate(acc_out[1], axis=1)], axis=0)

    def body(j, carry):
        step(j, False)
        return carry

    step(qi, True)

    qf = qt.astype(F32)
    qsq = qf * qf
    qnorm = jnp.sqrt(jnp.concatenate(
        [jnp.sum(qsq[:HEAD_DIM], axis=0, keepdims=True),
         jnp.sum(qsq[HEAD_DIM:], axis=0, keepdims=True)], axis=1)) * NORM_SLACK
    kmax = kmax_scr[...]
    kmax2 = jnp.concatenate([jnp.broadcast_to(kmax[:, 0:1], (1, t)),
                             jnp.broadcast_to(kmax[:, HEAD_DIM:HEAD_DIM + 1], (1, t))], axis=1)
    slack = qnorm * kmax2 - m_scr[...]
    cend = cend_ref[0][:, 0, :]
    lane = lax.broadcasted_iota(jnp.int32, cend.shape, 1)
    jrow = lax.broadcasted_iota(jnp.int32, (cend.shape[0], 1), 0)
    needed = jrow < 0
    for head in range(2):
        worst = jnp.max(slack[:, head * t:(head + 1) * t], axis=1, keepdims=True)
        c_head = jnp.sum(jnp.where(lane == 2 * pl.program_id(1) + head, cend, 0.0),
                         axis=1, keepdims=True)
        needed = needed | (worst - c_head * LOG2E > ZERO_PROB_EXP)
    first = jnp.min(jnp.where(needed & (jrow < qi), jrow, qi))
    lax.fori_loop(first, qi, body, 0)

    l = l_scr[...]
    acc = acc_scr[...]
    inv_n = 1.0 / HEAD_DIM
    halves = []
    for head in range(2):
        o = acc[head * HEAD_DIM:(head + 1) * HEAD_DIM] / l[:, head * t:(head + 1) * t]
        halves.append(o * lax.rsqrt(jnp.sum(o * o, axis=0, keepdims=True) * inv_n + RMS_EPS))
    o_t = jnp.concatenate(halves, axis=0)
    o_ref[0] = (jnp.transpose(o_t) * og_ref[...]).astype(o_ref.dtype)


def _fox_attn(qt5, k, cext, vt5, cend, out_g, tile):
    bsz, t_all, _ = k.shape
    pairs = N_HEADS // 2
    tiles = t_all // tile
    kern = functools.partial(_fox_attn_kernel, t=tile)
    return pl.pallas_call(
        kern,
        grid=(bsz, pairs, tiles),
        in_specs=[
            pl.BlockSpec((1, 1, 1, LANES, tile), lambda b, p, i: (b, p, i, 0, 0)),
            pl.BlockSpec((1, t_all, LANES), lambda b, p, i: (b, 0, p)),
            pl.BlockSpec((1, t_all, LANES), lambda b, p, i: (b, 0, p)),
            pl.BlockSpec((1, 1, tiles, LANES, tile), lambda b, p, i: (b, p, 0, 0, 0)),
            pl.BlockSpec((1, tiles, 1, LANES), lambda b, p, i: (b, 0, 0, 0)),
            pl.BlockSpec((1, LANES), lambda b, p, i: (0, p)),
        ],
        out_specs=pl.BlockSpec((1, tile, LANES), lambda b, p, i: (b, i, p)),
        out_shape=jax.ShapeDtypeStruct((bsz, t_all, GROUP_WIDTH), BF16),
        scratch_shapes=[
            pltpu.VMEM((1, 2 * tile), F32),
            pltpu.VMEM((1, 2 * tile), F32),
            pltpu.VMEM((LANES, tile), F32),
            pltpu.VMEM((1, LANES), F32),
        ],
        compiler_params=_cparams(("parallel", "parallel", "arbitrary")),
        name="fox_attn",
    )(qt5, k, cext, vt5, cend, out_g)


def _outproj_router_kernel(x_ref, yr_ref, yf_ref, wo_r_ref, wo_f_ref, g_ref, rwt_ref, rb_ref,
                           x1_ref, h_ref, idx_ref, gate_ref, rank_ref, count_ref):
    @pl.when(pl.program_id(0) == 0)
    def _():
        count_ref[...] = jnp.zeros_like(count_ref)

    x1 = x_ref[...] + _dot(yr_ref[...], wo_r_ref[...]) + _dot(yf_ref[...], wo_f_ref[...])
    x1_ref[...] = x1
    h = x1 * lax.rsqrt(jnp.mean(x1 * x1, axis=-1, keepdims=True) + RMS_EPS) * g_ref[...]
    bits = lax.bitcast_convert_type(h.astype(BF16).astype(F32), jnp.uint32)
    half = h.shape[1] // 2
    h_ref[:, 0, :] = (bits[:, half:] & jnp.uint32(0xFFFF0000)) | (bits[:, :half] >> 16)
    logits = lax.dot_general(rwt_ref[...], h, (((1,), (1,)), ((), ())),
                             precision=lax.Precision.HIGHEST,
                             preferred_element_type=F32) + rb_ref[...]
    eidx = lax.broadcasted_iota(jnp.int32, logits.shape, 0)
    vals, idxs, picks = [], [], []
    for _ in range(TOP_K):
        m = jnp.max(logits, axis=0, keepdims=True)
        i = jnp.min(jnp.where(logits == m, eidx, N_EXPERTS), axis=0, keepdims=True)
        vals.append(m)
        idxs.append(i)
        picks.append(eidx == i)
        logits = jnp.where(picks[-1], -jnp.inf, logits)
    es = [jnp.exp(val - vals[0]) for val in vals]
    denom = es[0] + es[1] + es[2] + es[3]
    idx_ref[...] = jnp.concatenate(idxs, axis=0)
    gate_ref[...] = jnp.concatenate([e / denom for e in es], axis=0)

    tm = logits.shape[1]
    chosen = [jnp.where(pk, 1.0, 0.0) for pk in picks]
    any_k = chosen[0] + chosen[1] + chosen[2] + chosen[3]
    before = (lax.broadcasted_iota(jnp.int32, (tm, tm), 0)
              < lax.broadcasted_iota(jnp.int32, (tm, tm), 1))
    prefix = _dot(any_k.astype(BF16), jnp.where(before, 1.0, 0.0).astype(BF16))
    seen = count_ref[:, 0:1] + prefix
    rank_ref[...] = jnp.concatenate(
        [jnp.sum(ch * seen, axis=0, keepdims=True) for ch in chosen], axis=0).astype(jnp.int32)
    count_ref[...] = count_ref[...] + jnp.sum(any_k, axis=1, keepdims=True)


def _outproj_router(x2, yr, yf, wo_r, wo_f, g, rwt, rb, tm):
    n, d = x2.shape
    const = lambda i: (0, 0)
    row = lambda i: (i, 0)
    col = lambda i: (0, i)
    return pl.pallas_call(
        _outproj_router_kernel,
        grid=(n // tm,),
        in_specs=[
            pl.BlockSpec((tm, d), row),
            pl.BlockSpec((tm, GROUP_WIDTH), row),
            pl.BlockSpec((tm, GROUP_WIDTH), row),
            pl.BlockSpec((GROUP_WIDTH, d), const),
            pl.BlockSpec((GROUP_WIDTH, d), const),
            pl.BlockSpec((1, d), const),
            pl.BlockSpec((N_EXPERTS, d), const),
            pl.BlockSpec((N_EXPERTS, 1), const),
        ],
        out_specs=[
            pl.BlockSpec((tm, d), row),
            pl.BlockSpec((tm, 1, d // 2), lambda i: (i, 0, 0)),
            pl.BlockSpec((TOP_K, tm), col),
            pl.BlockSpec((TOP_K, tm), col),
            pl.BlockSpec((TOP_K, tm), col),
            pl.BlockSpec((N_EXPERTS, LANES), const),
        ],
        out_shape=[
            jax.ShapeDtypeStruct((n, d), F32),
            jax.ShapeDtypeStruct((n, 1, d // 2), jnp.uint32),
            jax.ShapeDtypeStruct((TOP_K, n), jnp.int32),
            jax.ShapeDtypeStruct((TOP_K, n), F32),
            jax.ShapeDtypeStruct((TOP_K, n), jnp.int32),
            jax.ShapeDtypeStruct((N_EXPERTS, LANES), F32),
        ],
        compiler_params=_cparams(("arbitrary",)),
        name="outproj_router",
    )(x2, yr, yf, wo_r, wo_f, g, rwt, rb)


def _w1_split_kernel(w_ref, perm_ref, g_ref, l_ref):
    half = MXU_WIDTH // 2
    perm = perm_ref[...]
    for grp in range(w_ref.shape[2] // MXU_WIDTH):
        blk = w_ref[0, :, grp * MXU_WIDTH:(grp + 1) * MXU_WIDTH].astype(BF16)
        r = _dot(blk, perm)
        g_ref[0, :, grp * half:(grp + 1) * half] = r[:, :half].astype(BF16)
        l_ref[0, :, grp * half:(grp + 1) * half] = r[:, half:].astype(BF16)


def _w1_split(w1, tr):
    e, d, two_f = w1.shape
    half = MXU_WIDTH // 2
    src = jnp.arange(MXU_WIDTH, dtype=jnp.int32)[:, None]
    dst = jnp.arange(MXU_WIDTH, dtype=jnp.int32)[None, :]
    perm = (src == jnp.where(dst < half, 2 * dst, 2 * (dst - half) + 1)).astype(BF16)
    out_sds = jax.ShapeDtypeStruct((e, d, two_f // 2), BF16)
    return pl.pallas_call(
        _w1_split_kernel,
        grid=(e, d // tr),
        in_specs=[
            pl.BlockSpec((1, tr, two_f), lambda i, j: (i, j, 0)),
            pl.BlockSpec((MXU_WIDTH, MXU_WIDTH), lambda i, j: (0, 0)),
        ],
        out_specs=[pl.BlockSpec((1, tr, two_f // 2), lambda i, j: (i, j, 0))] * 2,
        out_shape=[out_sds, out_sds],
        compiler_params=_cparams(("parallel", "parallel")),
        name="w1_split",
    )(w1, perm)


def _expert_kernel(be_ref, tok_a_ref, tok_b_ref, tok_a_next_ref, dst_b_prev_ref, dst_a_ref,
                   dst_b_ref, h_hbm, w1g_a, w1l_a, b1g_a, b1l_a, w2_a, b2_a,
                   w1g_b, w1l_b, b1g_b, b1l_b, w2_b, b2_b,
                   y_hbm, xbuf_a, xbuf_b, obuf_a, obuf_b, xrows, gsem, osem, *, bm, n_real_rows):
    del be_ref
    i = pl.program_id(0)

    def gather_start(idx_ref, xbuf, sem, r, priority=0):
        pltpu.make_async_copy(h_hbm.at[idx_ref[0, 0, r]], xbuf.at[r], sem).start(priority=priority)

    def scatter_start(idx_ref, obuf, sem, r, priority=0):
        pltpu.make_async_copy(obuf.at[r], y_hbm.at[idx_ref[0, 0, r]], sem).start(priority=priority)

    def rows_wait(buf, sem):
        rows = buf.at[pl.ds(0, bm)]
        pltpu.make_async_copy(rows, rows, sem).wait()

    def mlp(xbuf, w1g_ref, w1l_ref, b1g_ref, b1l_ref, w2_ref, b2_ref, start_gather, next_xbuf):
        xrows[...] = xbuf[pl.ds(0, bm), 0, :]
        words = xrows[...]
        xb_lo = lax.bitcast_convert_type(words << 16, F32).astype(BF16)
        xb_hi = lax.bitcast_convert_type(words & jnp.uint32(0xFFFF0000), F32).astype(BF16)
        half = words.shape[1]

        def x_dot(w_ref, cs):
            return _dot(xb_lo, w_ref[0, :half, cs]) + _dot(xb_hi, w_ref[0, half:, cs])
        dff = w1g_ref.shape[2]
        n_pieces = dff // MXU_WIDTH
        per_piece = -(-bm // (n_pieces - 1))
        acts = []
        for piece in range(n_pieces):
            cs = slice(piece * MXU_WIDTH, (piece + 1) * MXU_WIDTH)
            for r in range(piece * per_piece, min((piece + 1) * per_piece, bm)):
                start_gather(r)
            zero = lax.bitcast_convert_type(next_xbuf[bm + piece, :, :MXU_WIDTH], F32)
            glu = x_dot(w1g_ref, cs) + (b1g_ref[0, :, cs] + zero)
            lin = x_dot(w1l_ref, cs) + b1l_ref[0, :, cs]
            glu = jnp.minimum(glu, SWIGLU_LIMIT)
            lin = jnp.clip(lin, -SWIGLU_LIMIT, SWIGLU_LIMIT)
            acts.append((glu * _sigmoid(SWIGLU_ALPHA * glu) * (lin + 1.0)).astype(BF16))
        return _dot(jnp.concatenate(acts, axis=1), w2_ref[0]) + b2_ref[0]

    @pl.when(i == 0)
    def _():
        obuf_a[...] = jnp.zeros_like(obuf_a)
        obuf_b[...] = jnp.zeros_like(obuf_b)
        spare = jnp.zeros((SUBLANES, 1, xbuf_a.shape[2]), jnp.uint32)
        xbuf_a[pl.ds(bm, SUBLANES)] = spare
        xbuf_b[pl.ds(bm, SUBLANES)] = spare

        def first(r, carry):
            pltpu.make_async_copy(obuf_a.at[r], y_hbm.at[n_real_rows + 2 * bm + r],
                                  osem.at[0]).start()
            gather_start(tok_a_ref, xbuf_a, gsem.at[0], r)
            return carry

        lax.fori_loop(0, bm, first, 0)

    def gather_b(r):
        gather_start(tok_b_ref, xbuf_b, gsem.at[1], r, r % 2)

    def gather_a_next(r):
        gather_start(tok_a_next_ref, xbuf_a, gsem.at[0], r, r % 2)

    for r in range(bm):
        scatter_start(dst_b_prev_ref, obuf_b, osem.at[1], r, r % 2)
    rows_wait(xbuf_a, gsem.at[0])
    rows_wait(obuf_a, osem.at[0])
    obuf_a[:, 0, :] = mlp(xbuf_a, w1g_a, w1l_a, b1g_a, b1l_a, w2_a, b2_a, gather_b, xbuf_b)

    for r in range(bm):
        scatter_start(dst_a_ref, obuf_a, osem.at[0], r, r % 2)
    rows_wait(xbuf_b, gsem.at[1])
    rows_wait(obuf_b, osem.at[1])
    obuf_b[:, 0, :] = mlp(xbuf_b, w1g_b, w1l_b, b1g_b, b1l_b, w2_b, b2_b, gather_a_next, xbuf_a)

    @pl.when(i == pl.num_programs(0) - 1)
    def _():
        def last(r, carry):
            scatter_start(dst_b_ref, obuf_b, osem.at[1], r)
            return carry

        lax.fori_loop(0, bm, last, 0)
        rows_wait(obuf_b, osem.at[1])
        rows_wait(obuf_a, osem.at[0])
        rows_wait(xbuf_a, gsem.at[0])


def _expert_mlp(block_e, tok_blocks, dst_blocks, h2, w1g, w1l, b1g, b1l, w2, b2, bm):
    n_blocks = tok_blocks.shape[0]
    assert n_blocks % 2 == 0
    n = h2.shape[0]
    d = w1g.shape[1]
    dw = h2.shape[2]
    dff = w1g.shape[2]
    n_real_rows = TOP_K * n
    idx_spec = lambda fn: pl.BlockSpec((1, 1, bm), fn, memory_space=pltpu.SMEM)

    def weight_specs(which):
        wmap = lambda i, be: (be[2 * i + which], 0, 0)
        return [
            pl.BlockSpec((1, d, dff), wmap),
            pl.BlockSpec((1, d, dff), wmap),
            pl.BlockSpec((1, 1, dff), wmap),
            pl.BlockSpec((1, 1, dff), wmap),
            pl.BlockSpec((1, dff, d), wmap),
            pl.BlockSpec((1, 1, d), wmap),
        ]

    grid_spec = pltpu.PrefetchScalarGridSpec(
        num_scalar_prefetch=1,
        grid=(n_blocks // 2,),
        in_specs=[
            idx_spec(lambda i, be: (2 * i, 0, 0)),
            idx_spec(lambda i, be: (2 * i + 1, 0, 0)),
            idx_spec(lambda i, be: (jnp.minimum(2 * i + 2, n_blocks - 1), 0, 0)),
            idx_spec(lambda i, be: (2 * i, 0, 0)),
            idx_spec(lambda i, be: (2 * i + 1, 0, 0)),
            idx_spec(lambda i, be: (2 * i + 2, 0, 0)),
            pl.BlockSpec(memory_space=pl.ANY),
        ] + weight_specs(0) + weight_specs(1),
        out_specs=pl.BlockSpec(memory_space=pl.ANY),
        scratch_shapes=[
            pltpu.VMEM((bm + SUBLANES, 1, dw), jnp.uint32),
            pltpu.VMEM((bm + SUBLANES, 1, dw), jnp.uint32),
            pltpu.VMEM((bm, 1, d), F32),
            pltpu.VMEM((bm, 1, d), F32),
            pltpu.VMEM((bm, dw), jnp.uint32),
            pltpu.SemaphoreType.DMA((2,)),
            pltpu.SemaphoreType.DMA((2,)),
        ],
    )
    weights = (w1g, w1l, b1g, b1l, w2, b2)
    return pl.pallas_call(
        functools.partial(_expert_kernel, bm=bm, n_real_rows=n_real_rows),
        grid_spec=grid_spec,
        out_shape=jax.ShapeDtypeStruct((n_real_rows + 3 * bm, 1, d), F32),
        compiler_params=_cparams(("arbitrary",)),
        name="expert_mlp",
    )(block_e, tok_blocks, tok_blocks, tok_blocks, dst_blocks, dst_blocks, dst_blocks, h2,
      *weights, *weights)


def _combine_kernel(gate_ref, x1_ref, g_ref, y0_ref, y1_ref, y2_ref, y3_ref, o_ref):
    gates = gate_ref[...]
    y = x1_ref[...]
    for kk, y_ref in enumerate((y0_ref, y1_ref, y2_ref, y3_ref)):
        y = y + y_ref[:, 0, :] * gates[:, kk:kk + 1]
    o_ref[...] = y * lax.rsqrt(jnp.mean(y * y, axis=-1, keepdims=True) + RMS_EPS) * g_ref[...]


def _combine(gates_t, x1, g, y_all, tc):
    n, d = x1.shape
    tiles = n // tc
    y_spec = lambda kk: pl.BlockSpec((tc, 1, d), lambda i: (kk * tiles + i, 0, 0))
    return pl.pallas_call(
        _combine_kernel,
        grid=(tiles,),
        in_specs=[
            pl.BlockSpec((tc, TOP_K), lambda i: (i, 0)),
            pl.BlockSpec((tc, d), lambda i: (i, 0)),
            pl.BlockSpec((1, d), lambda i: (0, 0)),
        ] + [y_spec(kk) for kk in range(TOP_K)],
        out_specs=pl.BlockSpec((tc, d), lambda i: (i, 0)),
        out_shape=jax.ShapeDtypeStruct((n, d), F32),
        compiler_params=_cparams(("parallel",)),
        name="combine",
    )(gates_t, x1, g, y_all, y_all, y_all, y_all)


def _slot_sources_kernel(gaps_ref, pos_ref, out_ref, *, chunk):
    i = pl.program_id(0)

    @pl.when(i == 0)
    def _():
        def init(p, carry):
            out_ref[p] = -1
            return carry

        for e in range(gaps_ref.shape[1]):
            lax.fori_loop(gaps_ref[0, e], gaps_ref[1, e], init, 0)

    base = i * chunk

    def place(s, carry):
        out_ref[pos_ref[0, 0, s]] = base + s
        return carry

    lax.fori_loop(0, chunk, place, 0, unroll=16)


def _slot_sources(pos, gaps, n_pad):
    n_slots = pos.shape[0]
    chunk = _pick(n_slots, SLOT_CHUNK)
    return pl.pallas_call(
        functools.partial(_slot_sources_kernel, chunk=chunk),
        grid=(n_slots // chunk,),
        in_specs=[
            pl.BlockSpec(memory_space=pltpu.SMEM),
            pl.BlockSpec((1, 1, chunk), lambda i: (i, 0, 0), memory_space=pltpu.SMEM),
        ],
        out_specs=pl.BlockSpec(memory_space=pltpu.SMEM),
        out_shape=jax.ShapeDtypeStruct((n_pad,), jnp.int32),
        compiler_params=_cparams(("arbitrary",)),
        name="slot_sources",
    )(gaps, pos.reshape(n_slots // chunk, 1, chunk))


def _dispatch_plan(idx, rank, counts, bm):
    n = idx.shape[1]
    n_slots = TOP_K * n
    sizes = counts[:, 0].astype(jnp.int32)
    padded = (sizes + bm - 1) // bm * bm
    pad_ends = jnp.cumsum(padded)
    pad_starts = pad_ends - padded
    experts = jnp.arange(N_EXPERTS, dtype=jnp.int32)
    start_of = jnp.sum(jnp.where(idx[..., None] == experts, pad_starts, 0), axis=-1)
    pos = (start_of + rank).reshape(-1)
    n_pad = n_slots + N_EXPERTS * bm
    n_blocks = n_pad // bm
    gap_lo = jnp.concatenate([pad_starts + sizes, pad_ends[-1:]])
    gap_hi = jnp.concatenate([pad_ends, jnp.full((1,), n_pad, jnp.int32)])
    slot_src = _slot_sources(pos, jnp.stack([gap_lo, gap_hi]).astype(jnp.int32), n_pad)
    p = jnp.arange(n_pad, dtype=jnp.int32)
    spare = n_slots + (p // bm) % 2 * bm + p % bm
    real = slot_src >= 0
    tok_blocks = jnp.where(real, slot_src % n, 0).reshape(n_blocks, 1, bm)
    dst = jnp.where(real, slot_src, spare)
    dst_blocks = jnp.concatenate([spare[bm:2 * bm], dst]).reshape(n_blocks + 1, 1, bm)
    block_start = jnp.arange(n_blocks, dtype=jnp.int32) * bm
    block_e = jnp.minimum(jnp.sum(pad_ends[None, :] <= block_start[:, None], axis=1),
                          N_EXPERTS - 1).astype(jnp.int32)
    return tok_blocks, dst_blocks, block_e


def _block_diag_ones():
    hid = jnp.arange(MXU_WIDTH, dtype=jnp.int32) // HEAD_DIM
    return (hid[:, None] == hid[None, :]).astype(BF16)


def _pick(n, pref):
    return pref if n % pref == 0 else n


def kernel(x, attn_norm_g, w_in, rwkv_mu, rwkv_w0, rwkv_w_up, rwkv_a0, rwkv_a_up, rwkv_g_up,
           rwkv_k_k, rwkv_k_a, rwkv_r_k, rwkv_ln_w, rwkv_ln_b, fox_f_bias, fox_out_g, w_out,
           ffn_norm_g, router_w, router_b, expert_w1, expert_b1, expert_w2, expert_b2,
           final_norm_g):
    bsz, t, d = x.shape
    n = bsz * t
    depth = w_in.shape[0]
    assert depth == 1, "the final norm is fused into the last stage of a single layer"
    bd = _block_diag_ones()
    x2 = x.reshape(n, d)
    for l in range(depth):
        w_l = w_in[l]
        w_r = w_l[:, :RWKV_IN].astype(BF16)
        w_qkv = w_l[:, RWKV_IN:RWKV_IN + 3 * GROUP_WIDTH].astype(BF16)
        w_qt = w_qkv[:, :GROUP_WIDTH].T
        w_k = w_qkv[:, GROUP_WIDTH:2 * GROUP_WIDTH]
        w_vt = w_qkv[:, 2 * GROUP_WIDTH:].T
        w_f = jnp.pad(w_l[:, RWKV_IN + 3 * GROUP_WIDTH:], ((0, 0), (0, LANES - N_HEADS))).astype(BF16)
        fb_pad = jnp.pad(fox_f_bias[l], (0, LANES - N_HEADS)).reshape(1, LANES)
        wup_pad = jnp.pad(rwkv_w_up[l], ((0, LANES - DECAY_LORA), (0, 0))).astype(BF16)
        aup_pad = jnp.pad(rwkv_a_up[l], ((DECAY_LORA, 0), (0, 0))).astype(BF16)
        gup = rwkv_g_up[l].astype(BF16)
        vec = lambda a: a.reshape(1, -1)

        u_r, qt5, k, vt5, fl = _inproj(x2, vec(attn_norm_g[l]), w_r, w_qt, w_k, w_vt, w_f,
                                       bsz, ATTN_TILE)
        cext, cend = _fox_gate(fl.reshape(bsz, t, LANES), fb_pad, _gate_piece_selectors(),
                               ATTN_TILE)
        y_rwkv = _rwkv_mix(
            u_r.reshape(bsz, t, RWKV_IN), vec(rwkv_mu[l]), vec(rwkv_w0[l]), wup_pad,
            vec(rwkv_a0[l]), aup_pad, gup, vec(rwkv_k_k[l]), vec(rwkv_k_a[l]), vec(rwkv_r_k[l]),
            vec(rwkv_ln_w[l]), vec(rwkv_ln_b[l]), bd, RWKV_CHUNKS_PER_STEP)
        y_fox = _fox_attn(qt5, k.reshape(bsz, t, GROUP_WIDTH), cext, vt5, cend,
                          vec(fox_out_g[l]), ATTN_TILE)

        wo = w_out[l].astype(BF16)
        x1, h2, idx, gates, rank, counts = _outproj_router(
            x2, y_rwkv.reshape(n, GROUP_WIDTH), y_fox.reshape(n, GROUP_WIDTH),
            wo[:GROUP_WIDTH], wo[GROUP_WIDTH:], vec(ffn_norm_g[l]),
            router_w[l].T, router_b[l].reshape(N_EXPERTS, 1), _pick(n, ROUTER_TILE))

        bm = EXPERT_BLOCK
        tok_blocks, dst_blocks, block_e = _dispatch_plan(idx, rank, counts, bm)
        w1g, w1l = _w1_split(expert_w1[l], W1_SPLIT_ROWS)
        b1 = expert_b1[l]
        b1g = b1[:, None, 0::2]
        b1l = b1[:, None, 1::2]
        y_all = _expert_mlp(block_e, tok_blocks, dst_blocks, h2, w1g, w1l, b1g, b1l,
                            expert_w2[l].astype(BF16), expert_b2[l][:, None, :], bm)
        x2 = _combine(gates.T, x1, vec(final_norm_g), y_all, _pick(n, COMBINE_TILE))
    return x2.reshape(bsz, t, d)
```

```python
import functools

import jax
import jax.numpy as jnp
from jax import lax
from jax.experimental import pallas as pl
from jax.experimental.pallas import tpu as pltpu

F32 = jnp.float32
BF16 = jnp.bfloat16

HEAD_DIM = 64
N_HEADS = 8
GROUP_WIDTH = N_HEADS * HEAD_DIM
DECAY_LORA = 64
AAA_LORA = 64
GATE_LORA = 128
RWKV_IN = 3 * GROUP_WIDTH + DECAY_LORA + AAA_LORA + GATE_LORA
LORA_OFF = 3 * GROUP_WIDTH
N_EXPERTS = 32
TOP_K = 4
SWIGLU_ALPHA = 1.702
SWIGLU_LIMIT = 7.0
RMS_EPS = 1e-5
RWKV_GN_EPS = 64e-5
LANES = 128
SUBLANES = 8
MXU_WIDTH = 256
RWKV_CHUNK = 64
RWKV_CHUNKS_PER_STEP = 8
ATTN_TILE = 512
ROUTER_TILE = 512
EXPERT_BLOCK = 256
W1_SPLIT_ROWS = 512
COMBINE_TILE = 256
SLOT_CHUNK = 8192
LOG2E = 1.4426950408889634
Q_SCALE = HEAD_DIM ** -0.5 * LOG2E
ZERO_PROB_EXP = -152.0
NORM_SLACK = 1.0 + 2.0 ** -6
VMEM_LIMIT = 56 * 1024 * 1024


def _cparams(semantics):
    return pltpu.CompilerParams(dimension_semantics=semantics, vmem_limit_bytes=VMEM_LIMIT)


def _dot(a, b):
    return jnp.dot(a, b, preferred_element_type=F32)


def _dot_nt(a, b):
    return lax.dot_general(a, b, (((1,), (1,)), ((), ())), preferred_element_type=F32)


def _dot_tn(a, b):
    return lax.dot_general(a, b, (((0,), (0,)), ((), ())), preferred_element_type=F32)


def _split3(x):
    hi = x.astype(BF16)
    r1 = x - hi.astype(F32)
    mid = r1.astype(BF16)
    lo = (r1 - mid.astype(F32)).astype(BF16)
    return hi, mid, lo


def _dot_exact_lhs(a_bf16, x):
    hi, mid, lo = _split3(x)
    return _dot(a_bf16, hi) + _dot(a_bf16, mid) + _dot(a_bf16, lo)


def _head_sums(x, same_head):
    w = same_head.shape[0]
    parts = []
    for g in range(x.shape[1] // w):
        xs = x[:, g * w:(g + 1) * w]
        hi = xs.astype(BF16)
        lo = (xs - hi.astype(F32)).astype(BF16)
        parts.append(_dot(hi, same_head) + _dot(lo, same_head))
    return jnp.concatenate(parts, axis=1)


def _softplus(z):
    return jnp.maximum(z, 0.0) + jnp.log1p(jnp.exp(-jnp.abs(z)))


def _sigmoid(z):
    return 1.0 / (1.0 + jnp.exp(-z))


def _inproj_kernel(x_ref, g_ref, wr_ref, wqt_ref, wk_ref, wvt_ref, wf_ref,
                   ur_ref, qt_ref, k_ref, vt_ref, fl_ref):
    x = x_ref[...]
    h = x * lax.rsqrt(jnp.mean(x * x, axis=-1, keepdims=True) + RMS_EPS) * g_ref[...]
    hb = h.astype(BF16)
    ur_ref[...] = _dot(hb, wr_ref[...])
    k_ref[...] = _dot(hb, wk_ref[...]).astype(BF16)
    fl_ref[...] = _dot(hb, wf_ref[...])
    qt = (_dot_nt(wqt_ref[...], hb) * Q_SCALE).astype(BF16)
    vt = _dot_nt(wvt_ref[...], hb).astype(BF16)
    for p in range(N_HEADS // 2):
        qt_ref[0, p, 0] = qt[p * LANES:(p + 1) * LANES]
        vt_ref[0, p, 0] = vt[p * LANES:(p + 1) * LANES]


def _inproj(x2, g, w_r, w_qt, w_k, w_vt, w_f, bsz, tm):
    n, d = x2.shape
    nt = n // bsz // tm
    pairs = N_HEADS // 2
    const = lambda i: (0, 0)
    row = lambda i: (i, 0)
    fm = lambda i: (i // nt, 0, i % nt, 0, 0)
    fm_sds = jax.ShapeDtypeStruct((bsz, pairs, nt, LANES, tm), BF16)
    return pl.pallas_call(
        _inproj_kernel,
        grid=(n // tm,),
        in_specs=[
            pl.BlockSpec((tm, d), row),
            pl.BlockSpec((1, d), const),
            pl.BlockSpec(w_r.shape, const),
            pl.BlockSpec(w_qt.shape, const),
            pl.BlockSpec(w_k.shape, const),
            pl.BlockSpec(w_vt.shape, const),
            pl.BlockSpec(w_f.shape, const),
        ],
        out_specs=[
            pl.BlockSpec((tm, RWKV_IN), row),
            pl.BlockSpec((1, pairs, 1, LANES, tm), fm),
            pl.BlockSpec((tm, GROUP_WIDTH), row),
            pl.BlockSpec((1, pairs, 1, LANES, tm), fm),
            pl.BlockSpec((tm, LANES), row),
        ],
        out_shape=[
            jax.ShapeDtypeStruct((n, RWKV_IN), F32),
            fm_sds,
            jax.ShapeDtypeStruct((n, GROUP_WIDTH), BF16),
            fm_sds,
            jax.ShapeDtypeStruct((n, LANES), F32),
        ],
        compiler_params=_cparams(("parallel",)),
        name="inproj",
    )(x2, g, w_r, w_qt, w_k, w_vt, w_f)


def _fox_gate_kernel(fl_ref, fb_ref, sel_ref, c_ref, cend_ref, carry):
    tt = fl_ref.shape[1]

    @pl.when(pl.program_id(1) == 0)
    def _():
        carry[...] = jnp.zeros_like(carry)

    z = fl_ref[0] + fb_ref[...]
    log_f = jnp.minimum(z, 0.0) - jnp.log1p(jnp.exp(-jnp.abs(z)))
    ri = lax.broadcasted_iota(jnp.int32, (tt, tt), 0)
    ci = lax.broadcasted_iota(jnp.int32, (tt, tt), 1)
    tri = jnp.where(ri >= ci, 1.0, 0.0).astype(BF16)
    c = _dot_exact_lhs(tri, log_f) + carry[...]
    carry[...] = c[tt - 1:tt, :]
    cend_ref[0, 0] = c[tt - 1:tt, :]
    hi, mid, lo = _split3(c * LOG2E)
    c_ref[0] = (_dot(hi, sel_ref[0]) + _dot(mid, sel_ref[1]) + _dot(lo, sel_ref[2])).astype(BF16)


def _gate_piece_selectors():
    h = jnp.arange(LANES, dtype=jnp.int32)[:, None]
    col = jnp.arange(GROUP_WIDTH, dtype=jnp.int32)[None, :]
    sels = []
    for m in range(3):
        target = LANES * (h // 2) + 3 * (h % 2) + m
        sels.append(((col == target) & (h < N_HEADS)).astype(BF16))
    return jnp.stack(sels)


def _fox_gate(fl3, fb_pad, sel, tt):
    b, t, _ = fl3.shape
    return pl.pallas_call(
        _fox_gate_kernel,
        grid=(b, t // tt),
        in_specs=[
            pl.BlockSpec((1, tt, LANES), lambda i, j: (i, j, 0)),
            pl.BlockSpec((1, LANES), lambda i, j: (0, 0)),
            pl.BlockSpec((3, LANES, GROUP_WIDTH), lambda i, j: (0, 0, 0)),
        ],
        out_specs=[
            pl.BlockSpec((1, tt, GROUP_WIDTH), lambda i, j: (i, j, 0)),
            pl.BlockSpec((1, 1, 1, LANES), lambda i, j: (i, j, 0, 0)),
        ],
        out_shape=[
            jax.ShapeDtypeStruct((b, t, GROUP_WIDTH), BF16),
            jax.ShapeDtypeStruct((b, t // tt, 1, LANES), F32),
        ],
        scratch_shapes=[pltpu.VMEM((1, LANES), F32)],
        compiler_params=_cparams(("parallel", "arbitrary")),
        name="fox_gate",
    )(fl3, fb_pad, sel)


def _rwkv_prep_body(u_ref, mu_ref, w0_ref, wup_ref, a0_ref, aup_ref, gup_ref, kk_ref, ka_ref,
                      bd_ref, r_out, k_out, v_out, lw_out, kk_out, b_out, g_out, carry):
    tt = u_ref.shape[1]

    @pl.when(pl.program_id(1) == 0)
    def _():
        carry[...] = jnp.zeros_like(carry)

    u = u_ref[0]
    prev = pltpu.roll(u, 1, axis=0)
    row = lax.broadcasted_iota(jnp.int32, u.shape, 0)
    prev = jnp.where(row == 0, carry[...], prev)
    carry[...] = u[tt - 1:tt, :]
    us = u + (prev - u) * mu_ref[...]

    r = us[:, :GROUP_WIDTH]
    k = us[:, GROUP_WIDTH:2 * GROUP_WIDTH]
    v = us[:, 2 * GROUP_WIDTH:LORA_OFF]
    wa = us[:, LORA_OFF:LORA_OFF + LANES]
    gl = us[:, LORA_OFF + LANES:]

    w_lin = _dot(jnp.tanh(wa).astype(BF16), wup_ref[...])
    a_lin = _dot(wa.astype(BF16), aup_ref[...])
    w = -_softplus(-(w0_ref[...] + w_lin)) - 0.5
    lw_out[0] = -jnp.exp(w)
    a = _sigmoid(a0_ref[...] + a_lin)
    g_out[0] = _dot(_sigmoid(gl).astype(BF16), gup_ref[...])

    kkr = k * kk_ref[...]
    ss = _head_sums(kkr * kkr, bd_ref[...])
    kk = kkr / jnp.maximum(jnp.sqrt(ss), 1e-12)
    r_out[0] = r
    k_out[0] = k * (1.0 + (a - 1.0) * ka_ref[...])
    v_out[0] = v
    kk_out[0] = kk
    b_out[0] = kk * a


def _rwkv_scan_body(r_ref, k_ref, v_ref, lw_ref, kk_ref, b_ref, g_ref, rk_ref, lnw_ref, lnb_ref,
                      bd_ref, o_ref, s_scr, *, n_chunks):
    c = RWKV_CHUNK
    w = MXU_WIDTH
    hpg = w // HEAD_DIM
    n_groups = GROUP_WIDTH // w

    @pl.when(pl.program_id(1) == 0)
    def _():
        s_scr[...] = jnp.zeros_like(s_scr)

    row = lax.broadcasted_iota(jnp.int32, (c, w), 0)
    u = lax.broadcasted_iota(jnp.int32, (c, w), 1) % HEAD_DIM
    strict = row > u
    incl = row >= u
    eye = jnp.where(row == u, 1.0, 0.0)
    level_masks = []
    s = 1
    while s < c:
        same = (row // (2 * s)) == (u // (2 * s))
        level_masks.append(same & ((row % (2 * s)) >= s) & ((u % (2 * s)) < s))
        s *= 2
    same_head = (lax.broadcasted_iota(jnp.int32, (w, w), 0) // HEAD_DIM
                 == lax.broadcasted_iota(jnp.int32, (w, w), 1) // HEAD_DIM)
    tri = jnp.where(lax.broadcasted_iota(jnp.int32, (c, c), 0)
                    >= lax.broadcasted_iota(jnp.int32, (c, c), 1), 1.0, 0.0).astype(BF16)

    def bdiag(x):
        xb = x.astype(BF16)
        tiled = jnp.concatenate([xb] * hpg, axis=0)
        return jnp.where(same_head, tiled, jnp.zeros_like(tiled))

    chains = [(ci, gi) for ci in range(n_chunks) for gi in range(n_groups)]
    lhs, rk_t, vbs, xcat, xneg, gam, rt32 = {}, {}, {}, {}, {}, {}, {}
    for ci in range(n_chunks):
        rs = slice(ci * c, (ci + 1) * c)
        r = r_ref[0, rs, :]
        k = k_ref[0, rs, :]
        lw = lw_ref[0, rs, :]
        kk = kk_ref[0, rs, :]
        b = b_ref[0, rs, :]
        g_cum = _dot_exact_lhs(tri, lw)
        g_last = g_cum[c - 1:c, :]
        r_t = r * jnp.exp(g_cum)
        kk_t = kk * jnp.exp(g_cum - lw)
        e_neg = jnp.exp(-g_cum)
        b_n = b * e_neg
        k_n = k * e_neg
        e_end = jnp.exp(g_last - g_cum)
        b_e = (b * e_end).astype(BF16)
        k_e = (k * e_end).astype(BF16)
        gamma = jnp.exp(g_last)
        vb = v_ref[0, rs, :].astype(BF16)
        for gi in range(n_groups):
            gs = slice(gi * w, (gi + 1) * w)
            ch = (ci, gi)
            lhs[ch] = jnp.concatenate([r_t[:, gs].astype(BF16), kk_t[:, gs].astype(BF16)], axis=0)
            rk_t[ch] = (jnp.concatenate([bdiag(b_n[:, gs]), bdiag(k_n[:, gs])], axis=0),
                        bdiag(kk_t[:, gs]))
            vbs[ch] = vb[:, gs]
            xcat[ch] = jnp.concatenate([b_e[:, gs], k_e[:, gs]], axis=0)
            xneg[ch] = jnp.concatenate([-b_e[:, gs], k_e[:, gs]], axis=0)
            gam[ch] = gamma[:, gs]
            rt32[ch] = r_t[:, gs]

    p = {ch: _dot_nt(lhs[ch], rk_t[ch][0]) for ch in chains}
    l_b = {ch: jnp.where(strict, p[ch][c:, :w], 0.0) for ch in chains}
    l_k = {ch: jnp.where(strict, p[ch][c:, w:], 0.0).astype(BF16) for ch in chains}
    p_br = {ch: jnp.where(incl, p[ch][:c, :w], 0.0).astype(BF16) for ch in chains}
    p_kr = {ch: jnp.where(incl, p[ch][:c, w:], 0.0).astype(BF16) for ch in chains}
    v_bd = {ch: bdiag(vbs[ch]) for ch in chains}
    lkv = {ch: _dot(l_k[ch], v_bd[ch]) for ch in chains}

    t_inv = {ch: eye - jnp.where(level_masks[0], l_b[ch], 0.0) for ch in chains}
    for m in level_masks[1:]:
        tb = {ch: t_inv[ch].astype(BF16) for ch in chains}
        ct = {ch: _dot(jnp.where(m, l_b[ch], 0.0).astype(BF16), bdiag(tb[ch])) for ch in chains}
        t_inv = {ch: t_inv[ch] - _dot(tb[ch], bdiag(ct[ch])) for ch in chains}

    mm = {ch: _dot(t_inv[ch].astype(BF16),
                   jnp.concatenate([rk_t[ch][1], bdiag(lkv[ch])], axis=1)) for ch in chains}
    pm = {ch: _dot(p_br[ch], jnp.concatenate([bdiag(mm[ch][:, :w]), bdiag(mm[ch][:, w:])], axis=1))
          for ch in chains}
    pkv = {ch: _dot(p_kr[ch], v_bd[ch]) for ch in chains}
    n1 = {ch: (rt32[ch] - pm[ch][:, :w]).astype(BF16) for ch in chains}
    n2 = {ch: pkv[ch] - pm[ch][:, w:] for ch in chains}
    omega = {ch: jnp.where(same_head, _dot_tn(mm[ch][:, :w].astype(BF16), xcat[ch][:c]), 0.0)
             .astype(BF16) for ch in chains}
    psi = {ch: jnp.where(same_head, _dot_tn(
        jnp.concatenate([mm[ch][:, w:].astype(BF16), vbs[ch]], axis=0), xneg[ch]), 0.0)
           for ch in chains}

    state = [s_scr[gi] for gi in range(n_groups)]
    ys = []
    for ci in range(n_chunks):
        sb = [state[gi].astype(BF16) for gi in range(n_groups)]
        ys.append(jnp.concatenate(
            [_dot_nt(n1[(ci, gi)], sb[gi]) + n2[(ci, gi)] for gi in range(n_groups)], axis=1))
        state = [state[gi] * gam[(ci, gi)] - _dot(sb[gi], omega[(ci, gi)]) + psi[(ci, gi)]
                 for gi in range(n_groups)]
    for gi in range(n_groups):
        s_scr[gi] = state[gi]

    y = jnp.concatenate(ys, axis=0)
    r = r_ref[0]
    k = k_ref[0]
    v = v_ref[0]
    bd = bd_ref[...]
    inv_n = 1.0 / HEAD_DIM
    mean = _head_sums(y, bd) * inv_n
    d = y - mean
    var = _head_sums(d * d, bd) * inv_n
    yn = d * lax.rsqrt(var + RWKV_GN_EPS) * lnw_ref[...] + lnb_ref[...]
    bonus = _head_sums(r * k * rk_ref[...], bd) * v
    o_ref[0] = ((yn + bonus) * g_ref[0]).astype(o_ref.dtype)


def _rwkv_kernel(u_ref, mu_ref, w0_ref, wup_ref, a0_ref, aup_ref, gup_ref, kkw_ref, ka_ref,
                 rk_ref, lnw_ref, lnb_ref, bd_ref, o_ref, carry, s_scr,
                 r_s, k_s, v_s, lw_s, kk_s, b_s, g_s, *, n_chunks):
    _rwkv_prep_body(u_ref, mu_ref, w0_ref, wup_ref, a0_ref, aup_ref, gup_ref, kkw_ref, ka_ref,
                    bd_ref, r_s, k_s, v_s, lw_s, kk_s, b_s, g_s, carry)
    _rwkv_scan_body(r_s, k_s, v_s, lw_s, kk_s, b_s, g_s, rk_ref, lnw_ref, lnb_ref, bd_ref,
                    o_ref, s_scr, n_chunks=n_chunks)


def _rwkv_mix(u3, mu, w0, wup_pad, a0, aup_pad, gup, k_k, k_a, r_k, ln_w, ln_b, bd, n_chunks):
    bsz, t, _ = u3.shape
    rows = RWKV_CHUNK * n_chunks
    const = lambda i, j: (0, 0)
    tile = lambda i, j: (i, j, 0)
    vec = pl.BlockSpec((1, GROUP_WIDTH), const)
    staged = pltpu.VMEM((1, rows, GROUP_WIDTH), F32)
    return pl.pallas_call(
        functools.partial(_rwkv_kernel, n_chunks=n_chunks),
        grid=(bsz, t // rows),
        in_specs=[
            pl.BlockSpec((1, rows, RWKV_IN), tile),
            pl.BlockSpec((1, RWKV_IN), const),
            vec,
            pl.BlockSpec((LANES, GROUP_WIDTH), const),
            vec,
            pl.BlockSpec((LANES, GROUP_WIDTH), const),
            pl.BlockSpec((GATE_LORA, GROUP_WIDTH), const),
            vec, vec, vec, vec, vec,
            pl.BlockSpec((MXU_WIDTH, MXU_WIDTH), const),
        ],
        out_specs=pl.BlockSpec((1, rows, GROUP_WIDTH), tile),
        out_shape=jax.ShapeDtypeStruct((bsz, t, GROUP_WIDTH), BF16),
        scratch_shapes=[
            pltpu.VMEM((1, RWKV_IN), F32),
            pltpu.VMEM((GROUP_WIDTH // MXU_WIDTH, MXU_WIDTH, MXU_WIDTH), F32),
        ] + [staged] * 7,
        compiler_params=_cparams(("parallel", "arbitrary")),
        name="rwkv_mix",
    )(u3, mu, w0, wup_pad, a0, aup_pad, gup, k_k, k_a, r_k, ln_w, ln_b, bd)


def _fox_attn_kernel(qt_ref, k_ref, ce_ref, vt_ref, cend_ref, og_ref, o_ref,
                     m_scr, l_scr, acc_scr, kmax_scr, *, t):
    qi = pl.program_id(2)
    n_strips = 2 * t // LANES
    qt = qt_ref[0, 0, 0]
    frow = lax.broadcasted_iota(jnp.int32, (LANES, t), 0)
    zero = jnp.zeros_like(qt)
    main = jnp.concatenate([jnp.where(frow < HEAD_DIM, qt, zero),
                            jnp.where(frow < HEAD_DIM, zero, qt)], axis=1)
    erow = lax.broadcasted_iota(jnp.int32, (LANES, 2 * t), 0)
    ecol = lax.broadcasted_iota(jnp.int32, (LANES, 2 * t), 1)
    off = jnp.where(ecol < t, 0, 3)
    extra = jnp.where((erow >= off) & (erow < off + 3), -1.0, 0.0).astype(BF16)
    q_aug = jnp.concatenate([main, extra], axis=0)

    m_scr[...] = jnp.full_like(m_scr, -jnp.inf)
    l_scr[...] = jnp.zeros_like(l_scr)
    acc_scr[...] = jnp.zeros_like(acc_scr)

    @pl.when(qi == 0)
    def _():
        hid_r = lax.broadcasted_iota(jnp.int32, (LANES, LANES), 0) // HEAD_DIM
        hid_c = lax.broadcasted_iota(jnp.int32, (LANES, LANES), 1) // HEAD_DIM
        same_head = jnp.where(hid_r == hid_c, 1.0, 0.0).astype(BF16)

        def tile_max(j, best):
            kf = k_ref[0, pl.ds(pl.multiple_of(j * t, t), t), :].astype(F32)
            sq = _dot((kf * kf).astype(BF16), same_head)
            return jnp.maximum(best, jnp.max(sq, axis=0, keepdims=True))

        best = lax.fori_loop(0, k_ref.shape[1] // t, tile_max, jnp.zeros((1, LANES), F32))
        kmax_scr[...] = jnp.sqrt(best) * NORM_SLACK

    def step(j, masked):
        start = pl.multiple_of(j * t, t)
        k_aug = jnp.concatenate([k_ref[0, pl.ds(start, t), :], ce_ref[0, pl.ds(start, t), :]],
                                axis=1)
        vt = vt_ref[0, 0, j]
        zt = _dot(k_aug, q_aug)
        m_prev = m_scr[...]
        l_prev = l_scr[...]
        acc_prev = acc_scr[...]
        m_out, l_out, acc_out = [], [], [[], []]
        for s in range(n_strips):
            head = s // (n_strips // 2)
            cs = slice(s * LANES, (s + 1) * LANES)
            z = zt[:, cs]
            if masked:
                key = lax.broadcasted_iota(jnp.int32, (t, LANES), 0)
                qry = lax.broadcasted_iota(jnp.int32, (t, LANES), 1) + (s * LANES) % t
                z = jnp.where(key <= qry, z, -jnp.inf)
            m_new = jnp.maximum(m_prev[:, cs], jnp.max(z, axis=0, keepdims=True))
            alpha = jnp.exp2(m_prev[:, cs] - m_new)
            p = jnp.exp2(z - m_new)
            l_out.append(alpha * l_prev[:, cs] + jnp.sum(p, axis=0, keepdims=True))
            m_out.append(m_new)
            hs = slice(head * HEAD_DIM, (head + 1) * HEAD_DIM)
            qs = slice((s * LANES) % t, (s * LANES) % t + LANES)
            pv = _dot(vt[hs], p.astype(BF16))
            acc_out[head].append(alpha * acc_prev[hs, qs] + pv)
        m_scr[...] = jnp.concatenate(m_out, axis=1)
        l_scr[...] = jnp.concatenate(l_out, axis=1)
        acc_scr[...] = jnp.concatenate([jnp.concatenate(acc_out[0], axis=1),
                                        jnp.concatenate(acc_out[1], axis=1)], axis=0)

    def body(j, carry):
        step(j, False)
        return carry

    step(qi, True)

    qf = qt.astype(F32)
    qsq = qf * qf
    qnorm = jnp.sqrt(jnp.concatenate(
        [jnp.sum(qsq[:HEAD_DIM], axis=0, keepdims=True),
         jnp.sum(qsq[HEAD_DIM:], axis=0, keepdims=True)], axis=1)) * NORM_SLACK
    kmax = kmax_scr[...]
    kmax2 = jnp.concatenate([jnp.broadcast_to(kmax[:, 0:1], (1, t)),
                             jnp.broadcast_to(kmax[:, HEAD_DIM:HEAD_DIM + 1], (1, t))], axis=1)
    slack = qnorm * kmax2 - m_scr[...]
    cend = cend_ref[0][:, 0, :]
    lane = lax.broadcasted_iota(jnp.int32, cend.shape, 1)
    jrow = lax.broadcasted_iota(jnp.int32, (cend.shape[0], 1), 0)
    needed = jrow < 0
    for head in range(2):
        worst = jnp.max(slack[:, head * t:(head + 1) * t], axis=1, keepdims=True)
        c_head = jnp.sum(jnp.where(lane == 2 * pl.program_id(1) + head, cend, 0.0),
                         axis=1, keepdims=True)
        needed = needed | (worst - c_head * LOG2E > ZERO_PROB_EXP)
    first = jnp.min(jnp.where(needed & (jrow < qi), jrow, qi))
    lax.fori_loop(first, qi, body, 0)

    l = l_scr[...]
    acc = acc_scr[...]
    inv_n = 1.0 / HEAD_DIM
    halves = []
    for head in range(2):
        o = acc[head * HEAD_DIM:(head + 1) * HEAD_DIM] / l[:, head * t:(head + 1) * t]
        halves.append(o * lax.rsqrt(jnp.sum(o * o, axis=0, keepdims=True) * inv_n + RMS_EPS))
    o_t = jnp.concatenate(halves, axis=0)
    o_ref[0] = (jnp.transpose(o_t) * og_ref[...]).astype(o_ref.dtype)


def _fox_attn(qt5, k, cext, vt5, cend, out_g, tile):
    bsz, t_all, _ = k.shape
    pairs = N_HEADS // 2
    tiles = t_all // tile
    kern = functools.partial(_fox_attn_kernel, t=tile)
    return pl.pallas_call(
        kern,
        grid=(bsz, pairs, tiles),
        in_specs=[
            pl.BlockSpec((1, 1, 1, LANES, tile), lambda b, p, i: (b, p, i, 0, 0)),
            pl.BlockSpec((1, t_all, LANES), lambda b, p, i: (b, 0, p)),
            pl.BlockSpec((1, t_all, LANES), lambda b, p, i: (b, 0, p)),
            pl.BlockSpec((1, 1, tiles, LANES, tile), lambda b, p, i: (b, p, 0, 0, 0)),
            pl.BlockSpec((1, tiles, 1, LANES), lambda b, p, i: (b, 0, 0, 0)),
            pl.BlockSpec((1, LANES), lambda b, p, i: (0, p)),
        ],
        out_specs=pl.BlockSpec((1, tile, LANES), lambda b, p, i: (b, i, p)),
        out_shape=jax.ShapeDtypeStruct((bsz, t_all, GROUP_WIDTH), BF16),
        scratch_shapes=[
            pltpu.VMEM((1, 2 * tile), F32),
            pltpu.VMEM((1, 2 * tile), F32),
            pltpu.VMEM((LANES, tile), F32),
            pltpu.VMEM((1, LANES), F32),
        ],
        compiler_params=_cparams(("parallel", "parallel", "arbitrary")),
        name="fox_attn",
    )(qt5, k, cext, vt5, cend, out_g)


def _outproj_router_kernel(x_ref, yr_ref, yf_ref, wo_r_ref, wo_f_ref, g_ref, rwt_ref, rb_ref,
                           x1_ref, h_ref, idx_ref, gate_ref, rank_ref, count_ref):
    @pl.when(pl.program_id(0) == 0)
    def _():
        count_ref[...] = jnp.zeros_like(count_ref)

    x1 = x_ref[...] + _dot(yr_ref[...], wo_r_ref[...]) + _dot(yf_ref[...], wo_f_ref[...])
    x1_ref[...] = x1
    h = x1 * lax.rsqrt(jnp.mean(x1 * x1, axis=-1, keepdims=True) + RMS_EPS) * g_ref[...]
    bits = lax.bitcast_convert_type(h.astype(BF16).astype(F32), jnp.uint32)
    half = h.shape[1] // 2
    h_ref[:, 0, :] = (bits[:, half:] & jnp.uint32(0xFFFF0000)) | (bits[:, :half] >> 16)
    logits = lax.dot_general(rwt_ref[...], h, (((1,), (1,)), ((), ())),
                             precision=lax.Precision.HIGHEST,
                             preferred_element_type=F32) + rb_ref[...]
    eidx = lax.broadcasted_iota(jnp.int32, logits.shape, 0)
    vals, idxs, picks = [], [], []
    for _ in range(TOP_K):
        m = jnp.max(logits, axis=0, keepdims=True)
        i = jnp.min(jnp.where(logits == m, eidx, N_EXPERTS), axis=0, keepdims=True)
        vals.append(m)
        idxs.append(i)
        picks.append(eidx == i)
        logits = jnp.where(picks[-1], -jnp.inf, logits)
    es = [jnp.exp(val - vals[0]) for val in vals]
    denom = es[0] + es[1] + es[2] + es[3]
    idx_ref[...] = jnp.concatenate(idxs, axis=0)
    gate_ref[...] = jnp.concatenate([e / denom for e in es], axis=0)

    tm = logits.shape[1]
    chosen = [jnp.where(pk, 1.0, 0.0) for pk in picks]
    any_k = chosen[0] + chosen[1] + chosen[2] + chosen[3]
    before = (lax.broadcasted_iota(jnp.int32, (tm, tm), 0)
              < lax.broadcasted_iota(jnp.int32, (tm, tm), 1))
    prefix = _dot(any_k.astype(BF16), jnp.where(before, 1.0, 0.0).astype(BF16))
    seen = count_ref[:, 0:1] + prefix
    rank_ref[...] = jnp.concatenate(
        [jnp.sum(ch * seen, axis=0, keepdims=True) for ch in chosen], axis=0).astype(jnp.int32)
    count_ref[...] = count_ref[...] + jnp.sum(any_k, axis=1, keepdims=True)


def _outproj_router(x2, yr, yf, wo_r, wo_f, g, rwt, rb, tm):
    n, d = x2.shape
    const = lambda i: (0, 0)
    row = lambda i: (i, 0)
    col = lambda i: (0, i)
    return pl.pallas_call(
        _outproj_router_kernel,
        grid=(n // tm,),
        in_specs=[
            pl.BlockSpec((tm, d), row),
            pl.BlockSpec((tm, GROUP_WIDTH), row),
            pl.BlockSpec((tm, GROUP_WIDTH), row),
            pl.BlockSpec((GROUP_WIDTH, d), const),
            pl.BlockSpec((GROUP_WIDTH, d), const),
            pl.BlockSpec((1, d), const),
            pl.BlockSpec((N_EXPERTS, d), const),
            pl.BlockSpec((N_EXPERTS, 1), const),
        ],
        out_specs=[
            pl.BlockSpec((tm, d), row),
            pl.BlockSpec((tm, 1, d // 2), lambda i: (i, 0, 0)),
            pl.BlockSpec((TOP_K, tm), col),
            pl.BlockSpec((TOP_K, tm), col),
            pl.BlockSpec((TOP_K, tm), col),
            pl.BlockSpec((N_EXPERTS, LANES), const),
        ],
        out_shape=[
            jax.ShapeDtypeStruct((n, d), F32),
            jax.ShapeDtypeStruct((n, 1, d // 2), jnp.uint32),
            jax.ShapeDtypeStruct((TOP_K, n), jnp.int32),
            jax.ShapeDtypeStruct((TOP_K, n), F32),
            jax.ShapeDtypeStruct((TOP_K, n), jnp.int32),
            jax.ShapeDtypeStruct((N_EXPERTS, LANES), F32),
        ],
        compiler_params=_cparams(("arbitrary",)),
        name="outproj_router",
    )(x2, yr, yf, wo_r, wo_f, g, rwt, rb)


def _w1_split_kernel(w_ref, perm_ref, g_ref, l_ref):
    half = MXU_WIDTH // 2
    perm = perm_ref[...]
    for grp in range(w_ref.shape[2] // MXU_WIDTH):
        blk = w_ref[0, :, grp * MXU_WIDTH:(grp + 1) * MXU_WIDTH].astype(BF16)
        r = _dot(blk, perm)
        g_ref[0, :, grp * half:(grp + 1) * half] = r[:, :half].astype(BF16)
        l_ref[0, :, grp * half:(grp + 1) * half] = r[:, half:].astype(BF16)


def _w1_split(w1, tr):
    e, d, two_f = w1.shape
    half = MXU_WIDTH // 2
    src = jnp.arange(MXU_WIDTH, dtype=jnp.int32)[:, None]
    dst = jnp.arange(MXU_WIDTH, dtype=jnp.int32)[None, :]
    perm = (src == jnp.where(dst < half, 2 * dst, 2 * (dst - half) + 1)).astype(BF16)
    out_sds = jax.ShapeDtypeStruct((e, d, two_f // 2), BF16)
    return pl.pallas_call(
        _w1_split_kernel,
        grid=(e, d // tr),
        in_specs=[
            pl.BlockSpec((1, tr, two_f), lambda i, j: (i, j, 0)),
            pl.BlockSpec((MXU_WIDTH, MXU_WIDTH), lambda i, j: (0, 0)),
        ],
        out_specs=[pl.BlockSpec((1, tr, two_f // 2), lambda i, j: (i, j, 0))] * 2,
        out_shape=[out_sds, out_sds],
        compiler_params=_cparams(("parallel", "parallel")),
        name="w1_split",
    )(w1, perm)


def _expert_kernel(be_ref, tok_a_ref, tok_b_ref, tok_a_next_ref, dst_b_prev_ref, dst_a_ref,
                   dst_b_ref, h_hbm, w1g_a, w1l_a, b1g_a, b1l_a, w2_a, b2_a,
                   w1g_b, w1l_b, b1g_b, b1l_b, w2_b, b2_b,
                   y_hbm, xbuf_a, xbuf_b, obuf_a, obuf_b, xrows, gsem, osem, *, bm, n_real_rows):
    del be_ref
    i = pl.program_id(0)

    def gather_start(idx_ref, xbuf, sem, r, priority=0):
        pltpu.make_async_copy(h_hbm.at[idx_ref[0, 0, r]], xbuf.at[r], sem).start(priority=priority)

    def scatter_start(idx_ref, obuf, sem, r, priority=0):
        pltpu.make_async_copy(obuf.at[r], y_hbm.at[idx_ref[0, 0, r]], sem).start(priority=priority)

    def rows_wait(buf, sem):
        rows = buf.at[pl.ds(0, bm)]
        pltpu.make_async_copy(rows, rows, sem).wait()

    def mlp(xbuf, w1g_ref, w1l_ref, b1g_ref, b1l_ref, w2_ref, b2_ref, start_gather, next_xbuf):
        xrows[...] = xbuf[pl.ds(0, bm), 0, :]
        words = xrows[...]
        xb_lo = lax.bitcast_convert_type(words << 16, F32).astype(BF16)
        xb_hi = lax.bitcast_convert_type(words & jnp.uint32(0xFFFF0000), F32).astype(BF16)
        half = words.shape[1]

        def x_dot(w_ref, cs):
            return _dot(xb_lo, w_ref[0, :half, cs]) + _dot(xb_hi, w_ref[0, half:, cs])
        dff = w1g_ref.shape[2]
        n_pieces = dff // MXU_WIDTH
        per_piece = -(-bm // (n_pieces - 1))
        acts = []
        for piece in range(n_pieces):
            cs = slice(piece * MXU_WIDTH, (piece + 1) * MXU_WIDTH)
            for r in range(piece * per_piece, min((piece + 1) * per_piece, bm)):
                start_gather(r)
            zero = lax.bitcast_convert_type(next_xbuf[bm + piece, :, :MXU_WIDTH], F32)
            glu = x_dot(w1g_ref, cs) + (b1g_ref[0, :, cs] + zero)
            lin = x_dot(w1l_ref, cs) + b1l_ref[0, :, cs]
            glu = jnp.minimum(glu, SWIGLU_LIMIT)
            lin = jnp.clip(lin, -SWIGLU_LIMIT, SWIGLU_LIMIT)
            acts.append((glu * _sigmoid(SWIGLU_ALPHA * glu) * (lin + 1.0)).astype(BF16))
        return _dot(jnp.concatenate(acts, axis=1), w2_ref[0]) + b2_ref[0]

    @pl.when(i == 0)
    def _():
        obuf_a[...] = jnp.zeros_like(obuf_a)
        obuf_b[...] = jnp.zeros_like(obuf_b)
        spare = jnp.zeros((SUBLANES, 1, xbuf_a.shape[2]), jnp.uint32)
        xbuf_a[pl.ds(bm, SUBLANES)] = spare
        xbuf_b[pl.ds(bm, SUBLANES)] = spare

        def first(r, carry):
            pltpu.make_async_copy(obuf_a.at[r], y_hbm.at[n_real_rows + 2 * bm + r],
                                  osem.at[0]).start()
            gather_start(tok_a_ref, xbuf_a, gsem.at[0], r)
            return carry

        lax.fori_loop(0, bm, first, 0)

    def gather_b(r):
        gather_start(tok_b_ref, xbuf_b, gsem.at[1], r, 0)

    def gather_a_next(r):
        gather_start(tok_a_next_ref, xbuf_a, gsem.at[0], r, 0)

    for r in range(bm):
        scatter_start(dst_b_prev_ref, obuf_b, osem.at[1], r, 1)
    rows_wait(xbuf_a, gsem.at[0])
    rows_wait(obuf_a, osem.at[0])
    obuf_a[:, 0, :] = mlp(xbuf_a, w1g_a, w1l_a, b1g_a, b1l_a, w2_a, b2_a, gather_b, xbuf_b)

    for r in range(bm):
        scatter_start(dst_a_ref, obuf_a, osem.at[0], r, 1)
    rows_wait(xbuf_b, gsem.at[1])
    rows_wait(obuf_b, osem.at[1])
    obuf_b[:, 0, :] = mlp(xbuf_b, w1g_b, w1l_b, b1g_b, b1l_b, w2_b, b2_b, gather_a_next, xbuf_a)

    @pl.when(i == pl.num_programs(0) - 1)
    def _():
        def last(r, carry):
            scatter_start(dst_b_ref, obuf_b, osem.at[1], r)
            return carry

        lax.fori_loop(0, bm, last, 0)
        rows_wait(obuf_b, osem.at[1])
        rows_wait(obuf_a, osem.at[0])
        rows_wait(xbuf_a, gsem.at[0])


def _expert_mlp(block_e, tok_blocks, dst_blocks, h2, w1g, w1l, b1g, b1l, w2, b2, bm):
    n_blocks = tok_blocks.shape[0]
    assert n_blocks % 2 == 0
    n = h2.shape[0]
    d = w1g.shape[1]
    dw = h2.shape[2]
    dff = w1g.shape[2]
    n_real_rows = TOP_K * n
    idx_spec = lambda fn: pl.BlockSpec((1, 1, bm), fn, memory_space=pltpu.SMEM)

    def weight_specs(which):
        wmap = lambda i, be: (be[2 * i + which], 0, 0)
        return [
            pl.BlockSpec((1, d, dff), wmap),
            pl.BlockSpec((1, d, dff), wmap),
            pl.BlockSpec((1, 1, dff), wmap),
            pl.BlockSpec((1, 1, dff), wmap),
            pl.BlockSpec((1, dff, d), wmap),
            pl.BlockSpec((1, 1, d), wmap),
        ]

    grid_spec = pltpu.PrefetchScalarGridSpec(
        num_scalar_prefetch=1,
        grid=(n_blocks // 2,),
        in_specs=[
            idx_spec(lambda i, be: (2 * i, 0, 0)),
            idx_spec(lambda i, be: (2 * i + 1, 0, 0)),
            idx_spec(lambda i, be: (jnp.minimum(2 * i + 2, n_blocks - 1), 0, 0)),
            idx_spec(lambda i, be: (2 * i, 0, 0)),
            idx_spec(lambda i, be: (2 * i + 1, 0, 0)),
            idx_spec(lambda i, be: (2 * i + 2, 0, 0)),
            pl.BlockSpec(memory_space=pl.ANY),
        ] + weight_specs(0) + weight_specs(1),
        out_specs=pl.BlockSpec(memory_space=pl.ANY),
        scratch_shapes=[
            pltpu.VMEM((bm + SUBLANES, 1, dw), jnp.uint32),
            pltpu.VMEM((bm + SUBLANES, 1, dw), jnp.uint32),
            pltpu.VMEM((bm, 1, d), F32),
            pltpu.VMEM((bm, 1, d), F32),
            pltpu.VMEM((bm, dw), jnp.uint32),
            pltpu.SemaphoreType.DMA((2,)),
            pltpu.SemaphoreType.DMA((2,)),
        ],
    )
    weights = (w1g, w1l, b1g, b1l, w2, b2)
    return pl.pallas_call(
        functools.partial(_expert_kernel, bm=bm, n_real_rows=n_real_rows),
        grid_spec=grid_spec,
        out_shape=jax.ShapeDtypeStruct((n_real_rows + 3 * bm, 1, d), F32),
        compiler_params=_cparams(("arbitrary",)),
        name="expert_mlp",
    )(block_e, tok_blocks, tok_blocks, tok_blocks, dst_blocks, dst_blocks, dst_blocks, h2,
      *weights, *weights)


def _combine_kernel(gate_ref, x1_ref, g_ref, y0_ref, y1_ref, y2_ref, y3_ref, o_ref):
    gates = gate_ref[...]
    y = x1_ref[...]
    for kk, y_ref in enumerate((y0_ref, y1_ref, y2_ref, y3_ref)):
        y = y + y_ref[:, 0, :] * gates[:, kk:kk + 1]
    o_ref[...] = y * lax.rsqrt(jnp.mean(y * y, axis=-1, keepdims=True) + RMS_EPS) * g_ref[...]


def _combine(gates_t, x1, g, y_all, tc):
    n, d = x1.shape
    tiles = n // tc
    y_spec = lambda kk: pl.BlockSpec((tc, 1, d), lambda i: (kk * tiles + i, 0, 0))
    return pl.pallas_call(
        _combine_kernel,
        grid=(tiles,),
        in_specs=[
            pl.BlockSpec((tc, TOP_K), lambda i: (i, 0)),
            pl.BlockSpec((tc, d), lambda i: (i, 0)),
            pl.BlockSpec((1, d), lambda i: (0, 0)),
        ] + [y_spec(kk) for kk in range(TOP_K)],
        out_specs=pl.BlockSpec((tc, d), lambda i: (i, 0)),
        out_shape=jax.ShapeDtypeStruct((n, d), F32),
        compiler_params=_cparams(("parallel",)),
        name="combine",
    )(gates_t, x1, g, y_all, y_all, y_all, y_all)


def _slot_sources_kernel(gaps_ref, pos_ref, out_ref, *, chunk):
    i = pl.program_id(0)

    @pl.when(i == 0)
    def _():
        def init(p, carry):
            out_ref[p] = -1
            return carry

        for e in range(gaps_ref.shape[1]):
            lax.fori_loop(gaps_ref[0, e], gaps_ref[1, e], init, 0)

    base = i * chunk

    def place(s, carry):
        out_ref[pos_ref[0, 0, s]] = base + s
        return carry

    lax.fori_loop(0, chunk, place, 0, unroll=16)


def _slot_sources(pos, gaps, n_pad):
    n_slots = pos.shape[0]
    chunk = _pick(n_slots, SLOT_CHUNK)
    return pl.pallas_call(
        functools.partial(_slot_sources_kernel, chunk=chunk),
        grid=(n_slots // chunk,),
        in_specs=[
            pl.BlockSpec(memory_space=pltpu.SMEM),
            pl.BlockSpec((1, 1, chunk), lambda i: (i, 0, 0), memory_space=pltpu.SMEM),
        ],
        out_specs=pl.BlockSpec(memory_space=pltpu.SMEM),
        out_shape=jax.ShapeDtypeStruct((n_pad,), jnp.int32),
        compiler_params=_cparams(("arbitrary",)),
        name="slot_sources",
    )(gaps, pos.reshape(n_slots // chunk, 1, chunk))


def _dispatch_plan(idx, rank, counts, bm):
    n = idx.shape[1]
    n_slots = TOP_K * n
    sizes = counts[:, 0].astype(jnp.int32)
    padded = (sizes + bm - 1) // bm * bm
    pad_ends = jnp.cumsum(padded)
    pad_starts = pad_ends - padded
    experts = jnp.arange(N_EXPERTS, dtype=jnp.int32)
    start_of = jnp.sum(jnp.where(idx[..., None] == experts, pad_starts, 0), axis=-1)
    pos = (start_of + rank).reshape(-1)
    n_pad = n_slots + N_EXPERTS * bm
    n_blocks = n_pad // bm
    gap_lo = jnp.concatenate([pad_starts + sizes, pad_ends[-1:]])
    gap_hi = jnp.concatenate([pad_ends, jnp.full((1,), n_pad, jnp.int32)])
    slot_src = _slot_sources(pos, jnp.stack([gap_lo, gap_hi]).astype(jnp.int32), n_pad)
    p = jnp.arange(n_pad, dtype=jnp.int32)
    spare = n_slots + (p // bm) % 2 * bm + p % bm
    real = slot_src >= 0
    tok_blocks = jnp.where(real, slot_src % n, 0).reshape(n_blocks, 1, bm)
    dst = jnp.where(real, slot_src, spare)
    dst_blocks = jnp.concatenate([spare[bm:2 * bm], dst]).reshape(n_blocks + 1, 1, bm)
    block_start = jnp.arange(n_blocks, dtype=jnp.int32) * bm
    block_e = jnp.minimum(jnp.sum(pad_ends[None, :] <= block_start[:, None], axis=1),
                          N_EXPERTS - 1).astype(jnp.int32)
    return tok_blocks, dst_blocks, block_e


def _block_diag_ones():
    hid = jnp.arange(MXU_WIDTH, dtype=jnp.int32) // HEAD_DIM
    return (hid[:, None] == hid[None, :]).astype(BF16)


def _pick(n, pref):
    return pref if n % pref == 0 else n


def kernel(x, attn_norm_g, w_in, rwkv_mu, rwkv_w0, rwkv_w_up, rwkv_a0, rwkv_a_up, rwkv_g_up,
           rwkv_k_k, rwkv_k_a, rwkv_r_k, rwkv_ln_w, rwkv_ln_b, fox_f_bias, fox_out_g, w_out,
           ffn_norm_g, router_w, router_b, expert_w1, expert_b1, expert_w2, expert_b2,
           final_norm_g):
    bsz, t, d = x.shape
    n = bsz * t
    depth = w_in.shape[0]
    assert depth == 1, "the final norm is fused into the last stage of a single layer"
    bd = _block_diag_ones()
    x2 = x.reshape(n, d)
    for l in range(depth):
        w_l = w_in[l]
        w_r = w_l[:, :RWKV_IN].astype(BF16)
        w_qkv = w_l[:, RWKV_IN:RWKV_IN + 3 * GROUP_WIDTH].astype(BF16)
        w_qt = w_qkv[:, :GROUP_WIDTH].T
        w_k = w_qkv[:, GROUP_WIDTH:2 * GROUP_WIDTH]
        w_vt = w_qkv[:, 2 * GROUP_WIDTH:].T
        w_f = jnp.pad(w_l[:, RWKV_IN + 3 * GROUP_WIDTH:], ((0, 0), (0, LANES - N_HEADS))).astype(BF16)
        fb_pad = jnp.pad(fox_f_bias[l], (0, LANES - N_HEADS)).reshape(1, LANES)
        wup_pad = jnp.pad(rwkv_w_up[l], ((0, LANES - DECAY_LORA), (0, 0))).astype(BF16)
        aup_pad = jnp.pad(rwkv_a_up[l], ((DECAY_LORA, 0), (0, 0))).astype(BF16)
        gup = rwkv_g_up[l].astype(BF16)
        vec = lambda a: a.reshape(1, -1)

        u_r, qt5, k, vt5, fl = _inproj(x2, vec(attn_norm_g[l]), w_r, w_qt, w_k, w_vt, w_f,
                                       bsz, ATTN_TILE)
        cext, cend = _fox_gate(fl.reshape(bsz, t, LANES), fb_pad, _gate_piece_selectors(),
                               ATTN_TILE)
        y_rwkv = _rwkv_mix(
            u_r.reshape(bsz, t, RWKV_IN), vec(rwkv_mu[l]), vec(rwkv_w0[l]), wup_pad,
            vec(rwkv_a0[l]), aup_pad, gup, vec(rwkv_k_k[l]), vec(rwkv_k_a[l]), vec(rwkv_r_k[l]),
            vec(rwkv_ln_w[l]), vec(rwkv_ln_b[l]), bd, RWKV_CHUNKS_PER_STEP)
        y_fox = _fox_attn(qt5, k.reshape(bsz, t, GROUP_WIDTH), cext, vt5, cend,
                          vec(fox_out_g[l]), ATTN_TILE)

        wo = w_out[l].astype(BF16)
        x1, h2, idx, gates, rank, counts = _outproj_router(
            x2, y_rwkv.reshape(n, GROUP_WIDTH), y_fox.reshape(n, GROUP_WIDTH),
            wo[:GROUP_WIDTH], wo[GROUP_WIDTH:], vec(ffn_norm_g[l]),
            router_w[l].T, router_b[l].reshape(N_EXPERTS, 1), _pick(n, ROUTER_TILE))

        bm = EXPERT_BLOCK
        tok_blocks, dst_blocks, block_e = _dispatch_plan(idx, rank, counts, bm)
        w1g, w1l = _w1_split(expert_w1[l], W1_SPLIT_ROWS)
        b1 = expert_b1[l]
        b1g = b1[:, None, 0::2]
        b1l = b1[:, None, 1::2]
        y_all = _expert_mlp(block_e, tok_blocks, dst_blocks, h2, w1g, w1l, b1g, b1l,
                            expert_w2[l].astype(BF16), expert_b2[l][:, None, :], bm)
        x2 = _combine(gates.T, x1, vec(final_norm_g), y_all, _pick(n, COMBINE_TILE))
    return x2.reshape(bsz, t, d)
```

```python
import functools

import jax
import jax.numpy as jnp
from jax import lax
from jax.experimental import pallas as pl
from jax.experimental.pallas import tpu as pltpu

F32 = jnp.float32
BF16 = jnp.bfloat16

HEAD_DIM = 64
N_HEADS = 8
GROUP_WIDTH = N_HEADS * HEAD_DIM
DECAY_LORA = 64
AAA_LORA = 64
GATE_LORA = 128
RWKV_IN = 3 * GROUP_WIDTH + DECAY_LORA + AAA_LORA + GATE_LORA
LORA_OFF = 3 * GROUP_WIDTH
N_EXPERTS = 32
TOP_K = 4
SWIGLU_ALPHA = 1.702
SWIGLU_LIMIT = 7.0
RMS_EPS = 1e-5
RWKV_GN_EPS = 64e-5
LANES = 128
SUBLANES = 8
MXU_WIDTH = 256
RWKV_CHUNK = 64
RWKV_CHUNKS_PER_STEP = 8
ATTN_TILE = 512
ROUTER_TILE = 1024
EXPERT_BLOCK = 256
W1_SPLIT_ROWS = 512
COMBINE_TILE = 512
SLOT_CHUNK = 8192
LOG2E = 1.4426950408889634
Q_SCALE = HEAD_DIM ** -0.5 * LOG2E
ZERO_PROB_EXP = -152.0
NORM_SLACK = 1.0 + 2.0 ** -6
VMEM_LIMIT = 56 * 1024 * 1024


def _cparams(semantics):
    return pltpu.CompilerParams(dimension_semantics=semantics, vmem_limit_bytes=VMEM_LIMIT)


def _dot(a, b):
    return jnp.dot(a, b, preferred_element_type=F32)


def _dot_nt(a, b):
    return lax.dot_general(a, b, (((1,), (1,)), ((), ())), preferred_element_type=F32)


def _dot_tn(a, b):
    return lax.dot_general(a, b, (((0,), (0,)), ((), ())), preferred_element_type=F32)


def _split3(x):
    hi = x.astype(BF16)
    r1 = x - hi.astype(F32)
    mid = r1.astype(BF16)
    lo = (r1 - mid.astype(F32)).astype(BF16)
    return hi, mid, lo


def _dot_exact_lhs(a_bf16, x):
    hi, mid, lo = _split3(x)
    return _dot(a_bf16, hi) + _dot(a_bf16, mid) + _dot(a_bf16, lo)


def _head_sums(x, same_head):
    w = same_head.shape[0]
    parts = []
    for g in range(x.shape[1] // w):
        xs = x[:, g * w:(g + 1) * w]
        hi = xs.astype(BF16)
        lo = (xs - hi.astype(F32)).astype(BF16)
        parts.append(_dot(hi, same_head) + _dot(lo, same_head))
    return jnp.concatenate(parts, axis=1)


def _softplus(z):
    return jnp.maximum(z, 0.0) + jnp.log1p(jnp.exp(-jnp.abs(z)))


def _sigmoid(z):
    return 1.0 / (1.0 + jnp.exp(-z))


def _inproj_kernel(x_ref, g_ref, wr_ref, wqt_ref, wk_ref, wvt_ref, wf_ref,
                   ur_ref, qt_ref, k_ref, vt_ref, fl_ref):
    x = x_ref[...]
    h = x * lax.rsqrt(jnp.mean(x * x, axis=-1, keepdims=True) + RMS_EPS) * g_ref[...]
    hb = h.astype(BF16)
    ur_ref[...] = _dot(hb, wr_ref[...])
    k_ref[...] = _dot(hb, wk_ref[...]).astype(BF16)
    fl_ref[...] = _dot(hb, wf_ref[...])
    qt = (_dot_nt(wqt_ref[...], hb) * Q_SCALE).astype(BF16)
    vt = _dot_nt(wvt_ref[...], hb).astype(BF16)
    for p in range(N_HEADS // 2):
        qt_ref[0, p, 0] = qt[p * LANES:(p + 1) * LANES]
        vt_ref[0, p, 0] = vt[p * LANES:(p + 1) * LANES]


def _inproj(x2, g, w_r, w_qt, w_k, w_vt, w_f, bsz, tm):
    n, d = x2.shape
    nt = n // bsz // tm
    pairs = N_HEADS // 2
    const = lambda i: (0, 0)
    row = lambda i: (i, 0)
    fm = lambda i: (i // nt, 0, i % nt, 0, 0)
    fm_sds = jax.ShapeDtypeStruct((bsz, pairs, nt, LANES, tm), BF16)
    return pl.pallas_call(
        _inproj_kernel,
        grid=(n // tm,),
        in_specs=[
            pl.BlockSpec((tm, d), row),
            pl.BlockSpec((1, d), const),
            pl.BlockSpec(w_r.shape, const),
            pl.BlockSpec(w_qt.shape, const),
            pl.BlockSpec(w_k.shape, const),
            pl.BlockSpec(w_vt.shape, const),
            pl.BlockSpec(w_f.shape, const),
        ],
        out_specs=[
            pl.BlockSpec((tm, RWKV_IN), row),
            pl.BlockSpec((1, pairs, 1, LANES, tm), fm),
            pl.BlockSpec((tm, GROUP_WIDTH), row),
            pl.BlockSpec((1, pairs, 1, LANES, tm), fm),
            pl.BlockSpec((tm, LANES), row),
        ],
        out_shape=[
            jax.ShapeDtypeStruct((n, RWKV_IN), F32),
            fm_sds,
            jax.ShapeDtypeStruct((n, GROUP_WIDTH), BF16),
            fm_sds,
            jax.ShapeDtypeStruct((n, LANES), F32),
        ],
        compiler_params=_cparams(("parallel",)),
        name="inproj",
    )(x2, g, w_r, w_qt, w_k, w_vt, w_f)


def _fox_gate_kernel(fl_ref, fb_ref, sel_ref, c_ref, cend_ref, carry):
    tt = fl_ref.shape[1]

    @pl.when(pl.program_id(1) == 0)
    def _():
        carry[...] = jnp.zeros_like(carry)

    z = fl_ref[0] + fb_ref[...]
    log_f = jnp.minimum(z, 0.0) - jnp.log1p(jnp.exp(-jnp.abs(z)))
    ri = lax.broadcasted_iota(jnp.int32, (tt, tt), 0)
    ci = lax.broadcasted_iota(jnp.int32, (tt, tt), 1)
    tri = jnp.where(ri >= ci, 1.0, 0.0).astype(BF16)
    c = _dot_exact_lhs(tri, log_f) + carry[...]
    carry[...] = c[tt - 1:tt, :]
    cend_ref[0, 0] = c[tt - 1:tt, :]
    hi, mid, lo = _split3(c * LOG2E)
    c_ref[0] = (_dot(hi, sel_ref[0]) + _dot(mid, sel_ref[1]) + _dot(lo, sel_ref[2])).astype(BF16)


def _gate_piece_selectors():
    h = jnp.arange(LANES, dtype=jnp.int32)[:, None]
    col = jnp.arange(GROUP_WIDTH, dtype=jnp.int32)[None, :]
    sels = []
    for m in range(3):
        target = LANES * (h // 2) + 3 * (h % 2) + m
        sels.append(((col == target) & (h < N_HEADS)).astype(BF16))
    return jnp.stack(sels)


def _fox_gate(fl3, fb_pad, sel, tt):
    b, t, _ = fl3.shape
    return pl.pallas_call(
        _fox_gate_kernel,
        grid=(b, t // tt),
        in_specs=[
            pl.BlockSpec((1, tt, LANES), lambda i, j: (i, j, 0)),
            pl.BlockSpec((1, LANES), lambda i, j: (0, 0)),
            pl.BlockSpec((3, LANES, GROUP_WIDTH), lambda i, j: (0, 0, 0)),
        ],
        out_specs=[
            pl.BlockSpec((1, tt, GROUP_WIDTH), lambda i, j: (i, j, 0)),
            pl.BlockSpec((1, 1, 1, LANES), lambda i, j: (i, j, 0, 0)),
        ],
        out_shape=[
            jax.ShapeDtypeStruct((b, t, GROUP_WIDTH), BF16),
            jax.ShapeDtypeStruct((b, t // tt, 1, LANES), F32),
        ],
        scratch_shapes=[pltpu.VMEM((1, LANES), F32)],
        compiler_params=_cparams(("parallel", "arbitrary")),
        name="fox_gate",
    )(fl3, fb_pad, sel)


def _rwkv_prep_body(u_ref, mu_ref, w0_ref, wup_ref, a0_ref, aup_ref, gup_ref, kk_ref, ka_ref,
                      bd_ref, r_out, k_out, v_out, lw_out, kk_out, b_out, g_out, carry):
    tt = u_ref.shape[1]

    @pl.when(pl.program_id(1) == 0)
    def _():
        carry[...] = jnp.zeros_like(carry)

    u = u_ref[0]
    prev = pltpu.roll(u, 1, axis=0)
    row = lax.broadcasted_iota(jnp.int32, u.shape, 0)
    prev = jnp.where(row == 0, carry[...], prev)
    carry[...] = u[tt - 1:tt, :]
    us = u + (prev - u) * mu_ref[...]

    r = us[:, :GROUP_WIDTH]
    k = us[:, GROUP_WIDTH:2 * GROUP_WIDTH]
    v = us[:, 2 * GROUP_WIDTH:LORA_OFF]
    wa = us[:, LORA_OFF:LORA_OFF + LANES]
    gl = us[:, LORA_OFF + LANES:]

    w_lin = _dot(jnp.tanh(wa).astype(BF16), wup_ref[...])
    a_lin = _dot(wa.astype(BF16), aup_ref[...])
    w = -_softplus(-(w0_ref[...] + w_lin)) - 0.5
    lw_out[0] = -jnp.exp(w)
    a = _sigmoid(a0_ref[...] + a_lin)
    g_out[0] = _dot(_sigmoid(gl).astype(BF16), gup_ref[...])

    kkr = k * kk_ref[...]
    ss = _head_sums(kkr * kkr, bd_ref[...])
    kk = kkr / jnp.maximum(jnp.sqrt(ss), 1e-12)
    r_out[0] = r
    k_out[0] = k * (1.0 + (a - 1.0) * ka_ref[...])
    v_out[0] = v
    kk_out[0] = kk
    b_out[0] = kk * a


def _rwkv_scan_body(r_ref, k_ref, v_ref, lw_ref, kk_ref, b_ref, g_ref, rk_ref, lnw_ref, lnb_ref,
                      bd_ref, o_ref, s_scr, *, n_chunks):
    c = RWKV_CHUNK
    w = MXU_WIDTH
    hpg = w // HEAD_DIM
    n_groups = GROUP_WIDTH // w

    @pl.when(pl.program_id(1) == 0)
    def _():
        s_scr[...] = jnp.zeros_like(s_scr)

    row = lax.broadcasted_iota(jnp.int32, (c, w), 0)
    u = lax.broadcasted_iota(jnp.int32, (c, w), 1) % HEAD_DIM
    strict = row > u
    incl = row >= u
    eye = jnp.where(row == u, 1.0, 0.0)
    level_masks = []
    s = 1
    while s < c:
        same = (row // (2 * s)) == (u // (2 * s))
        level_masks.append(same & ((row % (2 * s)) >= s) & ((u % (2 * s)) < s))
        s *= 2
    same_head = (lax.broadcasted_iota(jnp.int32, (w, w), 0) // HEAD_DIM
                 == lax.broadcasted_iota(jnp.int32, (w, w), 1) // HEAD_DIM)
    tri = jnp.where(lax.broadcasted_iota(jnp.int32, (c, c), 0)
                    >= lax.broadcasted_iota(jnp.int32, (c, c), 1), 1.0, 0.0).astype(BF16)

    def bdiag(x):
        xb = x.astype(BF16)
        tiled = jnp.concatenate([xb] * hpg, axis=0)
        return jnp.where(same_head, tiled, jnp.zeros_like(tiled))

    chains = [(ci, gi) for ci in range(n_chunks) for gi in range(n_groups)]
    lhs, rk_t, vbs, xcat, xneg, gam, rt32 = {}, {}, {}, {}, {}, {}, {}
    for ci in range(n_chunks):
        rs = slice(ci * c, (ci + 1) * c)
        r = r_ref[0, rs, :]
        k = k_ref[0, rs, :]
        lw = lw_ref[0, rs, :]
        kk = kk_ref[0, rs, :]
        b = b_ref[0, rs, :]
        g_cum = _dot_exact_lhs(tri, lw)
        g_last = g_cum[c - 1:c, :]
        r_t = r * jnp.exp(g_cum)
        kk_t = kk * jnp.exp(g_cum - lw)
        e_neg = jnp.exp(-g_cum)
        b_n = b * e_neg
        k_n = k * e_neg
        e_end = jnp.exp(g_last - g_cum)
        b_e = (b * e_end).astype(BF16)
        k_e = (k * e_end).astype(BF16)
        gamma = jnp.exp(g_last)
        vb = v_ref[0, rs, :].astype(BF16)
        for gi in range(n_groups):
            gs = slice(gi * w, (gi + 1) * w)
            ch = (ci, gi)
            lhs[ch] = jnp.concatenate([r_t[:, gs].astype(BF16), kk_t[:, gs].astype(BF16)], axis=0)
            rk_t[ch] = (jnp.concatenate([bdiag(b_n[:, gs]), bdiag(k_n[:, gs])], axis=0),
                        bdiag(kk_t[:, gs]))
            vbs[ch] = vb[:, gs]
            xcat[ch] = jnp.concatenate([b_e[:, gs], k_e[:, gs]], axis=0)
            xneg[ch] = jnp.concatenate([-b_e[:, gs], k_e[:, gs]], axis=0)
            gam[ch] = gamma[:, gs]
            rt32[ch] = r_t[:, gs]

    p = {ch: _dot_nt(lhs[ch], rk_t[ch][0]) for ch in chains}
    l_b = {ch: jnp.where(strict, p[ch][c:, :w], 0.0) for ch in chains}
    l_k = {ch: jnp.where(strict, p[ch][c:, w:], 0.0).astype(BF16) for ch in chains}
    p_br = {ch: jnp.where(incl, p[ch][:c, :w], 0.0).astype(BF16) for ch in chains}
    p_kr = {ch: jnp.where(incl, p[ch][:c, w:], 0.0).astype(BF16) for ch in chains}
    v_bd = {ch: bdiag(vbs[ch]) for ch in chains}
    lkv = {ch: _dot(l_k[ch], v_bd[ch]) for ch in chains}

    t_inv = {ch: eye - jnp.where(level_masks[0], l_b[ch], 0.0) for ch in chains}
    for m in level_masks[1:]:
        tb = {ch: t_inv[ch].astype(BF16) for ch in chains}
        ct = {ch: _dot(jnp.where(m, l_b[ch], 0.0).astype(BF16), bdiag(tb[ch])) for ch in chains}
        t_inv = {ch: t_inv[ch] - _dot(tb[ch], bdiag(ct[ch])) for ch in chains}

    mm = {ch: _dot(t_inv[ch].astype(BF16),
                   jnp.concatenate([rk_t[ch][1], bdiag(lkv[ch])], axis=1)) for ch in chains}
    pm = {ch: _dot(p_br[ch], jnp.concatenate([bdiag(mm[ch][:, :w]), bdiag(mm[ch][:, w:])], axis=1))
          for ch in chains}
    pkv = {ch: _dot(p_kr[ch], v_bd[ch]) for ch in chains}
    n1 = {ch: (rt32[ch] - pm[ch][:, :w]).astype(BF16) for ch in chains}
    n2 = {ch: pkv[ch] - pm[ch][:, w:] for ch in chains}
    omega = {ch: jnp.where(same_head, _dot_tn(mm[ch][:, :w].astype(BF16), xcat[ch][:c]), 0.0)
             .astype(BF16) for ch in chains}
    psi = {ch: jnp.where(same_head, _dot_tn(
        jnp.concatenate([mm[ch][:, w:].astype(BF16), vbs[ch]], axis=0), xneg[ch]), 0.0)
           for ch in chains}

    state = [s_scr[gi] for gi in range(n_groups)]
    ys = []
    for ci in range(n_chunks):
        sb = [state[gi].astype(BF16) for gi in range(n_groups)]
        ys.append(jnp.concatenate(
            [_dot_nt(n1[(ci, gi)], sb[gi]) + n2[(ci, gi)] for gi in range(n_groups)], axis=1))
        state = [state[gi] * gam[(ci, gi)] - _dot(sb[gi], omega[(ci, gi)]) + psi[(ci, gi)]
                 for gi in range(n_groups)]
    for gi in range(n_groups):
        s_scr[gi] = state[gi]

    y = jnp.concatenate(ys, axis=0)
    r = r_ref[0]
    k = k_ref[0]
    v = v_ref[0]
    bd = bd_ref[...]
    inv_n = 1.0 / HEAD_DIM
    mean = _head_sums(y, bd) * inv_n
    d = y - mean
    var = _head_sums(d * d, bd) * inv_n
    yn = d * lax.rsqrt(var + RWKV_GN_EPS) * lnw_ref[...] + lnb_ref[...]
    bonus = _head_sums(r * k * rk_ref[...], bd) * v
    o_ref[0] = ((yn + bonus) * g_ref[0]).astype(o_ref.dtype)


def _rwkv_kernel(u_ref, mu_ref, w0_ref, wup_ref, a0_ref, aup_ref, gup_ref, kkw_ref, ka_ref,
                 rk_ref, lnw_ref, lnb_ref, bd_ref, o_ref, carry, s_scr,
                 r_s, k_s, v_s, lw_s, kk_s, b_s, g_s, *, n_chunks):
    _rwkv_prep_body(u_ref, mu_ref, w0_ref, wup_ref, a0_ref, aup_ref, gup_ref, kkw_ref, ka_ref,
                    bd_ref, r_s, k_s, v_s, lw_s, kk_s, b_s, g_s, carry)
    _rwkv_scan_body(r_s, k_s, v_s, lw_s, kk_s, b_s, g_s, rk_ref, lnw_ref, lnb_ref, bd_ref,
                    o_ref, s_scr, n_chunks=n_chunks)


def _rwkv_mix(u3, mu, w0, wup_pad, a0, aup_pad, gup, k_k, k_a, r_k, ln_w, ln_b, bd, n_chunks):
    bsz, t, _ = u3.shape
    rows = RWKV_CHUNK * n_chunks
    const = lambda i, j: (0, 0)
    tile = lambda i, j: (i, j, 0)
    vec = pl.BlockSpec((1, GROUP_WIDTH), const)
    staged = pltpu.VMEM((1, rows, GROUP_WIDTH), F32)
    return pl.pallas_call(
        functools.partial(_rwkv_kernel, n_chunks=n_chunks),
        grid=(bsz, t // rows),
        in_specs=[
            pl.BlockSpec((1, rows, RWKV_IN), tile),
            pl.BlockSpec((1, RWKV_IN), const),
            vec,
            pl.BlockSpec((LANES, GROUP_WIDTH), const),
            vec,
            pl.BlockSpec((LANES, GROUP_WIDTH), const),
            pl.BlockSpec((GATE_LORA, GROUP_WIDTH), const),
            vec, vec, vec, vec, vec,
            pl.BlockSpec((MXU_WIDTH, MXU_WIDTH), const),
        ],
        out_specs=pl.BlockSpec((1, rows, GROUP_WIDTH), tile),
        out_shape=jax.ShapeDtypeStruct((bsz, t, GROUP_WIDTH), BF16),
        scratch_shapes=[
            pltpu.VMEM((1, RWKV_IN), F32),
            pltpu.VMEM((GROUP_WIDTH // MXU_WIDTH, MXU_WIDTH, MXU_WIDTH), F32),
        ] + [staged] * 7,
        compiler_params=_cparams(("parallel", "arbitrary")),
        name="rwkv_mix",
    )(u3, mu, w0, wup_pad, a0, aup_pad, gup, k_k, k_a, r_k, ln_w, ln_b, bd)


def _fox_attn_kernel(qt_ref, k_ref, ce_ref, vt_ref, cend_ref, og_ref, o_ref,
                     m_scr, l_scr, acc_scr, kmax_scr, *, t):
    qi = pl.program_id(2)
    n_strips = 2 * t // LANES
    qt = qt_ref[0, 0, 0]
    frow = lax.broadcasted_iota(jnp.int32, (LANES, t), 0)
    zero = jnp.zeros_like(qt)
    main = jnp.concatenate([jnp.where(frow < HEAD_DIM, qt, zero),
                            jnp.where(frow < HEAD_DIM, zero, qt)], axis=1)
    erow = lax.broadcasted_iota(jnp.int32, (LANES, 2 * t), 0)
    ecol = lax.broadcasted_iota(jnp.int32, (LANES, 2 * t), 1)
    off = jnp.where(ecol < t, 0, 3)
    extra = jnp.where((erow >= off) & (erow < off + 3), -1.0, 0.0).astype(BF16)
    q_aug = jnp.concatenate([main, extra], axis=0)

    m_scr[...] = jnp.full_like(m_scr, -jnp.inf)
    l_scr[...] = jnp.zeros_like(l_scr)
    acc_scr[...] = jnp.zeros_like(acc_scr)

    @pl.when(qi == 0)
    def _():
        hid_r = lax.broadcasted_iota(jnp.int32, (LANES, LANES), 0) // HEAD_DIM
        hid_c = lax.broadcasted_iota(jnp.int32, (LANES, LANES), 1) // HEAD_DIM
        same_head = jnp.where(hid_r == hid_c, 1.0, 0.0).astype(BF16)

        def tile_max(j, best):
            kf = k_ref[0, pl.ds(pl.multiple_of(j * t, t), t), :].astype(F32)
            sq = _dot((kf * kf).astype(BF16), same_head)
            return jnp.maximum(best, jnp.max(sq, axis=0, keepdims=True))

        best = lax.fori_loop(0, k_ref.shape[1] // t, tile_max, jnp.zeros((1, LANES), F32))
        kmax_scr[...] = jnp.sqrt(best) * NORM_SLACK

    def step(j, masked):
        start = pl.multiple_of(j * t, t)
        k_aug = jnp.concatenate([k_ref[0, pl.ds(start, t), :], ce_ref[0, pl.ds(start, t), :]],
                                axis=1)
        vt = vt_ref[0, 0, j]
        zt = _dot(k_aug, q_aug)
        m_prev = m_scr[...]
        l_prev = l_scr[...]
        acc_prev = acc_scr[...]
        m_out, l_out, acc_out = [], [], [[], []]
        for s in range(n_strips):
            head = s // (n_strips // 2)
            cs = slice(s * LANES, (s + 1) * LANES)
            z = zt[:, cs]
            if masked:
                key = lax.broadcasted_iota(jnp.int32, (t, LANES), 0)
                qry = lax.broadcasted_iota(jnp.int32, (t, LANES), 1) + (s * LANES) % t
                z = jnp.where(key <= qry, z, -jnp.inf)
            m_new = jnp.maximum(m_prev[:, cs], jnp.max(z, axis=0, keepdims=True))
            alpha = jnp.exp2(m_prev[:, cs] - m_new)
            p = jnp.exp2(z - m_new)
            l_out.append(alpha * l_prev[:, cs] + jnp.sum(p, axis=0, keepdims=True))
            m_out.append(m_new)
            hs = slice(head * HEAD_DIM, (head + 1) * HEAD_DIM)
            qs = slice((s * LANES) % t, (s * LANES) % t + LANES)
            pv = _dot(vt[hs], p.astype(BF16))
            acc_out[head].append(alpha * acc_prev[hs, qs] + pv)
        m_scr[...] = jnp.concatenate(m_out, axis=1)
        l_scr[...] = jnp.concatenate(l_out, axis=1)
        acc_scr[...] = jnp.concatenate([jnp.concatenate(acc_out[0], axis=1),
                                        jnp.concatenate(acc_out[1], axis=1)], axis=0)

    def body(j, carry):
        step(j, False)
        return carry

    step(qi, True)

    qf = qt.astype(F32)
    qsq = qf * qf
    qnorm = jnp.sqrt(jnp.concatenate(
        [jnp.sum(qsq[:HEAD_DIM], axis=0, keepdims=True),
         jnp.sum(qsq[HEAD_DIM:], axis=0, keepdims=True)], axis=1)) * NORM_SLACK
    kmax = kmax_scr[...]
    kmax2 = jnp.concatenate([jnp.broadcast_to(kmax[:, 0:1], (1, t)),
                             jnp.broadcast_to(kmax[:, HEAD_DIM:HEAD_DIM + 1], (1, t))], axis=1)
    slack = qnorm * kmax2 - m_scr[...]
    cend = cend_ref[0][:, 0, :]
    lane = lax.broadcasted_iota(jnp.int32, cend.shape, 1)
    jrow = lax.broadcasted_iota(jnp.int32, (cend.shape[0], 1), 0)
    needed = jrow < 0
    for head in range(2):
        worst = jnp.max(slack[:, head * t:(head + 1) * t], axis=1, keepdims=True)
        c_head = jnp.sum(jnp.where(lane == 2 * pl.program_id(1) + head, cend, 0.0),
                         axis=1, keepdims=True)
        needed = needed | (worst - c_head * LOG2E > ZERO_PROB_EXP)
    first = jnp.min(jnp.where(needed & (jrow < qi), jrow, qi))
    lax.fori_loop(first, qi, body, 0)

    l = l_scr[...]
    acc = acc_scr[...]
    inv_n = 1.0 / HEAD_DIM
    halves = []
    for head in range(2):
        o = acc[head * HEAD_DIM:(head + 1) * HEAD_DIM] / l[:, head * t:(head + 1) * t]
        halves.append(o * lax.rsqrt(jnp.sum(o * o, axis=0, keepdims=True) * inv_n + RMS_EPS))
    o_t = jnp.concatenate(halves, axis=0)
    o_ref[0] = (jnp.transpose(o_t) * og_ref[...]).astype(o_ref.dtype)


def _fox_attn(qt5, k, cext, vt5, cend, out_g, tile):
    bsz, t_all, _ = k.shape
    pairs = N_HEADS // 2
    tiles = t_all // tile
    kern = functools.partial(_fox_attn_kernel, t=tile)
    return pl.pallas_call(
        kern,
        grid=(bsz, pairs, tiles),
        in_specs=[
            pl.BlockSpec((1, 1, 1, LANES, tile), lambda b, p, i: (b, p, i, 0, 0)),
            pl.BlockSpec((1, t_all, LANES), lambda b, p, i: (b, 0, p)),
            pl.BlockSpec((1, t_all, LANES), lambda b, p, i: (b, 0, p)),
            pl.BlockSpec((1, 1, tiles, LANES, tile), lambda b, p, i: (b, p, 0, 0, 0)),
            pl.BlockSpec((1, tiles, 1, LANES), lambda b, p, i: (b, 0, 0, 0)),
            pl.BlockSpec((1, LANES), lambda b, p, i: (0, p)),
        ],
        out_specs=pl.BlockSpec((1, tile, LANES), lambda b, p, i: (b, i, p)),
        out_shape=jax.ShapeDtypeStruct((bsz, t_all, GROUP_WIDTH), BF16),
        scratch_shapes=[
            pltpu.VMEM((1, 2 * tile), F32),
            pltpu.VMEM((1, 2 * tile), F32),
            pltpu.VMEM((LANES, tile), F32),
            pltpu.VMEM((1, LANES), F32),
        ],
        compiler_params=_cparams(("parallel", "parallel", "arbitrary")),
        name="fox_attn",
    )(qt5, k, cext, vt5, cend, out_g)


def _outproj_router_kernel(x_ref, yr_ref, yf_ref, wo_r_ref, wo_f_ref, g_ref, rwt_ref, rb_ref,
                           x1_ref, h_ref, idx_ref, gate_ref, rank_ref, count_ref):
    @pl.when(pl.program_id(0) == 0)
    def _():
        count_ref[...] = jnp.zeros_like(count_ref)

    x1 = x_ref[...] + _dot(yr_ref[...], wo_r_ref[...]) + _dot(yf_ref[...], wo_f_ref[...])
    x1_ref[...] = x1
    h = x1 * lax.rsqrt(jnp.mean(x1 * x1, axis=-1, keepdims=True) + RMS_EPS) * g_ref[...]
    bits = lax.bitcast_convert_type(h.astype(BF16).astype(F32), jnp.uint32)
    half = h.shape[1] // 2
    h_ref[:, 0, :] = (bits[:, half:] & jnp.uint32(0xFFFF0000)) | (bits[:, :half] >> 16)
    logits = lax.dot_general(rwt_ref[...], h, (((1,), (1,)), ((), ())),
                             precision=lax.Precision.HIGHEST,
                             preferred_element_type=F32) + rb_ref[...]
    eidx = lax.broadcasted_iota(jnp.int32, logits.shape, 0)
    vals, idxs, picks = [], [], []
    for _ in range(TOP_K):
        m = jnp.max(logits, axis=0, keepdims=True)
        i = jnp.min(jnp.where(logits == m, eidx, N_EXPERTS), axis=0, keepdims=True)
        vals.append(m)
        idxs.append(i)
        picks.append(eidx == i)
        logits = jnp.where(picks[-1], -jnp.inf, logits)
    es = [jnp.exp(val - vals[0]) for val in vals]
    denom = es[0] + es[1] + es[2] + es[3]
    idx_ref[...] = jnp.concatenate(idxs, axis=0)
    gate_ref[...] = jnp.concatenate([e / denom for e in es], axis=0)

    tm = logits.shape[1]
    chosen = [jnp.where(pk, 1.0, 0.0) for pk in picks]
    any_k = chosen[0] + chosen[1] + chosen[2] + chosen[3]
    before = (lax.broadcasted_iota(jnp.int32, (tm, tm), 0)
              < lax.broadcasted_iota(jnp.int32, (tm, tm), 1))
    prefix = _dot(any_k.astype(BF16), jnp.where(before, 1.0, 0.0).astype(BF16))
    seen = count_ref[:, 0:1] + prefix
    rank_ref[...] = jnp.concatenate(
        [jnp.sum(ch * seen, axis=0, keepdims=True) for ch in chosen], axis=0).astype(jnp.int32)
    count_ref[...] = count_ref[...] + jnp.sum(any_k, axis=1, keepdims=True)


def _outproj_router(x2, yr, yf, wo_r, wo_f, g, rwt, rb, tm):
    n, d = x2.shape
    const = lambda i: (0, 0)
    row = lambda i: (i, 0)
    col = lambda i: (0, i)
    return pl.pallas_call(
        _outproj_router_kernel,
        grid=(n // tm,),
        in_specs=[
            pl.BlockSpec((tm, d), row),
            pl.BlockSpec((tm, GROUP_WIDTH), row),
            pl.BlockSpec((tm, GROUP_WIDTH), row),
            pl.BlockSpec((GROUP_WIDTH, d), const),
            pl.BlockSpec((GROUP_WIDTH, d), const),
            pl.BlockSpec((1, d), const),
            pl.BlockSpec((N_EXPERTS, d), const),
            pl.BlockSpec((N_EXPERTS, 1), const),
        ],
        out_specs=[
            pl.BlockSpec((tm, d), row),
            pl.BlockSpec((tm, 1, d // 2), lambda i: (i, 0, 0)),
            pl.BlockSpec((TOP_K, tm), col),
            pl.BlockSpec((TOP_K, tm), col),
            pl.BlockSpec((TOP_K, tm), col),
            pl.BlockSpec((N_EXPERTS, LANES), const),
        ],
        out_shape=[
            jax.ShapeDtypeStruct((n, d), F32),
            jax.ShapeDtypeStruct((n, 1, d // 2), jnp.uint32),
            jax.ShapeDtypeStruct((TOP_K, n), jnp.int32),
            jax.ShapeDtypeStruct((TOP_K, n), F32),
            jax.ShapeDtypeStruct((TOP_K, n), jnp.int32),
            jax.ShapeDtypeStruct((N_EXPERTS, LANES), F32),
        ],
        compiler_params=_cparams(("arbitrary",)),
        name="outproj_router",
    )(x2, yr, yf, wo_r, wo_f, g, rwt, rb)


def _w1_split_kernel(w_ref, perm_ref, g_ref, l_ref):
    half = MXU_WIDTH // 2
    perm = perm_ref[...]
    for grp in range(w_ref.shape[2] // MXU_WIDTH):
        blk = w_ref[0, :, grp * MXU_WIDTH:(grp + 1) * MXU_WIDTH].astype(BF16)
        r = _dot(blk, perm)
        g_ref[0, :, grp * half:(grp + 1) * half] = r[:, :half].astype(BF16)
        l_ref[0, :, grp * half:(grp + 1) * half] = r[:, half:].astype(BF16)


def _w1_split(w1, tr):
    e, d, two_f = w1.shape
    half = MXU_WIDTH // 2
    src = jnp.arange(MXU_WIDTH, dtype=jnp.int32)[:, None]
    dst = jnp.arange(MXU_WIDTH, dtype=jnp.int32)[None, :]
    perm = (src == jnp.where(dst < half, 2 * dst, 2 * (dst - half) + 1)).astype(BF16)
    out_sds = jax.ShapeDtypeStruct((e, d, two_f // 2), BF16)
    return pl.pallas_call(
        _w1_split_kernel,
        grid=(e, d // tr),
        in_specs=[
            pl.BlockSpec((1, tr, two_f), lambda i, j: (i, j, 0)),
            pl.BlockSpec((MXU_WIDTH, MXU_WIDTH), lambda i, j: (0, 0)),
        ],
        out_specs=[pl.BlockSpec((1, tr, two_f // 2), lambda i, j: (i, j, 0))] * 2,
        out_shape=[out_sds, out_sds],
        compiler_params=_cparams(("parallel", "parallel")),
        name="w1_split",
    )(w1, perm)


def _expert_kernel(be_ref, tok_a_ref, tok_b_ref, tok_a_next_ref, dst_b_prev_ref, dst_a_ref,
                   dst_b_ref, h_hbm, w1g_a, w1l_a, b1g_a, b1l_a, w2_a, b2_a,
                   w1g_b, w1l_b, b1g_b, b1l_b, w2_b, b2_b,
                   y_hbm, xbuf_a, xbuf_b, obuf_a, obuf_b, xrows, gsem, osem, *, bm, n_real_rows):
    del be_ref
    i = pl.program_id(0)

    def gather_start(idx_ref, xbuf, sem, r, priority=0):
        pltpu.make_async_copy(h_hbm.at[idx_ref[0, 0, r]], xbuf.at[r], sem).start(priority=priority)

    def scatter_start(idx_ref, obuf, sem, r, priority=0):
        pltpu.make_async_copy(obuf.at[r], y_hbm.at[idx_ref[0, 0, r]], sem).start(priority=priority)

    def rows_wait(buf, sem):
        rows = buf.at[pl.ds(0, bm)]
        pltpu.make_async_copy(rows, rows, sem).wait()

    def mlp(xbuf, w1g_ref, w1l_ref, b1g_ref, b1l_ref, w2_ref, b2_ref, start_gather, next_xbuf):
        xrows[...] = xbuf[pl.ds(0, bm), 0, :]
        words = xrows[...]
        xb_lo = lax.bitcast_convert_type(words << 16, F32).astype(BF16)
        xb_hi = lax.bitcast_convert_type(words & jnp.uint32(0xFFFF0000), F32).astype(BF16)
        half = words.shape[1]

        def x_dot(w_ref, cs):
            return _dot(xb_lo, w_ref[0, :half, cs]) + _dot(xb_hi, w_ref[0, half:, cs])
        dff = w1g_ref.shape[2]
        n_pieces = dff // MXU_WIDTH
        per_piece = -(-bm // (n_pieces - 1))
        acts = []
        for piece in range(n_pieces):
            cs = slice(piece * MXU_WIDTH, (piece + 1) * MXU_WIDTH)
            for r in range(piece * per_piece, min((piece + 1) * per_piece, bm)):
                start_gather(r)
            zero = lax.bitcast_convert_type(next_xbuf[bm + piece, :, :MXU_WIDTH], F32)
            glu = x_dot(w1g_ref, cs) + (b1g_ref[0, :, cs] + zero)
            lin = x_dot(w1l_ref, cs) + b1l_ref[0, :, cs]
            glu = jnp.minimum(glu, SWIGLU_LIMIT)
            lin = jnp.clip(lin, -SWIGLU_LIMIT, SWIGLU_LIMIT)
            acts.append((glu * _sigmoid(SWIGLU_ALPHA * glu) * (lin + 1.0)).astype(BF16))
        return _dot(jnp.concatenate(acts, axis=1), w2_ref[0]) + b2_ref[0]

    @pl.when(i == 0)
    def _():
        obuf_a[...] = jnp.zeros_like(obuf_a)
        obuf_b[...] = jnp.zeros_like(obuf_b)
        spare = jnp.zeros((SUBLANES, 1, xbuf_a.shape[2]), jnp.uint32)
        xbuf_a[pl.ds(bm, SUBLANES)] = spare
        xbuf_b[pl.ds(bm, SUBLANES)] = spare

        def first(r, carry):
            pltpu.make_async_copy(obuf_a.at[r], y_hbm.at[n_real_rows + 2 * bm + r],
                                  osem.at[0]).start()
            gather_start(tok_a_ref, xbuf_a, gsem.at[0], r)
            return carry

        lax.fori_loop(0, bm, first, 0)

    def gather_b(r):
        gather_start(tok_b_ref, xbuf_b, gsem.at[1], r, r % 2)

    def gather_a_next(r):
        gather_start(tok_a_next_ref, xbuf_a, gsem.at[0], r, r % 2)

    for r in range(bm):
        scatter_start(dst_b_prev_ref, obuf_b, osem.at[1], r, r % 2)
    rows_wait(xbuf_a, gsem.at[0])
    rows_wait(obuf_a, osem.at[0])
    obuf_a[:, 0, :] = mlp(xbuf_a, w1g_a, w1l_a, b1g_a, b1l_a, w2_a, b2_a, gather_b, xbuf_b)

    for r in range(bm):
        scatter_start(dst_a_ref, obuf_a, osem.at[0], r, r % 2)
    rows_wait(xbuf_b, gsem.at[1])
    rows_wait(obuf_b, osem.at[1])
    obuf_b[:, 0, :] = mlp(xbuf_b, w1g_b, w1l_b, b1g_b, b1l_b, w2_b, b2_b, gather_a_next, xbuf_a)

    @pl.when(i == pl.num_programs(0) - 1)
    def _():
        def last(r, carry):
            scatter_start(dst_b_ref, obuf_b, osem.at[1], r)
            return carry

        lax.fori_loop(0, bm, last, 0)
        rows_wait(obuf_b, osem.at[1])
        rows_wait(obuf_a, osem.at[0])
        rows_wait(xbuf_a, gsem.at[0])


def _expert_mlp(block_e, tok_blocks, dst_blocks, h2, w1g, w1l, b1g, b1l, w2, b2, bm):
    n_blocks = tok_blocks.shape[0]
    assert n_blocks % 2 == 0
    n = h2.shape[0]
    d = w1g.shape[1]
    dw = h2.shape[2]
    dff = w1g.shape[2]
    n_real_rows = TOP_K * n
    idx_spec = lambda fn: pl.BlockSpec((1, 1, bm), fn, memory_space=pltpu.SMEM)

    def weight_specs(which):
        wmap = lambda i, be: (be[2 * i + which], 0, 0)
        return [
            pl.BlockSpec((1, d, dff), wmap),
            pl.BlockSpec((1, d, dff), wmap),
            pl.BlockSpec((1, 1, dff), wmap),
            pl.BlockSpec((1, 1, dff), wmap),
            pl.BlockSpec((1, dff, d), wmap),
            pl.BlockSpec((1, 1, d), wmap),
        ]

    grid_spec = pltpu.PrefetchScalarGridSpec(
        num_scalar_prefetch=1,
        grid=(n_blocks // 2,),
        in_specs=[
            idx_spec(lambda i, be: (2 * i, 0, 0)),
            idx_spec(lambda i, be: (2 * i + 1, 0, 0)),
            idx_spec(lambda i, be: (jnp.minimum(2 * i + 2, n_blocks - 1), 0, 0)),
            idx_spec(lambda i, be: (2 * i, 0, 0)),
            idx_spec(lambda i, be: (2 * i + 1, 0, 0)),
            idx_spec(lambda i, be: (2 * i + 2, 0, 0)),
            pl.BlockSpec(memory_space=pl.ANY),
        ] + weight_specs(0) + weight_specs(1),
        out_specs=pl.BlockSpec(memory_space=pl.ANY),
        scratch_shapes=[
            pltpu.VMEM((bm + SUBLANES, 1, dw), jnp.uint32),
            pltpu.VMEM((bm + SUBLANES, 1, dw), jnp.uint32),
            pltpu.VMEM((bm, 1, d), F32),
            pltpu.VMEM((bm, 1, d), F32),
            pltpu.VMEM((bm, dw), jnp.uint32),
            pltpu.SemaphoreType.DMA((2,)),
            pltpu.SemaphoreType.DMA((2,)),
        ],
    )
    weights = (w1g, w1l, b1g, b1l, w2, b2)
    return pl.pallas_call(
        functools.partial(_expert_kernel, bm=bm, n_real_rows=n_real_rows),
        grid_spec=grid_spec,
        out_shape=jax.ShapeDtypeStruct((n_real_rows + 3 * bm, 1, d), F32),
        compiler_params=_cparams(("arbitrary",)),
        name="expert_mlp",
    )(block_e, tok_blocks, tok_blocks, tok_blocks, dst_blocks, dst_blocks, dst_blocks, h2,
      *weights, *weights)


def _combine_kernel(gate_ref, x1_ref, g_ref, y0_ref, y1_ref, y2_ref, y3_ref, o_ref):
    gates = gate_ref[...]
    y = x1_ref[...]
    for kk, y_ref in enumerate((y0_ref, y1_ref, y2_ref, y3_ref)):
        y = y + y_ref[:, 0, :] * gates[:, kk:kk + 1]
    o_ref[...] = y * lax.rsqrt(jnp.mean(y * y, axis=-1, keepdims=True) + RMS_EPS) * g_ref[...]


def _combine(gates_t, x1, g, y_all, tc):
    n, d = x1.shape
    tiles = n // tc
    y_spec = lambda kk: pl.BlockSpec((tc, 1, d), lambda i: (kk * tiles + i, 0, 0))
    return pl.pallas_call(
        _combine_kernel,
        grid=(tiles,),
        in_specs=[
            pl.BlockSpec((tc, TOP_K), lambda i: (i, 0)),
            pl.BlockSpec((tc, d), lambda i: (i, 0)),
            pl.BlockSpec((1, d), lambda i: (0, 0)),
        ] + [y_spec(kk) for kk in range(TOP_K)],
        out_specs=pl.BlockSpec((tc, d), lambda i: (i, 0)),
        out_shape=jax.ShapeDtypeStruct((n, d), F32),
        compiler_params=_cparams(("parallel",)),
        name="combine",
    )(gates_t, x1, g, y_all, y_all, y_all, y_all)


def _slot_sources_kernel(gaps_ref, pos_ref, out_ref, *, chunk):
    i = pl.program_id(0)

    @pl.when(i == 0)
    def _():
        def init(p, carry):
            out_ref[p] = -1
            return carry

        for e in range(gaps_ref.shape[1]):
            lax.fori_loop(gaps_ref[0, e], gaps_ref[1, e], init, 0)

    base = i * chunk

    def place(s, carry):
        out_ref[pos_ref[0, 0, s]] = base + s
        return carry

    lax.fori_loop(0, chunk, place, 0, unroll=16)


def _slot_sources(pos, gaps, n_pad):
    n_slots = pos.shape[0]
    chunk = _pick(n_slots, SLOT_CHUNK)
    return pl.pallas_call(
        functools.partial(_slot_sources_kernel, chunk=chunk),
        grid=(n_slots // chunk,),
        in_specs=[
            pl.BlockSpec(memory_space=pltpu.SMEM),
            pl.BlockSpec((1, 1, chunk), lambda i: (i, 0, 0), memory_space=pltpu.SMEM),
        ],
        out_specs=pl.BlockSpec(memory_space=pltpu.SMEM),
        out_shape=jax.ShapeDtypeStruct((n_pad,), jnp.int32),
        compiler_params=_cparams(("arbitrary",)),
        name="slot_sources",
    )(gaps, pos.reshape(n_slots // chunk, 1, chunk))


def _dispatch_plan(idx, rank, counts, bm):
    n = idx.shape[1]
    n_slots = TOP_K * n
    sizes = counts[:, 0].astype(jnp.int32)
    padded = (sizes + bm - 1) // bm * bm
    pad_ends = jnp.cumsum(padded)
    pad_starts = pad_ends - padded
    experts = jnp.arange(N_EXPERTS, dtype=jnp.int32)
    start_of = jnp.sum(jnp.where(idx[..., None] == experts, pad_starts, 0), axis=-1)
    pos = (start_of + rank).reshape(-1)
    n_pad = n_slots + N_EXPERTS * bm
    n_blocks = n_pad // bm
    gap_lo = jnp.concatenate([pad_starts + sizes, pad_ends[-1:]])
    gap_hi = jnp.concatenate([pad_ends, jnp.full((1,), n_pad, jnp.int32)])
    slot_src = _slot_sources(pos, jnp.stack([gap_lo, gap_hi]).astype(jnp.int32), n_pad)
    p = jnp.arange(n_pad, dtype=jnp.int32)
    spare = n_slots + (p // bm) % 2 * bm + p % bm
    real = slot_src >= 0
    tok_blocks = jnp.where(real, slot_src % n, 0).reshape(n_blocks, 1, bm)
    dst = jnp.where(real, slot_src, spare)
    dst_blocks = jnp.concatenate([spare[bm:2 * bm], dst]).reshape(n_blocks + 1, 1, bm)
    block_start = jnp.arange(n_blocks, dtype=jnp.int32) * bm
    block_e = jnp.minimum(jnp.sum(pad_ends[None, :] <= block_start[:, None], axis=1),
                          N_EXPERTS - 1).astype(jnp.int32)
    return tok_blocks, dst_blocks, block_e


def _block_diag_ones():
    hid = jnp.arange(MXU_WIDTH, dtype=jnp.int32) // HEAD_DIM
    return (hid[:, None] == hid[None, :]).astype(BF16)


def _pick(n, pref):
    return pref if n % pref == 0 else n


def kernel(x, attn_norm_g, w_in, rwkv_mu, rwkv_w0, rwkv_w_up, rwkv_a0, rwkv_a_up, rwkv_g_up,
           rwkv_k_k, rwkv_k_a, rwkv_r_k, rwkv_ln_w, rwkv_ln_b, fox_f_bias, fox_out_g, w_out,
           ffn_norm_g, router_w, router_b, expert_w1, expert_b1, expert_w2, expert_b2,
           final_norm_g):
    bsz, t, d = x.shape
    n = bsz * t
    depth = w_in.shape[0]
    assert depth == 1, "the final norm is fused into the last stage of a single layer"
    bd = _block_diag_ones()
    x2 = x.reshape(n, d)
    for l in range(depth):
        w_l = w_in[l]
        w_r = w_l[:, :RWKV_IN].astype(BF16)
        w_qkv = w_l[:, RWKV_IN:RWKV_IN + 3 * GROUP_WIDTH].astype(BF16)
        w_qt = w_qkv[:, :GROUP_WIDTH].T
        w_k = w_qkv[:, GROUP_WIDTH:2 * GROUP_WIDTH]
        w_vt = w_qkv[:, 2 * GROUP_WIDTH:].T
        w_f = jnp.pad(w_l[:, RWKV_IN + 3 * GROUP_WIDTH:], ((0, 0), (0, LANES - N_HEADS))).astype(BF16)
        fb_pad = jnp.pad(fox_f_bias[l], (0, LANES - N_HEADS)).reshape(1, LANES)
        wup_pad = jnp.pad(rwkv_w_up[l], ((0, LANES - DECAY_LORA), (0, 0))).astype(BF16)
        aup_pad = jnp.pad(rwkv_a_up[l], ((DECAY_LORA, 0), (0, 0))).astype(BF16)
        gup = rwkv_g_up[l].astype(BF16)
        vec = lambda a: a.reshape(1, -1)

        u_r, qt5, k, vt5, fl = _inproj(x2, vec(attn_norm_g[l]), w_r, w_qt, w_k, w_vt, w_f,
                                       bsz, ATTN_TILE)
        cext, cend = _fox_gate(fl.reshape(bsz, t, LANES), fb_pad, _gate_piece_selectors(),
                               ATTN_TILE)
        y_rwkv = _rwkv_mix(
            u_r.reshape(bsz, t, RWKV_IN), vec(rwkv_mu[l]), vec(rwkv_w0[l]), wup_pad,
            vec(rwkv_a0[l]), aup_pad, gup, vec(rwkv_k_k[l]), vec(rwkv_k_a[l]), vec(rwkv_r_k[l]),
            vec(rwkv_ln_w[l]), vec(rwkv_ln_b[l]), bd, RWKV_CHUNKS_PER_STEP)
        y_fox = _fox_attn(qt5, k.reshape(bsz, t, GROUP_WIDTH), cext, vt5, cend,
                          vec(fox_out_g[l]), ATTN_TILE)

        wo = w_out[l].astype(BF16)
        x1, h2, idx, gates, rank, counts = _outproj_router(
            x2, y_rwkv.reshape(n, GROUP_WIDTH), y_fox.reshape(n, GROUP_WIDTH),
            wo[:GROUP_WIDTH], wo[GROUP_WIDTH:], vec(ffn_norm_g[l]),
            router_w[l].T, router_b[l].reshape(N_EXPERTS, 1), _pick(n, ROUTER_TILE))

        bm = EXPERT_BLOCK
        tok_blocks, dst_blocks, block_e = _dispatch_plan(idx, rank, counts, bm)
        w1g, w1l = _w1_split(expert_w1[l], W1_SPLIT_ROWS)
        b1 = expert_b1[l]
        b1g = b1[:, None, 0::2]
        b1l = b1[:, None, 1::2]
        y_all = _expert_mlp(block_e, tok_blocks, dst_blocks, h2, w1g, w1l, b1g, b1l,
                            expert_w2[l].astype(BF16), expert_b2[l][:, None, :], bm)
        x2 = _combine(gates.T, x1, vec(final_norm_g), y_all, _pick(n, COMBINE_TILE))
    return x2.reshape(bsz, t, d)
```
